```python
import jax, jax.numpy as jnp
from jax import lax
import numpy as np

D_MODEL = 2048
BATCH = 2
SEQ = 8192
DEPTH = 1

HEAD_DIM = 64
N_HEADS_SWA = 16
N_KV_SWA = 4
N_HEADS_FOX = 16
WINDOW = 128
BLOCK = 128
SWA_WIDTH = N_HEADS_SWA * HEAD_DIM
KV_WIDTH = N_KV_SWA * HEAD_DIM
FOX_WIDTH = N_HEADS_FOX * HEAD_DIM
MIX_WIDTH = SWA_WIDTH + FOX_WIDTH
IN_COLS = SWA_WIDTH + 2 * KV_WIDTH + 3 * FOX_WIDTH + N_HEADS_FOX
N_GROUPS = 4
EXPERTS_PER_GROUP = 8
N_EXPERTS = N_GROUPS * EXPERTS_PER_GROUP
TOP_K = 2
D_EXPERT = 512
MOE_BLOCK = 128
EPS = 1e-6

kernel_name = "hymba_swa_sink_fox_hier_moe_adaln"


def rmsnorm(x, g):
    xf = x.astype(jnp.float32)
    y = xf * lax.rsqrt(jnp.mean(xf * xf, axis=-1, keepdims=True) + EPS)
    return (y * g.astype(jnp.float32)).astype(x.dtype)


def alibi_slopes(n):
    return jnp.asarray(2.0 ** (-8.0 * np.arange(1, n + 1) / n), dtype=jnp.float32)


def sliding_window_gqa(q, k, v, sinks):
    B, S, Hq, d = q.shape
    Hkv = k.shape[2]
    G = Hq // Hkv
    nb = S // BLOCK
    qb = q.reshape(B, nb, BLOCK, Hkv, G, d)

    def with_prev(t):
        tb = t.reshape(B, nb, BLOCK, Hkv, d)
        prev = jnp.pad(tb, ((0, 0), (1, 0), (0, 0), (0, 0), (0, 0)))[:, :-1]
        return jnp.concatenate([prev, tb], axis=2)

    kb, vb = with_prev(k), with_prev(v)
    scores = jnp.einsum('bnqhgd,bnkhd->bhgnqk', qb, kb).astype(jnp.float32) * (d ** -0.5)
    qpos = jnp.arange(BLOCK)[:, None] + BLOCK
    kpos = jnp.arange(2 * BLOCK)[None, :]
    dist = qpos - kpos
    in_band = (dist >= 0) & (dist < WINDOW)
    has_prev = (jnp.arange(nb)[:, None, None] > 0) | (kpos >= BLOCK)[None]
    mask = in_band[None] & has_prev
    slopes = alibi_slopes(Hq).reshape(Hkv, G, 1, 1, 1)
    scores = scores - slopes * dist.astype(jnp.float32)
    scores = jnp.where(mask, scores, -jnp.inf)
    sink = sinks.astype(jnp.float32).reshape(Hkv, G, 1, 1, 1)
    m = jnp.maximum(jnp.max(scores, axis=-1, keepdims=True), sink)
    p = jnp.exp(scores - m)
    p = p / (jnp.sum(p, axis=-1, keepdims=True) + jnp.exp(sink - m))
    out = jnp.einsum('bhgnqk,bnkhd->bnqhgd', p.astype(v.dtype), vb)
    return out.reshape(B, S, Hq * d)


def forgetting_attention(q, k, v, log_f):
    B, S, H, d = q.shape
    nb = S // BLOCK
    cum = lax.cumsum(log_f, axis=1).transpose(0, 2, 1)
    outs = []
    for i in range(nb):
        q0 = i * BLOCK
        kend = q0 + BLOCK
        s = jnp.einsum('bqhd,bkhd->bhqk', q[:, q0:kend], k[:, :kend]).astype(jnp.float32) * (d ** -0.5)
        s = s + cum[:, :, q0:kend, None] - cum[:, :, None, :kend]
        causal = (q0 + jnp.arange(BLOCK))[:, None] >= jnp.arange(kend)[None, :]
        p = jax.nn.softmax(jnp.where(causal, s, -jnp.inf), axis=-1)
        outs.append(jnp.einsum('bhqk,bkhd->bqhd', p.astype(v.dtype), v[:, :kend]))
    return jnp.concatenate(outs, axis=1).reshape(B, S, H * d)


def routed_expert_ffn(t, expert_id, weight, w_gate, w_up, w_down):
    N, D = t.shape
    A = expert_id.shape[0]
    tok = jnp.arange(A, dtype=jnp.int32) // TOP_K
    order = jnp.argsort(expert_id)
    se, stok, sw = expert_id[order], tok[order], weight[order]
    counts = jnp.bincount(expert_id, length=N_EXPERTS)
    starts = jnp.cumsum(counts) - counts
    padded = (counts + MOE_BLOCK - 1) // MOE_BLOCK * MOE_BLOCK
    pend = jnp.cumsum(padded)
    pstart = pend - padded
    dest = pstart[se] + jnp.arange(A) - starts[se]
    n_blocks = -(-A // MOE_BLOCK) + N_EXPERTS
    P = n_blocks * MOE_BLOCK
    row_tok = jnp.zeros((P,), jnp.int32).at[dest].set(stok)
    row_w = jnp.zeros((P,), t.dtype).at[dest].set(sw.astype(t.dtype))
    block_e = jnp.minimum(jnp.searchsorted(pend, jnp.arange(n_blocks) * MOE_BLOCK, side='right'), N_EXPERTS - 1)
    xb = t[row_tok].reshape(n_blocks, MOE_BLOCK, D)

    def expert_block(args):
        xblk, e = args
        hid = jax.nn.silu(xblk @ w_gate[e]) * (xblk @ w_up[e])
        return hid @ w_down[e]

    yb = lax.map(expert_block, (xb, block_e)).reshape(P, D)
    return jnp.zeros((N, D), t.dtype).at[row_tok].add(yb * row_w[:, None])


def hierarchical_moe(h, w_group, b_group, w_expert, b_expert, w_gate, w_up, w_down):
    B, S, D = h.shape
    N = B * S
    t = h.reshape(N, D)
    g_logits = (t @ w_group).astype(jnp.float32) + b_group.astype(jnp.float32)
    g_prob = jax.nn.softmax(g_logits, axis=-1)
    g_val, g_sel = lax.top_k(g_prob, 1)
    e_logits = ((t @ w_expert).astype(jnp.float32) + b_expert.astype(jnp.float32)).reshape(N, N_GROUPS, EXPERTS_PER_GROUP)
    e_in = jnp.take_along_axis(e_logits, g_sel[:, :, None], axis=1)[:, 0]
    top_val, top_idx = lax.top_k(e_in, TOP_K)
    gate = jax.nn.softmax(top_val, axis=-1) * g_val
    expert_id = (g_sel * EXPERTS_PER_GROUP + top_idx).astype(jnp.int32)
    y = routed_expert_ffn(t, expert_id.reshape(-1), gate.reshape(-1), w_gate, w_up, w_down)
    return y.reshape(B, S, D)


def setup_inputs(seed: int = 0) -> dict:
    key = jax.random.key(seed)
    ks = jax.random.split(key, 20)
    f32 = jnp.float32
    L, D = DEPTH, D_MODEL

    def nrm(k, shape, s):
        return s * jax.random.normal(k, shape, f32)

    return {
        "x": nrm(ks[0], (BATCH, SEQ, D), 1.0),
        "c": nrm(ks[1], (BATCH, D), 1.0),
        "w_ada": nrm(ks[2], (L, D, 6 * D), 0.5 * D ** -0.5),
        "b_ada": nrm(ks[3], (L, 6 * D), 0.01),
        "norm_mix_g": 1.0 + nrm(ks[4], (L, D), 0.05),
        "w_in": nrm(ks[5], (L, D, IN_COLS), D ** -0.5),
        "b_forget": 4.0 + nrm(ks[6], (L, N_HEADS_FOX), 0.5),
        "sinks": nrm(ks[7], (L, N_HEADS_SWA), 1.0),
        "out_norm_swa_g": 1.0 + nrm(ks[8], (L, SWA_WIDTH), 0.05),
        "out_norm_fox_g": 1.0 + nrm(ks[9], (L, FOX_WIDTH), 0.05),
        "w_out": nrm(ks[10], (L, MIX_WIDTH, D), MIX_WIDTH ** -0.5),
        "norm_moe_g": 1.0 + nrm(ks[11], (L, D), 0.05),
        "w_group": nrm(ks[12], (L, D, N_GROUPS), D ** -0.5),
        "b_group": nrm(ks[13], (L, N_GROUPS), 0.01),
        "w_expert": nrm(ks[14], (L, D, N_EXPERTS), D ** -0.5),
        "b_expert": nrm(ks[15], (L, N_EXPERTS), 0.01),
        "w_gate": nrm(ks[16], (L, N_EXPERTS, D, D_EXPERT), D ** -0.5),
        "w_up": nrm(ks[17], (L, N_EXPERTS, D, D_EXPERT), D ** -0.5),
        "w_down": nrm(ks[18], (L, N_EXPERTS, D_EXPERT, D), D_EXPERT ** -0.5),
        "final_g": 1.0 + nrm(ks[19], (D,), 0.05),
    }


def reference(x, c, w_ada, b_ada, norm_mix_g, w_in, b_forget, sinks, out_norm_swa_g, out_norm_fox_g, w_out,
              norm_moe_g, w_group, b_group, w_expert, b_expert, w_gate, w_up, w_down, final_g):
    B, S, D = x.shape
    col_splits = list(np.cumsum([SWA_WIDTH, KV_WIDTH, KV_WIDTH, FOX_WIDTH, FOX_WIDTH, FOX_WIDTH]))
    for l in range(DEPTH):
        mod = jax.nn.silu(c) @ w_ada[l] + b_ada[l]
        sh_a, sc_a, g_a, sh_m, sc_m, g_m = [m[:, None, :] for m in jnp.split(mod, 6, axis=-1)]

        h = rmsnorm(x, norm_mix_g[l]) * (1.0 + sc_a) + sh_a
        proj = h @ w_in[l]
        q_a, k_a, v_a, q_b, k_b, v_b, f_b = jnp.split(proj, col_splits, axis=-1)
        o_a = sliding_window_gqa(q_a.reshape(B, S, N_HEADS_SWA, HEAD_DIM),
                                 k_a.reshape(B, S, N_KV_SWA, HEAD_DIM),
                                 v_a.reshape(B, S, N_KV_SWA, HEAD_DIM), sinks[l])
        log_f = jax.nn.log_sigmoid(f_b.astype(jnp.float32) + b_forget[l].astype(jnp.float32))
        o_b = forgetting_attention(q_b.reshape(B, S, N_HEADS_FOX, HEAD_DIM),
                                   k_b.reshape(B, S, N_HEADS_FOX, HEAD_DIM),
                                   v_b.reshape(B, S, N_HEADS_FOX, HEAD_DIM), log_f)
        mixed = jnp.concatenate([rmsnorm(o_a, out_norm_swa_g[l]), rmsnorm(o_b, out_norm_fox_g[l])], axis=-1)
        x = x + g_a * (mixed @ w_out[l])

        h2 = rmsnorm(x, norm_moe_g[l]) * (1.0 + sc_m) + sh_m
        x = x + g_m * hierarchical_moe(h2, w_group[l], b_group[l], w_expert[l], b_expert[l],
                                       w_gate[l], w_up[l], w_down[l])
    return rmsnorm(x, final_g)
```

```python
import functools
import math

import numpy as np
import jax
import jax.numpy as jnp
from jax import lax
from jax.experimental import pallas as pl
from jax.experimental.pallas import tpu as pltpu

F32 = jnp.float32
BF16 = jnp.bfloat16

D_MODEL = 2048
BATCH = 2
SEQ = 8192
N_TOK = BATCH * SEQ
HEAD_DIM = 64
N_HEADS_SWA = 16
N_KV_SWA = 4
N_HEADS_FOX = 16
WINDOW = 128
SWA_WIDTH = N_HEADS_SWA * HEAD_DIM
KV_WIDTH = N_KV_SWA * HEAD_DIM
FOX_WIDTH = N_HEADS_FOX * HEAD_DIM
MAIN_COLS = SWA_WIDTH + 2 * KV_WIDTH + 3 * FOX_WIDTH
N_GROUPS = 4
EXPERTS_PER_GROUP = 8
N_EXPERTS = N_GROUPS * EXPERTS_PER_GROUP
TOP_K = 2
D_EXPERT = 512
MOE_BLOCK = 128
EPS = 1e-6
LOG2E = math.log2(math.e)

LANES = 128
VMEM_LIMIT = 56 * 1024 * 1024

QA_BLK = 0
KA_COL = SWA_WIDTH
VA_COL = SWA_WIDTH + KV_WIDTH
QB_BLK = (SWA_WIDTH + 2 * KV_WIDTH) // LANES
KB_BLK = QB_BLK + FOX_WIDTH // LANES
VB_COL = SWA_WIDTH + 2 * KV_WIDTH + 2 * FOX_WIDTH


def _cparams(sem, vmem=VMEM_LIMIT):
    return pltpu.CompilerParams(dimension_semantics=sem, vmem_limit_bytes=vmem)


ADA_TN = 1024


def _adaln_kernel(cb_ref, w_ref, b_ref, o_ref):
    for b in range(BATCH):
        s = jax.nn.silu(cb_ref[b])
        cols = []
        for j in range(ADA_TN // LANES):
            prod = w_ref[:, j * LANES:(j + 1) * LANES] * s
            cols.append(jnp.sum(prod, axis=0, keepdims=True))
        o_ref[b:b + 1, :] = jnp.concatenate(cols, axis=1) + b_ref[...]


def _adaln(c, w_ada, b_ada):
    ncol = w_ada.shape[1]
    cb = jnp.broadcast_to(c[:, :, None], (BATCH, D_MODEL, LANES))
    return pl.pallas_call(
        _adaln_kernel,
        grid=(ncol // ADA_TN,),
        in_specs=[
            pl.BlockSpec((BATCH, D_MODEL, LANES), lambda j: (0, 0, 0)),
            pl.BlockSpec((D_MODEL, ADA_TN), lambda j: (0, j)),
            pl.BlockSpec((1, ADA_TN), lambda j: (0, j)),
        ],
        out_specs=pl.BlockSpec((BATCH, ADA_TN), lambda j: (0, j)),
        out_shape=jax.ShapeDtypeStruct((BATCH, ncol), F32),
        compiler_params=_cparams(("arbitrary",)),
        name="adaln",
    )(cb, w_ada, b_ada.reshape(1, ncol))


def _rms(x):
    return x * lax.rsqrt(jnp.mean(x * x, axis=-1, keepdims=True) + EPS)


IN_TM = 512
IN_TN = 1536


def _inproj_kernel(x_ref, mod_ref, g_ref, w_ref, wf_ref, cs_ref, o_ref, f_ref, h_ref):
    @pl.when(pl.program_id(1) == 0)
    def _():
        y = _rms(x_ref[...])
        h = (y * g_ref[...]) * (1.0 + mod_ref[0, 1:2, :]) + mod_ref[0, 0:1, :]
        hb = h.astype(BF16)
        h_ref[...] = hb
        f_ref[...] = jnp.dot(hb, wf_ref[...], preferred_element_type=F32)

    acc = jnp.dot(h_ref[...], w_ref[...], preferred_element_type=F32)
    o_ref[...] = (acc * cs_ref[...]).astype(BF16)


def _inproj(x2, mod3, g, w_main, w_f, colscale):
    tiles_per_batch = SEQ // IN_TM
    return pl.pallas_call(
        _inproj_kernel,
        grid=(N_TOK // IN_TM, MAIN_COLS // IN_TN),
        in_specs=[
            pl.BlockSpec((IN_TM, D_MODEL), lambda i, j: (i, 0)),
            pl.BlockSpec((1, 6, D_MODEL), lambda i, j: (i // tiles_per_batch, 0, 0)),
            pl.BlockSpec((1, D_MODEL), lambda i, j: (0, 0)),
            pl.BlockSpec((D_MODEL, IN_TN), lambda i, j: (0, j)),
            pl.BlockSpec((D_MODEL, LANES), lambda i, j: (0, 0)),
            pl.BlockSpec((1, IN_TN), lambda i, j: (0, j)),
        ],
        out_specs=[
            pl.BlockSpec((IN_TM, IN_TN), lambda i, j: (i, j)),
            pl.BlockSpec((IN_TM, LANES), lambda i, j: (i, 0)),
        ],
        out_shape=[
            jax.ShapeDtypeStruct((N_TOK, MAIN_COLS), BF16),
            jax.ShapeDtypeStruct((N_TOK, LANES), F32),
        ],
        scratch_shapes=[pltpu.VMEM((IN_TM, D_MODEL), BF16)],
        compiler_params=_cparams(("arbitrary", "arbitrary")),
        name="inproj",
    )(x2, mod3, g, w_main, w_f, colscale)


CUM_TS = 512


def _cum_kernel(f_ref, bf_ref, tri_ref, o_ref, carry_ref):
    @pl.when(pl.program_id(1) == 0)
    def _():
        carry_ref[...] = jnp.zeros_like(carry_ref)

    lf = jax.nn.log_sigmoid(f_ref[...] + bf_ref[...])
    cs = jnp.dot(tri_ref[...], lf, preferred_element_type=F32,
                 precision=lax.Precision.HIGHEST) + carry_ref[...]
    o_ref[...] = cs * (-LOG2E)
    carry_ref[...] = cs[CUM_TS - 1:CUM_TS, :]


def _forget_bias(f, b_forget_row):
    tri = jnp.tril(jnp.ones((CUM_TS, CUM_TS), F32))
    nblk = SEQ // CUM_TS
    return pl.pallas_call(
        _cum_kernel,
        grid=(BATCH, nblk),
        in_specs=[
            pl.BlockSpec((CUM_TS, LANES), lambda b, i: (b * nblk + i, 0)),
            pl.BlockSpec((1, LANES), lambda b, i: (0, 0)),
            pl.BlockSpec((CUM_TS, CUM_TS), lambda b, i: (0, 0)),
        ],
        out_specs=pl.BlockSpec((CUM_TS, LANES), lambda b, i: (b * nblk + i, 0)),
        out_shape=jax.ShapeDtypeStruct((N_TOK, LANES), F32),
        scratch_shapes=[pltpu.VMEM((1, LANES), F32)],
        compiler_params=_cparams(("arbitrary", "arbitrary")),
        name="forget_bias",
    )(f, b_forget_row, tri)


SWA_TQ = WINDOW


def _swa_perm():
    perm = np.zeros((SWA_WIDTH,), np.int32)
    for t in range(2):
        for g in range(4):
            for e in range(2):
                head = (2 * t + e) * 4 + g
                base = (t * 4 + g) * LANES + e * HEAD_DIM
                perm[base:base + HEAD_DIM] = head * HEAD_DIM + np.arange(HEAD_DIM)
    return perm


def _swa_kernel(sinks_ref, q_ref, kc_ref, kp_ref, vc_ref, vp_ref, o_ref):
    i = pl.program_id(1)
    tq = SWA_TQ
    row = lax.broadcasted_iota(jnp.int32, (tq, 2 * tq), 0)
    col = lax.broadcasted_iota(jnp.int32, (tq, 2 * tq), 1)
    dist = row + tq - col
    mask = (dist >= 0) & (dist < WINDOW) & ((col >= tq) | (i > 0))
    distf = dist.astype(F32)
    lane = lax.broadcasted_iota(jnp.int32, (tq, LANES), 1)
    lo_half = lane < HEAD_DIM
    for t in range(2):
        kt = jnp.concatenate([kp_ref[:, t * LANES:(t + 1) * LANES],
                              kc_ref[:, t * LANES:(t + 1) * LANES]], axis=0)
        vt = jnp.concatenate([vp_ref[:, t * LANES:(t + 1) * LANES],
                              vc_ref[:, t * LANES:(t + 1) * LANES]], axis=0)
        for g in range(4):
            tile = t * 4 + g
            qt = q_ref[:, tile * LANES:(tile + 1) * LANES]
            outs = []
            for e in range(2):
                head = (2 * t + e) * 4 + g
                slope = float(2.0 ** (-8.0 * (head + 1) / N_HEADS_SWA))
                sink = sinks_ref[head]
                qm = jnp.where(lo_half if e == 0 else ~lo_half, qt, jnp.zeros_like(qt))
                s = lax.dot_general(qm, kt, (((1,), (1,)), ((), ())),
                                    preferred_element_type=F32)
                s = s - slope * distf
                s = jnp.where(mask, s, -jnp.inf)
                m = jnp.maximum(jnp.max(s, axis=-1, keepdims=True), sink)
                p = jnp.exp(s - m)
                p = p / (jnp.sum(p, axis=-1, keepdims=True) + jnp.exp(sink - m))
                outs.append(jnp.dot(p.astype(BF16), vt, preferred_element_type=F32))
            o_ref[:, tile * LANES:(tile + 1) * LANES] = jnp.where(lo_half, outs[0], outs[1])


def _swa(sinks, proj):
    nq = SEQ // SWA_TQ
    kblk = KA_COL // KV_WIDTH
    vblk = VA_COL // KV_WIDTH
    grid_spec = pltpu.PrefetchScalarGridSpec(
        num_scalar_prefetch=1,
        grid=(BATCH, nq),
        in_specs=[
            pl.BlockSpec((SWA_TQ, SWA_WIDTH), lambda b, i, s: (b * nq + i, 0)),
            pl.BlockSpec((SWA_TQ, KV_WIDTH), lambda b, i, s: (b * nq + i, kblk)),
            pl.BlockSpec((SWA_TQ, KV_WIDTH), lambda b, i, s: (b * nq + jnp.maximum(i - 1, 0), kblk)),
            pl.BlockSpec((SWA_TQ, KV_WIDTH), lambda b, i, s: (b * nq + i, vblk)),
            pl.BlockSpec((SWA_TQ, KV_WIDTH), lambda b, i, s: (b * nq + jnp.maximum(i - 1, 0), vblk)),
        ],
        out_specs=pl.BlockSpec((SWA_TQ, SWA_WIDTH), lambda b, i, s: (b * nq + i, 0)),
    )
    return pl.pallas_call(
        _swa_kernel,
        grid_spec=grid_spec,
        out_shape=jax.ShapeDtypeStruct((N_TOK, SWA_WIDTH), F32),
        compiler_params=_cparams(("arbitrary", "arbitrary")),
        name="swa_attn",
    )(sinks, proj, proj, proj, proj, proj)


FOX_T = 256
FOX_CH = 512


def _fox_kernel(q_ref, k_ref, vt_ref, bias_ref, o_ref, biasb_ref):
    hp = pl.program_id(1)
    qi = pl.program_id(2)
    t = FOX_T

    @pl.when(qi == 0)
    def _():
        lane = lax.broadcasted_iota(jnp.int32, (FOX_CH, LANES), 1)
        for hh in range(2):
            head = 2 * hp + hh

            def chunk(c, _):
                r0 = pl.multiple_of(c * FOX_CH, FOX_CH)
                blk = bias_ref[pl.ds(r0, FOX_CH), :]
                colv = jnp.sum(jnp.where(lane == head, blk, 0.0), axis=1, keepdims=True)
                biasb_ref[hh, pl.ds(r0, FOX_CH), :] = jnp.broadcast_to(colv, (FOX_CH, LANES))
                return 0

            lax.fori_loop(0, SEQ // FOX_CH, chunk, 0)

    q = q_ref[...]
    lane_q = lax.broadcasted_iota(jnp.int32, (t, LANES), 1)
    qm = (jnp.where(lane_q < HEAD_DIM, q, jnp.zeros_like(q)),
          jnp.where(lane_q >= HEAD_DIM, q, jnp.zeros_like(q)))

    def step(j, carry, masked):
        r0 = pl.multiple_of(j * t, t)
        kb = k_ref[pl.ds(r0, t), :]
        new = []
        for hh in range(2):
            m, l, acc = carry[hh]
            st = lax.dot_general(kb, qm[hh], (((1,), (1,)), ((), ())),
                                 preferred_element_type=F32)
            bb = biasb_ref[hh, pl.ds(r0, t), :]
            st = st + jnp.concatenate([bb, bb], axis=1)
            if masked:
                kr = lax.broadcasted_iota(jnp.int32, (t, t), 0)
                qc = lax.broadcasted_iota(jnp.int32, (t, t), 1)
                st = jnp.where(kr <= qc, st, -jnp.inf)
            m_new = jnp.maximum(m, jnp.max(st, axis=0, keepdims=True))
            alpha = jnp.exp2(m - m_new)
            p = jnp.exp2(st - m_new)
            l_new = alpha * l + jnp.sum(p, axis=0, keepdims=True)
            vth = vt_ref[0, 0, j, hh * HEAD_DIM:(hh + 1) * HEAD_DIM, :]
            pv = jnp.dot(vth, p.astype(BF16), preferred_element_type=F32)
            new.append((m_new, l_new, alpha * acc + pv))
        return tuple(new)

    init = tuple((jnp.full((1, t), -jnp.inf, F32), jnp.zeros((1, t), F32),
                  jnp.zeros((HEAD_DIM, t), F32)) for _ in range(2))
    carry = lax.fori_loop(0, qi, lambda j, c: step(j, c, False), init)
    carry = step(qi, carry, True)
    ot = jnp.concatenate([carry[0][2] / carry[0][1], carry[1][2] / carry[1][1]], axis=0)
    o_ref[...] = ot.T


def _fox(proj, vt5, bias):
    nq = SEQ // FOX_T
    npair = N_HEADS_FOX // 2
    return pl.pallas_call(
        _fox_kernel,
        grid=(BATCH, npair, nq),
        in_specs=[
            pl.BlockSpec((FOX_T, LANES), lambda b, h, i: (b * nq + i, QB_BLK + h)),
            pl.BlockSpec((SEQ, LANES), lambda b, h, i: (b, KB_BLK + h)),
            pl.BlockSpec((1, 1, nq, LANES, FOX_T), lambda b, h, i: (b, h, 0, 0, 0)),
            pl.BlockSpec((SEQ, LANES), lambda b, h, i: (b, 0)),
        ],
        out_specs=pl.BlockSpec((FOX_T, LANES), lambda b, h, i: (b * nq + i, h)),
        out_shape=jax.ShapeDtypeStruct((N_TOK, FOX_WIDTH), F32),
        scratch_shapes=[pltpu.VMEM((2, SEQ, LANES), F32)],
        compiler_params=_cparams(("arbitrary", "arbitrary", "arbitrary")),
        name="fox_attn",
    )(proj, proj, vt5, bias)


OUT_TM = 256


def _outproj_kernel(oa_ref, ob_ref, x_ref, mod_ref, ga_ref, gb_ref, w_ref, gm_ref, wr_ref, br_ref,
                    x1_ref, h2_ref, ids_ref, gates_ref):
    na = _rms(oa_ref[...]) * ga_ref[...]
    nb = _rms(ob_ref[...]) * gb_ref[...]
    mixed = jnp.concatenate([na, nb], axis=1).astype(BF16)
    y = jnp.dot(mixed, w_ref[...], preferred_element_type=F32)
    x1 = x_ref[...] + mod_ref[0, 2:3, :] * y
    x1_ref[...] = x1
    h2 = (_rms(x1) * gm_ref[...]) * (1.0 + mod_ref[0, 4:5, :]) + mod_ref[0, 3:4, :]
    h2_ref[...] = h2

    logits = jnp.dot(h2, wr_ref[...], preferred_element_type=F32,
                     precision=lax.Precision.HIGHEST) + br_ref[...]
    lane = lax.broadcasted_iota(jnp.int32, logits.shape, 1)
    neg = -jnp.inf
    is_g = lane < N_GROUPS
    gl = jnp.where(is_g, logits, neg)
    gmax = jnp.max(gl, axis=1, keepdims=True)
    gsel = jnp.min(jnp.where(gl == gmax, lane, LANES), axis=1, keepdims=True)
    gsum = jnp.sum(jnp.where(is_g, jnp.exp(gl - gmax), 0.0), axis=1, keepdims=True)
    g_val = 1.0 / gsum
    elane = lane - N_GROUPS
    in_sel = (elane >= 0) & (elane < N_EXPERTS) & ((elane >> 3) == gsel)
    ev = jnp.where(in_sel, logits, neg)
    t1 = jnp.max(ev, axis=1, keepdims=True)
    i1 = jnp.min(jnp.where(ev == t1, lane, LANES), axis=1, keepdims=True)
    ev2 = jnp.where(lane == i1, neg, ev)
    t2 = jnp.max(ev2, axis=1, keepdims=True)
    i2 = jnp.min(jnp.where(ev2 == t2, lane, LANES), axis=1, keepdims=True)
    e2 = jnp.exp(t2 - t1)
    den = 1.0 + e2
    w1 = (1.0 / den) * g_val
    w2 = (e2 / den) * g_val
    ids_ref[...] = jnp.where(lane == 0, i1 - N_GROUPS, jnp.where(lane == 1, i2 - N_GROUPS, 0))
    gates_ref[...] = jnp.where(lane == 0, w1, jnp.where(lane == 1, w2, 0.0))


def _outproj(oa, ob, x2, mod3, ga, gb, w_out, gm, wr, br):
    tiles_per_batch = SEQ // OUT_TM
    row = lambda i: (i, 0)
    const = lambda i: (0, 0)
    return pl.pallas_call(
        _outproj_kernel,
        grid=(N_TOK // OUT_TM,),
        in_specs=[
            pl.BlockSpec((OUT_TM, SWA_WIDTH), row),
            pl.BlockSpec((OUT_TM, FOX_WIDTH), row),
            pl.BlockSpec((OUT_TM, D_MODEL), row),
            pl.BlockSpec((1, 6, D_MODEL), lambda i: (i // tiles_per_batch, 0, 0)),
            pl.BlockSpec((1, SWA_WIDTH), const),
            pl.BlockSpec((1, FOX_WIDTH), const),
            pl.BlockSpec((D_MODEL, D_MODEL), const),
            pl.BlockSpec((1, D_MODEL), const),
            pl.BlockSpec((D_MODEL, LANES), const),
            pl.BlockSpec((1, LANES), const),
        ],
        out_specs=[
            pl.BlockSpec((OUT_TM, D_MODEL), row),
            pl.BlockSpec((OUT_TM, D_MODEL), row),
            pl.BlockSpec((OUT_TM, LANES), row),
            pl.BlockSpec((OUT_TM, LANES), row),
        ],
        out_shape=[
            jax.ShapeDtypeStruct((N_TOK, D_MODEL), F32),
            jax.ShapeDtypeStruct((N_TOK, D_MODEL), F32),
            jax.ShapeDtypeStruct((N_TOK, LANES), jnp.int32),
            jax.ShapeDtypeStruct((N_TOK, LANES), F32),
        ],
        compiler_params=_cparams(("arbitrary",)),
        name="outproj_router",
    )(oa, ob, x2, mod3, ga, gb, w_out, gm, wr, br)


N_BLOCKS = (N_TOK * TOP_K) // MOE_BLOCK + N_EXPERTS
P_ROWS = N_BLOCKS * MOE_BLOCK


def _expert_kernel(be_ref, tok_ref, nused_ref, h2_hbm, wg_ref, wu_ref, wd_ref, y_ref, xbuf, sem):
    i = pl.program_id(0)
    nused = nused_ref[0]

    def row_copy(blk, r, slot):
        tok = tok_ref[blk * MOE_BLOCK + r]
        return pltpu.make_async_copy(h2_hbm.at[pl.ds(tok, 1), :],
                                     xbuf.at[slot, pl.ds(r, 1), :], sem.at[slot])

    def issue(blk, slot):
        def body(r, _):
            row_copy(blk, r, slot).start()
            return 0
        lax.fori_loop(0, MOE_BLOCK, body, 0)

    @pl.when(i == 0)
    def _():
        issue(0, 0)

    @pl.when(i + 1 < nused)
    def _():
        issue(i + 1, (i + 1) % 2)

    @pl.when(i < nused)
    def _():
        slot = i % 2
        pltpu.make_async_copy(xbuf.at[slot], xbuf.at[slot], sem.at[slot]).wait()
        xb = xbuf[slot].astype(BF16)
        g = jnp.dot(xb, wg_ref[0], preferred_element_type=F32)
        u = jnp.dot(xb, wu_ref[0], preferred_element_type=F32)
        hid = (jax.nn.silu(g) * u).astype(BF16)
        y_ref[...] = jnp.dot(hid, wd_ref[0], preferred_element_type=F32)

    @pl.when(i >= nused)
    def _():
        y_ref[...] = jnp.zeros_like(y_ref)


def _experts(block_e, row_tok, nused, h2, wg, wu, wd):
    grid_spec = pltpu.PrefetchScalarGridSpec(
        num_scalar_prefetch=3,
        grid=(N_BLOCKS,),
        in_specs=[
            pl.BlockSpec(memory_space=pl.ANY),
            pl.BlockSpec((1, D_MODEL, D_EXPERT), lambda i, be, tk, nu: (be[i], 0, 0)),
            pl.BlockSpec((1, D_MODEL, D_EXPERT), lambda i, be, tk, nu: (be[i], 0, 0)),
            pl.BlockSpec((1, D_EXPERT, D_MODEL), lambda i, be, tk, nu: (be[i], 0, 0)),
        ],
        out_specs=pl.BlockSpec((MOE_BLOCK, D_MODEL), lambda i, be, tk, nu: (i, 0)),
        scratch_shapes=[pltpu.VMEM((2, MOE_BLOCK, D_MODEL), F32), pltpu.SemaphoreType.DMA((2,))],
    )
    return pl.pallas_call(
        _expert_kernel,
        grid_spec=grid_spec,
        out_shape=jax.ShapeDtypeStruct((P_ROWS, D_MODEL), F32),
        compiler_params=_cparams(("arbitrary",)),
        name="expert_ffn",
    )(block_e, row_tok, nused, h2, wg, wu, wd)


CMB_TM = 128


def _combine_kernel(dest_ref, ys_hbm, x1_ref, gates_ref, mod_ref, fg_ref, o_ref, ybuf, sem):
    i = pl.program_id(0)
    n = pl.num_programs(0)

    def issue(tile, slot):
        def body(r, _):
            for k in range(TOP_K):
                d = dest_ref[(tile * CMB_TM + r) * TOP_K + k]
                pltpu.make_async_copy(ys_hbm.at[pl.ds(d, 1), :],
                                      ybuf.at[slot, pl.ds(k * CMB_TM + r, 1), :], sem.at[slot]).start()
            return 0
        lax.fori_loop(0, CMB_TM, body, 0)

    @pl.when(i == 0)
    def _():
        issue(0, 0)

    @pl.when(i + 1 < n)
    def _():
        issue(i + 1, (i + 1) % 2)

    slot = i % 2
    pltpu.make_async_copy(ybuf.at[slot], ybuf.at[slot], sem.at[slot]).wait()
    gts = gates_ref[...]
    y = ybuf[slot, 0:CMB_TM, :] * gts[:, 0:1] + ybuf[slot, CMB_TM:2 * CMB_TM, :] * gts[:, 1:2]
    x2 = x1_ref[...] + mod_ref[0, 5:6, :] * y
    o_ref[...] = _rms(x2) * fg_ref[...]


def _combine(dest, ys, x1, gates, mod3, final_g):
    tiles_per_batch = SEQ // CMB_TM
    grid_spec = pltpu.PrefetchScalarGridSpec(
        num_scalar_prefetch=1,
        grid=(N_TOK // CMB_TM,),
        in_specs=[
            pl.BlockSpec(memory_space=pl.ANY),
            pl.BlockSpec((CMB_TM, D_MODEL), lambda i, d: (i, 0)),
            pl.BlockSpec((CMB_TM, LANES), lambda i, d: (i, 0)),
            pl.BlockSpec((1, 6, D_MODEL), lambda i, d: (i // tiles_per_batch, 0, 0)),
            pl.BlockSpec((1, D_MODEL), lambda i, d: (0, 0)),
        ],
        out_specs=pl.BlockSpec((CMB_TM, D_MODEL), lambda i, d: (i, 0)),
        scratch_shapes=[pltpu.VMEM((2, TOP_K * CMB_TM, D_MODEL), F32), pltpu.SemaphoreType.DMA((2,))],
    )
    return pl.pallas_call(
        _combine_kernel,
        grid_spec=grid_spec,
        out_shape=jax.ShapeDtypeStruct((N_TOK, D_MODEL), F32),
        compiler_params=_cparams(("arbitrary",)),
        name="combine_final",
    )(dest, ys, x1, gates, mod3, final_g)


def _routing_tables(eid):
    a = eid.shape[0]
    onehot = (eid[:, None] == jnp.arange(N_EXPERTS, dtype=jnp.int32)[None, :]).astype(jnp.int32)
    csum = jnp.cumsum(onehot, axis=0)
    rank = jnp.sum(onehot * csum, axis=1) - 1
    counts = csum[-1]
    padded = (counts + MOE_BLOCK - 1) // MOE_BLOCK * MOE_BLOCK
    pend = jnp.cumsum(padded)
    pstart = pend - padded
    dest = (pstart[eid] + rank).astype(jnp.int32)
    tok = jnp.arange(a, dtype=jnp.int32) // TOP_K
    row_tok = jnp.zeros((P_ROWS,), jnp.int32).at[dest].set(tok)
    block_e = jnp.minimum(jnp.searchsorted(pend, jnp.arange(N_BLOCKS, dtype=jnp.int32) * MOE_BLOCK, side='right'),
                          N_EXPERTS - 1).astype(jnp.int32)
    nused = (pend[-1] // MOE_BLOCK).astype(jnp.int32).reshape(1)
    return dest, row_tok, block_e, nused


def kernel(x, c, w_ada, b_ada, norm_mix_g, w_in, b_forget, sinks, out_norm_swa_g, out_norm_fox_g, w_out,
           norm_moe_g, w_group, b_group, w_expert, b_expert, w_gate, w_up, w_down, final_g):
    x2 = x.reshape(N_TOK, D_MODEL)
    perm = _swa_perm()

    w_in0 = w_in[0]
    w_main = jnp.concatenate([w_in0[:, :SWA_WIDTH][:, perm], w_in0[:, SWA_WIDTH:MAIN_COLS]], axis=1).astype(BF16)
    w_f = jnp.pad(w_in0[:, MAIN_COLS:], ((0, 0), (0, LANES - N_HEADS_FOX))).astype(BF16)
    colscale = jnp.concatenate([
        jnp.full((SWA_WIDTH,), HEAD_DIM ** -0.5, F32),
        jnp.ones((2 * KV_WIDTH,), F32),
        jnp.full((FOX_WIDTH,), HEAD_DIM ** -0.5 * LOG2E, F32),
        jnp.ones((2 * FOX_WIDTH,), F32)]).reshape(1, MAIN_COLS)
    bf_row = jnp.pad(b_forget[0], (0, LANES - N_HEADS_FOX)).reshape(1, LANES)
    w_out_p = jnp.concatenate([w_out[0][:SWA_WIDTH][perm], w_out[0][SWA_WIDTH:]], axis=0).astype(BF16)
    ga = out_norm_swa_g[0][perm].reshape(1, SWA_WIDTH)
    gb = out_norm_fox_g[0].reshape(1, FOX_WIDTH)
    wr = jnp.pad(jnp.concatenate([w_group[0], w_expert[0]], axis=1),
                 ((0, 0), (0, LANES - N_GROUPS - N_EXPERTS)))
    br = jnp.pad(jnp.concatenate([b_group[0], b_expert[0]]), (0, LANES - N_GROUPS - N_EXPERTS)).reshape(1, LANES)
    wg = w_gate[0].astype(BF16)
    wu = w_up[0].astype(BF16)
    wd = w_down[0].astype(BF16)

    mod = _adaln(c, w_ada[0], b_ada[0])
    mod3 = mod.reshape(BATCH, 6, D_MODEL)

    proj, f = _inproj(x2, mod3, norm_mix_g[0].reshape(1, D_MODEL), w_main, w_f, colscale)
    bias = _forget_bias(f, bf_row)

    o_a = _swa(sinks[0], proj)
    nk = SEQ // FOX_T
    vt5 = proj[:, VB_COL:].reshape(BATCH, nk, FOX_T, N_HEADS_FOX // 2, LANES).transpose(0, 3, 1, 4, 2)
    o_b = _fox(proj, vt5, bias)

    x1, h2, ids, gates = _outproj(o_a, o_b, x2, mod3, ga, gb, w_out_p,
                                  norm_moe_g[0].reshape(1, D_MODEL), wr, br)

    eid = ids[:, :TOP_K].reshape(-1)
    dest, row_tok, block_e, nused = _routing_tables(eid)
    ys = _experts(block_e, row_tok, nused, h2, wg, wu, wd)
    out = _combine(dest, ys, x1, gates, mod3, final_g.reshape(1, D_MODEL))
    return out.reshape(BATCH, SEQ, D_MODEL)
```

```python
import functools
import math

import numpy as np
import jax
import jax.numpy as jnp
from jax import lax
from jax.experimental import pallas as pl
from jax.experimental.pallas import tpu as pltpu

F32 = jnp.float32
BF16 = jnp.bfloat16

D_MODEL = 2048
BATCH = 2
SEQ = 8192
N_TOK = BATCH * SEQ
HEAD_DIM = 64
N_HEADS_SWA = 16
N_KV_SWA = 4
N_HEADS_FOX = 16
WINDOW = 128
SWA_WIDTH = N_HEADS_SWA * HEAD_DIM
KV_WIDTH = N_KV_SWA * HEAD_DIM
FOX_WIDTH = N_HEADS_FOX * HEAD_DIM
MAIN_COLS = SWA_WIDTH + 2 * KV_WIDTH + 3 * FOX_WIDTH
N_GROUPS = 4
EXPERTS_PER_GROUP = 8
N_EXPERTS = N_GROUPS * EXPERTS_PER_GROUP
TOP_K = 2
D_EXPERT = 512
MOE_BLOCK = 128
EPS = 1e-6
LOG2E = math.log2(math.e)

LANES = 128
VMEM_LIMIT = 56 * 1024 * 1024

QA_BLK = 0
KA_COL = SWA_WIDTH
VA_COL = SWA_WIDTH + KV_WIDTH
QB_BLK = (SWA_WIDTH + 2 * KV_WIDTH) // LANES
KB_BLK = QB_BLK + FOX_WIDTH // LANES
VB_COL = SWA_WIDTH + 2 * KV_WIDTH + 2 * FOX_WIDTH


def _cparams(sem, vmem=VMEM_LIMIT):
    return pltpu.CompilerParams(dimension_semantics=sem, vmem_limit_bytes=vmem)


ADA_TN = 1024


def _adaln_kernel(cb_ref, w_ref, b_ref, o_ref):
    for b in range(BATCH):
        s = jax.nn.silu(cb_ref[b])
        cols = []
        for j in range(ADA_TN // LANES):
            prod = w_ref[:, j * LANES:(j + 1) * LANES] * s
            cols.append(jnp.sum(prod, axis=0, keepdims=True))
        o_ref[b:b + 1, :] = jnp.concatenate(cols, axis=1) + b_ref[...]


def _adaln(c, w_ada, b_ada):
    ncol = w_ada.shape[1]
    cb = jnp.broadcast_to(c[:, :, None], (BATCH, D_MODEL, LANES))
    return pl.pallas_call(
        _adaln_kernel,
        grid=(ncol // ADA_TN,),
        in_specs=[
            pl.BlockSpec((BATCH, D_MODEL, LANES), lambda j: (0, 0, 0)),
            pl.BlockSpec((D_MODEL, ADA_TN), lambda j: (0, j)),
            pl.BlockSpec((1, ADA_TN), lambda j: (0, j)),
        ],
        out_specs=pl.BlockSpec((BATCH, ADA_TN), lambda j: (0, j)),
        out_shape=jax.ShapeDtypeStruct((BATCH, ncol), F32),
        compiler_params=_cparams(("arbitrary",)),
        name="adaln",
    )(cb, w_ada, b_ada.reshape(1, ncol))


def _rms(x):
    return x * lax.rsqrt(jnp.mean(x * x, axis=-1, keepdims=True) + EPS)


IN_TM = 512
IN_TN = 1536


def _inproj_kernel(x_ref, mod_ref, g_ref, w_ref, wf_ref, cs_ref, o_ref, f_ref, h_ref):
    @pl.when(pl.program_id(1) == 0)
    def _():
        y = _rms(x_ref[...])
        h = (y * g_ref[...]) * (1.0 + mod_ref[0, 1:2, :]) + mod_ref[0, 0:1, :]
        hb = h.astype(BF16)
        h_ref[...] = hb
        f_ref[...] = jnp.dot(hb, wf_ref[...], preferred_element_type=F32)

    acc = jnp.dot(h_ref[...], w_ref[...], preferred_element_type=F32)
    o_ref[...] = (acc * cs_ref[...]).astype(BF16)


def _inproj(x2, mod3, g, w_main, w_f, colscale):
    tiles_per_batch = SEQ // IN_TM
    return pl.pallas_call(
        _inproj_kernel,
        grid=(N_TOK // IN_TM, MAIN_COLS // IN_TN),
        in_specs=[
            pl.BlockSpec((IN_TM, D_MODEL), lambda i, j: (i, 0)),
            pl.BlockSpec((1, 6, D_MODEL), lambda i, j: (i // tiles_per_batch, 0, 0)),
            pl.BlockSpec((1, D_MODEL), lambda i, j: (0, 0)),
            pl.BlockSpec((D_MODEL, IN_TN), lambda i, j: (0, j)),
            pl.BlockSpec((D_MODEL, LANES), lambda i, j: (0, 0)),
            pl.BlockSpec((1, IN_TN), lambda i, j: (0, j)),
        ],
        out_specs=[
            pl.BlockSpec((IN_TM, IN_TN), lambda i, j: (i, j)),
            pl.BlockSpec((IN_TM, LANES), lambda i, j: (i, 0)),
        ],
        out_shape=[
            jax.ShapeDtypeStruct((N_TOK, MAIN_COLS), BF16),
            jax.ShapeDtypeStruct((N_TOK, LANES), F32),
        ],
        scratch_shapes=[pltpu.VMEM((IN_TM, D_MODEL), BF16)],
        compiler_params=_cparams(("arbitrary", "arbitrary")),
        name="inproj",
    )(x2, mod3, g, w_main, w_f, colscale)


CUM_TS = 512


def _cum_kernel(f_ref, bf_ref, tri_ref, o_ref, carry_ref):
    @pl.when(pl.program_id(1) == 0)
    def _():
        carry_ref[...] = jnp.zeros_like(carry_ref)

    lf = jax.nn.log_sigmoid(f_ref[...] + bf_ref[...])
    cs = jnp.dot(tri_ref[...], lf, preferred_element_type=F32,
                 precision=lax.Precision.HIGHEST) + carry_ref[...]
    o_ref[...] = cs * (-LOG2E)
    carry_ref[...] = cs[CUM_TS - 1:CUM_TS, :]


def _forget_bias(f, b_forget_row):
    tri = jnp.tril(jnp.ones((CUM_TS, CUM_TS), F32))
    nblk = SEQ // CUM_TS
    return pl.pallas_call(
        _cum_kernel,
        grid=(BATCH, nblk),
        in_specs=[
            pl.BlockSpec((CUM_TS, LANES), lambda b, i: (b * nblk + i, 0)),
            pl.BlockSpec((1, LANES), lambda b, i: (0, 0)),
            pl.BlockSpec((CUM_TS, CUM_TS), lambda b, i: (0, 0)),
        ],
        out_specs=pl.BlockSpec((CUM_TS, LANES), lambda b, i: (b * nblk + i, 0)),
        out_shape=jax.ShapeDtypeStruct((N_TOK, LANES), F32),
        scratch_shapes=[pltpu.VMEM((1, LANES), F32)],
        compiler_params=_cparams(("arbitrary", "arbitrary")),
        name="forget_bias",
    )(f, b_forget_row, tri)


SWA_TQ = WINDOW


def _swa_perm():
    perm = np.zeros((SWA_WIDTH,), np.int32)
    for t in range(2):
        for g in range(4):
            for e in range(2):
                head = (2 * t + e) * 4 + g
                base = (t * 4 + g) * LANES + e * HEAD_DIM
                perm[base:base + HEAD_DIM] = head * HEAD_DIM + np.arange(HEAD_DIM)
    return perm


def _swa_kernel(sinks_ref, q_ref, kc_ref, kp_ref, vc_ref, vp_ref, o_ref):
    i = pl.program_id(1)
    tq = SWA_TQ
    row = lax.broadcasted_iota(jnp.int32, (tq, 2 * tq), 0)
    col = lax.broadcasted_iota(jnp.int32, (tq, 2 * tq), 1)
    dist = row + tq - col
    mask = (dist >= 0) & (dist < WINDOW) & ((col >= tq) | (i > 0))
    distf = dist.astype(F32)
    lane = lax.broadcasted_iota(jnp.int32, (tq, LANES), 1)
    lo_half = lane < HEAD_DIM
    for t in range(2):
        kt = jnp.concatenate([kp_ref[:, t * LANES:(t + 1) * LANES],
                              kc_ref[:, t * LANES:(t + 1) * LANES]], axis=0)
        vt = jnp.concatenate([vp_ref[:, t * LANES:(t + 1) * LANES],
                              vc_ref[:, t * LANES:(t + 1) * LANES]], axis=0)
        for g in range(4):
            tile = t * 4 + g
            qt = q_ref[:, tile * LANES:(tile + 1) * LANES]
            outs = []
            for e in range(2):
                head = (2 * t + e) * 4 + g
                slope = float(2.0 ** (-8.0 * (head + 1) / N_HEADS_SWA))
                sink = sinks_ref[head]
                qm = jnp.where(lo_half if e == 0 else ~lo_half, qt, jnp.zeros_like(qt))
                s = lax.dot_general(qm, kt, (((1,), (1,)), ((), ())),
                                    preferred_element_type=F32)
                s = s - slope * distf
                s = jnp.where(mask, s, -jnp.inf)
                m = jnp.maximum(jnp.max(s, axis=-1, keepdims=True), sink)
                p = jnp.exp(s - m)
                p = p / (jnp.sum(p, axis=-1, keepdims=True) + jnp.exp(sink - m))
                outs.append(jnp.dot(p.astype(BF16), vt, preferred_element_type=F32))
            o_ref[:, tile * LANES:(tile + 1) * LANES] = jnp.where(lo_half, outs[0], outs[1])


def _swa(sinks, proj):
    nq = SEQ // SWA_TQ
    kblk = KA_COL // KV_WIDTH
    vblk = VA_COL // KV_WIDTH
    grid_spec = pltpu.PrefetchScalarGridSpec(
        num_scalar_prefetch=1,
        grid=(BATCH, nq),
        in_specs=[
            pl.BlockSpec((SWA_TQ, SWA_WIDTH), lambda b, i, s: (b * nq + i, 0)),
            pl.BlockSpec((SWA_TQ, KV_WIDTH), lambda b, i, s: (b * nq + i, kblk)),
            pl.BlockSpec((SWA_TQ, KV_WIDTH), lambda b, i, s: (b * nq + jnp.maximum(i - 1, 0), kblk)),
            pl.BlockSpec((SWA_TQ, KV_WIDTH), lambda b, i, s: (b * nq + i, vblk)),
            pl.BlockSpec((SWA_TQ, KV_WIDTH), lambda b, i, s: (b * nq + jnp.maximum(i - 1, 0), vblk)),
        ],
        out_specs=pl.BlockSpec((SWA_TQ, SWA_WIDTH), lambda b, i, s: (b * nq + i, 0)),
    )
    return pl.pallas_call(
        _swa_kernel,
        grid_spec=grid_spec,
        out_shape=jax.ShapeDtypeStruct((N_TOK, SWA_WIDTH), F32),
        compiler_params=_cparams(("arbitrary", "arbitrary")),
        name="swa_attn",
    )(sinks, proj, proj, proj, proj, proj)


FOX_T = 512
FOX_CH = 512


def _fox_kernel(q_ref, k_ref, vt_ref, bias_ref, o_ref, biasb_ref, mask_ref, t_ref, acc_ref):
    hp = pl.program_id(1)
    qi = pl.program_id(2)
    t = FOX_T

    @pl.when(qi == 0)
    def _():
        lane = lax.broadcasted_iota(jnp.int32, (FOX_CH, LANES), 1)
        for hh in range(2):
            head = 2 * hp + hh

            def chunk(c, _):
                r0 = pl.multiple_of(c * FOX_CH, FOX_CH)
                blk = bias_ref[pl.ds(r0, FOX_CH), :]
                colv = jnp.sum(jnp.where(lane == head, blk, 0.0), axis=1, keepdims=True)
                biasb_ref[hh, pl.ds(r0, FOX_CH), :] = jnp.broadcast_to(colv, (FOX_CH, LANES))
                return 0

            lax.fori_loop(0, SEQ // FOX_CH, chunk, 0)
        kr = lax.broadcasted_iota(jnp.int32, (t, t), 0)
        qc = lax.broadcasted_iota(jnp.int32, (t, t), 1)
        mask_ref[...] = jnp.where(kr <= qc, 0.0, -jnp.inf)

    qf = q_ref[...].astype(F32).T
    drow = lax.broadcasted_iota(jnp.int32, (LANES, t), 0)
    qmt = (jnp.where(drow < HEAD_DIM, qf, 0.0).astype(BF16),
           jnp.where(drow >= HEAD_DIM, qf, 0.0).astype(BF16))
    acc_ref[...] = jnp.zeros_like(acc_ref)

    def stage_a(j, slot, diag):
        r0 = pl.multiple_of(j * t, t)
        kb = k_ref[pl.ds(r0, t), :]
        sts = [jnp.dot(kb, qmt[hh], preferred_element_type=F32) for hh in range(2)]
        mbs = []
        for hh in range(2):
            bb = biasb_ref[hh, pl.ds(r0, t), :]
            tt = sts[hh] + jnp.concatenate([bb] * (t // LANES), axis=1)
            if diag:
                tt = tt + mask_ref[...]
            t_ref[slot, hh] = tt
            mbs.append(jnp.max(tt, axis=0, keepdims=True))
        return tuple(mbs)

    def stage_b(jv, slot, ml, mbs):
        new, ps, alphas = [], [], []
        for hh in range(2):
            m, l = ml[hh]
            m_new = jnp.maximum(m, mbs[hh])
            alpha = jnp.exp2(m - m_new)
            p = jnp.exp2(t_ref[slot, hh] - m_new)
            new.append((m_new, alpha * l + jnp.sum(p, axis=0, keepdims=True)))
            ps.append(p.astype(BF16))
            alphas.append(alpha)
        pvs = [jnp.dot(vt_ref[0, 0, jv, hh * HEAD_DIM:(hh + 1) * HEAD_DIM, :], ps[hh],
                       preferred_element_type=F32) for hh in range(2)]
        for hh in range(2):
            acc_ref[hh] = alphas[hh] * acc_ref[hh] + pvs[hh]
        return tuple(new)

    ml0 = tuple((jnp.full((1, t), -jnp.inf, F32), jnp.zeros((1, t), F32)) for _ in range(2))
    mb0 = stage_a(qi, 0, True)

    def pair(ii, c):
        ml, mbs, jprev = c
        j0 = 2 * ii
        mb1 = stage_a(j0, 1, False)
        ml = stage_b(jprev, 0, ml, mbs)
        mb2 = stage_a(j0 + 1, 0, False)
        ml = stage_b(j0, 1, ml, mb1)
        return ml, mb2, j0 + 1

    ml, mbs, jprev = lax.fori_loop(0, qi // 2, pair, (ml0, mb0, qi))

    def odd_tail(c):
        ml, mbs, jprev = c
        mb1 = stage_a(qi - 1, 1, False)
        ml = stage_b(jprev, 0, ml, mbs)
        return stage_b(qi - 1, 1, ml, mb1)

    def even_tail(c):
        ml, mbs, jprev = c
        return stage_b(jprev, 0, ml, mbs)

    ml = lax.cond(qi % 2 == 1, odd_tail, even_tail, (ml, mbs, jprev))
    ot = jnp.concatenate([acc_ref[0] / ml[0][1], acc_ref[1] / ml[1][1]], axis=0)
    o_ref[...] = ot.T


def _fox(proj, vt5, bias):
    nq = SEQ // FOX_T
    npair = N_HEADS_FOX // 2
    return pl.pallas_call(
        _fox_kernel,
        grid=(BATCH, npair, nq),
        in_specs=[
            pl.BlockSpec((FOX_T, LANES), lambda b, h, i: (b * nq + i, QB_BLK + h)),
            pl.BlockSpec((SEQ, LANES), lambda b, h, i: (b, KB_BLK + h)),
            pl.BlockSpec((1, 1, nq, LANES, FOX_T), lambda b, h, i: (b, h, 0, 0, 0)),
            pl.BlockSpec((SEQ, LANES), lambda b, h, i: (b, 0)),
        ],
        out_specs=pl.BlockSpec((FOX_T, LANES), lambda b, h, i: (b * nq + i, h)),
        out_shape=jax.ShapeDtypeStruct((N_TOK, FOX_WIDTH), F32),
        scratch_shapes=[
            pltpu.VMEM((2, SEQ, LANES), F32),
            pltpu.VMEM((FOX_T, FOX_T), F32),
            pltpu.VMEM((2, 2, FOX_T, FOX_T), F32),
            pltpu.VMEM((2, HEAD_DIM, FOX_T), F32),
        ],
        compiler_params=_cparams(("arbitrary", "arbitrary", "arbitrary")),
        name="fox_attn",
    )(proj, proj, vt5, bias)


OUT_TM = 256


def _outproj_kernel(oa_ref, ob_ref, x_ref, mod_ref, ga_ref, gb_ref, w_ref, gm_ref, wr_ref, br_ref,
                    x1_ref, h2_ref, ids_ref, gates_ref):
    na = _rms(oa_ref[...]) * ga_ref[...]
    nb = _rms(ob_ref[...]) * gb_ref[...]
    mixed = jnp.concatenate([na, nb], axis=1).astype(BF16)
    y = jnp.dot(mixed, w_ref[...], preferred_element_type=F32)
    x1 = x_ref[...] + mod_ref[0, 2:3, :] * y
    x1_ref[...] = x1
    h2 = (_rms(x1) * gm_ref[...]) * (1.0 + mod_ref[0, 4:5, :]) + mod_ref[0, 3:4, :]
    h2_ref[...] = h2

    logits = jnp.dot(h2, wr_ref[...], preferred_element_type=F32,
                     precision=lax.Precision.HIGHEST) + br_ref[...]
    lane = lax.broadcasted_iota(jnp.int32, logits.shape, 1)
    neg = -jnp.inf
    is_g = lane < N_GROUPS
    gl = jnp.where(is_g, logits, neg)
    gmax = jnp.max(gl, axis=1, keepdims=True)
    gsel = jnp.min(jnp.where(gl == gmax, lane, LANES), axis=1, keepdims=True)
    gsum = jnp.sum(jnp.where(is_g, jnp.exp(gl - gmax), 0.0), axis=1, keepdims=True)
    g_val = 1.0 / gsum
    elane = lane - N_GROUPS
    in_sel = (elane >= 0) & (elane < N_EXPERTS) & ((elane >> 3) == gsel)
    ev = jnp.where(in_sel, logits, neg)
    t1 = jnp.max(ev, axis=1, keepdims=True)
    i1 = jnp.min(jnp.where(ev == t1, lane, LANES), axis=1, keepdims=True)
    ev2 = jnp.where(lane == i1, neg, ev)
    t2 = jnp.max(ev2, axis=1, keepdims=True)
    i2 = jnp.min(jnp.where(ev2 == t2, lane, LANES), axis=1, keepdims=True)
    e2 = jnp.exp(t2 - t1)
    den = 1.0 + e2
    w1 = (1.0 / den) * g_val
    w2 = (e2 / den) * g_val
    ids_ref[...] = jnp.where(lane == 0, i1 - N_GROUPS, jnp.where(lane == 1, i2 - N_GROUPS, 0))
    gates_ref[...] = jnp.where(lane == 0, w1, jnp.where(lane == 1, w2, 0.0))


def _outproj(oa, ob, x2, mod3, ga, gb, w_out, gm, wr, br):
    tiles_per_batch = SEQ // OUT_TM
    row = lambda i: (i, 0)
    const = lambda i: (0, 0)
    return pl.pallas_call(
        _outproj_kernel,
        grid=(N_TOK // OUT_TM,),
        in_specs=[
            pl.BlockSpec((OUT_TM, SWA_WIDTH), row),
            pl.BlockSpec((OUT_TM, FOX_WIDTH), row),
            pl.BlockSpec((OUT_TM, D_MODEL), row),
            pl.BlockSpec((1, 6, D_MODEL), lambda i: (i // tiles_per_batch, 0, 0)),
            pl.BlockSpec((1, SWA_WIDTH), const),
            pl.BlockSpec((1, FOX_WIDTH), const),
            pl.BlockSpec((D_MODEL, D_MODEL), const),
            pl.BlockSpec((1, D_MODEL), const),
            pl.BlockSpec((D_MODEL, LANES), const),
            pl.BlockSpec((1, LANES), const),
        ],
        out_specs=[
            pl.BlockSpec((OUT_TM, D_MODEL), row),
            pl.BlockSpec((OUT_TM, D_MODEL), row),
            pl.BlockSpec((OUT_TM, LANES), row),
            pl.BlockSpec((OUT_TM, LANES), row),
        ],
        out_shape=[
            jax.ShapeDtypeStruct((N_TOK, D_MODEL), F32),
            jax.ShapeDtypeStruct((N_TOK, D_MODEL), F32),
            jax.ShapeDtypeStruct((N_TOK, LANES), jnp.int32),
            jax.ShapeDtypeStruct((N_TOK, LANES), F32),
        ],
        compiler_params=_cparams(("arbitrary",)),
        name="outproj_router",
    )(oa, ob, x2, mod3, ga, gb, w_out, gm, wr, br)


N_BLOCKS = (N_TOK * TOP_K) // MOE_BLOCK + N_EXPERTS
P_ROWS = N_BLOCKS * MOE_BLOCK


def _expert_kernel(be_ref, tok_ref, nused_ref, h2_hbm, wg_ref, wu_ref, wd_ref, y_ref, xbuf, sem):
    i = pl.program_id(0)
    nused = nused_ref[0]

    def row_copy(blk, r, slot):
        tok = tok_ref[blk * MOE_BLOCK + r]
        return pltpu.make_async_copy(h2_hbm.at[pl.ds(tok, 1), :],
                                     xbuf.at[slot, pl.ds(r, 1), :], sem.at[slot])

    def issue(blk, slot):
        def body(r, _):
            row_copy(blk, r, slot).start()
            return 0
        lax.fori_loop(0, MOE_BLOCK, body, 0)

    @pl.when(i == 0)
    def _():
        issue(0, 0)

    @pl.when(i + 1 < nused)
    def _():
        issue(i + 1, (i + 1) % 2)

    @pl.when(i < nused)
    def _():
        slot = i % 2
        pltpu.make_async_copy(xbuf.at[slot], xbuf.at[slot], sem.at[slot]).wait()
        xb = xbuf[slot].astype(BF16)
        g = jnp.dot(xb, wg_ref[0], preferred_element_type=F32)
        u = jnp.dot(xb, wu_ref[0], preferred_element_type=F32)
        hid = (jax.nn.silu(g) * u).astype(BF16)
        y_ref[...] = jnp.dot(hid, wd_ref[0], preferred_element_type=F32)

    @pl.when(i >= nused)
    def _():
        y_ref[...] = jnp.zeros_like(y_ref)


def _experts(block_e, row_tok, nused, h2, wg, wu, wd):
    grid_spec = pltpu.PrefetchScalarGridSpec(
        num_scalar_prefetch=3,
        grid=(N_BLOCKS,),
        in_specs=[
            pl.BlockSpec(memory_space=pl.ANY),
            pl.BlockSpec((1, D_MODEL, D_EXPERT), lambda i, be, tk, nu: (be[i], 0, 0)),
            pl.BlockSpec((1, D_MODEL, D_EXPERT), lambda i, be, tk, nu: (be[i], 0, 0)),
            pl.BlockSpec((1, D_EXPERT, D_MODEL), lambda i, be, tk, nu: (be[i], 0, 0)),
        ],
        out_specs=pl.BlockSpec((MOE_BLOCK, D_MODEL), lambda i, be, tk, nu: (i, 0)),
        scratch_shapes=[pltpu.VMEM((2, MOE_BLOCK, D_MODEL), F32), pltpu.SemaphoreType.DMA((2,))],
    )
    return pl.pallas_call(
        _expert_kernel,
        grid_spec=grid_spec,
        out_shape=jax.ShapeDtypeStruct((P_ROWS, D_MODEL), F32),
        compiler_params=_cparams(("arbitrary",)),
        name="expert_ffn",
    )(block_e, row_tok, nused, h2, wg, wu, wd)


CMB_TM = 128


def _combine_kernel(dest_ref, ys_hbm, x1_ref, gates_ref, mod_ref, fg_ref, o_ref, ybuf, sem):
    i = pl.program_id(0)
    n = pl.num_programs(0)

    def issue(tile, slot):
        def body(r, _):
            for k in range(TOP_K):
                d = dest_ref[(tile * CMB_TM + r) * TOP_K + k]
                pltpu.make_async_copy(ys_hbm.at[pl.ds(d, 1), :],
                                      ybuf.at[slot, pl.ds(k * CMB_TM + r, 1), :], sem.at[slot]).start()
            return 0
        lax.fori_loop(0, CMB_TM, body, 0)

    @pl.when(i == 0)
    def _():
        issue(0, 0)

    @pl.when(i + 1 < n)
    def _():
        issue(i + 1, (i + 1) % 2)

    slot = i % 2
    pltpu.make_async_copy(ybuf.at[slot], ybuf.at[slot], sem.at[slot]).wait()
    gts = gates_ref[...]
    y = ybuf[slot, 0:CMB_TM, :] * gts[:, 0:1] + ybuf[slot, CMB_TM:2 * CMB_TM, :] * gts[:, 1:2]
    x2 = x1_ref[...] + mod_ref[0, 5:6, :] * y
    o_ref[...] = _rms(x2) * fg_ref[...]


def _combine(dest, ys, x1, gates, mod3, final_g):
    tiles_per_batch = SEQ // CMB_TM
    grid_spec = pltpu.PrefetchScalarGridSpec(
        num_scalar_prefetch=1,
        grid=(N_TOK // CMB_TM,),
        in_specs=[
            pl.BlockSpec(memory_space=pl.ANY),
            pl.BlockSpec((CMB_TM, D_MODEL), lambda i, d: (i, 0)),
            pl.BlockSpec((CMB_TM, LANES), lambda i, d: (i, 0)),
            pl.BlockSpec((1, 6, D_MODEL), lambda i, d: (i // tiles_per_batch, 0, 0)),
            pl.BlockSpec((1, D_MODEL), lambda i, d: (0, 0)),
        ],
        out_specs=pl.BlockSpec((CMB_TM, D_MODEL), lambda i, d: (i, 0)),
        scratch_shapes=[pltpu.VMEM((2, TOP_K * CMB_TM, D_MODEL), F32), pltpu.SemaphoreType.DMA((2,))],
    )
    return pl.pallas_call(
        _combine_kernel,
        grid_spec=grid_spec,
        out_shape=jax.ShapeDtypeStruct((N_TOK, D_MODEL), F32),
        compiler_params=_cparams(("arbitrary",)),
        name="combine_final",
    )(dest, ys, x1, gates, mod3, final_g)


def _routing_tables(eid):
    a = eid.shape[0]
    onehot = (eid[:, None] == jnp.arange(N_EXPERTS, dtype=jnp.int32)[None, :]).astype(jnp.int32)
    csum = jnp.cumsum(onehot, axis=0)
    rank = jnp.sum(onehot * csum, axis=1) - 1
    counts = csum[-1]
    padded = (counts + MOE_BLOCK - 1) // MOE_BLOCK * MOE_BLOCK
    pend = jnp.cumsum(padded)
    pstart = pend - padded
    dest = (pstart[eid] + rank).astype(jnp.int32)
    tok = jnp.arange(a, dtype=jnp.int32) // TOP_K
    row_tok = jnp.zeros((P_ROWS,), jnp.int32).at[dest].set(tok)
    blk_row = jnp.arange(N_BLOCKS, dtype=jnp.int32) * MOE_BLOCK
    block_e = jnp.minimum(jnp.sum((pend[None, :] <= blk_row[:, None]).astype(jnp.int32), axis=1),
                          N_EXPERTS - 1).astype(jnp.int32)
    nused = (pend[-1] // MOE_BLOCK).astype(jnp.int32).reshape(1)
    return dest, row_tok, block_e, nused


def kernel(x, c, w_ada, b_ada, norm_mix_g, w_in, b_forget, sinks, out_norm_swa_g, out_norm_fox_g, w_out,
           norm_moe_g, w_group, b_group, w_expert, b_expert, w_gate, w_up, w_down, final_g):
    x2 = x.reshape(N_TOK, D_MODEL)
    perm = _swa_perm()

    w_in0 = w_in[0]
    w_main = jnp.concatenate([w_in0[:, :SWA_WIDTH][:, perm], w_in0[:, SWA_WIDTH:MAIN_COLS]], axis=1).astype(BF16)
    w_f = jnp.pad(w_in0[:, MAIN_COLS:], ((0, 0), (0, LANES - N_HEADS_FOX))).astype(BF16)
    colscale = jnp.concatenate([
        jnp.full((SWA_WIDTH,), HEAD_DIM ** -0.5, F32),
        jnp.ones((2 * KV_WIDTH,), F32),
        jnp.full((FOX_WIDTH,), HEAD_DIM ** -0.5 * LOG2E, F32),
        jnp.ones((2 * FOX_WIDTH,), F32)]).reshape(1, MAIN_COLS)
    bf_row = jnp.pad(b_forget[0], (0, LANES - N_HEADS_FOX)).reshape(1, LANES)
    w_out_p = jnp.concatenate([w_out[0][:SWA_WIDTH][perm], w_out[0][SWA_WIDTH:]], axis=0).astype(BF16)
    ga = out_norm_swa_g[0][perm].reshape(1, SWA_WIDTH)
    gb = out_norm_fox_g[0].reshape(1, FOX_WIDTH)
    wr = jnp.pad(jnp.concatenate([w_group[0], w_expert[0]], axis=1),
                 ((0, 0), (0, LANES - N_GROUPS - N_EXPERTS)))
    br = jnp.pad(jnp.concatenate([b_group[0], b_expert[0]]), (0, LANES - N_GROUPS - N_EXPERTS)).reshape(1, LANES)
    wg = w_gate[0].astype(BF16)
    wu = w_up[0].astype(BF16)
    wd = w_down[0].astype(BF16)

    mod = _adaln(c, w_ada[0], b_ada[0])
    mod3 = mod.reshape(BATCH, 6, D_MODEL)

    proj, f = _inproj(x2, mod3, norm_mix_g[0].reshape(1, D_MODEL), w_main, w_f, colscale)
    bias = _forget_bias(f, bf_row)

    o_a = _swa(sinks[0], proj)
    nk = SEQ // FOX_T
    vt5 = proj[:, VB_COL:].reshape(BATCH, nk, FOX_T, N_HEADS_FOX // 2, LANES).transpose(0, 3, 1, 4, 2)
    o_b = _fox(proj, vt5, bias)

    x1, h2, ids, gates = _outproj(o_a, o_b, x2, mod3, ga, gb, w_out_p,
                                  norm_moe_g[0].reshape(1, D_MODEL), wr, br)

    eid = ids[:, :TOP_K].reshape(-1)
    dest, row_tok, block_e, nused = _routing_tables(eid)
    ys = _experts(block_e, row_tok, nused, h2, wg, wu, wd)
    out = _combine(dest, ys, x1, gates, mod3, final_g.reshape(1, D_MODEL))
    return out.reshape(BATCH, SEQ, D_MODEL)
```

```python
import functools
import math

import numpy as np
import jax
import jax.numpy as jnp
from jax import lax
from jax.experimental import pallas as pl
from jax.experimental.pallas import tpu as pltpu

F32 = jnp.float32
BF16 = jnp.bfloat16

D_MODEL = 2048
BATCH = 2
SEQ = 8192
N_TOK = BATCH * SEQ
HEAD_DIM = 64
N_HEADS_SWA = 16
N_KV_SWA = 4
N_HEADS_FOX = 16
WINDOW = 128
SWA_WIDTH = N_HEADS_SWA * HEAD_DIM
KV_WIDTH = N_KV_SWA * HEAD_DIM
FOX_WIDTH = N_HEADS_FOX * HEAD_DIM
MAIN_COLS = SWA_WIDTH + 2 * KV_WIDTH + 3 * FOX_WIDTH
N_GROUPS = 4
EXPERTS_PER_GROUP = 8
N_EXPERTS = N_GROUPS * EXPERTS_PER_GROUP
TOP_K = 2
D_EXPERT = 512
MOE_BLOCK = 128
EPS = 1e-6
LOG2E = math.log2(math.e)

LANES = 128
VMEM_LIMIT = 56 * 1024 * 1024

QA_BLK = 0
KA_COL = SWA_WIDTH
VA_COL = SWA_WIDTH + KV_WIDTH
QB_BLK = (SWA_WIDTH + 2 * KV_WIDTH) // LANES
KB_BLK = QB_BLK + FOX_WIDTH // LANES
VB_COL = SWA_WIDTH + 2 * KV_WIDTH + 2 * FOX_WIDTH


def _cparams(sem, vmem=VMEM_LIMIT):
    return pltpu.CompilerParams(dimension_semantics=sem, vmem_limit_bytes=vmem)


ADA_TN = 1024


def _adaln_kernel(cb_ref, w_ref, b_ref, o_ref):
    for b in range(BATCH):
        s = jax.nn.silu(cb_ref[b])
        cols = []
        for j in range(ADA_TN // LANES):
            prod = w_ref[:, j * LANES:(j + 1) * LANES] * s
            cols.append(jnp.sum(prod, axis=0, keepdims=True))
        o_ref[b:b + 1, :] = jnp.concatenate(cols, axis=1) + b_ref[...]


def _adaln(c, w_ada, b_ada):
    ncol = w_ada.shape[1]
    cb = jnp.broadcast_to(c[:, :, None], (BATCH, D_MODEL, LANES))
    return pl.pallas_call(
        _adaln_kernel,
        grid=(ncol // ADA_TN,),
        in_specs=[
            pl.BlockSpec((BATCH, D_MODEL, LANES), lambda j: (0, 0, 0)),
            pl.BlockSpec((D_MODEL, ADA_TN), lambda j: (0, j)),
            pl.BlockSpec((1, ADA_TN), lambda j: (0, j)),
        ],
        out_specs=pl.BlockSpec((BATCH, ADA_TN), lambda j: (0, j)),
        out_shape=jax.ShapeDtypeStruct((BATCH, ncol), F32),
        compiler_params=_cparams(("arbitrary",)),
        name="adaln",
    )(cb, w_ada, b_ada.reshape(1, ncol))


def _rms(x):
    return x * lax.rsqrt(jnp.mean(x * x, axis=-1, keepdims=True) + EPS)


IN_TM = 512
IN_TN = 1536


def _inproj_kernel(x_ref, mod_ref, g_ref, w_ref, wf_ref, cs_ref, o_ref, f_ref, h_ref):
    @pl.when(pl.program_id(1) == 0)
    def _():
        y = _rms(x_ref[...])
        h = (y * g_ref[...]) * (1.0 + mod_ref[0, 1:2, :]) + mod_ref[0, 0:1, :]
        hb = h.astype(BF16)
        h_ref[...] = hb
        f_ref[...] = jnp.dot(hb, wf_ref[...], preferred_element_type=F32)

    acc = jnp.dot(h_ref[...], w_ref[...], preferred_element_type=F32)
    o_ref[...] = (acc * cs_ref[...]).astype(BF16)


def _inproj(x2, mod3, g, w_main, w_f, colscale):
    tiles_per_batch = SEQ // IN_TM
    return pl.pallas_call(
        _inproj_kernel,
        grid=(N_TOK // IN_TM, MAIN_COLS // IN_TN),
        in_specs=[
            pl.BlockSpec((IN_TM, D_MODEL), lambda i, j: (i, 0)),
            pl.BlockSpec((1, 6, D_MODEL), lambda i, j: (i // tiles_per_batch, 0, 0)),
            pl.BlockSpec((1, D_MODEL), lambda i, j: (0, 0)),
            pl.BlockSpec((D_MODEL, IN_TN), lambda i, j: (0, j)),
            pl.BlockSpec((D_MODEL, LANES), lambda i, j: (0, 0)),
            pl.BlockSpec((1, IN_TN), lambda i, j: (0, j)),
        ],
        out_specs=[
            pl.BlockSpec((IN_TM, IN_TN), lambda i, j: (i, j)),
            pl.BlockSpec((IN_TM, LANES), lambda i, j: (i, 0)),
        ],
        out_shape=[
            jax.ShapeDtypeStruct((N_TOK, MAIN_COLS), BF16),
            jax.ShapeDtypeStruct((N_TOK, LANES), F32),
        ],
        scratch_shapes=[pltpu.VMEM((IN_TM, D_MODEL), BF16)],
        compiler_params=_cparams(("arbitrary", "arbitrary")),
        name="inproj",
    )(x2, mod3, g, w_main, w_f, colscale)


CUM_TS = 512


def _cum_kernel(f_ref, bf_ref, tri_ref, o_ref, carry_ref):
    @pl.when(pl.program_id(1) == 0)
    def _():
        carry_ref[...] = jnp.zeros_like(carry_ref)

    lf = jax.nn.log_sigmoid(f_ref[...] + bf_ref[...])
    cs = jnp.dot(tri_ref[...], lf, preferred_element_type=F32,
                 precision=lax.Precision.HIGHEST) + carry_ref[...]
    o_ref[...] = cs * (-LOG2E)
    carry_ref[...] = cs[CUM_TS - 1:CUM_TS, :]


def _forget_bias(f, b_forget_row):
    tri = jnp.tril(jnp.ones((CUM_TS, CUM_TS), F32))
    nblk = SEQ // CUM_TS
    return pl.pallas_call(
        _cum_kernel,
        grid=(BATCH, nblk),
        in_specs=[
            pl.BlockSpec((CUM_TS, LANES), lambda b, i: (b * nblk + i, 0)),
            pl.BlockSpec((1, LANES), lambda b, i: (0, 0)),
            pl.BlockSpec((CUM_TS, CUM_TS), lambda b, i: (0, 0)),
        ],
        out_specs=pl.BlockSpec((CUM_TS, LANES), lambda b, i: (b * nblk + i, 0)),
        out_shape=jax.ShapeDtypeStruct((N_TOK, LANES), F32),
        scratch_shapes=[pltpu.VMEM((1, LANES), F32)],
        compiler_params=_cparams(("arbitrary", "arbitrary")),
        name="forget_bias",
    )(f, b_forget_row, tri)


SWA_TQ = WINDOW


def _swa_perm():
    perm = np.zeros((SWA_WIDTH,), np.int32)
    for t in range(2):
        for g in range(4):
            for e in range(2):
                head = (2 * t + e) * 4 + g
                base = (t * 4 + g) * LANES + e * HEAD_DIM
                perm[base:base + HEAD_DIM] = head * HEAD_DIM + np.arange(HEAD_DIM)
    return perm


def _swa_kernel(sinks_ref, q_ref, kc_ref, kp_ref, vc_ref, vp_ref, o_ref):
    i = pl.program_id(1)
    tq = SWA_TQ
    row = lax.broadcasted_iota(jnp.int32, (tq, 2 * tq), 0)
    col = lax.broadcasted_iota(jnp.int32, (tq, 2 * tq), 1)
    dist = row + tq - col
    mask = (dist >= 0) & (dist < WINDOW) & ((col >= tq) | (i > 0))
    distf = dist.astype(F32)
    lane = lax.broadcasted_iota(jnp.int32, (tq, LANES), 1)
    lo_half = lane < HEAD_DIM
    for t in range(2):
        kt = jnp.concatenate([kp_ref[:, t * LANES:(t + 1) * LANES],
                              kc_ref[:, t * LANES:(t + 1) * LANES]], axis=0)
        vt = jnp.concatenate([vp_ref[:, t * LANES:(t + 1) * LANES],
                              vc_ref[:, t * LANES:(t + 1) * LANES]], axis=0)
        for g in range(4):
            tile = t * 4 + g
            qt = q_ref[:, tile * LANES:(tile + 1) * LANES]
            outs = []
            for e in range(2):
                head = (2 * t + e) * 4 + g
                slope = float(2.0 ** (-8.0 * (head + 1) / N_HEADS_SWA))
                sink = sinks_ref[head]
                qm = jnp.where(lo_half if e == 0 else ~lo_half, qt, jnp.zeros_like(qt))
                s = lax.dot_general(qm, kt, (((1,), (1,)), ((), ())),
                                    preferred_element_type=F32)
                s = s - slope * distf
                s = jnp.where(mask, s, -jnp.inf)
                m = jnp.maximum(jnp.max(s, axis=-1, keepdims=True), sink)
                p = jnp.exp(s - m)
                p = p / (jnp.sum(p, axis=-1, keepdims=True) + jnp.exp(sink - m))
                outs.append(jnp.dot(p.astype(BF16), vt, preferred_element_type=F32))
            o_ref[:, tile * LANES:(tile + 1) * LANES] = jnp.where(lo_half, outs[0], outs[1])


def _swa(sinks, proj):
    nq = SEQ // SWA_TQ
    kblk = KA_COL // KV_WIDTH
    vblk = VA_COL // KV_WIDTH
    grid_spec = pltpu.PrefetchScalarGridSpec(
        num_scalar_prefetch=1,
        grid=(BATCH, nq),
        in_specs=[
            pl.BlockSpec((SWA_TQ, SWA_WIDTH), lambda b, i, s: (b * nq + i, 0)),
            pl.BlockSpec((SWA_TQ, KV_WIDTH), lambda b, i, s: (b * nq + i, kblk)),
            pl.BlockSpec((SWA_TQ, KV_WIDTH), lambda b, i, s: (b * nq + jnp.maximum(i - 1, 0), kblk)),
            pl.BlockSpec((SWA_TQ, KV_WIDTH), lambda b, i, s: (b * nq + i, vblk)),
            pl.BlockSpec((SWA_TQ, KV_WIDTH), lambda b, i, s: (b * nq + jnp.maximum(i - 1, 0), vblk)),
        ],
        out_specs=pl.BlockSpec((SWA_TQ, SWA_WIDTH), lambda b, i, s: (b * nq + i, 0)),
    )
    return pl.pallas_call(
        _swa_kernel,
        grid_spec=grid_spec,
        out_shape=jax.ShapeDtypeStruct((N_TOK, SWA_WIDTH), F32),
        compiler_params=_cparams(("arbitrary", "arbitrary")),
        name="swa_attn",
    )(sinks, proj, proj, proj, proj, proj)


FOX_T = 512
FOX_CH = 512
FOX_AUG = 3
FOX_ONES = 16
FOX_VROWS = HEAD_DIM + FOX_ONES


def _fox_kernel(q_ref, k_ref, vt_ref, bias_ref, o_ref, kaug_ref, mask_ref, t_ref, acc_ref):
    hp = pl.program_id(1)
    qi = pl.program_id(2)
    t = FOX_T

    @pl.when(qi == 0)
    def _():
        lane = lax.broadcasted_iota(jnp.int32, (FOX_CH, LANES), 1)

        def chunk(c, _):
            r0 = pl.multiple_of(c * FOX_CH, FOX_CH)
            blk = bias_ref[pl.ds(r0, FOX_CH), :]
            aug = jnp.zeros((FOX_CH, LANES), F32)
            for hh in range(2):
                colv = jnp.sum(jnp.where(lane == 2 * hp + hh, blk, 0.0), axis=1, keepdims=True)
                hi = colv.astype(BF16).astype(F32)
                r1 = colv - hi
                mid = r1.astype(BF16).astype(F32)
                lo = r1 - mid
                for piece, val in enumerate((hi, mid, lo)):
                    aug = jnp.where(lane == FOX_AUG * hh + piece, val, aug)
            kaug_ref[pl.ds(r0, FOX_CH), 0:LANES] = k_ref[pl.ds(r0, FOX_CH), :]
            kaug_ref[pl.ds(r0, FOX_CH), LANES:2 * LANES] = aug.astype(BF16)
            return 0

        lax.fori_loop(0, SEQ // FOX_CH, chunk, 0)
        kr = lax.broadcasted_iota(jnp.int32, (t, t), 0)
        qc = lax.broadcasted_iota(jnp.int32, (t, t), 1)
        mask_ref[...] = jnp.where(kr <= qc, 0.0, -jnp.inf)

    qf = q_ref[...].astype(F32).T
    drow = lax.broadcasted_iota(jnp.int32, (LANES, t), 0)
    qaug = []
    for hh in range(2):
        qh = jnp.where((drow >= HEAD_DIM * hh) & (drow < HEAD_DIM * (hh + 1)), qf, 0.0)
        ones = jnp.where((drow >= FOX_AUG * hh) & (drow < FOX_AUG * (hh + 1)), 1.0, 0.0)
        qaug.append(jnp.concatenate([qh, ones], axis=0).astype(BF16))
    acc_ref[...] = jnp.zeros_like(acc_ref)

    def stage_a(j, slot, diag):
        r0 = pl.multiple_of(j * t, t)
        kb = kaug_ref[pl.ds(r0, t), :]
        sts = [jnp.dot(kb, qaug[hh], preferred_element_type=F32) for hh in range(2)]
        mbs = []
        for hh in range(2):
            tt = sts[hh]
            if diag:
                tt = tt + mask_ref[...]
            t_ref[slot, hh] = tt
            mbs.append(jnp.max(tt, axis=0, keepdims=True))
        return tuple(mbs)

    def stage_b(jv, slot, ms, mbs):
        new, ps, alphas = [], [], []
        for hh in range(2):
            m_new = jnp.maximum(ms[hh], mbs[hh])
            alphas.append(jnp.exp2(ms[hh] - m_new))
            ps.append(jnp.exp2(t_ref[slot, hh] - m_new).astype(BF16))
            new.append(m_new)
        pvs = [jnp.dot(vt_ref[0, 0, jv, hh], ps[hh], preferred_element_type=F32)
               for hh in range(2)]
        for hh in range(2):
            acc_ref[hh] = alphas[hh] * acc_ref[hh] + pvs[hh]
        return tuple(new)

    ml0 = tuple(jnp.full((1, t), -jnp.inf, F32) for _ in range(2))
    mb0 = stage_a(qi, 0, True)

    def pair(ii, c):
        ml, mbs, jprev = c
        j0 = 2 * ii
        mb1 = stage_a(j0, 1, False)
        ml = stage_b(jprev, 0, ml, mbs)
        mb2 = stage_a(j0 + 1, 0, False)
        ml = stage_b(j0, 1, ml, mb1)
        return ml, mb2, j0 + 1

    ml, mbs, jprev = lax.fori_loop(0, qi // 2, pair, (ml0, mb0, qi))

    def odd_tail(c):
        ml, mbs, jprev = c
        mb1 = stage_a(qi - 1, 1, False)
        ml = stage_b(jprev, 0, ml, mbs)
        return stage_b(qi - 1, 1, ml, mb1)

    def even_tail(c):
        ml, mbs, jprev = c
        return stage_b(jprev, 0, ml, mbs)

    ml = lax.cond(qi % 2 == 1, odd_tail, even_tail, (ml, mbs, jprev))
    del ml
    ot = jnp.concatenate([acc_ref[hh, 0:HEAD_DIM, :] / acc_ref[hh, HEAD_DIM:HEAD_DIM + 1, :]
                          for hh in range(2)], axis=0)
    o_ref[...] = ot.T


def _fox(proj, vt5, bias):
    nq = SEQ // FOX_T
    npair = N_HEADS_FOX // 2
    return pl.pallas_call(
        _fox_kernel,
        grid=(BATCH, npair, nq),
        in_specs=[
            pl.BlockSpec((FOX_T, LANES), lambda b, h, i: (b * nq + i, QB_BLK + h)),
            pl.BlockSpec((SEQ, LANES), lambda b, h, i: (b, KB_BLK + h)),
            pl.BlockSpec((1, 1, nq, 2, FOX_VROWS, FOX_T), lambda b, h, i: (b, h, 0, 0, 0, 0)),
            pl.BlockSpec((SEQ, LANES), lambda b, h, i: (b, 0)),
        ],
        out_specs=pl.BlockSpec((FOX_T, LANES), lambda b, h, i: (b * nq + i, h)),
        out_shape=jax.ShapeDtypeStruct((N_TOK, FOX_WIDTH), F32),
        scratch_shapes=[
            pltpu.VMEM((SEQ, 2 * LANES), BF16),
            pltpu.VMEM((FOX_T, FOX_T), F32),
            pltpu.VMEM((2, 2, FOX_T, FOX_T), F32),
            pltpu.VMEM((2, FOX_VROWS, FOX_T), F32),
        ],
        compiler_params=_cparams(("arbitrary", "arbitrary", "arbitrary")),
        name="fox_attn",
    )(proj, proj, vt5, bias)


OUT_TM = 256
PACK_ROWS = D_MODEL // 2 // LANES
Y_ROWS = 8
Y_COLS = D_MODEL // Y_ROWS
DMA_UNROLL = 8


def _outproj_kernel(oa_ref, ob_ref, x_ref, mod_ref, ga_ref, gb_ref, w_ref, gm_ref, wr2_ref, br_ref,
                    x1_ref, h2p_ref, ids_ref, gates_ref):
    na = _rms(oa_ref[...]) * ga_ref[...]
    nb = _rms(ob_ref[...]) * gb_ref[...]
    mixed = jnp.concatenate([na, nb], axis=1).astype(BF16)
    y = jnp.dot(mixed, w_ref[...], preferred_element_type=F32)
    x1 = x_ref[...] + mod_ref[0, 2:3, :] * y
    x1_ref[...] = x1
    h2 = (_rms(x1) * gm_ref[...]) * (1.0 + mod_ref[0, 4:5, :]) + mod_ref[0, 3:4, :]
    hb = h2.astype(BF16)
    hbf = hb.astype(F32)
    half = D_MODEL // 2
    lo_bits = lax.shift_right_logical(pltpu.bitcast(hbf[:, :half], jnp.uint32), jnp.uint32(16))
    hi_bits = pltpu.bitcast(hbf[:, half:], jnp.uint32) & jnp.uint32(0xFFFF0000)
    packed = lo_bits | hi_bits
    for s in range(PACK_ROWS):
        h2p_ref[:, s, :] = packed[:, s * LANES:(s + 1) * LANES]

    hl = (h2 - hbf).astype(BF16)
    r2 = jnp.dot(hb, wr2_ref[...], preferred_element_type=F32)
    logits = (r2[:, :LANES] + (r2[:, LANES:] + jnp.dot(hl, wr2_ref[:, :LANES], preferred_element_type=F32))
              + br_ref[...])
    lane = lax.broadcasted_iota(jnp.int32, logits.shape, 1)
    neg = -jnp.inf
    is_g = lane < N_GROUPS
    gl = jnp.where(is_g, logits, neg)
    gmax = jnp.max(gl, axis=1, keepdims=True)
    gsel = jnp.min(jnp.where(gl == gmax, lane, LANES), axis=1, keepdims=True)
    gsum = jnp.sum(jnp.where(is_g, jnp.exp(gl - gmax), 0.0), axis=1, keepdims=True)
    g_val = 1.0 / gsum
    elane = lane - N_GROUPS
    in_sel = (elane >= 0) & (elane < N_EXPERTS) & ((elane >> 3) == gsel)
    ev = jnp.where(in_sel, logits, neg)
    t1 = jnp.max(ev, axis=1, keepdims=True)
    i1 = jnp.min(jnp.where(ev == t1, lane, LANES), axis=1, keepdims=True)
    ev2 = jnp.where(lane == i1, neg, ev)
    t2 = jnp.max(ev2, axis=1, keepdims=True)
    i2 = jnp.min(jnp.where(ev2 == t2, lane, LANES), axis=1, keepdims=True)
    e2 = jnp.exp(t2 - t1)
    den = 1.0 + e2
    w1 = (1.0 / den) * g_val
    w2 = (e2 / den) * g_val
    ids_ref[...] = jnp.where(lane == 0, i1 - N_GROUPS, jnp.where(lane == 1, i2 - N_GROUPS, 0))
    gates_ref[...] = jnp.where(lane == 0, w1, jnp.where(lane == 1, w2, 0.0))


def _outproj(oa, ob, x2, mod3, ga, gb, w_out, gm, wr, br):
    tiles_per_batch = SEQ // OUT_TM
    row = lambda i: (i, 0)
    const = lambda i: (0, 0)
    wr_hi = wr.astype(BF16)
    wr_lo = (wr - wr_hi.astype(F32)).astype(BF16)
    wr2 = jnp.concatenate([wr_hi, wr_lo], axis=1)
    return pl.pallas_call(
        _outproj_kernel,
        grid=(N_TOK // OUT_TM,),
        in_specs=[
            pl.BlockSpec((OUT_TM, SWA_WIDTH), row),
            pl.BlockSpec((OUT_TM, FOX_WIDTH), row),
            pl.BlockSpec((OUT_TM, D_MODEL), row),
            pl.BlockSpec((1, 6, D_MODEL), lambda i: (i // tiles_per_batch, 0, 0)),
            pl.BlockSpec((1, SWA_WIDTH), const),
            pl.BlockSpec((1, FOX_WIDTH), const),
            pl.BlockSpec((D_MODEL, D_MODEL), const),
            pl.BlockSpec((1, D_MODEL), const),
            pl.BlockSpec((D_MODEL, 2 * LANES), const),
            pl.BlockSpec((1, LANES), const),
        ],
        out_specs=[
            pl.BlockSpec((OUT_TM, D_MODEL), row),
            pl.BlockSpec((OUT_TM, PACK_ROWS, LANES), lambda i: (i, 0, 0)),
            pl.BlockSpec((OUT_TM, LANES), row),
            pl.BlockSpec((OUT_TM, LANES), row),
        ],
        out_shape=[
            jax.ShapeDtypeStruct((N_TOK, D_MODEL), F32),
            jax.ShapeDtypeStruct((N_TOK, PACK_ROWS, LANES), jnp.uint32),
            jax.ShapeDtypeStruct((N_TOK, LANES), jnp.int32),
            jax.ShapeDtypeStruct((N_TOK, LANES), F32),
        ],
        compiler_params=_cparams(("arbitrary",)),
        name="outproj_router",
    )(oa, ob, x2, mod3, ga, gb, w_out, gm, wr2, br)


N_BLOCKS = (N_TOK * TOP_K) // MOE_BLOCK + N_EXPERTS
P_ROWS = N_BLOCKS * MOE_BLOCK


def _expert_kernel(be_ref, tok_ref, nused_ref, h2_hbm, wg_ref, wu_ref, wd_ref, y_ref, xbuf, sem):
    i = pl.program_id(0)
    nused = nused_ref[0]

    def issue(blk, slot):
        def body(r, _):
            tok = tok_ref[blk * MOE_BLOCK + r]
            pltpu.make_async_copy(h2_hbm.at[tok], xbuf.at[slot, r], sem.at[slot]).start()
            return 0
        lax.fori_loop(0, MOE_BLOCK, body, 0, unroll=DMA_UNROLL)

    @pl.when(i == 0)
    def _():
        issue(0, 0)

    @pl.when(i + 1 < nused)
    def _():
        issue(i + 1, (i + 1) % 2)

    @pl.when(i < nused)
    def _():
        slot = i % 2
        pltpu.make_async_copy(xbuf.at[slot], xbuf.at[slot], sem.at[slot]).wait()
        xu = jnp.concatenate([xbuf[slot, :, s, :] for s in range(PACK_ROWS)], axis=1)
        lo = pltpu.bitcast(lax.shift_left(xu, jnp.uint32(16)), F32)
        hi = pltpu.bitcast(xu & jnp.uint32(0xFFFF0000), F32)
        xb = jnp.concatenate([lo, hi], axis=1).astype(BF16)
        g = jnp.dot(xb, wg_ref[0], preferred_element_type=F32)
        u = jnp.dot(xb, wu_ref[0], preferred_element_type=F32)
        hid = (jax.nn.silu(g) * u).astype(BF16)
        y = jnp.dot(hid, wd_ref[0], preferred_element_type=F32)
        for s in range(Y_ROWS):
            y_ref[:, s, :] = y[:, s * Y_COLS:(s + 1) * Y_COLS]

    @pl.when(i >= nused)
    def _():
        y_ref[...] = jnp.zeros_like(y_ref)


def _experts(block_e, row_tok, nused, h2p, wg, wu, wd):
    grid_spec = pltpu.PrefetchScalarGridSpec(
        num_scalar_prefetch=3,
        grid=(N_BLOCKS,),
        in_specs=[
            pl.BlockSpec(memory_space=pl.ANY),
            pl.BlockSpec((1, D_MODEL, D_EXPERT), lambda i, be, tk, nu: (be[i], 0, 0)),
            pl.BlockSpec((1, D_MODEL, D_EXPERT), lambda i, be, tk, nu: (be[i], 0, 0)),
            pl.BlockSpec((1, D_EXPERT, D_MODEL), lambda i, be, tk, nu: (be[i], 0, 0)),
        ],
        out_specs=pl.BlockSpec((MOE_BLOCK, Y_ROWS, Y_COLS), lambda i, be, tk, nu: (i, 0, 0)),
        scratch_shapes=[pltpu.VMEM((2, MOE_BLOCK, PACK_ROWS, LANES), jnp.uint32),
                        pltpu.SemaphoreType.DMA((2,))],
    )
    return pl.pallas_call(
        _expert_kernel,
        grid_spec=grid_spec,
        out_shape=jax.ShapeDtypeStruct((P_ROWS, Y_ROWS, Y_COLS), F32),
        compiler_params=_cparams(("arbitrary",)),
        name="expert_ffn",
    )(block_e, row_tok, nused, h2p, wg, wu, wd)


CMB_TM = 128


def _combine_kernel(dest_ref, ys_hbm, x1_ref, gates_ref, mod_ref, fg_ref, o_ref, ybuf, sem):
    i = pl.program_id(0)
    n = pl.num_programs(0)

    def issue(tile, slot):
        def body(r, _):
            for k in range(TOP_K):
                d = dest_ref[(tile * CMB_TM + r) * TOP_K + k]
                pltpu.make_async_copy(ys_hbm.at[d], ybuf.at[slot, k * CMB_TM + r], sem.at[slot]).start()
            return 0
        lax.fori_loop(0, CMB_TM, body, 0, unroll=DMA_UNROLL)

    @pl.when(i == 0)
    def _():
        issue(0, 0)

    @pl.when(i + 1 < n)
    def _():
        issue(i + 1, (i + 1) % 2)

    slot = i % 2
    pltpu.make_async_copy(ybuf.at[slot], ybuf.at[slot], sem.at[slot]).wait()
    gts = gates_ref[...]
    w0 = gts[:, 0:1]
    w1 = gts[:, 1:2]
    ssq = jnp.zeros((CMB_TM, 1), F32)
    for s in range(Y_ROWS):
        cols = slice(s * Y_COLS, (s + 1) * Y_COLS)
        y = ybuf[slot, 0:CMB_TM, s, :] * w0 + ybuf[slot, CMB_TM:2 * CMB_TM, s, :] * w1
        x2 = x1_ref[:, cols] + mod_ref[0, 5:6, cols] * y
        ssq = ssq + jnp.sum(x2 * x2, axis=-1, keepdims=True)
        o_ref[:, cols] = x2
    rinv = lax.rsqrt(ssq * (1.0 / D_MODEL) + EPS)
    for s in range(Y_ROWS):
        cols = slice(s * Y_COLS, (s + 1) * Y_COLS)
        o_ref[:, cols] = (o_ref[:, cols] * rinv) * fg_ref[:, cols]


def _combine(dest, ys, x1, gates, mod3, final_g):
    tiles_per_batch = SEQ // CMB_TM
    grid_spec = pltpu.PrefetchScalarGridSpec(
        num_scalar_prefetch=1,
        grid=(N_TOK // CMB_TM,),
        in_specs=[
            pl.BlockSpec(memory_space=pl.ANY),
            pl.BlockSpec((CMB_TM, D_MODEL), lambda i, d: (i, 0)),
            pl.BlockSpec((CMB_TM, LANES), lambda i, d: (i, 0)),
            pl.BlockSpec((1, 6, D_MODEL), lambda i, d: (i // tiles_per_batch, 0, 0)),
            pl.BlockSpec((1, D_MODEL), lambda i, d: (0, 0)),
        ],
        out_specs=pl.BlockSpec((CMB_TM, D_MODEL), lambda i, d: (i, 0)),
        scratch_shapes=[pltpu.VMEM((2, TOP_K * CMB_TM, Y_ROWS, Y_COLS), F32), pltpu.SemaphoreType.DMA((2,))],
    )
    return pl.pallas_call(
        _combine_kernel,
        grid_spec=grid_spec,
        out_shape=jax.ShapeDtypeStruct((N_TOK, D_MODEL), F32),
        compiler_params=_cparams(("arbitrary",)),
        name="combine_final",
    )(dest, ys, x1, gates, mod3, final_g)


def _routing_tables(eid):
    a = eid.shape[0]
    onehot = (eid[:, None] == jnp.arange(N_EXPERTS, dtype=jnp.int32)[None, :]).astype(jnp.int32)
    csum = jnp.cumsum(onehot, axis=0)
    rank = jnp.sum(onehot * csum, axis=1) - 1
    counts = csum[-1]
    padded = (counts + MOE_BLOCK - 1) // MOE_BLOCK * MOE_BLOCK
    pend = jnp.cumsum(padded)
    pstart = pend - padded
    dest = (pstart[eid] + rank).astype(jnp.int32)
    tok = jnp.arange(a, dtype=jnp.int32) // TOP_K
    row_tok = jnp.zeros((P_ROWS,), jnp.int32).at[dest].set(tok)
    blk_row = jnp.arange(N_BLOCKS, dtype=jnp.int32) * MOE_BLOCK
    block_e = jnp.minimum(jnp.sum((pend[None, :] <= blk_row[:, None]).astype(jnp.int32), axis=1),
                          N_EXPERTS - 1).astype(jnp.int32)
    nused = (pend[-1] // MOE_BLOCK).astype(jnp.int32).reshape(1)
    return dest, row_tok, block_e, nused


def kernel(x, c, w_ada, b_ada, norm_mix_g, w_in, b_forget, sinks, out_norm_swa_g, out_norm_fox_g, w_out,
           norm_moe_g, w_group, b_group, w_expert, b_expert, w_gate, w_up, w_down, final_g):
    x2 = x.reshape(N_TOK, D_MODEL)
    perm = _swa_perm()

    w_in0 = w_in[0]
    w_main = jnp.concatenate([w_in0[:, :SWA_WIDTH][:, perm], w_in0[:, SWA_WIDTH:MAIN_COLS]], axis=1).astype(BF16)
    w_f = jnp.pad(w_in0[:, MAIN_COLS:], ((0, 0), (0, LANES - N_HEADS_FOX))).astype(BF16)
    colscale = jnp.concatenate([
        jnp.full((SWA_WIDTH,), HEAD_DIM ** -0.5, F32),
        jnp.ones((2 * KV_WIDTH,), F32),
        jnp.full((FOX_WIDTH,), HEAD_DIM ** -0.5 * LOG2E, F32),
        jnp.ones((2 * FOX_WIDTH,), F32)]).reshape(1, MAIN_COLS)
    bf_row = jnp.pad(b_forget[0], (0, LANES - N_HEADS_FOX)).reshape(1, LANES)
    w_out_p = jnp.concatenate([w_out[0][:SWA_WIDTH][perm], w_out[0][SWA_WIDTH:]], axis=0).astype(BF16)
    ga = out_norm_swa_g[0][perm].reshape(1, SWA_WIDTH)
    gb = out_norm_fox_g[0].reshape(1, FOX_WIDTH)
    wr = jnp.pad(jnp.concatenate([w_group[0], w_expert[0]], axis=1),
                 ((0, 0), (0, LANES - N_GROUPS - N_EXPERTS)))
    br = jnp.pad(jnp.concatenate([b_group[0], b_expert[0]]), (0, LANES - N_GROUPS - N_EXPERTS)).reshape(1, LANES)
    wg = w_gate[0].astype(BF16)
    wu = w_up[0].astype(BF16)
    wd = w_down[0].astype(BF16)

    mod = _adaln(c, w_ada[0], b_ada[0])
    mod3 = mod.reshape(BATCH, 6, D_MODEL)

    proj, f = _inproj(x2, mod3, norm_mix_g[0].reshape(1, D_MODEL), w_main, w_f, colscale)
    bias = _forget_bias(f, bf_row)

    o_a = _swa(sinks[0], proj)
    nk = SEQ // FOX_T
    vt = proj[:, VB_COL:].reshape(BATCH, nk, FOX_T, N_HEADS_FOX // 2, 2, HEAD_DIM).transpose(0, 3, 1, 4, 5, 2)
    vt6 = jnp.concatenate([vt, jnp.ones(vt.shape[:4] + (FOX_ONES, FOX_T), BF16)], axis=4)
    o_b = _fox(proj, vt6, bias)

    x1, h2, ids, gates = _outproj(o_a, o_b, x2, mod3, ga, gb, w_out_p,
                                  norm_moe_g[0].reshape(1, D_MODEL), wr, br)

    eid = ids[:, :TOP_K].reshape(-1)
    dest, row_tok, block_e, nused = _routing_tables(eid)
    ys = _experts(block_e, row_tok, nused, h2, wg, wu, wd)
    out = _combine(dest, ys, x1, gates, mod3, final_g.reshape(1, D_MODEL))
    return out.reshape(BATCH, SEQ, D_MODEL)
```

```python
import functools
import math

import numpy as np
import jax
import jax.numpy as jnp
from jax import lax
from jax.experimental import pallas as pl
from jax.experimental.pallas import tpu as pltpu

F32 = jnp.float32
BF16 = jnp.bfloat16

D_MODEL = 2048
BATCH = 2
SEQ = 8192
N_TOK = BATCH * SEQ
HEAD_DIM = 64
N_HEADS_SWA = 16
N_KV_SWA = 4
N_HEADS_FOX = 16
WINDOW = 128
SWA_WIDTH = N_HEADS_SWA * HEAD_DIM
KV_WIDTH = N_KV_SWA * HEAD_DIM
FOX_WIDTH = N_HEADS_FOX * HEAD_DIM
MAIN_COLS = SWA_WIDTH + 2 * KV_WIDTH + 3 * FOX_WIDTH
N_GROUPS = 4
EXPERTS_PER_GROUP = 8
N_EXPERTS = N_GROUPS * EXPERTS_PER_GROUP
TOP_K = 2
D_EXPERT = 512
MOE_BLOCK = 256
EPS = 1e-6
LOG2E = math.log2(math.e)

LANES = 128
VMEM_LIMIT = 56 * 1024 * 1024

QA_BLK = 0
KA_COL = SWA_WIDTH
VA_COL = SWA_WIDTH + KV_WIDTH
QB_BLK = (SWA_WIDTH + 2 * KV_WIDTH) // LANES
KB_BLK = QB_BLK + FOX_WIDTH // LANES
VB_COL = SWA_WIDTH + 2 * KV_WIDTH + 2 * FOX_WIDTH


def _cparams(sem, vmem=VMEM_LIMIT):
    return pltpu.CompilerParams(dimension_semantics=sem, vmem_limit_bytes=vmem)


ADA_TN = 1024


def _adaln_kernel(cb_ref, w_ref, b_ref, o_ref):
    for b in range(BATCH):
        s = jax.nn.silu(cb_ref[b])
        cols = []
        for j in range(ADA_TN // LANES):
            prod = w_ref[:, j * LANES:(j + 1) * LANES] * s
            cols.append(jnp.sum(prod, axis=0, keepdims=True))
        o_ref[b:b + 1, :] = jnp.concatenate(cols, axis=1) + b_ref[...]


def _adaln(c, w_ada, b_ada):
    ncol = w_ada.shape[1]
    cb = jnp.broadcast_to(c[:, :, None], (BATCH, D_MODEL, LANES))
    return pl.pallas_call(
        _adaln_kernel,
        grid=(ncol // ADA_TN,),
        in_specs=[
            pl.BlockSpec((BATCH, D_MODEL, LANES), lambda j: (0, 0, 0)),
            pl.BlockSpec((D_MODEL, ADA_TN), lambda j: (0, j)),
            pl.BlockSpec((1, ADA_TN), lambda j: (0, j)),
        ],
        out_specs=pl.BlockSpec((BATCH, ADA_TN), lambda j: (0, j)),
        out_shape=jax.ShapeDtypeStruct((BATCH, ncol), F32),
        compiler_params=_cparams(("arbitrary",)),
        name="adaln",
    )(cb, w_ada, b_ada.reshape(1, ncol))


def _rms(x):
    return x * lax.rsqrt(jnp.mean(x * x, axis=-1, keepdims=True) + EPS)


IN_TM = 512
IN_TN = 1536


def _inproj_kernel(x_ref, mod_ref, g_ref, w_ref, wf_ref, cs_ref, o_ref, f_ref, h_ref):
    @pl.when(pl.program_id(1) == 0)
    def _():
        y = _rms(x_ref[...])
        h = (y * g_ref[...]) * (1.0 + mod_ref[0, 1:2, :]) + mod_ref[0, 0:1, :]
        hb = h.astype(BF16)
        h_ref[...] = hb
        f_ref[...] = jnp.dot(hb, wf_ref[...], preferred_element_type=F32)

    acc = jnp.dot(h_ref[...], w_ref[...], preferred_element_type=F32)
    o_ref[...] = (acc * cs_ref[...]).astype(BF16)


def _inproj(x2, mod3, g, w_main, w_f, colscale):
    tiles_per_batch = SEQ // IN_TM
    return pl.pallas_call(
        _inproj_kernel,
        grid=(N_TOK // IN_TM, MAIN_COLS // IN_TN),
        in_specs=[
            pl.BlockSpec((IN_TM, D_MODEL), lambda i, j: (i, 0)),
            pl.BlockSpec((1, 6, D_MODEL), lambda i, j: (i // tiles_per_batch, 0, 0)),
            pl.BlockSpec((1, D_MODEL), lambda i, j: (0, 0)),
            pl.BlockSpec((D_MODEL, IN_TN), lambda i, j: (0, j)),
            pl.BlockSpec((D_MODEL, LANES), lambda i, j: (0, 0)),
            pl.BlockSpec((1, IN_TN), lambda i, j: (0, j)),
        ],
        out_specs=[
            pl.BlockSpec((IN_TM, IN_TN), lambda i, j: (i, j)),
            pl.BlockSpec((IN_TM, LANES), lambda i, j: (i, 0)),
        ],
        out_shape=[
            jax.ShapeDtypeStruct((N_TOK, MAIN_COLS), BF16),
            jax.ShapeDtypeStruct((N_TOK, LANES), F32),
        ],
        scratch_shapes=[pltpu.VMEM((IN_TM, D_MODEL), BF16)],
        compiler_params=_cparams(("arbitrary", "arbitrary")),
        name="inproj",
    )(x2, mod3, g, w_main, w_f, colscale)


CUM_TS = 512


def _cum_kernel(f_ref, bf_ref, tri_ref, o_ref, carry_ref):
    @pl.when(pl.program_id(1) == 0)
    def _():
        carry_ref[...] = jnp.zeros_like(carry_ref)

    lf = jax.nn.log_sigmoid(f_ref[...] + bf_ref[...])
    cs = jnp.dot(tri_ref[...], lf, preferred_element_type=F32,
                 precision=lax.Precision.HIGHEST) + carry_ref[...]
    o_ref[...] = cs * (-LOG2E)
    carry_ref[...] = cs[CUM_TS - 1:CUM_TS, :]


def _forget_bias(f, b_forget_row):
    tri = jnp.tril(jnp.ones((CUM_TS, CUM_TS), F32))
    nblk = SEQ // CUM_TS
    return pl.pallas_call(
        _cum_kernel,
        grid=(BATCH, nblk),
        in_specs=[
            pl.BlockSpec((CUM_TS, LANES), lambda b, i: (b * nblk + i, 0)),
            pl.BlockSpec((1, LANES), lambda b, i: (0, 0)),
            pl.BlockSpec((CUM_TS, CUM_TS), lambda b, i: (0, 0)),
        ],
        out_specs=pl.BlockSpec((CUM_TS, LANES), lambda b, i: (b * nblk + i, 0)),
        out_shape=jax.ShapeDtypeStruct((N_TOK, LANES), F32),
        scratch_shapes=[pltpu.VMEM((1, LANES), F32)],
        compiler_params=_cparams(("arbitrary", "arbitrary")),
        name="forget_bias",
    )(f, b_forget_row, tri)


SWA_TQ = WINDOW


def _swa_perm():
    perm = np.zeros((SWA_WIDTH,), np.int32)
    for t in range(2):
        for g in range(4):
            for e in range(2):
                head = (2 * t + e) * 4 + g
                base = (t * 4 + g) * LANES + e * HEAD_DIM
                perm[base:base + HEAD_DIM] = head * HEAD_DIM + np.arange(HEAD_DIM)
    return perm


def _swa_kernel(sinks_ref, q_ref, kc_ref, kp_ref, vc_ref, vp_ref, o_ref, bias_ref):
    i = pl.program_id(1)
    tq = SWA_TQ

    @pl.when((pl.program_id(0) == 0) & (i == 0))
    def _():
        row = lax.broadcasted_iota(jnp.int32, (tq, 2 * tq), 0)
        col = lax.broadcasted_iota(jnp.int32, (tq, 2 * tq), 1)
        dist = row + tq - col
        band = (dist >= 0) & (dist < WINDOW)
        distf = dist.astype(F32)
        for head in range(N_HEADS_SWA):
            slope2 = float(2.0 ** (-8.0 * (head + 1) / N_HEADS_SWA)) * LOG2E
            base = jnp.where(band, -slope2 * distf, -jnp.inf)
            bias_ref[1, head] = base
            bias_ref[0, head] = jnp.where(col >= tq, base, -jnp.inf)

    table = jnp.minimum(i, 1)
    lane = lax.broadcasted_iota(jnp.int32, (tq, LANES), 1)
    lo_half = lane < HEAD_DIM
    for t in range(2):
        kt = jnp.concatenate([kp_ref[:, t * LANES:(t + 1) * LANES],
                              kc_ref[:, t * LANES:(t + 1) * LANES]], axis=0)
        vt = jnp.concatenate([vp_ref[:, t * LANES:(t + 1) * LANES],
                              vc_ref[:, t * LANES:(t + 1) * LANES]], axis=0)
        heads = [(g, e) for g in range(4) for e in range(2)]
        scores = []
        for g, e in heads:
            tile = t * 4 + g
            qt = q_ref[:, tile * LANES:(tile + 1) * LANES]
            qm = jnp.where(lo_half if e == 0 else ~lo_half, qt, jnp.zeros_like(qt))
            scores.append(lax.dot_general(qm, kt, (((1,), (1,)), ((), ())),
                                          preferred_element_type=F32))
        probs, rdens = [], []
        for (g, e), s in zip(heads, scores):
            head = (2 * t + e) * 4 + g
            sink = sinks_ref[head] * LOG2E
            s = s + bias_ref[table, head]
            m = jnp.maximum(jnp.max(s, axis=-1, keepdims=True), sink)
            p = jnp.exp2(s - m)
            rdens.append(1.0 / (jnp.sum(p, axis=-1, keepdims=True) + jnp.exp2(sink - m)))
            probs.append(p.astype(BF16))
        outs = [jnp.dot(p, vt, preferred_element_type=F32) * r for p, r in zip(probs, rdens)]
        for g in range(4):
            tile = t * 4 + g
            o_ref[:, tile * LANES:(tile + 1) * LANES] = jnp.where(lo_half, outs[2 * g], outs[2 * g + 1])


def _swa(sinks, proj):
    nq = SEQ // SWA_TQ
    kblk = KA_COL // KV_WIDTH
    vblk = VA_COL // KV_WIDTH
    grid_spec = pltpu.PrefetchScalarGridSpec(
        num_scalar_prefetch=1,
        grid=(BATCH, nq),
        in_specs=[
            pl.BlockSpec((SWA_TQ, SWA_WIDTH), lambda b, i, s: (b * nq + i, 0)),
            pl.BlockSpec((SWA_TQ, KV_WIDTH), lambda b, i, s: (b * nq + i, kblk)),
            pl.BlockSpec((SWA_TQ, KV_WIDTH), lambda b, i, s: (b * nq + jnp.maximum(i - 1, 0), kblk)),
            pl.BlockSpec((SWA_TQ, KV_WIDTH), lambda b, i, s: (b * nq + i, vblk)),
            pl.BlockSpec((SWA_TQ, KV_WIDTH), lambda b, i, s: (b * nq + jnp.maximum(i - 1, 0), vblk)),
        ],
        out_specs=pl.BlockSpec((SWA_TQ, SWA_WIDTH), lambda b, i, s: (b * nq + i, 0)),
        scratch_shapes=[pltpu.VMEM((2, N_HEADS_SWA, SWA_TQ, 2 * SWA_TQ), F32)],
    )
    return pl.pallas_call(
        _swa_kernel,
        grid_spec=grid_spec,
        out_shape=jax.ShapeDtypeStruct((N_TOK, SWA_WIDTH), F32),
        compiler_params=_cparams(("arbitrary", "arbitrary")),
        name="swa_attn",
    )(sinks, proj, proj, proj, proj, proj)


FOX_T = 512
FOX_CH = 512
FOX_AUG = 3
FOX_ONES = 16
FOX_VROWS = HEAD_DIM + FOX_ONES


def _fox_kernel(q_ref, k_ref, vt_ref, bias_ref, o_ref, kaug_ref, mask_ref, t_ref, acc_ref):
    hp = pl.program_id(1)
    qi = pl.program_id(2)
    t = FOX_T

    @pl.when(qi == 0)
    def _():
        lane = lax.broadcasted_iota(jnp.int32, (FOX_CH, LANES), 1)

        def chunk(c, _):
            r0 = pl.multiple_of(c * FOX_CH, FOX_CH)
            blk = bias_ref[pl.ds(r0, FOX_CH), :]
            aug = jnp.zeros((FOX_CH, LANES), F32)
            for hh in range(2):
                colv = jnp.sum(jnp.where(lane == 2 * hp + hh, blk, 0.0), axis=1, keepdims=True)
                hi = colv.astype(BF16).astype(F32)
                r1 = colv - hi
                mid = r1.astype(BF16).astype(F32)
                lo = r1 - mid
                for piece, val in enumerate((hi, mid, lo)):
                    aug = jnp.where(lane == FOX_AUG * hh + piece, val, aug)
            kaug_ref[pl.ds(r0, FOX_CH), 0:LANES] = k_ref[pl.ds(r0, FOX_CH), :]
            kaug_ref[pl.ds(r0, FOX_CH), LANES:2 * LANES] = aug.astype(BF16)
            return 0

        lax.fori_loop(0, SEQ // FOX_CH, chunk, 0)
        kr = lax.broadcasted_iota(jnp.int32, (t, t), 0)
        qc = lax.broadcasted_iota(jnp.int32, (t, t), 1)
        mask_ref[...] = jnp.where(kr <= qc, 0.0, -jnp.inf)

    qf = q_ref[...].astype(F32).T
    drow = lax.broadcasted_iota(jnp.int32, (LANES, t), 0)
    qaug = []
    for hh in range(2):
        qh = jnp.where((drow >= HEAD_DIM * hh) & (drow < HEAD_DIM * (hh + 1)), qf, 0.0)
        ones = jnp.where((drow >= FOX_AUG * hh) & (drow < FOX_AUG * (hh + 1)), 1.0, 0.0)
        qaug.append(jnp.concatenate([qh, ones], axis=0).astype(BF16))
    acc_ref[...] = jnp.zeros_like(acc_ref)

    def stage_a(j, slot, diag):
        r0 = pl.multiple_of(j * t, t)
        kb = kaug_ref[pl.ds(r0, t), :]
        sts = [jnp.dot(kb, qaug[hh], preferred_element_type=F32) for hh in range(2)]
        mbs = []
        for hh in range(2):
            tt = sts[hh]
            if diag:
                tt = tt + mask_ref[...]
            t_ref[slot, hh] = tt
            mbs.append(jnp.max(tt, axis=0, keepdims=True))
        return tuple(mbs)

    def stage_b(jv, slot, ms, mbs):
        new, ps, alphas = [], [], []
        for hh in range(2):
            m_new = jnp.maximum(ms[hh], mbs[hh])
            alphas.append(jnp.exp2(ms[hh] - m_new))
            ps.append(jnp.exp2(t_ref[slot, hh] - m_new).astype(BF16))
            new.append(m_new)
        ones = jnp.ones((FOX_ONES, t), BF16)
        pvs = [jnp.dot(jnp.concatenate([vt_ref[0, 0, jv, hh * HEAD_DIM:(hh + 1) * HEAD_DIM, :], ones], axis=0),
                       ps[hh], preferred_element_type=F32)
               for hh in range(2)]
        for hh in range(2):
            acc_ref[hh] = alphas[hh] * acc_ref[hh] + pvs[hh]
        return tuple(new)

    ml0 = tuple(jnp.full((1, t), -jnp.inf, F32) for _ in range(2))
    mb0 = stage_a(qi, 0, True)

    def pair(ii, c):
        ml, mbs, jprev = c
        j0 = 2 * ii
        mb1 = stage_a(j0, 1, False)
        ml = stage_b(jprev, 0, ml, mbs)
        mb2 = stage_a(j0 + 1, 0, False)
        ml = stage_b(j0, 1, ml, mb1)
        return ml, mb2, j0 + 1

    ml, mbs, jprev = lax.fori_loop(0, qi // 2, pair, (ml0, mb0, qi))

    def odd_tail(c):
        ml, mbs, jprev = c
        mb1 = stage_a(qi - 1, 1, False)
        ml = stage_b(jprev, 0, ml, mbs)
        return stage_b(qi - 1, 1, ml, mb1)

    def even_tail(c):
        ml, mbs, jprev = c
        return stage_b(jprev, 0, ml, mbs)

    ml = lax.cond(qi % 2 == 1, odd_tail, even_tail, (ml, mbs, jprev))
    del ml
    ot = jnp.concatenate([acc_ref[hh, 0:HEAD_DIM, :] / acc_ref[hh, HEAD_DIM:HEAD_DIM + 1, :]
                          for hh in range(2)], axis=0)
    o_ref[...] = ot.T


def _fox(proj, vt5, bias):
    nq = SEQ // FOX_T
    npair = N_HEADS_FOX // 2
    return pl.pallas_call(
        _fox_kernel,
        grid=(BATCH, npair, nq),
        in_specs=[
            pl.BlockSpec((FOX_T, LANES), lambda b, h, i: (b * nq + i, QB_BLK + h)),
            pl.BlockSpec((SEQ, LANES), lambda b, h, i: (b, KB_BLK + h)),
            pl.BlockSpec((1, 1, nq, LANES, FOX_T), lambda b, h, i: (b, h, 0, 0, 0)),
            pl.BlockSpec((SEQ, LANES), lambda b, h, i: (b, 0)),
        ],
        out_specs=pl.BlockSpec((FOX_T, LANES), lambda b, h, i: (b * nq + i, h)),
        out_shape=jax.ShapeDtypeStruct((N_TOK, FOX_WIDTH), F32),
        scratch_shapes=[
            pltpu.VMEM((SEQ, 2 * LANES), BF16),
            pltpu.VMEM((FOX_T, FOX_T), F32),
            pltpu.VMEM((2, 2, FOX_T, FOX_T), F32),
            pltpu.VMEM((2, FOX_VROWS, FOX_T), F32),
        ],
        compiler_params=_cparams(("arbitrary", "arbitrary", "arbitrary")),
        name="fox_attn",
    )(proj, proj, vt5, bias)


OUT_TM = 256
PACK_ROWS = D_MODEL // 2 // LANES
Y_ROWS = 8
Y_COLS = D_MODEL // Y_ROWS
DMA_UNROLL = 8


def _outproj_kernel(oa_ref, ob_ref, x_ref, mod_ref, ga_ref, gb_ref, w_ref, gm_ref, wr2_ref, br_ref,
                    x1_ref, h2p_ref, ids_ref, gates_ref):
    na = _rms(oa_ref[...]) * ga_ref[...]
    nb = _rms(ob_ref[...]) * gb_ref[...]
    mixed = jnp.concatenate([na, nb], axis=1).astype(BF16)
    y = jnp.dot(mixed, w_ref[...], preferred_element_type=F32)
    x1 = x_ref[...] + mod_ref[0, 2:3, :] * y
    x1_ref[...] = x1
    h2 = (_rms(x1) * gm_ref[...]) * (1.0 + mod_ref[0, 4:5, :]) + mod_ref[0, 3:4, :]
    hb = h2.astype(BF16)
    hbf = hb.astype(F32)
    half = D_MODEL // 2
    lo_bits = lax.shift_right_logical(pltpu.bitcast(hbf[:, :half], jnp.uint32), jnp.uint32(16))
    hi_bits = pltpu.bitcast(hbf[:, half:], jnp.uint32) & jnp.uint32(0xFFFF0000)
    packed = lo_bits | hi_bits
    for s in range(PACK_ROWS):
        h2p_ref[:, s, :] = packed[:, s * LANES:(s + 1) * LANES]

    hl = (h2 - hbf).astype(BF16)
    r2 = jnp.dot(hb, wr2_ref[...], preferred_element_type=F32)
    logits = (r2[:, :LANES] + (r2[:, LANES:] + jnp.dot(hl, wr2_ref[:, :LANES], preferred_element_type=F32))
              + br_ref[...])
    lane = lax.broadcasted_iota(jnp.int32, logits.shape, 1)
    neg = -jnp.inf
    is_g = lane < N_GROUPS
    gl = jnp.where(is_g, logits, neg)
    gmax = jnp.max(gl, axis=1, keepdims=True)
    gsel = jnp.min(jnp.where(gl == gmax, lane, LANES), axis=1, keepdims=True)
    gsum = jnp.sum(jnp.where(is_g, jnp.exp(gl - gmax), 0.0), axis=1, keepdims=True)
    g_val = 1.0 / gsum
    elane = lane - N_GROUPS
    in_sel = (elane >= 0) & (elane < N_EXPERTS) & ((elane >> 3) == gsel)
    ev = jnp.where(in_sel, logits, neg)
    t1 = jnp.max(ev, axis=1, keepdims=True)
    i1 = jnp.min(jnp.where(ev == t1, lane, LANES), axis=1, keepdims=True)
    ev2 = jnp.where(lane == i1, neg, ev)
    t2 = jnp.max(ev2, axis=1, keepdims=True)
    i2 = jnp.min(jnp.where(ev2 == t2, lane, LANES), axis=1, keepdims=True)
    e2 = jnp.exp(t2 - t1)
    den = 1.0 + e2
    w1 = (1.0 / den) * g_val
    w2 = (e2 / den) * g_val
    ids_ref[...] = jnp.where(lane == 0, i1 - N_GROUPS, jnp.where(lane == 1, i2 - N_GROUPS, 0))
    gates_ref[...] = jnp.where(lane == 0, w1, jnp.where(lane == 1, w2, 0.0))


def _outproj(oa, ob, x2, mod3, ga, gb, w_out, gm, wr, br):
    tiles_per_batch = SEQ // OUT_TM
    row = lambda i: (i, 0)
    const = lambda i: (0, 0)
    wr_hi = wr.astype(BF16)
    wr_lo = (wr - wr_hi.astype(F32)).astype(BF16)
    wr2 = jnp.concatenate([wr_hi, wr_lo], axis=1)
    return pl.pallas_call(
        _outproj_kernel,
        grid=(N_TOK // OUT_TM,),
        in_specs=[
            pl.BlockSpec((OUT_TM, SWA_WIDTH), row),
            pl.BlockSpec((OUT_TM, FOX_WIDTH), row),
            pl.BlockSpec((OUT_TM, D_MODEL), row),
            pl.BlockSpec((1, 6, D_MODEL), lambda i: (i // tiles_per_batch, 0, 0)),
            pl.BlockSpec((1, SWA_WIDTH), const),
            pl.BlockSpec((1, FOX_WIDTH), const),
            pl.BlockSpec((D_MODEL, D_MODEL), const),
            pl.BlockSpec((1, D_MODEL), const),
            pl.BlockSpec((D_MODEL, 2 * LANES), const),
            pl.BlockSpec((1, LANES), const),
        ],
        out_specs=[
            pl.BlockSpec((OUT_TM, D_MODEL), row),
            pl.BlockSpec((OUT_TM, PACK_ROWS, LANES), lambda i: (i, 0, 0)),
            pl.BlockSpec((OUT_TM, LANES), row),
            pl.BlockSpec((OUT_TM, LANES), row),
        ],
        out_shape=[
            jax.ShapeDtypeStruct((N_TOK, D_MODEL), F32),
            jax.ShapeDtypeStruct((N_TOK, PACK_ROWS, LANES), jnp.uint32),
            jax.ShapeDtypeStruct((N_TOK, LANES), jnp.int32),
            jax.ShapeDtypeStruct((N_TOK, LANES), F32),
        ],
        compiler_params=_cparams(("arbitrary",)),
        name="outproj_router",
    )(oa, ob, x2, mod3, ga, gb, w_out, gm, wr2, br)


N_BLOCKS = (N_TOK * TOP_K) // MOE_BLOCK + N_EXPERTS
P_ROWS = N_BLOCKS * MOE_BLOCK


def _expert_kernel(be_ref, tok_ref, nused_ref, h2_hbm, wg_ref, wu_ref, wd_ref, y_ref, xbuf, sem,
                   wgb, wub, wdb):
    i = pl.program_id(0)
    nused = nused_ref[0]

    def issue(blk, slot):
        def body(r, _):
            tok = tok_ref[blk * MOE_BLOCK + r]
            pltpu.make_async_copy(h2_hbm.at[tok], xbuf.at[slot, r], sem.at[slot]).start()
            return 0
        lax.fori_loop(0, MOE_BLOCK, body, 0, unroll=DMA_UNROLL)

    @pl.when(i == 0)
    def _():
        issue(0, 0)

    @pl.when(i + 1 < nused)
    def _():
        issue(i + 1, (i + 1) % 2)

    new_expert = (i == 0) | (be_ref[i] != be_ref[jnp.maximum(i - 1, 0)])

    @pl.when(new_expert & (i < nused))
    def _():
        wgb[...] = wg_ref[0].astype(BF16)
        wub[...] = wu_ref[0].astype(BF16)
        wdb[...] = wd_ref[0].astype(BF16)

    @pl.when(i < nused)
    def _():
        slot = i % 2
        pltpu.make_async_copy(xbuf.at[slot], xbuf.at[slot], sem.at[slot]).wait()
        xu = jnp.concatenate([xbuf[slot, :, s, :] for s in range(PACK_ROWS)], axis=1)
        lo = pltpu.bitcast(lax.shift_left(xu, jnp.uint32(16)), F32)
        hi = pltpu.bitcast(xu & jnp.uint32(0xFFFF0000), F32)
        xb = jnp.concatenate([lo, hi], axis=1).astype(BF16)
        g = jnp.dot(xb, wgb[...], preferred_element_type=F32)
        u = jnp.dot(xb, wub[...], preferred_element_type=F32)
        hid = (jax.nn.silu(g) * u).astype(BF16)
        y = jnp.dot(hid, wdb[...], preferred_element_type=F32)
        for s in range(Y_ROWS):
            y_ref[:, s, :] = y[:, s * Y_COLS:(s + 1) * Y_COLS]

    @pl.when(i >= nused)
    def _():
        y_ref[...] = jnp.zeros_like(y_ref)


def _experts(block_e, row_tok, nused, h2p, wg, wu, wd):
    grid_spec = pltpu.PrefetchScalarGridSpec(
        num_scalar_prefetch=3,
        grid=(N_BLOCKS,),
        in_specs=[
            pl.BlockSpec(memory_space=pl.ANY),
            pl.BlockSpec((1, D_MODEL, D_EXPERT), lambda i, be, tk, nu: (be[i], 0, 0)),
            pl.BlockSpec((1, D_MODEL, D_EXPERT), lambda i, be, tk, nu: (be[i], 0, 0)),
            pl.BlockSpec((1, D_EXPERT, D_MODEL), lambda i, be, tk, nu: (be[i], 0, 0)),
        ],
        out_specs=pl.BlockSpec((MOE_BLOCK, Y_ROWS, Y_COLS), lambda i, be, tk, nu: (i, 0, 0)),
        scratch_shapes=[pltpu.VMEM((2, MOE_BLOCK, PACK_ROWS, LANES), jnp.uint32),
                        pltpu.SemaphoreType.DMA((2,)),
                        pltpu.VMEM((D_MODEL, D_EXPERT), BF16),
                        pltpu.VMEM((D_MODEL, D_EXPERT), BF16),
                        pltpu.VMEM((D_EXPERT, D_MODEL), BF16)],
    )
    return pl.pallas_call(
        _expert_kernel,
        grid_spec=grid_spec,
        out_shape=jax.ShapeDtypeStruct((P_ROWS, Y_ROWS, Y_COLS), F32),
        compiler_params=_cparams(("arbitrary",)),
        name="expert_ffn",
    )(block_e, row_tok, nused, h2p, wg, wu, wd)


CMB_TM = 128


def _combine_kernel(dest_ref, ys_hbm, x1_ref, gates_ref, mod_ref, fg_ref, o_ref, ybuf, sem):
    i = pl.program_id(0)
    n = pl.num_programs(0)

    def issue(tile, slot):
        def body(r, _):
            for k in range(TOP_K):
                d = dest_ref[(tile * CMB_TM + r) * TOP_K + k]
                pltpu.make_async_copy(ys_hbm.at[d], ybuf.at[slot, k * CMB_TM + r], sem.at[slot]).start()
            return 0
        lax.fori_loop(0, CMB_TM, body, 0, unroll=DMA_UNROLL)

    @pl.when(i == 0)
    def _():
        issue(0, 0)

    @pl.when(i + 1 < n)
    def _():
        issue(i + 1, (i + 1) % 2)

    slot = i % 2
    pltpu.make_async_copy(ybuf.at[slot], ybuf.at[slot], sem.at[slot]).wait()
    gts = gates_ref[...]
    w0 = gts[:, 0:1]
    w1 = gts[:, 1:2]
    ssq = jnp.zeros((CMB_TM, 1), F32)
    for s in range(Y_ROWS):
        cols = slice(s * Y_COLS, (s + 1) * Y_COLS)
        y = ybuf[slot, 0:CMB_TM, s, :] * w0 + ybuf[slot, CMB_TM:2 * CMB_TM, s, :] * w1
        x2 = x1_ref[:, cols] + mod_ref[0, 5:6, cols] * y
        ssq = ssq + jnp.sum(x2 * x2, axis=-1, keepdims=True)
        o_ref[:, cols] = x2
    rinv = lax.rsqrt(ssq * (1.0 / D_MODEL) + EPS)
    for s in range(Y_ROWS):
        cols = slice(s * Y_COLS, (s + 1) * Y_COLS)
        o_ref[:, cols] = (o_ref[:, cols] * rinv) * fg_ref[:, cols]


def _combine(dest, ys, x1, gates, mod3, final_g):
    tiles_per_batch = SEQ // CMB_TM
    grid_spec = pltpu.PrefetchScalarGridSpec(
        num_scalar_prefetch=1,
        grid=(N_TOK // CMB_TM,),
        in_specs=[
            pl.BlockSpec(memory_space=pl.ANY),
            pl.BlockSpec((CMB_TM, D_MODEL), lambda i, d: (i, 0)),
            pl.BlockSpec((CMB_TM, LANES), lambda i, d: (i, 0)),
            pl.BlockSpec((1, 6, D_MODEL), lambda i, d: (i // tiles_per_batch, 0, 0)),
            pl.BlockSpec((1, D_MODEL), lambda i, d: (0, 0)),
        ],
        out_specs=pl.BlockSpec((CMB_TM, D_MODEL), lambda i, d: (i, 0)),
        scratch_shapes=[pltpu.VMEM((2, TOP_K * CMB_TM, Y_ROWS, Y_COLS), F32), pltpu.SemaphoreType.DMA((2,))],
    )
    return pl.pallas_call(
        _combine_kernel,
        grid_spec=grid_spec,
        out_shape=jax.ShapeDtypeStruct((N_TOK, D_MODEL), F32),
        compiler_params=_cparams(("arbitrary",)),
        name="combine_final",
    )(dest, ys, x1, gates, mod3, final_g)


def _routing_tables(eid):
    a = eid.shape[0]
    onehot = (eid[:, None] == jnp.arange(N_EXPERTS, dtype=jnp.int32)[None, :]).astype(jnp.int32)
    csum = jnp.cumsum(onehot, axis=0)
    rank = jnp.sum(onehot * csum, axis=1) - 1
    counts = csum[-1]
    padded = (counts + MOE_BLOCK - 1) // MOE_BLOCK * MOE_BLOCK
    pend = jnp.cumsum(padded)
    pstart = pend - padded
    dest = (pstart[eid] + rank).astype(jnp.int32)
    tok = jnp.arange(a, dtype=jnp.int32) // TOP_K
    row_tok = jnp.zeros((P_ROWS,), jnp.int32).at[dest].set(tok)
    blk_row = jnp.arange(N_BLOCKS, dtype=jnp.int32) * MOE_BLOCK
    block_e = jnp.minimum(jnp.sum((pend[None, :] <= blk_row[:, None]).astype(jnp.int32), axis=1),
                          N_EXPERTS - 1).astype(jnp.int32)
    nused = (pend[-1] // MOE_BLOCK).astype(jnp.int32).reshape(1)
    return dest, row_tok, block_e, nused


def kernel(x, c, w_ada, b_ada, norm_mix_g, w_in, b_forget, sinks, out_norm_swa_g, out_norm_fox_g, w_out,
           norm_moe_g, w_group, b_group, w_expert, b_expert, w_gate, w_up, w_down, final_g):
    x2 = x.reshape(N_TOK, D_MODEL)
    perm = _swa_perm()

    w_in0 = w_in[0]
    w_main = jnp.concatenate([w_in0[:, :SWA_WIDTH][:, perm], w_in0[:, SWA_WIDTH:MAIN_COLS]], axis=1).astype(BF16)
    w_f = jnp.pad(w_in0[:, MAIN_COLS:], ((0, 0), (0, LANES - N_HEADS_FOX))).astype(BF16)
    colscale = jnp.concatenate([
        jnp.full((SWA_WIDTH,), HEAD_DIM ** -0.5 * LOG2E, F32),
        jnp.ones((2 * KV_WIDTH,), F32),
        jnp.full((FOX_WIDTH,), HEAD_DIM ** -0.5 * LOG2E, F32),
        jnp.ones((2 * FOX_WIDTH,), F32)]).reshape(1, MAIN_COLS)
    bf_row = jnp.pad(b_forget[0], (0, LANES - N_HEADS_FOX)).reshape(1, LANES)
    w_out_p = jnp.concatenate([w_out[0][:SWA_WIDTH][perm], w_out[0][SWA_WIDTH:]], axis=0).astype(BF16)
    ga = out_norm_swa_g[0][perm].reshape(1, SWA_WIDTH)
    gb = out_norm_fox_g[0].reshape(1, FOX_WIDTH)
    wr = jnp.pad(jnp.concatenate([w_group[0], w_expert[0]], axis=1),
                 ((0, 0), (0, LANES - N_GROUPS - N_EXPERTS)))
    br = jnp.pad(jnp.concatenate([b_group[0], b_expert[0]]), (0, LANES - N_GROUPS - N_EXPERTS)).reshape(1, LANES)
    wg, wu, wd = w_gate[0], w_up[0], w_down[0]

    mod = _adaln(c, w_ada[0], b_ada[0])
    mod3 = mod.reshape(BATCH, 6, D_MODEL)

    proj, f = _inproj(x2, mod3, norm_mix_g[0].reshape(1, D_MODEL), w_main, w_f, colscale)
    bias = _forget_bias(f, bf_row)

    o_a = _swa(sinks[0], proj)
    nk = SEQ // FOX_T
    vt5 = proj[:, VB_COL:].reshape(BATCH, nk, FOX_T, N_HEADS_FOX // 2, LANES).transpose(0, 3, 1, 4, 2)
    o_b = _fox(proj, vt5, bias)

    x1, h2, ids, gates = _outproj(o_a, o_b, x2, mod3, ga, gb, w_out_p,
                                  norm_moe_g[0].reshape(1, D_MODEL), wr, br)

    eid = ids[:, :TOP_K].reshape(-1)
    dest, row_tok, block_e, nused = _routing_tables(eid)
    ys = _experts(block_e, row_tok, nused, h2, wg, wu, wd)
    out = _combine(dest, ys, x1, gates, mod3, final_g.reshape(1, D_MODEL))
    return out.reshape(BATCH, SEQ, D_MODEL)
```

```python
import functools
import math

import numpy as np
import jax
import jax.numpy as jnp
from jax import lax
from jax.experimental import pallas as pl
from jax.experimental.pallas import tpu as pltpu

F32 = jnp.float32
BF16 = jnp.bfloat16

D_MODEL = 2048
BATCH = 2
SEQ = 8192
N_TOK = BATCH * SEQ
HEAD_DIM = 64
N_HEADS_SWA = 16
N_KV_SWA = 4
N_HEADS_FOX = 16
WINDOW = 128
SWA_WIDTH = N_HEADS_SWA * HEAD_DIM
KV_WIDTH = N_KV_SWA * HEAD_DIM
FOX_WIDTH = N_HEADS_FOX * HEAD_DIM
MAIN_COLS = SWA_WIDTH + 2 * KV_WIDTH + 3 * FOX_WIDTH
N_GROUPS = 4
EXPERTS_PER_GROUP = 8
N_EXPERTS = N_GROUPS * EXPERTS_PER_GROUP
TOP_K = 2
D_EXPERT = 512
MOE_BLOCK = 256
EPS = 1e-6
LOG2E = math.log2(math.e)

LANES = 128
VMEM_LIMIT = 56 * 1024 * 1024

QA_BLK = 0
KA_COL = SWA_WIDTH
VA_COL = SWA_WIDTH + KV_WIDTH
QB_BLK = (SWA_WIDTH + 2 * KV_WIDTH) // LANES
KB_BLK = QB_BLK + FOX_WIDTH // LANES
VB_COL = SWA_WIDTH + 2 * KV_WIDTH + 2 * FOX_WIDTH


def _cparams(sem, vmem=VMEM_LIMIT):
    return pltpu.CompilerParams(dimension_semantics=sem, vmem_limit_bytes=vmem)


ADA_TN = 1024


def _adaln_kernel(cb_ref, w_ref, b_ref, o_ref):
    for b in range(BATCH):
        s = jax.nn.silu(cb_ref[b])
        cols = []
        for j in range(ADA_TN // LANES):
            prod = w_ref[:, j * LANES:(j + 1) * LANES] * s
            cols.append(jnp.sum(prod, axis=0, keepdims=True))
        o_ref[b:b + 1, :] = jnp.concatenate(cols, axis=1) + b_ref[...]


def _adaln(c, w_ada, b_ada):
    ncol = w_ada.shape[1]
    cb = jnp.broadcast_to(c[:, :, None], (BATCH, D_MODEL, LANES))
    return pl.pallas_call(
        _adaln_kernel,
        grid=(ncol // ADA_TN,),
        in_specs=[
            pl.BlockSpec((BATCH, D_MODEL, LANES), lambda j: (0, 0, 0)),
            pl.BlockSpec((D_MODEL, ADA_TN), lambda j: (0, j)),
            pl.BlockSpec((1, ADA_TN), lambda j: (0, j)),
        ],
        out_specs=pl.BlockSpec((BATCH, ADA_TN), lambda j: (0, j)),
        out_shape=jax.ShapeDtypeStruct((BATCH, ncol), F32),
        compiler_params=_cparams(("arbitrary",)),
        name="adaln",
    )(cb, w_ada, b_ada.reshape(1, ncol))


def _rms(x):
    return x * lax.rsqrt(jnp.mean(x * x, axis=-1, keepdims=True) + EPS)


IN_TM = 512
IN_TN = 1536


def _inproj_kernel(x_ref, mod_ref, g_ref, w_ref, wf_ref, cs_ref, o_ref, f_ref, h_ref):
    @pl.when(pl.program_id(1) == 0)
    def _():
        y = _rms(x_ref[...])
        h = (y * g_ref[...]) * (1.0 + mod_ref[0, 1:2, :]) + mod_ref[0, 0:1, :]
        hb = h.astype(BF16)
        h_ref[...] = hb
        f_ref[...] = jnp.dot(hb, wf_ref[...], preferred_element_type=F32)

    acc = jnp.dot(h_ref[...], w_ref[...], preferred_element_type=F32)
    o_ref[...] = (acc * cs_ref[...]).astype(BF16)


def _inproj(x2, mod3, g, w_main, w_f, colscale):
    tiles_per_batch = SEQ // IN_TM
    return pl.pallas_call(
        _inproj_kernel,
        grid=(N_TOK // IN_TM, MAIN_COLS // IN_TN),
        in_specs=[
            pl.BlockSpec((IN_TM, D_MODEL), lambda i, j: (i, 0)),
            pl.BlockSpec((1, 6, D_MODEL), lambda i, j: (i // tiles_per_batch, 0, 0)),
            pl.BlockSpec((1, D_MODEL), lambda i, j: (0, 0)),
            pl.BlockSpec((D_MODEL, IN_TN), lambda i, j: (0, j)),
            pl.BlockSpec((D_MODEL, LANES), lambda i, j: (0, 0)),
            pl.BlockSpec((1, IN_TN), lambda i, j: (0, j)),
        ],
        out_specs=[
            pl.BlockSpec((IN_TM, IN_TN), lambda i, j: (i, j)),
            pl.BlockSpec((IN_TM, LANES), lambda i, j: (i, 0)),
        ],
        out_shape=[
            jax.ShapeDtypeStruct((N_TOK, MAIN_COLS), BF16),
            jax.ShapeDtypeStruct((N_TOK, LANES), F32),
        ],
        scratch_shapes=[pltpu.VMEM((IN_TM, D_MODEL), BF16)],
        compiler_params=_cparams(("arbitrary", "arbitrary")),
        name="inproj",
    )(x2, mod3, g, w_main, w_f, colscale)


CUM_TS = 512


def _cum_kernel(f_ref, bf_ref, tri_ref, o_ref, carry_ref):
    @pl.when(pl.program_id(1) == 0)
    def _():
        carry_ref[...] = jnp.zeros_like(carry_ref)

    lf = jax.nn.log_sigmoid(f_ref[...] + bf_ref[...])
    cs = jnp.dot(tri_ref[...], lf, preferred_element_type=F32,
                 precision=lax.Precision.HIGHEST) + carry_ref[...]
    o_ref[...] = cs * (-LOG2E)
    carry_ref[...] = cs[CUM_TS - 1:CUM_TS, :]


def _forget_bias(f, b_forget_row):
    tri = jnp.tril(jnp.ones((CUM_TS, CUM_TS), F32))
    nblk = SEQ // CUM_TS
    return pl.pallas_call(
        _cum_kernel,
        grid=(BATCH, nblk),
        in_specs=[
            pl.BlockSpec((CUM_TS, LANES), lambda b, i: (b * nblk + i, 0)),
            pl.BlockSpec((1, LANES), lambda b, i: (0, 0)),
            pl.BlockSpec((CUM_TS, CUM_TS), lambda b, i: (0, 0)),
        ],
        out_specs=pl.BlockSpec((CUM_TS, LANES), lambda b, i: (b * nblk + i, 0)),
        out_shape=jax.ShapeDtypeStruct((N_TOK, LANES), F32),
        scratch_shapes=[pltpu.VMEM((1, LANES), F32)],
        compiler_params=_cparams(("arbitrary", "arbitrary")),
        name="forget_bias",
    )(f, b_forget_row, tri)


SWA_TQ = WINDOW


def _swa_perm():
    perm = np.zeros((SWA_WIDTH,), np.int32)
    for t in range(2):
        for g in range(4):
            for e in range(2):
                head = (2 * t + e) * 4 + g
                base = (t * 4 + g) * LANES + e * HEAD_DIM
                perm[base:base + HEAD_DIM] = head * HEAD_DIM + np.arange(HEAD_DIM)
    return perm


def _swa_kernel(sinks_ref, q_ref, kc_ref, kp_ref, vc_ref, vp_ref, o_ref, bias_ref):
    i = pl.program_id(1)
    tq = SWA_TQ

    @pl.when((pl.program_id(0) == 0) & (i == 0))
    def _():
        row = lax.broadcasted_iota(jnp.int32, (tq, 2 * tq), 0)
        col = lax.broadcasted_iota(jnp.int32, (tq, 2 * tq), 1)
        dist = row + tq - col
        band = (dist >= 0) & (dist < WINDOW)
        distf = dist.astype(F32)
        for head in range(N_HEADS_SWA):
            slope2 = float(2.0 ** (-8.0 * (head + 1) / N_HEADS_SWA)) * LOG2E
            base = jnp.where(band, -slope2 * distf, -jnp.inf)
            bias_ref[1, head] = base
            bias_ref[0, head] = jnp.where(col >= tq, base, -jnp.inf)

    table = jnp.minimum(i, 1)
    lane = lax.broadcasted_iota(jnp.int32, (tq, LANES), 1)
    lo_half = lane < HEAD_DIM
    for t in range(2):
        kt = jnp.concatenate([kp_ref[:, t * LANES:(t + 1) * LANES],
                              kc_ref[:, t * LANES:(t + 1) * LANES]], axis=0)
        vt = jnp.concatenate([vp_ref[:, t * LANES:(t + 1) * LANES],
                              vc_ref[:, t * LANES:(t + 1) * LANES]], axis=0)
        heads = [(g, e) for g in range(4) for e in range(2)]
        scores = []
        for g, e in heads:
            tile = t * 4 + g
            qt = q_ref[:, tile * LANES:(tile + 1) * LANES]
            qm = jnp.where(lo_half if e == 0 else ~lo_half, qt, jnp.zeros_like(qt))
            scores.append(lax.dot_general(qm, kt, (((1,), (1,)), ((), ())),
                                          preferred_element_type=F32))
        probs, rdens = [], []
        for (g, e), s in zip(heads, scores):
            head = (2 * t + e) * 4 + g
            sink = sinks_ref[head] * LOG2E
            s = s + bias_ref[table, head]
            m = jnp.maximum(jnp.max(s, axis=-1, keepdims=True), sink)
            p = jnp.exp2(s - m)
            rdens.append(1.0 / (jnp.sum(p, axis=-1, keepdims=True) + jnp.exp2(sink - m)))
            probs.append(p.astype(BF16))
        outs = [jnp.dot(p, vt, preferred_element_type=F32) * r for p, r in zip(probs, rdens)]
        for g in range(4):
            tile = t * 4 + g
            o_ref[:, tile * LANES:(tile + 1) * LANES] = jnp.where(lo_half, outs[2 * g], outs[2 * g + 1])


def _swa(sinks, proj):
    nq = SEQ // SWA_TQ
    kblk = KA_COL // KV_WIDTH
    vblk = VA_COL // KV_WIDTH
    grid_spec = pltpu.PrefetchScalarGridSpec(
        num_scalar_prefetch=1,
        grid=(BATCH, nq),
        in_specs=[
            pl.BlockSpec((SWA_TQ, SWA_WIDTH), lambda b, i, s: (b * nq + i, 0)),
            pl.BlockSpec((SWA_TQ, KV_WIDTH), lambda b, i, s: (b * nq + i, kblk)),
            pl.BlockSpec((SWA_TQ, KV_WIDTH), lambda b, i, s: (b * nq + jnp.maximum(i - 1, 0), kblk)),
            pl.BlockSpec((SWA_TQ, KV_WIDTH), lambda b, i, s: (b * nq + i, vblk)),
            pl.BlockSpec((SWA_TQ, KV_WIDTH), lambda b, i, s: (b * nq + jnp.maximum(i - 1, 0), vblk)),
        ],
        out_specs=pl.BlockSpec((SWA_TQ, SWA_WIDTH), lambda b, i, s: (b * nq + i, 0)),
        scratch_shapes=[pltpu.VMEM((2, N_HEADS_SWA, SWA_TQ, 2 * SWA_TQ), F32)],
    )
    return pl.pallas_call(
        _swa_kernel,
        grid_spec=grid_spec,
        out_shape=jax.ShapeDtypeStruct((N_TOK, SWA_WIDTH), F32),
        compiler_params=_cparams(("arbitrary", "arbitrary")),
        name="swa_attn",
    )(sinks, proj, proj, proj, proj, proj)


FOX_T = 512
FOX_CH = 512
FOX_AUG = 3
FOX_ONES = 16
FOX_VROWS = HEAD_DIM + FOX_ONES


def _fox_kernel(q_ref, k_ref, vt_ref, bias_ref, o_ref, kaug_ref, mask_ref, t_ref, acc_ref):
    hp = pl.program_id(1)
    qi = pl.program_id(2)
    t = FOX_T

    @pl.when(qi == 0)
    def _():
        lane = lax.broadcasted_iota(jnp.int32, (FOX_CH, LANES), 1)

        def chunk(c, _):
            r0 = pl.multiple_of(c * FOX_CH, FOX_CH)
            blk = bias_ref[pl.ds(r0, FOX_CH), :]
            aug = jnp.zeros((FOX_CH, LANES), F32)
            for hh in range(2):
                colv = jnp.sum(jnp.where(lane == 2 * hp + hh, blk, 0.0), axis=1, keepdims=True)
                hi = colv.astype(BF16).astype(F32)
                r1 = colv - hi
                mid = r1.astype(BF16).astype(F32)
                lo = r1 - mid
                for piece, val in enumerate((hi, mid, lo)):
                    aug = jnp.where(lane == FOX_AUG * hh + piece, val, aug)
            kaug_ref[pl.ds(r0, FOX_CH), 0:LANES] = k_ref[pl.ds(r0, FOX_CH), :]
            kaug_ref[pl.ds(r0, FOX_CH), LANES:2 * LANES] = aug.astype(BF16)
            return 0

        lax.fori_loop(0, SEQ // FOX_CH, chunk, 0)
        kr = lax.broadcasted_iota(jnp.int32, (t, t), 0)
        qc = lax.broadcasted_iota(jnp.int32, (t, t), 1)
        mask_ref[...] = jnp.where(kr <= qc, 0.0, -jnp.inf)

    qf = q_ref[...].astype(F32).T
    drow = lax.broadcasted_iota(jnp.int32, (LANES, t), 0)
    qaug = []
    for hh in range(2):
        qh = jnp.where((drow >= HEAD_DIM * hh) & (drow < HEAD_DIM * (hh + 1)), qf, 0.0)
        ones = jnp.where((drow >= FOX_AUG * hh) & (drow < FOX_AUG * (hh + 1)), 1.0, 0.0)
        qaug.append(jnp.concatenate([qh, ones], axis=0).astype(BF16))
    acc_ref[...] = jnp.zeros_like(acc_ref)

    def stage_a(j, slot, diag):
        r0 = pl.multiple_of(j * t, t)
        kb = kaug_ref[pl.ds(r0, t), :]
        sts = [jnp.dot(kb, qaug[hh], preferred_element_type=F32) for hh in range(2)]
        mbs = []
        for hh in range(2):
            tt = sts[hh]
            if diag:
                tt = tt + mask_ref[...]
            t_ref[slot, hh] = tt
            mbs.append(jnp.max(tt, axis=0, keepdims=True))
        return tuple(mbs)

    def stage_b(jv, slot, ms, mbs):
        new, ps, alphas = [], [], []
        for hh in range(2):
            m_new = jnp.maximum(ms[hh], mbs[hh])
            alphas.append(jnp.exp2(ms[hh] - m_new))
            ps.append(jnp.exp2(t_ref[slot, hh] - m_new).astype(BF16))
            new.append(m_new)
        ones = jnp.ones((FOX_ONES, t), BF16)
        pvs = [jnp.dot(jnp.concatenate([vt_ref[0, 0, jv, hh * HEAD_DIM:(hh + 1) * HEAD_DIM, :], ones], axis=0),
                       ps[hh], preferred_element_type=F32)
               for hh in range(2)]
        for hh in range(2):
            acc_ref[hh] = alphas[hh] * acc_ref[hh] + pvs[hh]
        return tuple(new)

    ml0 = tuple(jnp.full((1, t), -jnp.inf, F32) for _ in range(2))
    mb0 = stage_a(qi, 0, True)

    def pair(ii, c):
        ml, mbs, jprev = c
        j0 = 2 * ii
        mb1 = stage_a(j0, 1, False)
        ml = stage_b(jprev, 0, ml, mbs)
        mb2 = stage_a(j0 + 1, 0, False)
        ml = stage_b(j0, 1, ml, mb1)
        return ml, mb2, j0 + 1

    ml, mbs, jprev = lax.fori_loop(0, qi // 2, pair, (ml0, mb0, qi))

    def odd_tail(c):
        ml, mbs, jprev = c
        mb1 = stage_a(qi - 1, 1, False)
        ml = stage_b(jprev, 0, ml, mbs)
        return stage_b(qi - 1, 1, ml, mb1)

    def even_tail(c):
        ml, mbs, jprev = c
        return stage_b(jprev, 0, ml, mbs)

    ml = lax.cond(qi % 2 == 1, odd_tail, even_tail, (ml, mbs, jprev))
    del ml
    ot = jnp.concatenate([acc_ref[hh, 0:HEAD_DIM, :] / acc_ref[hh, HEAD_DIM:HEAD_DIM + 1, :]
                          for hh in range(2)], axis=0)
    o_ref[...] = ot.T


def _fox(proj, vt5, bias):
    nq = SEQ // FOX_T
    npair = N_HEADS_FOX // 2
    return pl.pallas_call(
        _fox_kernel,
        grid=(BATCH, npair, nq),
        in_specs=[
            pl.BlockSpec((FOX_T, LANES), lambda b, h, i: (b * nq + i, QB_BLK + h)),
            pl.BlockSpec((SEQ, LANES), lambda b, h, i: (b, KB_BLK + h)),
            pl.BlockSpec((1, 1, nq, LANES, FOX_T), lambda b, h, i: (b, h, 0, 0, 0)),
            pl.BlockSpec((SEQ, LANES), lambda b, h, i: (b, 0)),
        ],
        out_specs=pl.BlockSpec((FOX_T, LANES), lambda b, h, i: (b * nq + i, h)),
        out_shape=jax.ShapeDtypeStruct((N_TOK, FOX_WIDTH), F32),
        scratch_shapes=[
            pltpu.VMEM((SEQ, 2 * LANES), BF16),
            pltpu.VMEM((FOX_T, FOX_T), F32),
            pltpu.VMEM((2, 2, FOX_T, FOX_T), F32),
            pltpu.VMEM((2, FOX_VROWS, FOX_T), F32),
        ],
        compiler_params=_cparams(("arbitrary", "arbitrary", "arbitrary")),
        name="fox_attn",
    )(proj, proj, vt5, bias)


OUT_TM = 256
SUBLANES = 8
PACKED_COLS = D_MODEL // 2
PACK_ROWS = PACKED_COLS // LANES
DMA_UNROLL = 8


def _outproj_kernel(oa_ref, ob_ref, x_ref, mod_ref, ga_ref, gb_ref, w_ref, gm_ref, wr2_ref, br_ref,
                    x1_ref, h2p_ref, ids_ref, gates_ref):
    na = _rms(oa_ref[...]) * ga_ref[...]
    nb = _rms(ob_ref[...]) * gb_ref[...]
    mixed = jnp.concatenate([na, nb], axis=1).astype(BF16)
    y = jnp.dot(mixed, w_ref[...], preferred_element_type=F32)
    x1 = x_ref[...] + mod_ref[0, 2:3, :] * y
    x1_ref[...] = x1
    h2 = (_rms(x1) * gm_ref[...]) * (1.0 + mod_ref[0, 4:5, :]) + mod_ref[0, 3:4, :]
    hb = h2.astype(BF16)
    hbf = hb.astype(F32)
    packed = _pack_bf16_pairs(h2)
    for s in range(PACK_ROWS):
        h2p_ref[:, s, :] = packed[:, s * LANES:(s + 1) * LANES]

    hl = (h2 - hbf).astype(BF16)
    r2 = jnp.dot(hb, wr2_ref[...], preferred_element_type=F32)
    logits = (r2[:, :LANES] + (r2[:, LANES:] + jnp.dot(hl, wr2_ref[:, :LANES], preferred_element_type=F32))
              + br_ref[...])
    lane = lax.broadcasted_iota(jnp.int32, logits.shape, 1)
    neg = -jnp.inf
    is_g = lane < N_GROUPS
    gl = jnp.where(is_g, logits, neg)
    gmax = jnp.max(gl, axis=1, keepdims=True)
    gsel = jnp.min(jnp.where(gl == gmax, lane, LANES), axis=1, keepdims=True)
    gsum = jnp.sum(jnp.where(is_g, jnp.exp(gl - gmax), 0.0), axis=1, keepdims=True)
    g_val = 1.0 / gsum
    elane = lane - N_GROUPS
    in_sel = (elane >= 0) & (elane < N_EXPERTS) & ((elane >> 3) == gsel)
    ev = jnp.where(in_sel, logits, neg)
    t1 = jnp.max(ev, axis=1, keepdims=True)
    i1 = jnp.min(jnp.where(ev == t1, lane, LANES), axis=1, keepdims=True)
    ev2 = jnp.where(lane == i1, neg, ev)
    t2 = jnp.max(ev2, axis=1, keepdims=True)
    i2 = jnp.min(jnp.where(ev2 == t2, lane, LANES), axis=1, keepdims=True)
    e2 = jnp.exp(t2 - t1)
    den = 1.0 + e2
    w1 = (1.0 / den) * g_val
    w2 = (e2 / den) * g_val
    ids_ref[...] = jnp.where(lane == 0, i1 - N_GROUPS, jnp.where(lane == 1, i2 - N_GROUPS, 0))
    gates_ref[...] = jnp.where(lane == 0, w1, jnp.where(lane == 1, w2, 0.0))


def _outproj(oa, ob, x2, mod3, ga, gb, w_out, gm, wr, br):
    tiles_per_batch = SEQ // OUT_TM
    row = lambda i: (i, 0)
    const = lambda i: (0, 0)
    wr_hi = wr.astype(BF16)
    wr_lo = (wr - wr_hi.astype(F32)).astype(BF16)
    wr2 = jnp.concatenate([wr_hi, wr_lo], axis=1)
    return pl.pallas_call(
        _outproj_kernel,
        grid=(N_TOK // OUT_TM,),
        in_specs=[
            pl.BlockSpec((OUT_TM, SWA_WIDTH), row),
            pl.BlockSpec((OUT_TM, FOX_WIDTH), row),
            pl.BlockSpec((OUT_TM, D_MODEL), row),
            pl.BlockSpec((1, 6, D_MODEL), lambda i: (i // tiles_per_batch, 0, 0)),
            pl.BlockSpec((1, SWA_WIDTH), const),
            pl.BlockSpec((1, FOX_WIDTH), const),
            pl.BlockSpec((D_MODEL, D_MODEL), const),
            pl.BlockSpec((1, D_MODEL), const),
            pl.BlockSpec((D_MODEL, 2 * LANES), const),
            pl.BlockSpec((1, LANES), const),
        ],
        out_specs=[
            pl.BlockSpec((OUT_TM, D_MODEL), row),
            pl.BlockSpec((OUT_TM, PACK_ROWS, LANES), lambda i: (i, 0, 0)),
            pl.BlockSpec((OUT_TM, LANES), row),
            pl.BlockSpec((OUT_TM, LANES), row),
        ],
        out_shape=[
            jax.ShapeDtypeStruct((N_TOK, D_MODEL), F32),
            jax.ShapeDtypeStruct((N_TOK, PACK_ROWS, LANES), jnp.uint32),
            jax.ShapeDtypeStruct((N_TOK, LANES), jnp.int32),
            jax.ShapeDtypeStruct((N_TOK, LANES), F32),
        ],
        compiler_params=_cparams(("arbitrary",)),
        name="outproj_router",
    )(oa, ob, x2, mod3, ga, gb, w_out, gm, wr2, br)


N_BLOCKS = (N_TOK * TOP_K) // MOE_BLOCK + N_EXPERTS
P_ROWS = N_BLOCKS * MOE_BLOCK


def _pack_bf16_pairs(x):
    xb = x.astype(BF16).astype(F32)
    half = x.shape[1] // 2
    lo = lax.shift_right_logical(pltpu.bitcast(xb[:, :half], jnp.uint32), jnp.uint32(16))
    hi = pltpu.bitcast(xb[:, half:], jnp.uint32) & jnp.uint32(0xFFFF0000)
    return lo | hi


def _unpack_bf16_pairs(w):
    return (pltpu.bitcast(lax.shift_left(w, jnp.uint32(16)), F32),
            pltpu.bitcast(w & jnp.uint32(0xFFFF0000), F32))


def _expert_kernel(be_ref, tok_ref, nused_ref, nxt_ref, h2_hbm, wg_hbm, wu_hbm, wd_hbm, y_ref, xbuf, sem,
                   wgf, wuf, wdf, wsem, wgb, wub, wdb):
    i = pl.program_id(0)
    nused = nused_ref[0]

    def weight_copies(e):
        return (pltpu.make_async_copy(wg_hbm.at[e], wgf, wsem.at[0]),
                pltpu.make_async_copy(wu_hbm.at[e], wuf, wsem.at[1]),
                pltpu.make_async_copy(wd_hbm.at[e], wdf, wsem.at[2]))

    def issue(blk, slot):
        def body(g, _):
            for q in range(SUBLANES):
                tok = tok_ref[blk * MOE_BLOCK + g * SUBLANES + q]
                pltpu.make_async_copy(h2_hbm.at[tok], xbuf.at[slot, g, :, q, :], sem.at[slot]).start(
                    priority=q % 2)
            return 0
        lax.fori_loop(0, MOE_BLOCK // SUBLANES, body, 0)

    @pl.when(i == 0)
    def _():
        issue(0, 0)

    @pl.when(i + 1 < nused)
    def _():
        issue(i + 1, (i + 1) % 2)

    e = be_ref[i]
    new_expert = ((i == 0) | (e != be_ref[jnp.maximum(i - 1, 0)])) & (i < nused)

    @pl.when(i == 0)
    def _():
        for cp in weight_copies(e):
            cp.start()

    @pl.when(new_expert)
    def _():
        for cp in weight_copies(e):
            cp.wait()
        wgb[...] = wgf[...].astype(BF16)
        wub[...] = wuf[...].astype(BF16)
        wdb[...] = wdf[...].astype(BF16)

    @pl.when(new_expert & (nxt_ref[e] >= 0))
    def _():
        for cp in weight_copies(nxt_ref[e]):
            cp.start()

    @pl.when(i < nused)
    def _():
        slot = i % 2
        pltpu.make_async_copy(xbuf.at[slot], xbuf.at[slot], sem.at[slot]).wait()
        xu = jnp.concatenate([xbuf[slot, :, s].reshape(MOE_BLOCK, LANES) for s in range(PACK_ROWS)],
                             axis=1)
        xb = jnp.concatenate(_unpack_bf16_pairs(xu), axis=1).astype(BF16)
        g = jnp.dot(xb, wgb[...], preferred_element_type=F32)
        u = jnp.dot(xb, wub[...], preferred_element_type=F32)
        hid = (jax.nn.silu(g) * u).astype(BF16)
        y_ref[...] = _pack_bf16_pairs(jnp.dot(hid, wdb[...], preferred_element_type=F32))

    @pl.when(i >= nused)
    def _():
        y_ref[...] = jnp.zeros_like(y_ref)


def _experts(block_e, row_tok, nused, next_used, h2p, wg, wu, wd):
    grid_spec = pltpu.PrefetchScalarGridSpec(
        num_scalar_prefetch=4,
        grid=(N_BLOCKS,),
        in_specs=[pl.BlockSpec(memory_space=pl.ANY)] * 4,
        out_specs=pl.BlockSpec((MOE_BLOCK, PACKED_COLS), lambda i, be, tk, nu, nx: (i, 0)),
        scratch_shapes=[pltpu.VMEM((2, MOE_BLOCK // SUBLANES, PACK_ROWS, SUBLANES, LANES), jnp.uint32),
                        pltpu.SemaphoreType.DMA((2,)),
                        pltpu.VMEM((D_MODEL, D_EXPERT), F32),
                        pltpu.VMEM((D_MODEL, D_EXPERT), F32),
                        pltpu.VMEM((D_EXPERT, D_MODEL), F32),
                        pltpu.SemaphoreType.DMA((3,)),
                        pltpu.VMEM((D_MODEL, D_EXPERT), BF16),
                        pltpu.VMEM((D_MODEL, D_EXPERT), BF16),
                        pltpu.VMEM((D_EXPERT, D_MODEL), BF16)],
    )
    return pl.pallas_call(
        _expert_kernel,
        grid_spec=grid_spec,
        out_shape=jax.ShapeDtypeStruct((P_ROWS, PACKED_COLS), jnp.uint32),
        compiler_params=_cparams(("arbitrary",)),
        name="expert_ffn",
    )(block_e, row_tok, nused, next_used, h2p, wg, wu, wd)


CMB_TM = 128


def _combine_kernel(dest_ref, ys_hbm, x1_ref, gates_ref, mod_ref, fg_ref, o_ref, ybuf, sem):
    i = pl.program_id(0)
    n = pl.num_programs(0)

    def issue(tile, slot):
        def body(g, _):
            for q in range(DMA_UNROLL):
                r = g * DMA_UNROLL + q
                for k in range(TOP_K):
                    d = dest_ref[(tile * CMB_TM + r) * TOP_K + k]
                    pltpu.make_async_copy(ys_hbm.at[pl.ds(d, 1), :],
                                          ybuf.at[slot, pl.ds(k * CMB_TM + r, 1), :],
                                          sem.at[slot]).start(priority=k)
            return 0
        lax.fori_loop(0, CMB_TM // DMA_UNROLL, body, 0)

    @pl.when(i == 0)
    def _():
        issue(0, 0)

    @pl.when(i + 1 < n)
    def _():
        issue(i + 1, (i + 1) % 2)

    slot = i % 2
    pltpu.make_async_copy(ybuf.at[slot], ybuf.at[slot], sem.at[slot]).wait()
    gts = gates_ref[...]
    w0 = gts[:, 0:1]
    w1 = gts[:, 1:2]
    y0 = _unpack_bf16_pairs(ybuf[slot, 0:CMB_TM, :])
    y1 = _unpack_bf16_pairs(ybuf[slot, CMB_TM:2 * CMB_TM, :])
    y = jnp.concatenate([y0[part] * w0 + y1[part] * w1 for part in range(2)], axis=1)
    x2 = x1_ref[...] + mod_ref[0, 5:6, :] * y
    o_ref[...] = _rms(x2) * fg_ref[...]


def _combine(dest, ys, x1, gates, mod3, final_g):
    tiles_per_batch = SEQ // CMB_TM
    grid_spec = pltpu.PrefetchScalarGridSpec(
        num_scalar_prefetch=1,
        grid=(N_TOK // CMB_TM,),
        in_specs=[
            pl.BlockSpec(memory_space=pl.ANY),
            pl.BlockSpec((CMB_TM, D_MODEL), lambda i, d: (i, 0)),
            pl.BlockSpec((CMB_TM, LANES), lambda i, d: (i, 0)),
            pl.BlockSpec((1, 6, D_MODEL), lambda i, d: (i // tiles_per_batch, 0, 0)),
            pl.BlockSpec((1, D_MODEL), lambda i, d: (0, 0)),
        ],
        out_specs=pl.BlockSpec((CMB_TM, D_MODEL), lambda i, d: (i, 0)),
        scratch_shapes=[pltpu.VMEM((2, TOP_K * CMB_TM, PACKED_COLS), jnp.uint32),
                        pltpu.SemaphoreType.DMA((2,))],
    )
    return pl.pallas_call(
        _combine_kernel,
        grid_spec=grid_spec,
        out_shape=jax.ShapeDtypeStruct((N_TOK, D_MODEL), F32),
        compiler_params=_cparams(("arbitrary",)),
        name="combine_final",
    )(dest, ys, x1, gates, mod3, final_g)


def _routing_tables(eid):
    a = eid.shape[0]
    onehot = (eid[:, None] == jnp.arange(N_EXPERTS, dtype=jnp.int32)[None, :]).astype(jnp.int32)
    csum = jnp.cumsum(onehot, axis=0)
    rank = jnp.sum(onehot * csum, axis=1) - 1
    counts = csum[-1]
    padded = (counts + MOE_BLOCK - 1) // MOE_BLOCK * MOE_BLOCK
    pend = jnp.cumsum(padded)
    pstart = pend - padded
    dest = (pstart[eid] + rank).astype(jnp.int32)
    tok = jnp.arange(a, dtype=jnp.int32) // TOP_K
    row_tok = jnp.zeros((P_ROWS,), jnp.int32).at[dest].set(tok)
    blk_row = jnp.arange(N_BLOCKS, dtype=jnp.int32) * MOE_BLOCK
    block_e = jnp.minimum(jnp.sum((pend[None, :] <= blk_row[:, None]).astype(jnp.int32), axis=1),
                          N_EXPERTS - 1).astype(jnp.int32)
    nused = (pend[-1] // MOE_BLOCK).astype(jnp.int32).reshape(1)
    ids = jnp.arange(N_EXPERTS, dtype=jnp.int32)
    later_used = (ids[None, :] > ids[:, None]) & (counts[None, :] > 0)
    next_used = jnp.min(jnp.where(later_used, ids[None, :], N_EXPERTS), axis=1)
    next_used = jnp.where(next_used == N_EXPERTS, -1, next_used).astype(jnp.int32)
    return dest, row_tok, block_e, nused, next_used


def kernel(x, c, w_ada, b_ada, norm_mix_g, w_in, b_forget, sinks, out_norm_swa_g, out_norm_fox_g, w_out,
           norm_moe_g, w_group, b_group, w_expert, b_expert, w_gate, w_up, w_down, final_g):
    x2 = x.reshape(N_TOK, D_MODEL)
    perm = _swa_perm()

    w_in0 = w_in[0]
    w_main = jnp.concatenate([w_in0[:, :SWA_WIDTH][:, perm], w_in0[:, SWA_WIDTH:MAIN_COLS]], axis=1).astype(BF16)
    w_f = jnp.pad(w_in0[:, MAIN_COLS:], ((0, 0), (0, LANES - N_HEADS_FOX))).astype(BF16)
    colscale = jnp.concatenate([
        jnp.full((SWA_WIDTH,), HEAD_DIM ** -0.5 * LOG2E, F32),
        jnp.ones((2 * KV_WIDTH,), F32),
        jnp.full((FOX_WIDTH,), HEAD_DIM ** -0.5 * LOG2E, F32),
        jnp.ones((2 * FOX_WIDTH,), F32)]).reshape(1, MAIN_COLS)
    bf_row = jnp.pad(b_forget[0], (0, LANES - N_HEADS_FOX)).reshape(1, LANES)
    w_out_p = jnp.concatenate([w_out[0][:SWA_WIDTH][perm], w_out[0][SWA_WIDTH:]], axis=0).astype(BF16)
    ga = out_norm_swa_g[0][perm].reshape(1, SWA_WIDTH)
    gb = out_norm_fox_g[0].reshape(1, FOX_WIDTH)
    wr = jnp.pad(jnp.concatenate([w_group[0], w_expert[0]], axis=1),
                 ((0, 0), (0, LANES - N_GROUPS - N_EXPERTS)))
    br = jnp.pad(jnp.concatenate([b_group[0], b_expert[0]]), (0, LANES - N_GROUPS - N_EXPERTS)).reshape(1, LANES)
    wg, wu, wd = w_gate[0], w_up[0], w_down[0]

    mod = _adaln(c, w_ada[0], b_ada[0])
    mod3 = mod.reshape(BATCH, 6, D_MODEL)

    proj, f = _inproj(x2, mod3, norm_mix_g[0].reshape(1, D_MODEL), w_main, w_f, colscale)
    bias = _forget_bias(f, bf_row)

    o_a = _swa(sinks[0], proj)
    nk = SEQ // FOX_T
    vt5 = proj[:, VB_COL:].reshape(BATCH, nk, FOX_T, N_HEADS_FOX // 2, LANES).transpose(0, 3, 1, 4, 2)
    o_b = _fox(proj, vt5, bias)

    x1, h2, ids, gates = _outproj(o_a, o_b, x2, mod3, ga, gb, w_out_p,
                                  norm_moe_g[0].reshape(1, D_MODEL), wr, br)

    eid = ids[:, :TOP_K].reshape(-1)
    dest, row_tok, block_e, nused, next_used = _routing_tables(eid)
    ys = _experts(block_e, row_tok, nused, next_used, h2, wg, wu, wd)
    out = _combine(dest, ys, x1, gates, mod3, final_g.reshape(1, D_MODEL))
    return out.reshape(BATCH, SEQ, D_MODEL)
```

```python
import functools
import math

import numpy as np
import jax
import jax.numpy as jnp
from jax import lax
from jax.experimental import pallas as pl
from jax.experimental.pallas import tpu as pltpu

F32 = jnp.float32
BF16 = jnp.bfloat16

D_MODEL = 2048
BATCH = 2
SEQ = 8192
N_TOK = BATCH * SEQ
HEAD_DIM = 64
N_HEADS_SWA = 16
N_KV_SWA = 4
N_HEADS_FOX = 16
WINDOW = 128
SWA_WIDTH = N_HEADS_SWA * HEAD_DIM
KV_WIDTH = N_KV_SWA * HEAD_DIM
FOX_WIDTH = N_HEADS_FOX * HEAD_DIM
MAIN_COLS = SWA_WIDTH + 2 * KV_WIDTH + 3 * FOX_WIDTH
N_GROUPS = 4
EXPERTS_PER_GROUP = 8
N_EXPERTS = N_GROUPS * EXPERTS_PER_GROUP
TOP_K = 2
D_EXPERT = 512
MOE_BLOCK = 256
EPS = 1e-6
LOG2E = math.log2(math.e)

LANES = 128
VMEM_LIMIT = 56 * 1024 * 1024

QA_BLK = 0
KA_COL = SWA_WIDTH
VA_COL = SWA_WIDTH + KV_WIDTH
QB_BLK = (SWA_WIDTH + 2 * KV_WIDTH) // LANES
KB_BLK = QB_BLK + FOX_WIDTH // LANES
VB_COL = SWA_WIDTH + 2 * KV_WIDTH + 2 * FOX_WIDTH
ROW_COLS = VB_COL


def _cparams(sem, vmem=VMEM_LIMIT):
    return pltpu.CompilerParams(dimension_semantics=sem, vmem_limit_bytes=vmem)


ADA_TN = 1024


def _adaln_kernel(cb_ref, w_ref, b_ref, o_ref):
    for b in range(BATCH):
        s = jax.nn.silu(cb_ref[b])
        cols = []
        for j in range(ADA_TN // LANES):
            prod = w_ref[:, j * LANES:(j + 1) * LANES] * s
            cols.append(jnp.sum(prod, axis=0, keepdims=True))
        o_ref[b:b + 1, :] = jnp.concatenate(cols, axis=1) + b_ref[...]


def _adaln(c, w_ada, b_ada):
    ncol = w_ada.shape[1]
    cb = jnp.broadcast_to(c[:, :, None], (BATCH, D_MODEL, LANES))
    return pl.pallas_call(
        _adaln_kernel,
        grid=(ncol // ADA_TN,),
        in_specs=[
            pl.BlockSpec((BATCH, D_MODEL, LANES), lambda j: (0, 0, 0)),
            pl.BlockSpec((D_MODEL, ADA_TN), lambda j: (0, j)),
            pl.BlockSpec((1, ADA_TN), lambda j: (0, j)),
        ],
        out_specs=pl.BlockSpec((BATCH, ADA_TN), lambda j: (0, j)),
        out_shape=jax.ShapeDtypeStruct((BATCH, ncol), F32),
        compiler_params=_cparams(("arbitrary",)),
        name="adaln",
    )(cb, w_ada, b_ada.reshape(1, ncol))


def _rms(x):
    return x * lax.rsqrt(jnp.mean(x * x, axis=-1, keepdims=True) + EPS)


IN_TM = 512
IN_TN = 1792
IN_NJ = ROW_COLS // IN_TN


def _inproj_kernel(x_ref, mod_ref, g_ref, w_ref, wf_ref, cs_ref, wvt_ref, o_ref, f_ref, vt_ref, h_ref):
    j = pl.program_id(1)

    @pl.when(j == 0)
    def _():
        y = _rms(x_ref[...])
        h = (y * g_ref[...]) * (1.0 + mod_ref[0, 1:2, :]) + mod_ref[0, 0:1, :]
        hb = h.astype(BF16)
        h_ref[...] = hb
        f_ref[...] = jnp.dot(hb, wf_ref[...], preferred_element_type=F32)

    @pl.when(j < IN_NJ)
    def _():
        acc = jnp.dot(h_ref[...], w_ref[...], preferred_element_type=F32)
        o_ref[...] = (acc * cs_ref[...]).astype(BF16)

    @pl.when(j == IN_NJ)
    def _():
        vt = lax.dot_general(wvt_ref[...], h_ref[...], (((1,), (1,)), ((), ())),
                             preferred_element_type=F32)
        vt_ref[0, :, 0] = vt.reshape(N_HEADS_FOX // 2, LANES, IN_TM).astype(BF16)


def _inproj(x2, mod3, g, w_row, w_f, colscale, w_vt):
    tiles_per_batch = SEQ // IN_TM
    last = IN_NJ - 1
    return pl.pallas_call(
        _inproj_kernel,
        grid=(N_TOK // IN_TM, IN_NJ + 1),
        in_specs=[
            pl.BlockSpec((IN_TM, D_MODEL), lambda i, j: (i, 0)),
            pl.BlockSpec((1, 6, D_MODEL), lambda i, j: (i // tiles_per_batch, 0, 0)),
            pl.BlockSpec((1, D_MODEL), lambda i, j: (0, 0)),
            pl.BlockSpec((D_MODEL, IN_TN), lambda i, j: (0, jnp.minimum(j, last))),
            pl.BlockSpec((D_MODEL, LANES), lambda i, j: (0, 0)),
            pl.BlockSpec((1, IN_TN), lambda i, j: (0, jnp.minimum(j, last))),
            pl.BlockSpec((FOX_WIDTH, D_MODEL), lambda i, j: (0, 0)),
        ],
        out_specs=[
            pl.BlockSpec((IN_TM, IN_TN), lambda i, j: (i, jnp.minimum(j, last))),
            pl.BlockSpec((IN_TM, LANES), lambda i, j: (i, 0)),
            pl.BlockSpec((1, N_HEADS_FOX // 2, 1, LANES, IN_TM),
                         lambda i, j: (i // tiles_per_batch, 0, i % tiles_per_batch, 0, 0)),
        ],
        out_shape=[
            jax.ShapeDtypeStruct((N_TOK, ROW_COLS), BF16),
            jax.ShapeDtypeStruct((N_TOK, LANES), F32),
            jax.ShapeDtypeStruct((BATCH, N_HEADS_FOX // 2, SEQ // IN_TM, LANES, IN_TM), BF16),
        ],
        scratch_shapes=[pltpu.VMEM((IN_TM, D_MODEL), BF16)],
        compiler_params=_cparams(("arbitrary", "arbitrary")),
        name="inproj",
    )(x2, mod3, g, w_row, w_f, colscale, w_vt)


CUM_TS = 512


def _cum_kernel(f_ref, bf_ref, tri_ref, o_ref, carry_ref):
    @pl.when(pl.program_id(1) == 0)
    def _():
        carry_ref[...] = jnp.zeros_like(carry_ref)

    lf = jax.nn.log_sigmoid(f_ref[...] + bf_ref[...])
    cs = jnp.dot(tri_ref[...], lf, preferred_element_type=F32,
                 precision=lax.Precision.HIGHEST) + carry_ref[...]
    o_ref[...] = cs * (-LOG2E)
    carry_ref[...] = cs[CUM_TS - 1:CUM_TS, :]


def _forget_bias(f, b_forget_row):
    tri = jnp.tril(jnp.ones((CUM_TS, CUM_TS), F32))
    nblk = SEQ // CUM_TS
    return pl.pallas_call(
        _cum_kernel,
        grid=(BATCH, nblk),
        in_specs=[
            pl.BlockSpec((CUM_TS, LANES), lambda b, i: (b * nblk + i, 0)),
            pl.BlockSpec((1, LANES), lambda b, i: (0, 0)),
            pl.BlockSpec((CUM_TS, CUM_TS), lambda b, i: (0, 0)),
        ],
        out_specs=pl.BlockSpec((CUM_TS, LANES), lambda b, i: (b * nblk + i, 0)),
        out_shape=jax.ShapeDtypeStruct((N_TOK, LANES), F32),
        scratch_shapes=[pltpu.VMEM((1, LANES), F32)],
        compiler_params=_cparams(("arbitrary", "arbitrary")),
        name="forget_bias",
    )(f, b_forget_row, tri)


SWA_TQ = WINDOW


def _swa_perm():
    perm = np.zeros((SWA_WIDTH,), np.int32)
    for t in range(2):
        for g in range(4):
            for e in range(2):
                head = (2 * t + e) * 4 + g
                base = (t * 4 + g) * LANES + e * HEAD_DIM
                perm[base:base + HEAD_DIM] = head * HEAD_DIM + np.arange(HEAD_DIM)
    return perm


def _swa_kernel(sinks_ref, q_ref, kc_ref, kp_ref, vc_ref, vp_ref, o_ref, bias_ref):
    i = pl.program_id(1)
    tq = SWA_TQ

    @pl.when((pl.program_id(0) == 0) & (i == 0))
    def _():
        row = lax.broadcasted_iota(jnp.int32, (tq, 2 * tq), 0)
        col = lax.broadcasted_iota(jnp.int32, (tq, 2 * tq), 1)
        dist = row + tq - col
        band = (dist >= 0) & (dist < WINDOW)
        distf = dist.astype(F32)
        for head in range(N_HEADS_SWA):
            slope2 = float(2.0 ** (-8.0 * (head + 1) / N_HEADS_SWA)) * LOG2E
            base = jnp.where(band, -slope2 * distf, -jnp.inf)
            bias_ref[1, head] = base
            bias_ref[0, head] = jnp.where(col >= tq, base, -jnp.inf)

    table = jnp.minimum(i, 1)
    lane = lax.broadcasted_iota(jnp.int32, (tq, LANES), 1)
    lo_half = lane < HEAD_DIM
    for t in range(2):
        kt = jnp.concatenate([kp_ref[:, t * LANES:(t + 1) * LANES],
                              kc_ref[:, t * LANES:(t + 1) * LANES]], axis=0)
        vt = jnp.concatenate([vp_ref[:, t * LANES:(t + 1) * LANES],
                              vc_ref[:, t * LANES:(t + 1) * LANES]], axis=0)
        heads = [(g, e) for g in range(4) for e in range(2)]
        scores = []
        for g, e in heads:
            tile = t * 4 + g
            qt = q_ref[:, tile * LANES:(tile + 1) * LANES]
            qm = jnp.where(lo_half if e == 0 else ~lo_half, qt, jnp.zeros_like(qt))
            scores.append(lax.dot_general(qm, kt, (((1,), (1,)), ((), ())),
                                          preferred_element_type=F32))
        probs, rdens = [], []
        for (g, e), s in zip(heads, scores):
            head = (2 * t + e) * 4 + g
            sink = sinks_ref[head] * LOG2E
            s = s + bias_ref[table, head]
            m = jnp.maximum(jnp.max(s, axis=-1, keepdims=True), sink)
            p = jnp.exp2(s - m)
            rdens.append(1.0 / (jnp.sum(p, axis=-1, keepdims=True) + jnp.exp2(sink - m)))
            probs.append(p.astype(BF16))
        outs = [jnp.dot(p, vt, preferred_element_type=F32) * r for p, r in zip(probs, rdens)]
        for g in range(4):
            tile = t * 4 + g
            o_ref[:, tile * LANES:(tile + 1) * LANES] = jnp.where(lo_half, outs[2 * g], outs[2 * g + 1])


def _swa(sinks, proj):
    nq = SEQ // SWA_TQ
    kblk = KA_COL // KV_WIDTH
    vblk = VA_COL // KV_WIDTH
    grid_spec = pltpu.PrefetchScalarGridSpec(
        num_scalar_prefetch=1,
        grid=(BATCH, nq),
        in_specs=[
            pl.BlockSpec((SWA_TQ, SWA_WIDTH), lambda b, i, s: (b * nq + i, 0)),
            pl.BlockSpec((SWA_TQ, KV_WIDTH), lambda b, i, s: (b * nq + i, kblk)),
            pl.BlockSpec((SWA_TQ, KV_WIDTH), lambda b, i, s: (b * nq + jnp.maximum(i - 1, 0), kblk)),
            pl.BlockSpec((SWA_TQ, KV_WIDTH), lambda b, i, s: (b * nq + i, vblk)),
            pl.BlockSpec((SWA_TQ, KV_WIDTH), lambda b, i, s: (b * nq + jnp.maximum(i - 1, 0), vblk)),
        ],
        out_specs=pl.BlockSpec((SWA_TQ, SWA_WIDTH), lambda b, i, s: (b * nq + i, 0)),
        scratch_shapes=[pltpu.VMEM((2, N_HEADS_SWA, SWA_TQ, 2 * SWA_TQ), F32)],
    )
    return pl.pallas_call(
        _swa_kernel,
        grid_spec=grid_spec,
        out_shape=jax.ShapeDtypeStruct((N_TOK, SWA_WIDTH), F32),
        compiler_params=_cparams(("arbitrary", "arbitrary")),
        name="swa_attn",
    )(sinks, proj, proj, proj, proj, proj)


FOX_T = 512
FOX_CH = 512
FOX_AUG = 3
FOX_ONES = 16
FOX_VROWS = HEAD_DIM + FOX_ONES


def _fox_kernel(q_ref, k_ref, vt_ref, bias_ref, o_ref, kaug_ref, mask_ref, t_ref, acc_ref):
    hp = pl.program_id(1)
    qi = pl.program_id(2)
    t = FOX_T

    @pl.when(qi == 0)
    def _():
        lane = lax.broadcasted_iota(jnp.int32, (FOX_CH, LANES), 1)

        def chunk(c, _):
            r0 = pl.multiple_of(c * FOX_CH, FOX_CH)
            blk = bias_ref[pl.ds(r0, FOX_CH), :]
            aug = jnp.zeros((FOX_CH, LANES), F32)
            for hh in range(2):
                colv = jnp.sum(jnp.where(lane == 2 * hp + hh, blk, 0.0), axis=1, keepdims=True)
                hi = colv.astype(BF16).astype(F32)
                r1 = colv - hi
                mid = r1.astype(BF16).astype(F32)
                lo = r1 - mid
                for piece, val in enumerate((hi, mid, lo)):
                    aug = jnp.where(lane == FOX_AUG * hh + piece, val, aug)
            kaug_ref[pl.ds(r0, FOX_CH), 0:LANES] = k_ref[pl.ds(r0, FOX_CH), :]
            kaug_ref[pl.ds(r0, FOX_CH), LANES:2 * LANES] = aug.astype(BF16)
            return 0

        lax.fori_loop(0, SEQ // FOX_CH, chunk, 0)
        kr = lax.broadcasted_iota(jnp.int32, (t, t), 0)
        qc = lax.broadcasted_iota(jnp.int32, (t, t), 1)
        mask_ref[...] = jnp.where(kr <= qc, 0.0, -jnp.inf)

    qf = q_ref[...].astype(F32).T
    drow = lax.broadcasted_iota(jnp.int32, (LANES, t), 0)
    qaug = []
    for hh in range(2):
        qh = jnp.where((drow >= HEAD_DIM * hh) & (drow < HEAD_DIM * (hh + 1)), qf, 0.0)
        ones = jnp.where((drow >= FOX_AUG * hh) & (drow < FOX_AUG * (hh + 1)), 1.0, 0.0)
        qaug.append(jnp.concatenate([qh, ones], axis=0).astype(BF16))
    acc_ref[...] = jnp.zeros_like(acc_ref)

    def stage_a(j, slot, diag):
        r0 = pl.multiple_of(j * t, t)
        kb = kaug_ref[pl.ds(r0, t), :]
        sts = [jnp.dot(kb, qaug[hh], preferred_element_type=F32) for hh in range(2)]
        mbs = []
        for hh in range(2):
            tt = sts[hh]
            if diag:
                tt = tt + mask_ref[...]
            t_ref[slot, hh] = tt
            mbs.append(jnp.max(tt, axis=0, keepdims=True))
        return tuple(mbs)

    def stage_b(jv, slot, ms, mbs):
        new, ps, alphas = [], [], []
        for hh in range(2):
            m_new = jnp.maximum(ms[hh], mbs[hh])
            alphas.append(jnp.exp2(ms[hh] - m_new))
            ps.append(jnp.exp2(t_ref[slot, hh] - m_new).astype(BF16))
            new.append(m_new)
        ones = jnp.ones((FOX_ONES, t), BF16)
        pvs = [jnp.dot(jnp.concatenate([vt_ref[0, 0, jv, hh * HEAD_DIM:(hh + 1) * HEAD_DIM, :], ones], axis=0),
                       ps[hh], preferred_element_type=F32)
               for hh in range(2)]
        for hh in range(2):
            acc_ref[hh] = alphas[hh] * acc_ref[hh] + pvs[hh]
        return tuple(new)

    ml0 = tuple(jnp.full((1, t), -jnp.inf, F32) for _ in range(2))
    mb0 = stage_a(qi, 0, True)

    def pair(ii, c):
        ml, mbs, jprev = c
        j0 = 2 * ii
        mb1 = stage_a(j0, 1, False)
        ml = stage_b(jprev, 0, ml, mbs)
        mb2 = stage_a(j0 + 1, 0, False)
        ml = stage_b(j0, 1, ml, mb1)
        return ml, mb2, j0 + 1

    ml, mbs, jprev = lax.fori_loop(0, qi // 2, pair, (ml0, mb0, qi))

    def odd_tail(c):
        ml, mbs, jprev = c
        mb1 = stage_a(qi - 1, 1, False)
        ml = stage_b(jprev, 0, ml, mbs)
        return stage_b(qi - 1, 1, ml, mb1)

    def even_tail(c):
        ml, mbs, jprev = c
        return stage_b(jprev, 0, ml, mbs)

    ml = lax.cond(qi % 2 == 1, odd_tail, even_tail, (ml, mbs, jprev))
    del ml
    ot = jnp.concatenate([acc_ref[hh, 0:HEAD_DIM, :] / acc_ref[hh, HEAD_DIM:HEAD_DIM + 1, :]
                          for hh in range(2)], axis=0)
    o_ref[...] = ot.T


def _fox(proj, vt5, bias):
    nq = SEQ // FOX_T
    npair = N_HEADS_FOX // 2
    return pl.pallas_call(
        _fox_kernel,
        grid=(BATCH, npair, nq),
        in_specs=[
            pl.BlockSpec((FOX_T, LANES), lambda b, h, i: (b * nq + i, QB_BLK + h)),
            pl.BlockSpec((SEQ, LANES), lambda b, h, i: (b, KB_BLK + h)),
            pl.BlockSpec((1, 1, nq, LANES, FOX_T), lambda b, h, i: (b, h, 0, 0, 0)),
            pl.BlockSpec((SEQ, LANES), lambda b, h, i: (b, 0)),
        ],
        out_specs=pl.BlockSpec((FOX_T, LANES), lambda b, h, i: (b * nq + i, h)),
        out_shape=jax.ShapeDtypeStruct((N_TOK, FOX_WIDTH), F32),
        scratch_shapes=[
            pltpu.VMEM((SEQ, 2 * LANES), BF16),
            pltpu.VMEM((FOX_T, FOX_T), F32),
            pltpu.VMEM((2, 2, FOX_T, FOX_T), F32),
            pltpu.VMEM((2, FOX_VROWS, FOX_T), F32),
        ],
        compiler_params=_cparams(("arbitrary", "arbitrary", "arbitrary")),
        name="fox_attn",
    )(proj, proj, vt5, bias)


OUT_TM = 256
SUBLANES = 8
PACKED_COLS = D_MODEL // 2
PACK_ROWS = PACKED_COLS // LANES
DMA_UNROLL = 8


def _outproj_kernel(oa_ref, ob_ref, x_ref, mod_ref, ga_ref, gb_ref, w_ref, gm_ref, wr2_ref, br_ref,
                    x1_ref, h2p_ref, ids_ref, gates_ref):
    na = _rms(oa_ref[...]) * ga_ref[...]
    nb = _rms(ob_ref[...]) * gb_ref[...]
    mixed = jnp.concatenate([na, nb], axis=1).astype(BF16)
    y = jnp.dot(mixed, w_ref[...], preferred_element_type=F32)
    x1 = x_ref[...] + mod_ref[0, 2:3, :] * y
    x1_ref[...] = x1
    h2 = (_rms(x1) * gm_ref[...]) * (1.0 + mod_ref[0, 4:5, :]) + mod_ref[0, 3:4, :]
    hb = h2.astype(BF16)
    hbf = hb.astype(F32)
    packed = _pack_bf16_pairs(h2)
    for s in range(PACK_ROWS):
        h2p_ref[:, s, :] = packed[:, s * LANES:(s + 1) * LANES]

    hl = (h2 - hbf).astype(BF16)
    r2 = jnp.dot(hb, wr2_ref[...], preferred_element_type=F32)
    logits = (r2[:, :LANES] + (r2[:, LANES:] + jnp.dot(hl, wr2_ref[:, :LANES], preferred_element_type=F32))
              + br_ref[...])
    lane = lax.broadcasted_iota(jnp.int32, logits.shape, 1)
    neg = -jnp.inf
    is_g = lane < N_GROUPS
    gl = jnp.where(is_g, logits, neg)
    gmax = jnp.max(gl, axis=1, keepdims=True)
    gsel = jnp.min(jnp.where(gl == gmax, lane, LANES), axis=1, keepdims=True)
    gsum = jnp.sum(jnp.where(is_g, jnp.exp(gl - gmax), 0.0), axis=1, keepdims=True)
    g_val = 1.0 / gsum
    elane = lane - N_GROUPS
    in_sel = (elane >= 0) & (elane < N_EXPERTS) & ((elane >> 3) == gsel)
    ev = jnp.where(in_sel, logits, neg)
    t1 = jnp.max(ev, axis=1, keepdims=True)
    i1 = jnp.min(jnp.where(ev == t1, lane, LANES), axis=1, keepdims=True)
    ev2 = jnp.where(lane == i1, neg, ev)
    t2 = jnp.max(ev2, axis=1, keepdims=True)
    i2 = jnp.min(jnp.where(ev2 == t2, lane, LANES), axis=1, keepdims=True)
    e2 = jnp.exp(t2 - t1)
    den = 1.0 + e2
    w1 = (1.0 / den) * g_val
    w2 = (e2 / den) * g_val
    ids_ref[...] = jnp.where(lane == 0, i1 - N_GROUPS, jnp.where(lane == 1, i2 - N_GROUPS, 0))
    gates_ref[...] = jnp.where(lane == 0, w1, jnp.where(lane == 1, w2, 0.0))


def _outproj(oa, ob, x2, mod3, ga, gb, w_out, gm, wr, br):
    tiles_per_batch = SEQ // OUT_TM
    row = lambda i: (i, 0)
    const = lambda i: (0, 0)
    wr_hi = wr.astype(BF16)
    wr_lo = (wr - wr_hi.astype(F32)).astype(BF16)
    wr2 = jnp.concatenate([wr_hi, wr_lo], axis=1)
    return pl.pallas_call(
        _outproj_kernel,
        grid=(N_TOK // OUT_TM,),
        in_specs=[
            pl.BlockSpec((OUT_TM, SWA_WIDTH), row),
            pl.BlockSpec((OUT_TM, FOX_WIDTH), row),
            pl.BlockSpec((OUT_TM, D_MODEL), row),
            pl.BlockSpec((1, 6, D_MODEL), lambda i: (i // tiles_per_batch, 0, 0)),
            pl.BlockSpec((1, SWA_WIDTH), const),
            pl.BlockSpec((1, FOX_WIDTH), const),
            pl.BlockSpec((D_MODEL, D_MODEL), const),
            pl.BlockSpec((1, D_MODEL), const),
            pl.BlockSpec((D_MODEL, 2 * LANES), const),
            pl.BlockSpec((1, LANES), const),
        ],
        out_specs=[
            pl.BlockSpec((OUT_TM, D_MODEL), row),
            pl.BlockSpec((OUT_TM, PACK_ROWS, LANES), lambda i: (i, 0, 0)),
            pl.BlockSpec((OUT_TM, LANES), row),
            pl.BlockSpec((OUT_TM, LANES), row),
        ],
        out_shape=[
            jax.ShapeDtypeStruct((N_TOK, D_MODEL), F32),
            jax.ShapeDtypeStruct((N_TOK, PACK_ROWS, LANES), jnp.uint32),
            jax.ShapeDtypeStruct((N_TOK, LANES), jnp.int32),
            jax.ShapeDtypeStruct((N_TOK, LANES), F32),
        ],
        compiler_params=_cparams(("arbitrary",)),
        name="outproj_router",
    )(oa, ob, x2, mod3, ga, gb, w_out, gm, wr2, br)


N_BLOCKS = (N_TOK * TOP_K) // MOE_BLOCK + N_EXPERTS
P_ROWS = N_BLOCKS * MOE_BLOCK


def _pack_bf16_pairs(x):
    xb = x.astype(BF16).astype(F32)
    half = x.shape[1] // 2
    lo = lax.shift_right_logical(pltpu.bitcast(xb[:, :half], jnp.uint32), jnp.uint32(16))
    hi = pltpu.bitcast(xb[:, half:], jnp.uint32) & jnp.uint32(0xFFFF0000)
    return lo | hi


def _unpack_bf16_pairs(w):
    return (pltpu.bitcast(lax.shift_left(w, jnp.uint32(16)), F32),
            pltpu.bitcast(w & jnp.uint32(0xFFFF0000), F32))


def _expert_kernel(be_ref, tok_ref, nused_ref, nxt_ref, h2_hbm, wg_hbm, wu_hbm, wd_hbm, y_ref, xbuf, sem,
                   wgf, wuf, wdf, wsem, wgb, wub, wdb):
    i = pl.program_id(0)
    nused = nused_ref[0]

    def weight_copies(e):
        return (pltpu.make_async_copy(wg_hbm.at[e], wgf, wsem.at[0]),
                pltpu.make_async_copy(wu_hbm.at[e], wuf, wsem.at[1]),
                pltpu.make_async_copy(wd_hbm.at[e], wdf, wsem.at[2]))

    def issue(blk, slot):
        def body(g, _):
            for q in range(SUBLANES):
                tok = tok_ref[blk * MOE_BLOCK + g * SUBLANES + q]
                pltpu.make_async_copy(h2_hbm.at[tok], xbuf.at[slot, g, :, q, :], sem.at[slot]).start(
                    priority=q % 2)
            return 0
        lax.fori_loop(0, MOE_BLOCK // SUBLANES, body, 0)

    @pl.when(i == 0)
    def _():
        issue(0, 0)

    @pl.when(i + 1 < nused)
    def _():
        issue(i + 1, (i + 1) % 2)

    e = be_ref[i]
    new_expert = ((i == 0) | (e != be_ref[jnp.maximum(i - 1, 0)])) & (i < nused)

    @pl.when(i == 0)
    def _():
        for cp in weight_copies(e):
            cp.start()

    @pl.when(new_expert)
    def _():
        for cp in weight_copies(e):
            cp.wait()
        wgb[...] = wgf[...].astype(BF16)
        wub[...] = wuf[...].astype(BF16)
        wdb[...] = wdf[...].astype(BF16)

    @pl.when(new_expert & (nxt_ref[e] >= 0))
    def _():
        for cp in weight_copies(nxt_ref[e]):
            cp.start()

    @pl.when(i < nused)
    def _():
        slot = i % 2
        pltpu.make_async_copy(xbuf.at[slot], xbuf.at[slot], sem.at[slot]).wait()
        xu = jnp.concatenate([xbuf[slot, :, s].reshape(MOE_BLOCK, LANES) for s in range(PACK_ROWS)],
                             axis=1)
        xb = jnp.concatenate(_unpack_bf16_pairs(xu), axis=1).astype(BF16)
        g = jnp.dot(xb, wgb[...], preferred_element_type=F32)
        u = jnp.dot(xb, wub[...], preferred_element_type=F32)
        hid = (jax.nn.silu(g) * u).astype(BF16)
        y_ref[...] = _pack_bf16_pairs(jnp.dot(hid, wdb[...], preferred_element_type=F32))

    @pl.when(i >= nused)
    def _():
        y_ref[...] = jnp.zeros_like(y_ref)


def _experts(block_e, row_tok, nused, next_used, h2p, wg, wu, wd):
    grid_spec = pltpu.PrefetchScalarGridSpec(
        num_scalar_prefetch=4,
        grid=(N_BLOCKS,),
        in_specs=[pl.BlockSpec(memory_space=pl.ANY)] * 4,
        out_specs=pl.BlockSpec((MOE_BLOCK, PACKED_COLS), lambda i, be, tk, nu, nx: (i, 0)),
        scratch_shapes=[pltpu.VMEM((2, MOE_BLOCK // SUBLANES, PACK_ROWS, SUBLANES, LANES), jnp.uint32),
                        pltpu.SemaphoreType.DMA((2,)),
                        pltpu.VMEM((D_MODEL, D_EXPERT), F32),
                        pltpu.VMEM((D_MODEL, D_EXPERT), F32),
                        pltpu.VMEM((D_EXPERT, D_MODEL), F32),
                        pltpu.SemaphoreType.DMA((3,)),
                        pltpu.VMEM((D_MODEL, D_EXPERT), BF16),
                        pltpu.VMEM((D_MODEL, D_EXPERT), BF16),
                        pltpu.VMEM((D_EXPERT, D_MODEL), BF16)],
    )
    return pl.pallas_call(
        _expert_kernel,
        grid_spec=grid_spec,
        out_shape=jax.ShapeDtypeStruct((P_ROWS, PACKED_COLS), jnp.uint32),
        compiler_params=_cparams(("arbitrary",)),
        name="expert_ffn",
    )(block_e, row_tok, nused, next_used, h2p, wg, wu, wd)


CMB_TM = 128


def _combine_kernel(dest_ref, ys_hbm, x1_ref, gates_ref, mod_ref, fg_ref, o_ref, ybuf, sem):
    i = pl.program_id(0)
    n = pl.num_programs(0)

    def issue(tile, slot):
        def body(g, _):
            for q in range(DMA_UNROLL):
                r = g * DMA_UNROLL + q
                for k in range(TOP_K):
                    d = dest_ref[(tile * CMB_TM + r) * TOP_K + k]
                    pltpu.make_async_copy(ys_hbm.at[pl.ds(d, 1), :],
                                          ybuf.at[slot, pl.ds(k * CMB_TM + r, 1), :],
                                          sem.at[slot]).start(priority=k)
            return 0
        lax.fori_loop(0, CMB_TM // DMA_UNROLL, body, 0)

    @pl.when(i == 0)
    def _():
        issue(0, 0)

    @pl.when(i + 1 < n)
    def _():
        issue(i + 1, (i + 1) % 2)

    slot = i % 2
    pltpu.make_async_copy(ybuf.at[slot], ybuf.at[slot], sem.at[slot]).wait()
    gts = gates_ref[...]
    w0 = gts[:, 0:1]
    w1 = gts[:, 1:2]
    y0 = _unpack_bf16_pairs(ybuf[slot, 0:CMB_TM, :])
    y1 = _unpack_bf16_pairs(ybuf[slot, CMB_TM:2 * CMB_TM, :])
    y = jnp.concatenate([y0[part] * w0 + y1[part] * w1 for part in range(2)], axis=1)
    x2 = x1_ref[...] + mod_ref[0, 5:6, :] * y
    o_ref[...] = _rms(x2) * fg_ref[...]


def _combine(dest, ys, x1, gates, mod3, final_g):
    tiles_per_batch = SEQ // CMB_TM
    grid_spec = pltpu.PrefetchScalarGridSpec(
        num_scalar_prefetch=1,
        grid=(N_TOK // CMB_TM,),
        in_specs=[
            pl.BlockSpec(memory_space=pl.ANY),
            pl.BlockSpec((CMB_TM, D_MODEL), lambda i, d: (i, 0)),
            pl.BlockSpec((CMB_TM, LANES), lambda i, d: (i, 0)),
            pl.BlockSpec((1, 6, D_MODEL), lambda i, d: (i // tiles_per_batch, 0, 0)),
            pl.BlockSpec((1, D_MODEL), lambda i, d: (0, 0)),
        ],
        out_specs=pl.BlockSpec((CMB_TM, D_MODEL), lambda i, d: (i, 0)),
        scratch_shapes=[pltpu.VMEM((2, TOP_K * CMB_TM, PACKED_COLS), jnp.uint32),
                        pltpu.SemaphoreType.DMA((2,))],
    )
    return pl.pallas_call(
        _combine_kernel,
        grid_spec=grid_spec,
        out_shape=jax.ShapeDtypeStruct((N_TOK, D_MODEL), F32),
        compiler_params=_cparams(("arbitrary",)),
        name="combine_final",
    )(dest, ys, x1, gates, mod3, final_g)


def _routing_tables(eid):
    a = eid.shape[0]
    onehot = (eid[:, None] == jnp.arange(N_EXPERTS, dtype=jnp.int32)[None, :]).astype(jnp.int32)
    csum = jnp.cumsum(onehot, axis=0)
    rank = jnp.sum(onehot * csum, axis=1) - 1
    counts = csum[-1]
    padded = (counts + MOE_BLOCK - 1) // MOE_BLOCK * MOE_BLOCK
    pend = jnp.cumsum(padded)
    pstart = pend - padded
    dest = (pstart[eid] + rank).astype(jnp.int32)
    tok = jnp.arange(a, dtype=jnp.int32) // TOP_K
    row_tok = jnp.zeros((P_ROWS,), jnp.int32).at[dest].set(tok, unique_indices=True, mode='promise_in_bounds')
    blk_row = jnp.arange(N_BLOCKS, dtype=jnp.int32) * MOE_BLOCK
    block_e = jnp.minimum(jnp.sum((pend[None, :] <= blk_row[:, None]).astype(jnp.int32), axis=1),
                          N_EXPERTS - 1).astype(jnp.int32)
    nused = (pend[-1] // MOE_BLOCK).astype(jnp.int32).reshape(1)
    ids = jnp.arange(N_EXPERTS, dtype=jnp.int32)
    later_used = (ids[None, :] > ids[:, None]) & (counts[None, :] > 0)
    next_used = jnp.min(jnp.where(later_used, ids[None, :], N_EXPERTS), axis=1)
    next_used = jnp.where(next_used == N_EXPERTS, -1, next_used).astype(jnp.int32)
    return dest, row_tok, block_e, nused, next_used


def kernel(x, c, w_ada, b_ada, norm_mix_g, w_in, b_forget, sinks, out_norm_swa_g, out_norm_fox_g, w_out,
           norm_moe_g, w_group, b_group, w_expert, b_expert, w_gate, w_up, w_down, final_g):
    assert IN_TM == FOX_T
    x2 = x.reshape(N_TOK, D_MODEL)
    perm = _swa_perm()

    w_in0 = w_in[0]
    w_row = jnp.concatenate([w_in0[:, :SWA_WIDTH][:, perm], w_in0[:, SWA_WIDTH:VB_COL]], axis=1).astype(BF16)
    w_vt = w_in0[:, VB_COL:MAIN_COLS].T.astype(BF16)
    w_f = jnp.pad(w_in0[:, MAIN_COLS:], ((0, 0), (0, LANES - N_HEADS_FOX))).astype(BF16)
    colscale = jnp.concatenate([
        jnp.full((SWA_WIDTH,), HEAD_DIM ** -0.5 * LOG2E, F32),
        jnp.ones((2 * KV_WIDTH,), F32),
        jnp.full((FOX_WIDTH,), HEAD_DIM ** -0.5 * LOG2E, F32),
        jnp.ones((FOX_WIDTH,), F32)]).reshape(1, ROW_COLS)
    bf_row = jnp.pad(b_forget[0], (0, LANES - N_HEADS_FOX)).reshape(1, LANES)
    w_out_p = jnp.concatenate([w_out[0][:SWA_WIDTH][perm], w_out[0][SWA_WIDTH:]], axis=0).astype(BF16)
    ga = out_norm_swa_g[0][perm].reshape(1, SWA_WIDTH)
    gb = out_norm_fox_g[0].reshape(1, FOX_WIDTH)
    wr = jnp.pad(jnp.concatenate([w_group[0], w_expert[0]], axis=1),
                 ((0, 0), (0, LANES - N_GROUPS - N_EXPERTS)))
    br = jnp.pad(jnp.concatenate([b_group[0], b_expert[0]]), (0, LANES - N_GROUPS - N_EXPERTS)).reshape(1, LANES)
    wg, wu, wd = w_gate[0], w_up[0], w_down[0]

    mod = _adaln(c, w_ada[0], b_ada[0])
    mod3 = mod.reshape(BATCH, 6, D_MODEL)

    proj, f, vt5 = _inproj(x2, mod3, norm_mix_g[0].reshape(1, D_MODEL), w_row, w_f, colscale, w_vt)
    bias = _forget_bias(f, bf_row)

    o_a = _swa(sinks[0], proj)
    o_b = _fox(proj, vt5, bias)

    x1, h2, ids, gates = _outproj(o_a, o_b, x2, mod3, ga, gb, w_out_p,
                                  norm_moe_g[0].reshape(1, D_MODEL), wr, br)

    eid = ids[:, :TOP_K].reshape(-1)
    dest, row_tok, block_e, nused, next_used = _routing_tables(eid)
    ys = _experts(block_e, row_tok, nused, next_used, h2, wg, wu, wd)
    out = _combine(dest, ys, x1, gates, mod3, final_g.reshape(1, D_MODEL))
    return out.reshape(BATCH, SEQ, D_MODEL)
```

```python
import functools
import math

import numpy as np
import jax
import jax.numpy as jnp
from jax import lax
from jax.experimental import pallas as pl
from jax.experimental.pallas import tpu as pltpu

F32 = jnp.float32
BF16 = jnp.bfloat16

D_MODEL = 2048
BATCH = 2
SEQ = 8192
N_TOK = BATCH * SEQ
HEAD_DIM = 64
N_HEADS_SWA = 16
N_KV_SWA = 4
N_HEADS_FOX = 16
WINDOW = 128
SWA_WIDTH = N_HEADS_SWA * HEAD_DIM
KV_WIDTH = N_KV_SWA * HEAD_DIM
FOX_WIDTH = N_HEADS_FOX * HEAD_DIM
MAIN_COLS = SWA_WIDTH + 2 * KV_WIDTH + 3 * FOX_WIDTH
N_GROUPS = 4
EXPERTS_PER_GROUP = 8
N_EXPERTS = N_GROUPS * EXPERTS_PER_GROUP
TOP_K = 2
D_EXPERT = 512
MOE_BLOCK = 256
MOE_SUB = 128
EPS = 1e-6
LOG2E = math.log2(math.e)

LANES = 128
VMEM_LIMIT = 56 * 1024 * 1024

QA_BLK = 0
KA_COL = SWA_WIDTH
VA_COL = SWA_WIDTH + KV_WIDTH
QB_BLK = (SWA_WIDTH + 2 * KV_WIDTH) // LANES
KB_BLK = QB_BLK + FOX_WIDTH // LANES
VB_COL = SWA_WIDTH + 2 * KV_WIDTH + 2 * FOX_WIDTH
ROW_COLS = VB_COL


def _cparams(sem, vmem=VMEM_LIMIT):
    return pltpu.CompilerParams(dimension_semantics=sem, vmem_limit_bytes=vmem)


ADA_TN = 1024


def _adaln_kernel(cb_ref, w_ref, b_ref, o_ref):
    for b in range(BATCH):
        s = jax.nn.silu(cb_ref[b])
        cols = []
        for j in range(ADA_TN // LANES):
            prod = w_ref[:, j * LANES:(j + 1) * LANES] * s
            cols.append(jnp.sum(prod, axis=0, keepdims=True))
        o_ref[b:b + 1, :] = jnp.concatenate(cols, axis=1) + b_ref[...]


def _adaln(c, w_ada, b_ada):
    ncol = w_ada.shape[1]
    cb = jnp.broadcast_to(c[:, :, None], (BATCH, D_MODEL, LANES))
    return pl.pallas_call(
        _adaln_kernel,
        grid=(ncol // ADA_TN,),
        in_specs=[
            pl.BlockSpec((BATCH, D_MODEL, LANES), lambda j: (0, 0, 0)),
            pl.BlockSpec((D_MODEL, ADA_TN), lambda j: (0, j)),
            pl.BlockSpec((1, ADA_TN), lambda j: (0, j)),
        ],
        out_specs=pl.BlockSpec((BATCH, ADA_TN), lambda j: (0, j)),
        out_shape=jax.ShapeDtypeStruct((BATCH, ncol), F32),
        compiler_params=_cparams(("arbitrary",)),
        name="adaln",
    )(cb, w_ada, b_ada.reshape(1, ncol))


def _rms(x):
    return x * lax.rsqrt(jnp.mean(x * x, axis=-1, keepdims=True) + EPS)


IN_TM = 512
IN_TN = 1792
IN_NJ = ROW_COLS // IN_TN


def _inproj_kernel(x_ref, mod_ref, g_ref, w_ref, wf_ref, cs_ref, wvt_ref, o_ref, f_ref, vt_ref, h_ref):
    j = pl.program_id(1)

    @pl.when(j == 0)
    def _():
        y = _rms(x_ref[...])
        h = (y * g_ref[...]) * (1.0 + mod_ref[0, 1:2, :]) + mod_ref[0, 0:1, :]
        hb = h.astype(BF16)
        h_ref[...] = hb
        f_ref[...] = jnp.dot(hb, wf_ref[...], preferred_element_type=F32)

    @pl.when(j < IN_NJ)
    def _():
        acc = jnp.dot(h_ref[...], w_ref[...], preferred_element_type=F32)
        o_ref[...] = (acc * cs_ref[...]).astype(BF16)

    @pl.when(j == IN_NJ)
    def _():
        vt = lax.dot_general(wvt_ref[...], h_ref[...], (((1,), (1,)), ((), ())),
                             preferred_element_type=F32)
        vt_ref[0, :, 0] = vt.reshape(N_HEADS_FOX // 2, LANES, IN_TM).astype(BF16)


def _inproj(x2, mod3, g, w_row, w_f, colscale, w_vt):
    tiles_per_batch = SEQ // IN_TM
    last = IN_NJ - 1
    return pl.pallas_call(
        _inproj_kernel,
        grid=(N_TOK // IN_TM, IN_NJ + 1),
        in_specs=[
            pl.BlockSpec((IN_TM, D_MODEL), lambda i, j: (i, 0)),
            pl.BlockSpec((1, 6, D_MODEL), lambda i, j: (i // tiles_per_batch, 0, 0)),
            pl.BlockSpec((1, D_MODEL), lambda i, j: (0, 0)),
            pl.BlockSpec((D_MODEL, IN_TN), lambda i, j: (0, jnp.minimum(j, last))),
            pl.BlockSpec((D_MODEL, LANES), lambda i, j: (0, 0)),
            pl.BlockSpec((1, IN_TN), lambda i, j: (0, jnp.minimum(j, last))),
            pl.BlockSpec((FOX_WIDTH, D_MODEL), lambda i, j: (0, 0)),
        ],
        out_specs=[
            pl.BlockSpec((IN_TM, IN_TN), lambda i, j: (i, jnp.minimum(j, last))),
            pl.BlockSpec((IN_TM, LANES), lambda i, j: (i, 0)),
            pl.BlockSpec((1, N_HEADS_FOX // 2, 1, LANES, IN_TM),
                         lambda i, j: (i // tiles_per_batch, 0, i % tiles_per_batch, 0, 0)),
        ],
        out_shape=[
            jax.ShapeDtypeStruct((N_TOK, ROW_COLS), BF16),
            jax.ShapeDtypeStruct((N_TOK, LANES), F32),
            jax.ShapeDtypeStruct((BATCH, N_HEADS_FOX // 2, SEQ // IN_TM, LANES, IN_TM), BF16),
        ],
        scratch_shapes=[pltpu.VMEM((IN_TM, D_MODEL), BF16)],
        compiler_params=_cparams(("arbitrary", "arbitrary")),
        name="inproj",
    )(x2, mod3, g, w_row, w_f, colscale, w_vt)


CUM_TS = 512


def _cum_kernel(f_ref, bf_ref, tri_ref, o_ref, carry_ref):
    @pl.when(pl.program_id(1) == 0)
    def _():
        carry_ref[...] = jnp.zeros_like(carry_ref)

    lf = jax.nn.log_sigmoid(f_ref[...] + bf_ref[...])
    cs = jnp.dot(tri_ref[...], lf, preferred_element_type=F32,
                 precision=lax.Precision.HIGHEST) + carry_ref[...]
    o_ref[...] = cs * (-LOG2E)
    carry_ref[...] = cs[CUM_TS - 1:CUM_TS, :]


def _forget_bias(f, b_forget_row):
    tri = jnp.tril(jnp.ones((CUM_TS, CUM_TS), F32))
    nblk = SEQ // CUM_TS
    return pl.pallas_call(
        _cum_kernel,
        grid=(BATCH, nblk),
        in_specs=[
            pl.BlockSpec((CUM_TS, LANES), lambda b, i: (b * nblk + i, 0)),
            pl.BlockSpec((1, LANES), lambda b, i: (0, 0)),
            pl.BlockSpec((CUM_TS, CUM_TS), lambda b, i: (0, 0)),
        ],
        out_specs=pl.BlockSpec((CUM_TS, LANES), lambda b, i: (b * nblk + i, 0)),
        out_shape=jax.ShapeDtypeStruct((N_TOK, LANES), F32),
        scratch_shapes=[pltpu.VMEM((1, LANES), F32)],
        compiler_params=_cparams(("arbitrary", "arbitrary")),
        name="forget_bias",
    )(f, b_forget_row, tri)


SWA_TQ = WINDOW


def _swa_perm():
    perm = np.zeros((SWA_WIDTH,), np.int32)
    for t in range(2):
        for g in range(4):
            for e in range(2):
                head = (2 * t + e) * 4 + g
                base = (t * 4 + g) * LANES + e * HEAD_DIM
                perm[base:base + HEAD_DIM] = head * HEAD_DIM + np.arange(HEAD_DIM)
    return perm


def _swa_kernel(sinks_ref, q_ref, kc_ref, kp_ref, vc_ref, vp_ref, o_ref, bias_ref):
    i = pl.program_id(1)
    tq = SWA_TQ

    @pl.when((pl.program_id(0) == 0) & (i == 0))
    def _():
        row = lax.broadcasted_iota(jnp.int32, (tq, 2 * tq), 0)
        col = lax.broadcasted_iota(jnp.int32, (tq, 2 * tq), 1)
        dist = row + tq - col
        band = (dist >= 0) & (dist < WINDOW)
        distf = dist.astype(F32)
        for head in range(N_HEADS_SWA):
            slope2 = float(2.0 ** (-8.0 * (head + 1) / N_HEADS_SWA)) * LOG2E
            base = jnp.where(band, -slope2 * distf, -jnp.inf)
            bias_ref[1, head] = base
            bias_ref[0, head] = jnp.where(col >= tq, base, -jnp.inf)

    table = jnp.minimum(i, 1)
    lane = lax.broadcasted_iota(jnp.int32, (tq, LANES), 1)
    lo_half = lane < HEAD_DIM
    for t in range(2):
        kt = jnp.concatenate([kp_ref[:, t * LANES:(t + 1) * LANES],
                              kc_ref[:, t * LANES:(t + 1) * LANES]], axis=0)
        vt = jnp.concatenate([vp_ref[:, t * LANES:(t + 1) * LANES],
                              vc_ref[:, t * LANES:(t + 1) * LANES]], axis=0)
        heads = [(g, e) for g in range(4) for e in range(2)]
        scores = []
        for g, e in heads:
            tile = t * 4 + g
            qt = q_ref[:, tile * LANES:(tile + 1) * LANES]
            qm = jnp.where(lo_half if e == 0 else ~lo_half, qt, jnp.zeros_like(qt))
            scores.append(lax.dot_general(qm, kt, (((1,), (1,)), ((), ())),
                                          preferred_element_type=F32))
        probs, rdens = [], []
        for (g, e), s in zip(heads, scores):
            head = (2 * t + e) * 4 + g
            sink = sinks_ref[head] * LOG2E
            s = s + bias_ref[table, head]
            m = jnp.maximum(jnp.max(s, axis=-1, keepdims=True), sink)
            p = jnp.exp2(s - m)
            rdens.append(1.0 / (jnp.sum(p, axis=-1, keepdims=True) + jnp.exp2(sink - m)))
            probs.append(p.astype(BF16))
        outs = [jnp.dot(p, vt, preferred_element_type=F32) * r for p, r in zip(probs, rdens)]
        for g in range(4):
            tile = t * 4 + g
            o_ref[:, tile * LANES:(tile + 1) * LANES] = jnp.where(lo_half, outs[2 * g], outs[2 * g + 1]).astype(BF16)


def _swa(sinks, proj):
    nq = SEQ // SWA_TQ
    kblk = KA_COL // KV_WIDTH
    vblk = VA_COL // KV_WIDTH
    grid_spec = pltpu.PrefetchScalarGridSpec(
        num_scalar_prefetch=1,
        grid=(BATCH, nq),
        in_specs=[
            pl.BlockSpec((SWA_TQ, SWA_WIDTH), lambda b, i, s: (b * nq + i, 0)),
            pl.BlockSpec((SWA_TQ, KV_WIDTH), lambda b, i, s: (b * nq + i, kblk)),
            pl.BlockSpec((SWA_TQ, KV_WIDTH), lambda b, i, s: (b * nq + jnp.maximum(i - 1, 0), kblk)),
            pl.BlockSpec((SWA_TQ, KV_WIDTH), lambda b, i, s: (b * nq + i, vblk)),
            pl.BlockSpec((SWA_TQ, KV_WIDTH), lambda b, i, s: (b * nq + jnp.maximum(i - 1, 0), vblk)),
        ],
        out_specs=pl.BlockSpec((SWA_TQ, SWA_WIDTH), lambda b, i, s: (b * nq + i, 0)),
        scratch_shapes=[pltpu.VMEM((2, N_HEADS_SWA, SWA_TQ, 2 * SWA_TQ), F32)],
    )
    return pl.pallas_call(
        _swa_kernel,
        grid_spec=grid_spec,
        out_shape=jax.ShapeDtypeStruct((N_TOK, SWA_WIDTH), BF16),
        compiler_params=_cparams(("arbitrary", "arbitrary")),
        name="swa_attn",
    )(sinks, proj, proj, proj, proj, proj)


FOX_T = 512
FOX_CH = 512
FOX_AUG = 3
FOX_ONES = 16
FOX_VROWS = HEAD_DIM + FOX_ONES


def _fox_kernel(q_ref, k_ref, vt_ref, bias_ref, o_ref, kaug_ref, mask_ref, t_ref, acc_ref):
    hp = pl.program_id(1)
    qi = pl.program_id(2)
    t = FOX_T

    @pl.when(qi == 0)
    def _():
        lane = lax.broadcasted_iota(jnp.int32, (FOX_CH, LANES), 1)

        def chunk(c, _):
            r0 = pl.multiple_of(c * FOX_CH, FOX_CH)
            blk = bias_ref[pl.ds(r0, FOX_CH), :]
            aug = jnp.zeros((FOX_CH, LANES), F32)
            for hh in range(2):
                colv = jnp.sum(jnp.where(lane == 2 * hp + hh, blk, 0.0), axis=1, keepdims=True)
                hi = colv.astype(BF16).astype(F32)
                r1 = colv - hi
                mid = r1.astype(BF16).astype(F32)
                lo = r1 - mid
                for piece, val in enumerate((hi, mid, lo)):
                    aug = jnp.where(lane == FOX_AUG * hh + piece, val, aug)
            kaug_ref[pl.ds(r0, FOX_CH), 0:LANES] = k_ref[pl.ds(r0, FOX_CH), :]
            kaug_ref[pl.ds(r0, FOX_CH), LANES:2 * LANES] = aug.astype(BF16)
            return 0

        lax.fori_loop(0, SEQ // FOX_CH, chunk, 0)
        kr = lax.broadcasted_iota(jnp.int32, (t, t), 0)
        qc = lax.broadcasted_iota(jnp.int32, (t, t), 1)
        mask_ref[...] = jnp.where(kr <= qc, 0.0, -jnp.inf)

    qf = q_ref[...].astype(F32).T
    drow = lax.broadcasted_iota(jnp.int32, (LANES, t), 0)
    qaug = []
    for hh in range(2):
        qh = jnp.where((drow >= HEAD_DIM * hh) & (drow < HEAD_DIM * (hh + 1)), qf, 0.0)
        ones = jnp.where((drow >= FOX_AUG * hh) & (drow < FOX_AUG * (hh + 1)), 1.0, 0.0)
        qaug.append(jnp.concatenate([qh, ones], axis=0).astype(BF16))
    acc_ref[...] = jnp.zeros_like(acc_ref)

    def stage_a(j, slot, diag):
        r0 = pl.multiple_of(j * t, t)
        kb = kaug_ref[pl.ds(r0, t), :]
        sts = [jnp.dot(kb, qaug[hh], preferred_element_type=F32) for hh in range(2)]
        mbs = []
        for hh in range(2):
            tt = sts[hh]
            if diag:
                tt = tt + mask_ref[...]
            t_ref[slot, hh] = tt
            mbs.append(jnp.max(tt, axis=0, keepdims=True))
        return tuple(mbs)

    def stage_b(jv, slot, ms, mbs):
        new, ps, alphas = [], [], []
        for hh in range(2):
            m_new = jnp.maximum(ms[hh], mbs[hh])
            alphas.append(jnp.exp2(ms[hh] - m_new))
            ps.append(jnp.exp2(t_ref[slot, hh] - m_new).astype(BF16))
            new.append(m_new)
        ones = jnp.ones((FOX_ONES, t), BF16)
        pvs = [jnp.dot(jnp.concatenate([vt_ref[0, 0, jv, hh * HEAD_DIM:(hh + 1) * HEAD_DIM, :], ones], axis=0),
                       ps[hh], preferred_element_type=F32)
               for hh in range(2)]
        for hh in range(2):
            acc_ref[hh] = alphas[hh] * acc_ref[hh] + pvs[hh]
        return tuple(new)

    ml0 = tuple(jnp.full((1, t), -jnp.inf, F32) for _ in range(2))
    mb0 = stage_a(qi, 0, True)

    def pair(ii, c):
        ml, mbs, jprev = c
        j0 = 2 * ii
        mb1 = stage_a(j0, 1, False)
        ml = stage_b(jprev, 0, ml, mbs)
        mb2 = stage_a(j0 + 1, 0, False)
        ml = stage_b(j0, 1, ml, mb1)
        return ml, mb2, j0 + 1

    ml, mbs, jprev = lax.fori_loop(0, qi // 2, pair, (ml0, mb0, qi))

    def odd_tail(c):
        ml, mbs, jprev = c
        mb1 = stage_a(qi - 1, 1, False)
        ml = stage_b(jprev, 0, ml, mbs)
        return stage_b(qi - 1, 1, ml, mb1)

    def even_tail(c):
        ml, mbs, jprev = c
        return stage_b(jprev, 0, ml, mbs)

    ml = lax.cond(qi % 2 == 1, odd_tail, even_tail, (ml, mbs, jprev))
    del ml
    ot = jnp.concatenate([acc_ref[hh, 0:HEAD_DIM, :] / acc_ref[hh, HEAD_DIM:HEAD_DIM + 1, :]
                          for hh in range(2)], axis=0)
    o_ref[...] = ot.T.astype(BF16)


def _fox(proj, vt5, bias):
    nq = SEQ // FOX_T
    npair = N_HEADS_FOX // 2
    return pl.pallas_call(
        _fox_kernel,
        grid=(BATCH, npair, nq),
        in_specs=[
            pl.BlockSpec((FOX_T, LANES), lambda b, h, i: (b * nq + i, QB_BLK + h)),
            pl.BlockSpec((SEQ, LANES), lambda b, h, i: (b, KB_BLK + h)),
            pl.BlockSpec((1, 1, nq, LANES, FOX_T), lambda b, h, i: (b, h, 0, 0, 0)),
            pl.BlockSpec((SEQ, LANES), lambda b, h, i: (b, 0)),
        ],
        out_specs=pl.BlockSpec((FOX_T, LANES), lambda b, h, i: (b * nq + i, h)),
        out_shape=jax.ShapeDtypeStruct((N_TOK, FOX_WIDTH), BF16),
        scratch_shapes=[
            pltpu.VMEM((SEQ, 2 * LANES), BF16),
            pltpu.VMEM((FOX_T, FOX_T), F32),
            pltpu.VMEM((2, 2, FOX_T, FOX_T), F32),
            pltpu.VMEM((2, FOX_VROWS, FOX_T), F32),
        ],
        compiler_params=_cparams(("arbitrary", "arbitrary", "arbitrary")),
        name="fox_attn",
    )(proj, proj, vt5, bias)


OUT_TM = 512
OUT_SUB = 256
SUBLANES = 8
PACKED_COLS = D_MODEL // 2
PACK_ROWS = PACKED_COLS // LANES
DMA_UNROLL = 8


def _outproj_kernel(oa_ref, ob_ref, x_ref, mod_ref, ga_ref, gb_ref, w_ref, gm_ref, wr2_ref, br_ref,
                    x1_ref, h2p_ref, ids_ref, gates_ref):
    subs = [slice(k * OUT_SUB, (k + 1) * OUT_SUB) for k in range(OUT_TM // OUT_SUB)]
    mixed = [jnp.concatenate([_rms(oa_ref[r, :].astype(F32)) * ga_ref[...],
                              _rms(ob_ref[r, :].astype(F32)) * gb_ref[...]], axis=1).astype(BF16) for r in subs]
    ys = [jnp.dot(m, w_ref[...], preferred_element_type=F32) for m in mixed]
    h2s = []
    for r, y in zip(subs, ys):
        x1 = x_ref[r, :] + mod_ref[0, 2:3, :] * y
        x1_ref[r, :] = x1
        h2s.append((_rms(x1) * gm_ref[...]) * (1.0 + mod_ref[0, 4:5, :]) + mod_ref[0, 3:4, :])
    hbs = [h2.astype(BF16) for h2 in h2s]
    for r, h2 in zip(subs, h2s):
        packed = _pack_bf16_pairs(h2)
        for s in range(PACK_ROWS):
            h2p_ref[r, s, :] = packed[:, s * LANES:(s + 1) * LANES]

    hls = [(h2 - hb.astype(F32)).astype(BF16) for h2, hb in zip(h2s, hbs)]
    r2s = [jnp.dot(hb, wr2_ref[...], preferred_element_type=F32) for hb in hbs]
    r3s = [jnp.dot(hl, wr2_ref[:, :LANES], preferred_element_type=F32) for hl in hls]
    for r, r2, r3 in zip(subs, r2s, r3s):
        logits = (r2[:, :LANES] + (r2[:, LANES:] + r3)) + br_ref[...]
        ids, gates = _route(logits)
        ids_ref[r, :] = ids
        gates_ref[r, :] = gates


def _route(logits):
    lane = lax.broadcasted_iota(jnp.int32, logits.shape, 1)
    neg = -jnp.inf
    is_g = lane < N_GROUPS
    gl = jnp.where(is_g, logits, neg)
    gmax = jnp.max(gl, axis=1, keepdims=True)
    gsel = jnp.min(jnp.where(gl == gmax, lane, LANES), axis=1, keepdims=True)
    gsum = jnp.sum(jnp.where(is_g, jnp.exp(gl - gmax), 0.0), axis=1, keepdims=True)
    g_val = 1.0 / gsum
    elane = lane - N_GROUPS
    in_sel = (elane >= 0) & (elane < N_EXPERTS) & ((elane >> 3) == gsel)
    ev = jnp.where(in_sel, logits, neg)
    t1 = jnp.max(ev, axis=1, keepdims=True)
    i1 = jnp.min(jnp.where(ev == t1, lane, LANES), axis=1, keepdims=True)
    ev2 = jnp.where(lane == i1, neg, ev)
    t2 = jnp.max(ev2, axis=1, keepdims=True)
    i2 = jnp.min(jnp.where(ev2 == t2, lane, LANES), axis=1, keepdims=True)
    e2 = jnp.exp(t2 - t1)
    den = 1.0 + e2
    w1 = (1.0 / den) * g_val
    w2 = (e2 / den) * g_val
    return (jnp.where(lane == 0, i1 - N_GROUPS, jnp.where(lane == 1, i2 - N_GROUPS, 0)),
            jnp.where(lane == 0, w1, jnp.where(lane == 1, w2, 0.0)))


def _outproj(oa, ob, x2, mod3, ga, gb, w_out, gm, wr, br):
    tiles_per_batch = SEQ // OUT_TM
    row = lambda i: (i, 0)
    const = lambda i: (0, 0)
    wr_hi = wr.astype(BF16)
    wr_lo = (wr - wr_hi.astype(F32)).astype(BF16)
    wr2 = jnp.concatenate([wr_hi, wr_lo], axis=1)
    return pl.pallas_call(
        _outproj_kernel,
        grid=(N_TOK // OUT_TM,),
        in_specs=[
            pl.BlockSpec((OUT_TM, SWA_WIDTH), row),
            pl.BlockSpec((OUT_TM, FOX_WIDTH), row),
            pl.BlockSpec((OUT_TM, D_MODEL), row),
            pl.BlockSpec((1, 6, D_MODEL), lambda i: (i // tiles_per_batch, 0, 0)),
            pl.BlockSpec((1, SWA_WIDTH), const),
            pl.BlockSpec((1, FOX_WIDTH), const),
            pl.BlockSpec((D_MODEL, D_MODEL), const),
            pl.BlockSpec((1, D_MODEL), const),
            pl.BlockSpec((D_MODEL, 2 * LANES), const),
            pl.BlockSpec((1, LANES), const),
        ],
        out_specs=[
            pl.BlockSpec((OUT_TM, D_MODEL), row),
            pl.BlockSpec((OUT_TM, PACK_ROWS, LANES), lambda i: (i, 0, 0)),
            pl.BlockSpec((OUT_TM, LANES), row),
            pl.BlockSpec((OUT_TM, LANES), row),
        ],
        out_shape=[
            jax.ShapeDtypeStruct((N_TOK, D_MODEL), F32),
            jax.ShapeDtypeStruct((N_TOK, PACK_ROWS, LANES), jnp.uint32),
            jax.ShapeDtypeStruct((N_TOK, LANES), jnp.int32),
            jax.ShapeDtypeStruct((N_TOK, LANES), F32),
        ],
        compiler_params=_cparams(("arbitrary",)),
        name="outproj_router",
    )(oa, ob, x2, mod3, ga, gb, w_out, gm, wr2, br)


N_BLOCKS = (N_TOK * TOP_K) // MOE_BLOCK + N_EXPERTS
P_ROWS = N_BLOCKS * MOE_BLOCK


def _pack_bf16_pairs(x):
    xb = x.astype(BF16).astype(F32)
    half = x.shape[1] // 2
    lo = lax.shift_right_logical(pltpu.bitcast(xb[:, :half], jnp.uint32), jnp.uint32(16))
    hi = pltpu.bitcast(xb[:, half:], jnp.uint32) & jnp.uint32(0xFFFF0000)
    return lo | hi


def _unpack_bf16_pairs(w):
    return (pltpu.bitcast(lax.shift_left(w, jnp.uint32(16)), F32),
            pltpu.bitcast(w & jnp.uint32(0xFFFF0000), F32))


def _expert_kernel(be_ref, tok_ref, nused_ref, nxt_ref, h2_hbm, wg_hbm, wu_hbm, wd_hbm, y_ref, xbuf, sem,
                   wgf, wuf, wdf, wsem, wgb, wub, wdb):
    i = pl.program_id(0)
    nused = nused_ref[0]

    def weight_copies(e):
        return (pltpu.make_async_copy(wg_hbm.at[e], wgf, wsem.at[0]),
                pltpu.make_async_copy(wu_hbm.at[e], wuf, wsem.at[1]),
                pltpu.make_async_copy(wd_hbm.at[e], wdf, wsem.at[2]))

    def issue(blk, slot):
        def body(g, _):
            for q in range(SUBLANES):
                tok = tok_ref[blk * MOE_BLOCK + g * SUBLANES + q]
                pltpu.make_async_copy(h2_hbm.at[tok], xbuf.at[slot, g, :, q, :], sem.at[slot]).start(
                    priority=q % 2)
            return 0
        lax.fori_loop(0, MOE_BLOCK // SUBLANES, body, 0)

    @pl.when(i == 0)
    def _():
        issue(0, 0)

    @pl.when(i + 1 < nused)
    def _():
        issue(i + 1, (i + 1) % 2)

    e = be_ref[i]
    new_expert = ((i == 0) | (e != be_ref[jnp.maximum(i - 1, 0)])) & (i < nused)

    @pl.when(i == 0)
    def _():
        for cp in weight_copies(e):
            cp.start()

    @pl.when(new_expert)
    def _():
        for cp in weight_copies(e):
            cp.wait()
        wgb[...] = wgf[...].astype(BF16)
        wub[...] = wuf[...].astype(BF16)
        wdb[...] = wdf[...].astype(BF16)

    @pl.when(new_expert & (nxt_ref[e] >= 0))
    def _():
        for cp in weight_copies(nxt_ref[e]):
            cp.start()

    @pl.when(i < nused)
    def _():
        slot = i % 2
        pltpu.make_async_copy(xbuf.at[slot], xbuf.at[slot], sem.at[slot]).wait()
        gsub = MOE_SUB // SUBLANES
        xbs = []
        for k in range(MOE_BLOCK // MOE_SUB):
            xu = jnp.concatenate([xbuf[slot, k * gsub:(k + 1) * gsub, s].reshape(MOE_SUB, LANES)
                                  for s in range(PACK_ROWS)], axis=1)
            xbs.append(jnp.concatenate(_unpack_bf16_pairs(xu), axis=1).astype(BF16))
        gs = [jnp.dot(xb, wgb[...], preferred_element_type=F32) for xb in xbs]
        us = [jnp.dot(xb, wub[...], preferred_element_type=F32) for xb in xbs]
        hids = [(jax.nn.silu(g) * u).astype(BF16) for g, u in zip(gs, us)]
        ys = [jnp.dot(hid, wdb[...], preferred_element_type=F32) for hid in hids]
        for k, y in enumerate(ys):
            y_ref[k * MOE_SUB:(k + 1) * MOE_SUB, :] = _pack_bf16_pairs(y)

    @pl.when(i >= nused)
    def _():
        y_ref[...] = jnp.zeros_like(y_ref)


def _experts(block_e, row_tok, nused, next_used, h2p, wg, wu, wd):
    grid_spec = pltpu.PrefetchScalarGridSpec(
        num_scalar_prefetch=4,
        grid=(N_BLOCKS,),
        in_specs=[pl.BlockSpec(memory_space=pl.ANY)] * 4,
        out_specs=pl.BlockSpec((MOE_BLOCK, PACKED_COLS), lambda i, be, tk, nu, nx: (i, 0)),
        scratch_shapes=[pltpu.VMEM((2, MOE_BLOCK // SUBLANES, PACK_ROWS, SUBLANES, LANES), jnp.uint32),
                        pltpu.SemaphoreType.DMA((2,)),
                        pltpu.VMEM((D_MODEL, D_EXPERT), F32),
                        pltpu.VMEM((D_MODEL, D_EXPERT), F32),
                        pltpu.VMEM((D_EXPERT, D_MODEL), F32),
                        pltpu.SemaphoreType.DMA((3,)),
                        pltpu.VMEM((D_MODEL, D_EXPERT), BF16),
                        pltpu.VMEM((D_MODEL, D_EXPERT), BF16),
                        pltpu.VMEM((D_EXPERT, D_MODEL), BF16)],
    )
    return pl.pallas_call(
        _expert_kernel,
        grid_spec=grid_spec,
        out_shape=jax.ShapeDtypeStruct((P_ROWS, PACKED_COLS), jnp.uint32),
        compiler_params=_cparams(("arbitrary",)),
        name="expert_ffn",
    )(block_e, row_tok, nused, next_used, h2p, wg, wu, wd)


CMB_TM = 128


def _combine_kernel(dest_ref, ys_hbm, x1_ref, gates_ref, mod_ref, fg_ref, o_ref, ybuf, sem):
    i = pl.program_id(0)
    n = pl.num_programs(0)

    def issue(tile, slot):
        def body(g, _):
            for q in range(DMA_UNROLL):
                r = g * DMA_UNROLL + q
                for k in range(TOP_K):
                    d = dest_ref[(tile * CMB_TM + r) * TOP_K + k]
                    pltpu.make_async_copy(ys_hbm.at[pl.ds(d, 1), :],
                                          ybuf.at[slot, pl.ds(k * CMB_TM + r, 1), :],
                                          sem.at[slot]).start(priority=k)
            return 0
        lax.fori_loop(0, CMB_TM // DMA_UNROLL, body, 0)

    @pl.when(i == 0)
    def _():
        issue(0, 0)

    @pl.when(i + 1 < n)
    def _():
        issue(i + 1, (i + 1) % 2)

    slot = i % 2
    pltpu.make_async_copy(ybuf.at[slot], ybuf.at[slot], sem.at[slot]).wait()
    gts = gates_ref[...]
    w0 = gts[:, 0:1]
    w1 = gts[:, 1:2]
    y0 = _unpack_bf16_pairs(ybuf[slot, 0:CMB_TM, :])
    y1 = _unpack_bf16_pairs(ybuf[slot, CMB_TM:2 * CMB_TM, :])
    y = jnp.concatenate([y0[part] * w0 + y1[part] * w1 for part in range(2)], axis=1)
    x2 = x1_ref[...] + mod_ref[0, 5:6, :] * y
    o_ref[...] = _rms(x2) * fg_ref[...]


def _combine(dest, ys, x1, gates, mod3, final_g):
    tiles_per_batch = SEQ // CMB_TM
    grid_spec = pltpu.PrefetchScalarGridSpec(
        num_scalar_prefetch=1,
        grid=(N_TOK // CMB_TM,),
        in_specs=[
            pl.BlockSpec(memory_space=pl.ANY),
            pl.BlockSpec((CMB_TM, D_MODEL), lambda i, d: (i, 0)),
            pl.BlockSpec((CMB_TM, LANES), lambda i, d: (i, 0)),
            pl.BlockSpec((1, 6, D_MODEL), lambda i, d: (i // tiles_per_batch, 0, 0)),
            pl.BlockSpec((1, D_MODEL), lambda i, d: (0, 0)),
        ],
        out_specs=pl.BlockSpec((CMB_TM, D_MODEL), lambda i, d: (i, 0)),
        scratch_shapes=[pltpu.VMEM((2, TOP_K * CMB_TM, PACKED_COLS), jnp.uint32),
                        pltpu.SemaphoreType.DMA((2,))],
    )
    return pl.pallas_call(
        _combine_kernel,
        grid_spec=grid_spec,
        out_shape=jax.ShapeDtypeStruct((N_TOK, D_MODEL), F32),
        compiler_params=_cparams(("arbitrary",)),
        name="combine_final",
    )(dest, ys, x1, gates, mod3, final_g)


def _routing_tables(eid):
    a = eid.shape[0]
    onehot = (eid[:, None] == jnp.arange(N_EXPERTS, dtype=jnp.int32)[None, :]).astype(jnp.int32)
    csum = jnp.cumsum(onehot, axis=0)
    rank = jnp.sum(onehot * csum, axis=1) - 1
    counts = csum[-1]
    padded = (counts + MOE_BLOCK - 1) // MOE_BLOCK * MOE_BLOCK
    pend = jnp.cumsum(padded)
    pstart = pend - padded
    dest = (pstart[eid] + rank).astype(jnp.int32)
    tok = jnp.arange(a, dtype=jnp.int32) // TOP_K
    row_tok = jnp.zeros((P_ROWS,), jnp.int32).at[dest].set(tok, unique_indices=True, mode='promise_in_bounds')
    blk_row = jnp.arange(N_BLOCKS, dtype=jnp.int32) * MOE_BLOCK
    block_e = jnp.minimum(jnp.sum((pend[None, :] <= blk_row[:, None]).astype(jnp.int32), axis=1),
                          N_EXPERTS - 1).astype(jnp.int32)
    nused = (pend[-1] // MOE_BLOCK).astype(jnp.int32).reshape(1)
    ids = jnp.arange(N_EXPERTS, dtype=jnp.int32)
    later_used = (ids[None, :] > ids[:, None]) & (counts[None, :] > 0)
    next_used = jnp.min(jnp.where(later_used, ids[None, :], N_EXPERTS), axis=1)
    next_used = jnp.where(next_used == N_EXPERTS, -1, next_used).astype(jnp.int32)
    return dest, row_tok, block_e, nused, next_used


def kernel(x, c, w_ada, b_ada, norm_mix_g, w_in, b_forget, sinks, out_norm_swa_g, out_norm_fox_g, w_out,
           norm_moe_g, w_group, b_group, w_expert, b_expert, w_gate, w_up, w_down, final_g):
    assert IN_TM == FOX_T
    x2 = x.reshape(N_TOK, D_MODEL)
    perm = _swa_perm()

    w_in0 = w_in[0]
    w_row = jnp.concatenate([w_in0[:, :SWA_WIDTH][:, perm], w_in0[:, SWA_WIDTH:VB_COL]], axis=1).astype(BF16)
    w_vt = w_in0[:, VB_COL:MAIN_COLS].T.astype(BF16)
    w_f = jnp.pad(w_in0[:, MAIN_COLS:], ((0, 0), (0, LANES - N_HEADS_FOX))).astype(BF16)
    colscale = jnp.concatenate([
        jnp.full((SWA_WIDTH,), HEAD_DIM ** -0.5 * LOG2E, F32),
        jnp.ones((2 * KV_WIDTH,), F32),
        jnp.full((FOX_WIDTH,), HEAD_DIM ** -0.5 * LOG2E, F32),
        jnp.ones((FOX_WIDTH,), F32)]).reshape(1, ROW_COLS)
    bf_row = jnp.pad(b_forget[0], (0, LANES - N_HEADS_FOX)).reshape(1, LANES)
    w_out_p = jnp.concatenate([w_out[0][:SWA_WIDTH][perm], w_out[0][SWA_WIDTH:]], axis=0).astype(BF16)
    ga = out_norm_swa_g[0][perm].reshape(1, SWA_WIDTH)
    gb = out_norm_fox_g[0].reshape(1, FOX_WIDTH)
    wr = jnp.pad(jnp.concatenate([w_group[0], w_expert[0]], axis=1),
                 ((0, 0), (0, LANES - N_GROUPS - N_EXPERTS)))
    br = jnp.pad(jnp.concatenate([b_group[0], b_expert[0]]), (0, LANES - N_GROUPS - N_EXPERTS)).reshape(1, LANES)
    wg, wu, wd = w_gate[0], w_up[0], w_down[0]

    mod = _adaln(c, w_ada[0], b_ada[0])
    mod3 = mod.reshape(BATCH, 6, D_MODEL)

    proj, f, vt5 = _inproj(x2, mod3, norm_mix_g[0].reshape(1, D_MODEL), w_row, w_f, colscale, w_vt)
    bias = _forget_bias(f, bf_row)

    o_a = _swa(sinks[0], proj)
    o_b = _fox(proj, vt5, bias)

    x1, h2, ids, gates = _outproj(o_a, o_b, x2, mod3, ga, gb, w_out_p,
                                  norm_moe_g[0].reshape(1, D_MODEL), wr, br)

    eid = ids[:, :TOP_K].reshape(-1)
    dest, row_tok, block_e, nused, next_used = _routing_tables(eid)
    ys = _experts(block_e, row_tok, nused, next_used, h2, wg, wu, wd)
    out = _combine(dest, ys, x1, gates, mod3, final_g.reshape(1, D_MODEL))
    return out.reshape(BATCH, SEQ, D_MODEL)
```

```python
import functools
import math

import numpy as np
import jax
import jax.numpy as jnp
from jax import lax
from jax.experimental import pallas as pl
from jax.experimental.pallas import tpu as pltpu

F32 = jnp.float32
BF16 = jnp.bfloat16

D_MODEL = 2048
BATCH = 2
SEQ = 8192
N_TOK = BATCH * SEQ
HEAD_DIM = 64
N_HEADS_SWA = 16
N_KV_SWA = 4
N_HEADS_FOX = 16
WINDOW = 128
SWA_WIDTH = N_HEADS_SWA * HEAD_DIM
KV_WIDTH = N_KV_SWA * HEAD_DIM
FOX_WIDTH = N_HEADS_FOX * HEAD_DIM
MAIN_COLS = SWA_WIDTH + 2 * KV_WIDTH + 3 * FOX_WIDTH
N_GROUPS = 4
EXPERTS_PER_GROUP = 8
N_EXPERTS = N_GROUPS * EXPERTS_PER_GROUP
TOP_K = 2
D_EXPERT = 512
MOE_BLOCK = 256
MOE_SUB = 128
EPS = 1e-6
LOG2E = math.log2(math.e)

LANES = 128
VMEM_LIMIT = 56 * 1024 * 1024

QA_BLK = 0
KA_COL = SWA_WIDTH
VA_COL = SWA_WIDTH + KV_WIDTH
QB_BLK = (SWA_WIDTH + 2 * KV_WIDTH) // LANES
KB_BLK = QB_BLK + FOX_WIDTH // LANES
VB_COL = SWA_WIDTH + 2 * KV_WIDTH + 2 * FOX_WIDTH
ROW_COLS = VB_COL


def _cparams(sem, vmem=VMEM_LIMIT):
    return pltpu.CompilerParams(dimension_semantics=sem, vmem_limit_bytes=vmem)


ADA_TN = 1024


def _adaln_kernel(cb_ref, w_ref, b_ref, o_ref):
    for b in range(BATCH):
        s = jax.nn.silu(cb_ref[b])
        cols = []
        for j in range(ADA_TN // LANES):
            prod = w_ref[:, j * LANES:(j + 1) * LANES] * s
            cols.append(jnp.sum(prod, axis=0, keepdims=True))
        o_ref[b:b + 1, :] = jnp.concatenate(cols, axis=1) + b_ref[...]


def _adaln(c, w_ada, b_ada):
    ncol = w_ada.shape[1]
    cb = jnp.broadcast_to(c[:, :, None], (BATCH, D_MODEL, LANES))
    return pl.pallas_call(
        _adaln_kernel,
        grid=(ncol // ADA_TN,),
        in_specs=[
            pl.BlockSpec((BATCH, D_MODEL, LANES), lambda j: (0, 0, 0)),
            pl.BlockSpec((D_MODEL, ADA_TN), lambda j: (0, j)),
            pl.BlockSpec((1, ADA_TN), lambda j: (0, j)),
        ],
        out_specs=pl.BlockSpec((BATCH, ADA_TN), lambda j: (0, j)),
        out_shape=jax.ShapeDtypeStruct((BATCH, ncol), F32),
        compiler_params=_cparams(("arbitrary",)),
        name="adaln",
    )(cb, w_ada, b_ada.reshape(1, ncol))


def _rms(x):
    return x * lax.rsqrt(jnp.mean(x * x, axis=-1, keepdims=True) + EPS)


IN_TM = 512
IN_TN = 1792
IN_NJ = ROW_COLS // IN_TN


def _inproj_kernel(x_ref, mod_ref, g_ref, w_ref, wf_ref, cs_ref, wvt_ref, o_ref, f_ref, vt_ref, h_ref):
    j = pl.program_id(1)

    @pl.when(j == 0)
    def _():
        y = _rms(x_ref[...])
        h = (y * g_ref[...]) * (1.0 + mod_ref[0, 1:2, :]) + mod_ref[0, 0:1, :]
        hb = h.astype(BF16)
        h_ref[...] = hb
        f_ref[...] = jnp.dot(hb, wf_ref[...], preferred_element_type=F32)

    @pl.when(j < IN_NJ)
    def _():
        acc = jnp.dot(h_ref[...], w_ref[...], preferred_element_type=F32)
        o_ref[...] = (acc * cs_ref[...]).astype(BF16)

    @pl.when(j == IN_NJ)
    def _():
        vt = lax.dot_general(wvt_ref[...], h_ref[...], (((0,), (1,)), ((), ())),
                             preferred_element_type=F32)
        vt_ref[0, :, 0] = vt.reshape(N_HEADS_FOX // 2, LANES, IN_TM).astype(BF16)


def _inproj(x2, mod3, g, w_row, w_f, colscale, w_vt):
    tiles_per_batch = SEQ // IN_TM
    last = IN_NJ - 1
    return pl.pallas_call(
        _inproj_kernel,
        grid=(N_TOK // IN_TM, IN_NJ + 1),
        in_specs=[
            pl.BlockSpec((IN_TM, D_MODEL), lambda i, j: (i, 0)),
            pl.BlockSpec((1, 6, D_MODEL), lambda i, j: (i // tiles_per_batch, 0, 0)),
            pl.BlockSpec((1, D_MODEL), lambda i, j: (0, 0)),
            pl.BlockSpec((D_MODEL, IN_TN), lambda i, j: (0, jnp.minimum(j, last))),
            pl.BlockSpec((D_MODEL, LANES), lambda i, j: (0, 0)),
            pl.BlockSpec((1, IN_TN), lambda i, j: (0, jnp.minimum(j, last))),
            pl.BlockSpec((D_MODEL, FOX_WIDTH), lambda i, j: (0, 0)),
        ],
        out_specs=[
            pl.BlockSpec((IN_TM, IN_TN), lambda i, j: (i, jnp.minimum(j, last))),
            pl.BlockSpec((IN_TM, LANES), lambda i, j: (i, 0)),
            pl.BlockSpec((1, N_HEADS_FOX // 2, 1, LANES, IN_TM),
                         lambda i, j: (i // tiles_per_batch, 0, i % tiles_per_batch, 0, 0)),
        ],
        out_shape=[
            jax.ShapeDtypeStruct((N_TOK, ROW_COLS), BF16),
            jax.ShapeDtypeStruct((N_TOK, LANES), F32),
            jax.ShapeDtypeStruct((BATCH, N_HEADS_FOX // 2, SEQ // IN_TM, LANES, IN_TM), BF16),
        ],
        scratch_shapes=[pltpu.VMEM((IN_TM, D_MODEL), BF16)],
        compiler_params=_cparams(("arbitrary", "arbitrary")),
        name="inproj",
    )(x2, mod3, g, w_row, w_f, colscale, w_vt)


CUM_TS = 512
FOX_AUG = 3


def _cum_kernel(f_ref, bf_ref, tri_ref, sel_ref, aug_ref, carry_ref):
    @pl.when(pl.program_id(1) == 0)
    def _():
        carry_ref[...] = jnp.zeros_like(carry_ref)

    lf = jax.nn.log_sigmoid(f_ref[...] + bf_ref[...])
    l_hi = lf.astype(BF16)
    l_r = lf - l_hi.astype(F32)
    l_mid = l_r.astype(BF16)
    l_lo = (l_r - l_mid.astype(F32)).astype(BF16)
    c3 = jnp.dot(tri_ref[...], jnp.concatenate([l_hi, l_mid, l_lo], axis=1), preferred_element_type=F32)
    cs = (c3[:, :LANES] + (c3[:, LANES:2 * LANES] + c3[:, 2 * LANES:])) + carry_ref[...]
    carry_ref[...] = cs[CUM_TS - 1:CUM_TS, :]
    bias = cs * (-LOG2E)
    hi = bias.astype(BF16)
    r1 = bias - hi.astype(F32)
    mid = r1.astype(BF16)
    lo = (r1 - mid.astype(F32)).astype(BF16)
    pieces = jnp.concatenate([hi, mid, lo], axis=1)
    aug_ref[...] = jnp.dot(pieces, sel_ref[...], preferred_element_type=F32).astype(BF16)


def _aug_selector():
    sel = np.zeros((FOX_AUG * LANES, FOX_WIDTH), np.float32)
    for head in range(N_HEADS_FOX):
        for piece in range(FOX_AUG):
            sel[piece * LANES + head, (head // 2) * LANES + FOX_AUG * (head % 2) + piece] = 1.0
    return sel


def _forget_bias(f, b_forget_row):
    tri = jnp.tril(jnp.ones((CUM_TS, CUM_TS), BF16))
    sel = jnp.asarray(_aug_selector(), BF16)
    nblk = SEQ // CUM_TS
    return pl.pallas_call(
        _cum_kernel,
        grid=(BATCH, nblk),
        in_specs=[
            pl.BlockSpec((CUM_TS, LANES), lambda b, i: (b * nblk + i, 0)),
            pl.BlockSpec((1, LANES), lambda b, i: (0, 0)),
            pl.BlockSpec((CUM_TS, CUM_TS), lambda b, i: (0, 0)),
            pl.BlockSpec((FOX_AUG * LANES, FOX_WIDTH), lambda b, i: (0, 0)),
        ],
        out_specs=pl.BlockSpec((CUM_TS, FOX_WIDTH), lambda b, i: (b * nblk + i, 0)),
        out_shape=jax.ShapeDtypeStruct((N_TOK, FOX_WIDTH), BF16),
        scratch_shapes=[pltpu.VMEM((1, LANES), F32)],
        compiler_params=_cparams(("arbitrary", "arbitrary")),
        name="forget_bias",
    )(f, b_forget_row, tri, sel)


SWA_TQ = WINDOW


def _swa_perm():
    perm = np.zeros((SWA_WIDTH,), np.int32)
    for t in range(2):
        for g in range(4):
            for e in range(2):
                head = (2 * t + e) * 4 + g
                base = (t * 4 + g) * LANES + e * HEAD_DIM
                perm[base:base + HEAD_DIM] = head * HEAD_DIM + np.arange(HEAD_DIM)
    return perm


def _swa_kernel(sinks_ref, q_ref, kc_ref, kp_ref, vc_ref, vp_ref, o_ref, bias_ref):
    i = pl.program_id(1)
    tq = SWA_TQ

    @pl.when((pl.program_id(0) == 0) & (i == 0))
    def _():
        row = lax.broadcasted_iota(jnp.int32, (tq, 2 * tq), 0)
        col = lax.broadcasted_iota(jnp.int32, (tq, 2 * tq), 1)
        dist = row + tq - col
        band = (dist >= 0) & (dist < WINDOW)
        distf = dist.astype(F32)
        for head in range(N_HEADS_SWA):
            slope2 = float(2.0 ** (-8.0 * (head + 1) / N_HEADS_SWA)) * LOG2E
            base = jnp.where(band, -slope2 * distf, -jnp.inf)
            bias_ref[1, head] = base
            bias_ref[0, head] = jnp.where(col >= tq, base, -jnp.inf)

    table = jnp.minimum(i, 1)
    lane = lax.broadcasted_iota(jnp.int32, (tq, LANES), 1)
    lo_half = lane < HEAD_DIM
    for t in range(2):
        kt = jnp.concatenate([kp_ref[:, t * LANES:(t + 1) * LANES],
                              kc_ref[:, t * LANES:(t + 1) * LANES]], axis=0)
        vt = jnp.concatenate([vp_ref[:, t * LANES:(t + 1) * LANES],
                              vc_ref[:, t * LANES:(t + 1) * LANES]], axis=0)
        heads = [(g, e) for g in range(4) for e in range(2)]
        scores = []
        for g, e in heads:
            tile = t * 4 + g
            qt = q_ref[:, tile * LANES:(tile + 1) * LANES]
            qm = jnp.where(lo_half if e == 0 else ~lo_half, qt, jnp.zeros_like(qt))
            scores.append(lax.dot_general(qm, kt, (((1,), (1,)), ((), ())),
                                          preferred_element_type=F32))
        probs, rdens = [], []
        for (g, e), s in zip(heads, scores):
            head = (2 * t + e) * 4 + g
            sink = sinks_ref[head] * LOG2E
            s = s + bias_ref[table, head]
            m = jnp.maximum(jnp.max(s, axis=-1, keepdims=True), sink)
            p = jnp.exp2(s - m)
            rdens.append(1.0 / (jnp.sum(p, axis=-1, keepdims=True) + jnp.exp2(sink - m)))
            probs.append(p.astype(BF16))
        outs = [jnp.dot(p, vt, preferred_element_type=F32) * r for p, r in zip(probs, rdens)]
        for g in range(4):
            tile = t * 4 + g
            o_ref[:, tile * LANES:(tile + 1) * LANES] = jnp.where(lo_half, outs[2 * g], outs[2 * g + 1]).astype(BF16)


def _swa(sinks, proj):
    nq = SEQ // SWA_TQ
    kblk = KA_COL // KV_WIDTH
    vblk = VA_COL // KV_WIDTH
    grid_spec = pltpu.PrefetchScalarGridSpec(
        num_scalar_prefetch=1,
        grid=(BATCH, nq),
        in_specs=[
            pl.BlockSpec((SWA_TQ, SWA_WIDTH), lambda b, i, s: (b * nq + i, 0)),
            pl.BlockSpec((SWA_TQ, KV_WIDTH), lambda b, i, s: (b * nq + i, kblk)),
            pl.BlockSpec((SWA_TQ, KV_WIDTH), lambda b, i, s: (b * nq + jnp.maximum(i - 1, 0), kblk)),
            pl.BlockSpec((SWA_TQ, KV_WIDTH), lambda b, i, s: (b * nq + i, vblk)),
            pl.BlockSpec((SWA_TQ, KV_WIDTH), lambda b, i, s: (b * nq + jnp.maximum(i - 1, 0), vblk)),
        ],
        out_specs=pl.BlockSpec((SWA_TQ, SWA_WIDTH), lambda b, i, s: (b * nq + i, 0)),
        scratch_shapes=[pltpu.VMEM((2, N_HEADS_SWA, SWA_TQ, 2 * SWA_TQ), F32)],
    )
    return pl.pallas_call(
        _swa_kernel,
        grid_spec=grid_spec,
        out_shape=jax.ShapeDtypeStruct((N_TOK, SWA_WIDTH), BF16),
        compiler_params=_cparams(("arbitrary", "arbitrary")),
        name="swa_attn",
    )(sinks, proj, proj, proj, proj, proj)


FOX_T = 512
FOX_ONES = 16
FOX_VROWS = HEAD_DIM + FOX_ONES


def _fox_kernel(q_ref, k_ref, aug_ref, vt_ref, o_ref, mask_ref, t_ref, acc_ref):
    qi = pl.program_id(2)
    t = FOX_T

    @pl.when(qi == 0)
    def _():
        kr = lax.broadcasted_iota(jnp.int32, (t, t), 0)
        qc = lax.broadcasted_iota(jnp.int32, (t, t), 1)
        mask_ref[...] = jnp.where(kr <= qc, 0.0, -jnp.inf)

    qf = q_ref[...].astype(F32).T
    drow = lax.broadcasted_iota(jnp.int32, (LANES, t), 0)
    qaug = []
    for hh in range(2):
        qh = jnp.where((drow >= HEAD_DIM * hh) & (drow < HEAD_DIM * (hh + 1)), qf, 0.0)
        ones = jnp.where((drow >= FOX_AUG * hh) & (drow < FOX_AUG * (hh + 1)), 1.0, 0.0)
        qaug.append(jnp.concatenate([qh, ones], axis=0).astype(BF16))
    acc_ref[...] = jnp.zeros_like(acc_ref)

    def stage_a(j, slot, diag):
        r0 = pl.multiple_of(j * t, t)
        kb = jnp.concatenate([k_ref[pl.ds(r0, t), :], aug_ref[pl.ds(r0, t), :]], axis=1)
        sts = [jnp.dot(kb, qaug[hh], preferred_element_type=F32) for hh in range(2)]
        mbs = []
        for hh in range(2):
            tt = sts[hh]
            if diag:
                tt = tt + mask_ref[...]
            t_ref[slot, hh] = tt
            mbs.append(jnp.max(tt, axis=0, keepdims=True))
        return tuple(mbs)

    def stage_b(jv, slot, ms, mbs):
        new, ps, alphas = [], [], []
        for hh in range(2):
            m_new = jnp.maximum(ms[hh], mbs[hh])
            alphas.append(jnp.exp2(ms[hh] - m_new))
            ps.append(jnp.exp2(t_ref[slot, hh] - m_new).astype(BF16))
            new.append(m_new)
        ones = jnp.ones((FOX_ONES, t), BF16)
        pvs = [jnp.dot(jnp.concatenate([vt_ref[0, 0, jv, hh * HEAD_DIM:(hh + 1) * HEAD_DIM, :], ones], axis=0),
                       ps[hh], preferred_element_type=F32)
               for hh in range(2)]
        for hh in range(2):
            acc_ref[hh] = alphas[hh] * acc_ref[hh] + pvs[hh]
        return tuple(new)

    ml0 = tuple(jnp.full((1, t), -jnp.inf, F32) for _ in range(2))
    mb0 = stage_a(qi, 0, True)

    def pair(j0, c):
        ml, mbs, jprev = c
        mb1 = stage_a(j0, 1, False)
        ml = stage_b(jprev, 0, ml, mbs)
        mb2 = stage_a(j0 + 1, 0, False)
        ml = stage_b(j0, 1, ml, mb1)
        return ml, mb2, j0 + 1

    def quad(ii, c):
        return pair(4 * ii + 2, pair(4 * ii, c))

    carry = lax.fori_loop(0, qi // 4, quad, (ml0, mb0, qi))
    carry = lax.cond(qi % 4 >= 2, lambda c: pair(4 * (qi // 4), c), lambda c: c, carry)
    ml, mbs, jprev = carry

    def odd_tail(c):
        ml, mbs, jprev = c
        mb1 = stage_a(qi - 1, 1, False)
        ml = stage_b(jprev, 0, ml, mbs)
        return stage_b(qi - 1, 1, ml, mb1)

    def even_tail(c):
        ml, mbs, jprev = c
        return stage_b(jprev, 0, ml, mbs)

    ml = lax.cond(qi % 2 == 1, odd_tail, even_tail, (ml, mbs, jprev))
    del ml
    ot = jnp.concatenate([acc_ref[hh, 0:HEAD_DIM, :] / acc_ref[hh, HEAD_DIM:HEAD_DIM + 1, :]
                          for hh in range(2)], axis=0)
    o_ref[...] = ot.T.astype(BF16)


def _fox(proj, aug, vt5):
    nq = SEQ // FOX_T
    npair = N_HEADS_FOX // 2
    return pl.pallas_call(
        _fox_kernel,
        grid=(BATCH, npair, nq),
        in_specs=[
            pl.BlockSpec((FOX_T, LANES), lambda b, h, i: (b * nq + i, QB_BLK + h)),
            pl.BlockSpec((SEQ, LANES), lambda b, h, i: (b, KB_BLK + h)),
            pl.BlockSpec((SEQ, LANES), lambda b, h, i: (b, h)),
            pl.BlockSpec((1, 1, nq, LANES, FOX_T), lambda b, h, i: (b, h, 0, 0, 0)),
        ],
        out_specs=pl.BlockSpec((FOX_T, LANES), lambda b, h, i: (b * nq + i, h)),
        out_shape=jax.ShapeDtypeStruct((N_TOK, FOX_WIDTH), BF16),
        scratch_shapes=[
            pltpu.VMEM((FOX_T, FOX_T), F32),
            pltpu.VMEM((2, 2, FOX_T, FOX_T), F32),
            pltpu.VMEM((2, FOX_VROWS, FOX_T), F32),
        ],
        compiler_params=_cparams(("arbitrary", "arbitrary", "arbitrary")),
        name="fox_attn",
    )(proj, proj, aug, vt5)


OUT_TM = 512
OUT_SUB = 256
SUBLANES = 8
PACKED_COLS = D_MODEL // 2
PACK_ROWS = PACKED_COLS // LANES
DMA_UNROLL = 8


def _outproj_kernel(oa_ref, ob_ref, x_ref, mod_ref, ga_ref, gb_ref, w_ref, gm_ref, wr2_ref, br_ref,
                    x1_ref, h2p_ref, ids_ref, gates_ref):
    subs = [slice(k * OUT_SUB, (k + 1) * OUT_SUB) for k in range(OUT_TM // OUT_SUB)]
    mixed = [jnp.concatenate([_rms(oa_ref[r, :].astype(F32)) * ga_ref[...],
                              _rms(ob_ref[r, :].astype(F32)) * gb_ref[...]], axis=1).astype(BF16) for r in subs]
    ys = [jnp.dot(m, w_ref[...], preferred_element_type=F32) for m in mixed]
    h2s = []
    for r, y in zip(subs, ys):
        x1 = x_ref[r, :] + mod_ref[0, 2:3, :] * y
        x1_ref[r, :] = x1
        h2s.append((_rms(x1) * gm_ref[...]) * (1.0 + mod_ref[0, 4:5, :]) + mod_ref[0, 3:4, :])
    hbs = [h2.astype(BF16) for h2 in h2s]
    for r, h2 in zip(subs, h2s):
        packed = _pack_bf16_pairs(h2)
        for s in range(PACK_ROWS):
            h2p_ref[r, s, :] = packed[:, s * LANES:(s + 1) * LANES]

    hls = [(h2 - hb.astype(F32)).astype(BF16) for h2, hb in zip(h2s, hbs)]
    r2s = [jnp.dot(hb, wr2_ref[...], preferred_element_type=F32) for hb in hbs]
    r3s = [jnp.dot(hl, wr2_ref[:, :LANES], preferred_element_type=F32) for hl in hls]
    for r, r2, r3 in zip(subs, r2s, r3s):
        logits = (r2[:, :LANES] + (r2[:, LANES:] + r3)) + br_ref[...]
        ids, gates = _route(logits)
        ids_ref[r, :] = ids
        gates_ref[r, :] = gates


def _route(logits):
    lane = lax.broadcasted_iota(jnp.int32, logits.shape, 1)
    neg = -jnp.inf
    is_g = lane < N_GROUPS
    gl = jnp.where(is_g, logits, neg)
    gmax = jnp.max(gl, axis=1, keepdims=True)
    gsel = jnp.min(jnp.where(gl == gmax, lane, LANES), axis=1, keepdims=True)
    gsum = jnp.sum(jnp.where(is_g, jnp.exp(gl - gmax), 0.0), axis=1, keepdims=True)
    g_val = 1.0 / gsum
    elane = lane - N_GROUPS
    in_sel = (elane >= 0) & (elane < N_EXPERTS) & ((elane >> 3) == gsel)
    ev = jnp.where(in_sel, logits, neg)
    t1 = jnp.max(ev, axis=1, keepdims=True)
    i1 = jnp.min(jnp.where(ev == t1, lane, LANES), axis=1, keepdims=True)
    ev2 = jnp.where(lane == i1, neg, ev)
    t2 = jnp.max(ev2, axis=1, keepdims=True)
    i2 = jnp.min(jnp.where(ev2 == t2, lane, LANES), axis=1, keepdims=True)
    e2 = jnp.exp(t2 - t1)
    den = 1.0 + e2
    w1 = (1.0 / den) * g_val
    w2 = (e2 / den) * g_val
    return (jnp.where(lane == 0, i1 - N_GROUPS, jnp.where(lane == 1, i2 - N_GROUPS, 0)),
            jnp.where(lane == 0, w1, jnp.where(lane == 1, w2, 0.0)))


def _outproj(oa, ob, x2, mod3, ga, gb, w_out, gm, wr, br):
    tiles_per_batch = SEQ // OUT_TM
    row = lambda i: (i, 0)
    const = lambda i: (0, 0)
    wr_hi = wr.astype(BF16)
    wr_lo = (wr - wr_hi.astype(F32)).astype(BF16)
    wr2 = jnp.concatenate([wr_hi, wr_lo], axis=1)
    return pl.pallas_call(
        _outproj_kernel,
        grid=(N_TOK // OUT_TM,),
        in_specs=[
            pl.BlockSpec((OUT_TM, SWA_WIDTH), row),
            pl.BlockSpec((OUT_TM, FOX_WIDTH), row),
            pl.BlockSpec((OUT_TM, D_MODEL), row),
            pl.BlockSpec((1, 6, D_MODEL), lambda i: (i // tiles_per_batch, 0, 0)),
            pl.BlockSpec((1, SWA_WIDTH), const),
            pl.BlockSpec((1, FOX_WIDTH), const),
            pl.BlockSpec((D_MODEL, D_MODEL), const),
            pl.BlockSpec((1, D_MODEL), const),
            pl.BlockSpec((D_MODEL, 2 * LANES), const),
            pl.BlockSpec((1, LANES), const),
        ],
        out_specs=[
            pl.BlockSpec((OUT_TM, D_MODEL), row),
            pl.BlockSpec((OUT_TM, PACK_ROWS, LANES), lambda i: (i, 0, 0)),
            pl.BlockSpec((OUT_TM, LANES), row),
            pl.BlockSpec((OUT_TM, LANES), row),
        ],
        out_shape=[
            jax.ShapeDtypeStruct((N_TOK, D_MODEL), F32),
            jax.ShapeDtypeStruct((N_TOK, PACK_ROWS, LANES), jnp.uint32),
            jax.ShapeDtypeStruct((N_TOK, LANES), jnp.int32),
            jax.ShapeDtypeStruct((N_TOK, LANES), F32),
        ],
        compiler_params=_cparams(("arbitrary",)),
        name="outproj_router",
    )(oa, ob, x2, mod3, ga, gb, w_out, gm, wr2, br)


N_BLOCKS = (N_TOK * TOP_K) // MOE_BLOCK + N_EXPERTS
P_ROWS = N_BLOCKS * MOE_BLOCK


def _pack_bf16_pairs(x):
    xb = x.astype(BF16).astype(F32)
    half = x.shape[1] // 2
    lo = lax.shift_right_logical(pltpu.bitcast(xb[:, :half], jnp.uint32), jnp.uint32(16))
    hi = pltpu.bitcast(xb[:, half:], jnp.uint32) & jnp.uint32(0xFFFF0000)
    return lo | hi


def _unpack_bf16_pairs(w):
    return (pltpu.bitcast(lax.shift_left(w, jnp.uint32(16)), F32),
            pltpu.bitcast(w & jnp.uint32(0xFFFF0000), F32))


def _expert_kernel(be_ref, tok_ref, nused_ref, nxt_ref, h2_hbm, wg_hbm, wu_hbm, wd_hbm, y_hbm, xbuf, sem,
                   wgf, wuf, wdf, wsem, wgb, wub, wdb, ystage, osem):
    i = pl.program_id(0)
    nused = nused_ref[0]

    def weight_copies(e):
        return (pltpu.make_async_copy(wg_hbm.at[e], wgf, wsem.at[0]),
                pltpu.make_async_copy(wu_hbm.at[e], wuf, wsem.at[1]),
                pltpu.make_async_copy(wd_hbm.at[e], wdf, wsem.at[2]))

    def issue(blk, slot):
        def body(g, _):
            for q in range(SUBLANES):
                tok = tok_ref[blk * MOE_BLOCK + g * SUBLANES + q]
                pltpu.make_async_copy(h2_hbm.at[tok], xbuf.at[slot, g, :, q, :], sem.at[slot]).start(
                    priority=q % 2)
            return 0
        lax.fori_loop(0, MOE_BLOCK // SUBLANES, body, 0)

    def out_copies(blk, slot):
        r0 = pl.multiple_of(blk * MOE_BLOCK, MOE_BLOCK)
        return [pltpu.make_async_copy(ystage.at[slot, :, pl.ds(s * LANES, LANES)],
                                      y_hbm.at[pl.ds(r0, MOE_BLOCK), s, :], osem.at[slot])
                for s in range(PACK_ROWS)]

    def out_wait(slot):
        pltpu.make_async_copy(ystage.at[slot], ystage.at[slot], osem.at[slot]).wait()

    @pl.when(i == 0)
    def _():
        issue(0, 0)

    @pl.when(i + 1 < nused)
    def _():
        issue(i + 1, (i + 1) % 2)

    oslot = i % 2

    @pl.when(i >= 2)
    def _():
        out_wait(oslot)

    e = be_ref[i]
    new_expert = ((i == 0) | (e != be_ref[jnp.maximum(i - 1, 0)])) & (i < nused)

    @pl.when(i == 0)
    def _():
        for cp in weight_copies(e):
            cp.start()

    @pl.when(new_expert)
    def _():
        for cp in weight_copies(e):
            cp.wait()
        wgb[...] = wgf[...].astype(BF16)
        wub[...] = wuf[...].astype(BF16)
        wdb[...] = wdf[...].astype(BF16)

    @pl.when(new_expert & (nxt_ref[e] >= 0))
    def _():
        for cp in weight_copies(nxt_ref[e]):
            cp.start()

    @pl.when(i < nused)
    def _():
        slot = i % 2
        pltpu.make_async_copy(xbuf.at[slot], xbuf.at[slot], sem.at[slot]).wait()
        gsub = MOE_SUB // SUBLANES
        xbs = []
        for k in range(MOE_BLOCK // MOE_SUB):
            xu = jnp.concatenate([xbuf[slot, k * gsub:(k + 1) * gsub, s].reshape(MOE_SUB, LANES)
                                  for s in range(PACK_ROWS)], axis=1)
            xbs.append(jnp.concatenate(_unpack_bf16_pairs(xu), axis=1).astype(BF16))
        gs = [jnp.dot(xb, wgb[...], preferred_element_type=F32) for xb in xbs]
        us = [jnp.dot(xb, wub[...], preferred_element_type=F32) for xb in xbs]
        hids = [(jax.nn.silu(g) * u).astype(BF16) for g, u in zip(gs, us)]
        ys = [jnp.dot(hid, wdb[...], preferred_element_type=F32) for hid in hids]
        for k, y in enumerate(ys):
            ystage[oslot, k * MOE_SUB:(k + 1) * MOE_SUB, :] = _pack_bf16_pairs(y)

    @pl.when(i >= nused)
    def _():
        ystage[oslot] = jnp.zeros((MOE_BLOCK, PACKED_COLS), jnp.uint32)

    for cp in out_copies(i, oslot):
        cp.start()

    @pl.when(i == pl.num_programs(0) - 1)
    def _():
        out_wait(oslot)
        out_wait(1 - oslot)


def _experts(block_e, row_tok, nused, next_used, h2p, wg, wu, wd):
    grid_spec = pltpu.PrefetchScalarGridSpec(
        num_scalar_prefetch=4,
        grid=(N_BLOCKS,),
        in_specs=[pl.BlockSpec(memory_space=pl.ANY)] * 4,
        out_specs=pl.BlockSpec(memory_space=pl.ANY),
        scratch_shapes=[pltpu.VMEM((2, MOE_BLOCK // SUBLANES, PACK_ROWS, SUBLANES, LANES), jnp.uint32),
                        pltpu.SemaphoreType.DMA((2,)),
                        pltpu.VMEM((D_MODEL, D_EXPERT), F32),
                        pltpu.VMEM((D_MODEL, D_EXPERT), F32),
                        pltpu.VMEM((D_EXPERT, D_MODEL), F32),
                        pltpu.SemaphoreType.DMA((3,)),
                        pltpu.VMEM((D_MODEL, D_EXPERT), BF16),
                        pltpu.VMEM((D_MODEL, D_EXPERT), BF16),
                        pltpu.VMEM((D_EXPERT, D_MODEL), BF16),
                        pltpu.VMEM((2, MOE_BLOCK, PACKED_COLS), jnp.uint32),
                        pltpu.SemaphoreType.DMA((2,))],
    )
    return pl.pallas_call(
        _expert_kernel,
        grid_spec=grid_spec,
        out_shape=jax.ShapeDtypeStruct((P_ROWS, PACK_ROWS, LANES), jnp.uint32),
        compiler_params=_cparams(("arbitrary",)),
        name="expert_ffn",
    )(block_e, row_tok, nused, next_used, h2p, wg, wu, wd)


CMB_TM = 128


def _combine_kernel(dest_ref, ys_hbm, x1_ref, gates_ref, mod_ref, fg_ref, o_ref, ybuf, sem):
    i = pl.program_id(0)
    n = pl.num_programs(0)
    groups = CMB_TM // SUBLANES

    def issue(tile, slot):
        def body(g, _):
            for q in range(SUBLANES):
                for k in range(TOP_K):
                    d = dest_ref[(tile * CMB_TM + g * SUBLANES + q) * TOP_K + k]
                    pltpu.make_async_copy(ys_hbm.at[d], ybuf.at[slot, k * groups + g, :, q, :],
                                          sem.at[slot]).start(priority=k)
            return 0
        lax.fori_loop(0, groups, body, 0)

    @pl.when(i == 0)
    def _():
        issue(0, 0)

    @pl.when(i + 1 < n)
    def _():
        issue(i + 1, (i + 1) % 2)

    slot = i % 2
    pltpu.make_async_copy(ybuf.at[slot], ybuf.at[slot], sem.at[slot]).wait()
    gts = gates_ref[...]
    w0 = gts[:, 0:1]
    w1 = gts[:, 1:2]
    half = D_MODEL // 2
    ssq = jnp.zeros((CMB_TM, 1), F32)
    for s in range(PACK_ROWS):
        y0 = _unpack_bf16_pairs(ybuf[slot, 0:groups, s].reshape(CMB_TM, LANES))
        y1 = _unpack_bf16_pairs(ybuf[slot, groups:2 * groups, s].reshape(CMB_TM, LANES))
        for part in range(2):
            cols = slice(part * half + s * LANES, part * half + (s + 1) * LANES)
            x2 = x1_ref[:, cols] + mod_ref[0, 5:6, cols] * (y0[part] * w0 + y1[part] * w1)
            ssq = ssq + jnp.sum(x2 * x2, axis=-1, keepdims=True)
            o_ref[:, cols] = x2
    o_ref[...] = (o_ref[...] * lax.rsqrt(ssq * (1.0 / D_MODEL) + EPS)) * fg_ref[...]


def _combine(dest, ys, x1, gates, mod3, final_g):
    tiles_per_batch = SEQ // CMB_TM
    grid_spec = pltpu.PrefetchScalarGridSpec(
        num_scalar_prefetch=1,
        grid=(N_TOK // CMB_TM,),
        in_specs=[
            pl.BlockSpec(memory_space=pl.ANY),
            pl.BlockSpec((CMB_TM, D_MODEL), lambda i, d: (i, 0)),
            pl.BlockSpec((CMB_TM, LANES), lambda i, d: (i, 0)),
            pl.BlockSpec((1, 6, D_MODEL), lambda i, d: (i // tiles_per_batch, 0, 0)),
            pl.BlockSpec((1, D_MODEL), lambda i, d: (0, 0)),
        ],
        out_specs=pl.BlockSpec((CMB_TM, D_MODEL), lambda i, d: (i, 0)),
        scratch_shapes=[pltpu.VMEM((2, TOP_K * CMB_TM // SUBLANES, PACK_ROWS, SUBLANES, LANES), jnp.uint32),
                        pltpu.SemaphoreType.DMA((2,))],
    )
    return pl.pallas_call(
        _combine_kernel,
        grid_spec=grid_spec,
        out_shape=jax.ShapeDtypeStruct((N_TOK, D_MODEL), F32),
        compiler_params=_cparams(("arbitrary",)),
        name="combine_final",
    )(dest, ys, x1, gates, mod3, final_g)


def _routing_tables(eid):
    a = eid.shape[0]
    onehot = (eid[:, None] == jnp.arange(N_EXPERTS, dtype=jnp.int32)[None, :]).astype(jnp.int32)
    csum = jnp.cumsum(onehot, axis=0)
    rank = jnp.sum(onehot * csum, axis=1) - 1
    counts = csum[-1]
    padded = (counts + MOE_BLOCK - 1) // MOE_BLOCK * MOE_BLOCK
    pend = jnp.cumsum(padded)
    pstart = pend - padded
    dest = (pstart[eid] + rank).astype(jnp.int32)
    tok = jnp.arange(a, dtype=jnp.int32) // TOP_K
    row_tok = jnp.zeros((P_ROWS,), jnp.int32).at[dest].set(tok, unique_indices=True, mode='promise_in_bounds')
    blk_row = jnp.arange(N_BLOCKS, dtype=jnp.int32) * MOE_BLOCK
    block_e = jnp.minimum(jnp.sum((pend[None, :] <= blk_row[:, None]).astype(jnp.int32), axis=1),
                          N_EXPERTS - 1).astype(jnp.int32)
    nused = (pend[-1] // MOE_BLOCK).astype(jnp.int32).reshape(1)
    ids = jnp.arange(N_EXPERTS, dtype=jnp.int32)
    later_used = (ids[None, :] > ids[:, None]) & (counts[None, :] > 0)
    next_used = jnp.min(jnp.where(later_used, ids[None, :], N_EXPERTS), axis=1)
    next_used = jnp.where(next_used == N_EXPERTS, -1, next_used).astype(jnp.int32)
    return dest, row_tok, block_e, nused, next_used


def kernel(x, c, w_ada, b_ada, norm_mix_g, w_in, b_forget, sinks, out_norm_swa_g, out_norm_fox_g, w_out,
           norm_moe_g, w_group, b_group, w_expert, b_expert, w_gate, w_up, w_down, final_g):
    assert IN_TM == FOX_T
    x2 = x.reshape(N_TOK, D_MODEL)
    perm = _swa_perm()

    w_in0 = w_in[0]
    w_row = jnp.concatenate([w_in0[:, :SWA_WIDTH][:, perm], w_in0[:, SWA_WIDTH:VB_COL]], axis=1).astype(BF16)
    w_vt = w_in0[:, VB_COL:MAIN_COLS].astype(BF16)
    w_f = jnp.pad(w_in0[:, MAIN_COLS:], ((0, 0), (0, LANES - N_HEADS_FOX))).astype(BF16)
    colscale = jnp.concatenate([
        jnp.full((SWA_WIDTH,), HEAD_DIM ** -0.5 * LOG2E, F32),
        jnp.ones((2 * KV_WIDTH,), F32),
        jnp.full((FOX_WIDTH,), HEAD_DIM ** -0.5 * LOG2E, F32),
        jnp.ones((FOX_WIDTH,), F32)]).reshape(1, ROW_COLS)
    bf_row = jnp.pad(b_forget[0], (0, LANES - N_HEADS_FOX)).reshape(1, LANES)
    w_out_p = jnp.concatenate([w_out[0][:SWA_WIDTH][perm], w_out[0][SWA_WIDTH:]], axis=0).astype(BF16)
    ga = out_norm_swa_g[0][perm].reshape(1, SWA_WIDTH)
    gb = out_norm_fox_g[0].reshape(1, FOX_WIDTH)
    wr = jnp.pad(jnp.concatenate([w_group[0], w_expert[0]], axis=1),
                 ((0, 0), (0, LANES - N_GROUPS - N_EXPERTS)))
    br = jnp.pad(jnp.concatenate([b_group[0], b_expert[0]]), (0, LANES - N_GROUPS - N_EXPERTS)).reshape(1, LANES)
    wg, wu, wd = w_gate[0], w_up[0], w_down[0]

    mod = _adaln(c, w_ada[0], b_ada[0])
    mod3 = mod.reshape(BATCH, 6, D_MODEL)

    proj, f, vt5 = _inproj(x2, mod3, norm_mix_g[0].reshape(1, D_MODEL), w_row, w_f, colscale, w_vt)
    aug = _forget_bias(f, bf_row)

    o_a = _swa(sinks[0], proj)
    o_b = _fox(proj, aug, vt5)

    x1, h2, ids, gates = _outproj(o_a, o_b, x2, mod3, ga, gb, w_out_p,
                                  norm_moe_g[0].reshape(1, D_MODEL), wr, br)

    eid = ids[:, :TOP_K].reshape(-1)
    dest, row_tok, block_e, nused, next_used = _routing_tables(eid)
    ys = _experts(block_e, row_tok, nused, next_used, h2, wg, wu, wd)
    out = _combine(dest, ys, x1, gates, mod3, final_g.reshape(1, D_MODEL))
    return out.reshape(BATCH, SEQ, D_MODEL)
```

```python
import functools
import math

import numpy as np
import jax
import jax.numpy as jnp
from jax import lax
from jax.experimental import pallas as pl
from jax.experimental.pallas import tpu as pltpu

F32 = jnp.float32
BF16 = jnp.bfloat16

D_MODEL = 2048
BATCH = 2
SEQ = 8192
N_TOK = BATCH * SEQ
HEAD_DIM = 64
N_HEADS_SWA = 16
N_KV_SWA = 4
N_HEADS_FOX = 16
WINDOW = 128
SWA_WIDTH = N_HEADS_SWA * HEAD_DIM
KV_WIDTH = N_KV_SWA * HEAD_DIM
FOX_WIDTH = N_HEADS_FOX * HEAD_DIM
MAIN_COLS = SWA_WIDTH + 2 * KV_WIDTH + 3 * FOX_WIDTH
N_GROUPS = 4
EXPERTS_PER_GROUP = 8
N_EXPERTS = N_GROUPS * EXPERTS_PER_GROUP
TOP_K = 2
D_EXPERT = 512
MOE_BLOCK = 256
MOE_SUB = 128
EPS = 1e-6
LOG2E = math.log2(math.e)

LANES = 128
VMEM_LIMIT = 56 * 1024 * 1024

QA_BLK = 0
KA_COL = SWA_WIDTH
VA_COL = SWA_WIDTH + KV_WIDTH
QB_BLK = (SWA_WIDTH + 2 * KV_WIDTH) // LANES
KB_BLK = QB_BLK + FOX_WIDTH // LANES
VB_COL = SWA_WIDTH + 2 * KV_WIDTH + 2 * FOX_WIDTH
ROW_COLS = VB_COL


def _cparams(sem, vmem=VMEM_LIMIT):
    return pltpu.CompilerParams(dimension_semantics=sem, vmem_limit_bytes=vmem)


ADA_TN = 1024


def _adaln_kernel(cb_ref, w_ref, b_ref, o_ref):
    for b in range(BATCH):
        s = jax.nn.silu(cb_ref[b])
        cols = []
        for j in range(ADA_TN // LANES):
            prod = w_ref[:, j * LANES:(j + 1) * LANES] * s
            cols.append(jnp.sum(prod, axis=0, keepdims=True))
        o_ref[b:b + 1, :] = jnp.concatenate(cols, axis=1) + b_ref[...]


def _adaln(c, w_ada, b_ada):
    ncol = w_ada.shape[1]
    cb = jnp.broadcast_to(c[:, :, None], (BATCH, D_MODEL, LANES))
    return pl.pallas_call(
        _adaln_kernel,
        grid=(ncol // ADA_TN,),
        in_specs=[
            pl.BlockSpec((BATCH, D_MODEL, LANES), lambda j: (0, 0, 0)),
            pl.BlockSpec((D_MODEL, ADA_TN), lambda j: (0, j)),
            pl.BlockSpec((1, ADA_TN), lambda j: (0, j)),
        ],
        out_specs=pl.BlockSpec((BATCH, ADA_TN), lambda j: (0, j)),
        out_shape=jax.ShapeDtypeStruct((BATCH, ncol), F32),
        compiler_params=_cparams(("arbitrary",)),
        name="adaln",
    )(cb, w_ada, b_ada.reshape(1, ncol))


def _rms(x):
    return x * lax.rsqrt(jnp.mean(x * x, axis=-1, keepdims=True) + EPS)


IN_TM = 512
IN_TN = 1792
IN_NJ = ROW_COLS // IN_TN


def _inproj_kernel(x_ref, mod_ref, g_ref, w_ref, wf_ref, cs_ref, wvt_ref, o_ref, f_ref, vt_ref, h_ref):
    j = pl.program_id(1)

    @pl.when(j == 0)
    def _():
        y = _rms(x_ref[...])
        h = (y * g_ref[...]) * (1.0 + mod_ref[0, 1:2, :]) + mod_ref[0, 0:1, :]
        hb = h.astype(BF16)
        h_ref[...] = hb
        f_ref[...] = jnp.dot(hb, wf_ref[...], preferred_element_type=F32)

    @pl.when(j < IN_NJ)
    def _():
        acc = jnp.dot(h_ref[...], w_ref[...], preferred_element_type=F32)
        o_ref[...] = (acc * cs_ref[...]).astype(BF16)

    @pl.when(j == IN_NJ)
    def _():
        vt = lax.dot_general(wvt_ref[...], h_ref[...], (((0,), (1,)), ((), ())),
                             preferred_element_type=F32)
        vt_ref[0, :, 0] = vt.reshape(N_HEADS_FOX // 2, LANES, IN_TM).astype(BF16)


def _inproj(x2, mod3, g, w_row, w_f, colscale, w_vt):
    tiles_per_batch = SEQ // IN_TM
    last = IN_NJ - 1
    return pl.pallas_call(
        _inproj_kernel,
        grid=(N_TOK // IN_TM, IN_NJ + 1),
        in_specs=[
            pl.BlockSpec((IN_TM, D_MODEL), lambda i, j: (i, 0)),
            pl.BlockSpec((1, 6, D_MODEL), lambda i, j: (i // tiles_per_batch, 0, 0)),
            pl.BlockSpec((1, D_MODEL), lambda i, j: (0, 0)),
            pl.BlockSpec((D_MODEL, IN_TN), lambda i, j: (0, jnp.minimum(j, last))),
            pl.BlockSpec((D_MODEL, LANES), lambda i, j: (0, 0)),
            pl.BlockSpec((1, IN_TN), lambda i, j: (0, jnp.minimum(j, last))),
            pl.BlockSpec((D_MODEL, FOX_WIDTH), lambda i, j: (0, 0)),
        ],
        out_specs=[
            pl.BlockSpec((IN_TM, IN_TN), lambda i, j: (i, jnp.minimum(j, last))),
            pl.BlockSpec((IN_TM, LANES), lambda i, j: (i, 0)),
            pl.BlockSpec((1, N_HEADS_FOX // 2, 1, LANES, IN_TM),
                         lambda i, j: (i // tiles_per_batch, 0, i % tiles_per_batch, 0, 0)),
        ],
        out_shape=[
            jax.ShapeDtypeStruct((N_TOK, ROW_COLS), BF16),
            jax.ShapeDtypeStruct((N_TOK, LANES), F32),
            jax.ShapeDtypeStruct((BATCH, N_HEADS_FOX // 2, SEQ // IN_TM, LANES, IN_TM), BF16),
        ],
        scratch_shapes=[pltpu.VMEM((IN_TM, D_MODEL), BF16)],
        compiler_params=_cparams(("arbitrary", "arbitrary")),
        name="inproj",
    )(x2, mod3, g, w_row, w_f, colscale, w_vt)


CUM_TS = 512
FOX_AUG = 3


def _cum_kernel(f_ref, bf_ref, tri_ref, sel_ref, aug_ref, carry_ref):
    @pl.when(pl.program_id(1) == 0)
    def _():
        carry_ref[...] = jnp.zeros_like(carry_ref)

    lf = jax.nn.log_sigmoid(f_ref[...] + bf_ref[...])
    l_hi = lf.astype(BF16)
    l_r = lf - l_hi.astype(F32)
    l_mid = l_r.astype(BF16)
    l_lo = (l_r - l_mid.astype(F32)).astype(BF16)
    c3 = jnp.dot(tri_ref[...], jnp.concatenate([l_hi, l_mid, l_lo], axis=1), preferred_element_type=F32)
    cs = (c3[:, :LANES] + (c3[:, LANES:2 * LANES] + c3[:, 2 * LANES:])) + carry_ref[...]
    carry_ref[...] = cs[CUM_TS - 1:CUM_TS, :]
    bias = cs * (-LOG2E)
    hi = bias.astype(BF16)
    r1 = bias - hi.astype(F32)
    mid = r1.astype(BF16)
    lo = (r1 - mid.astype(F32)).astype(BF16)
    pieces = jnp.concatenate([hi, mid, lo], axis=1)
    aug_ref[...] = jnp.dot(pieces, sel_ref[...], preferred_element_type=F32).astype(BF16)


def _aug_selector():
    sel = np.zeros((FOX_AUG * LANES, FOX_WIDTH), np.float32)
    for head in range(N_HEADS_FOX):
        for piece in range(FOX_AUG):
            sel[piece * LANES + head, (head // 2) * LANES + FOX_AUG * (head % 2) + piece] = 1.0
    return sel


def _forget_bias(f, b_forget_row):
    tri = jnp.tril(jnp.ones((CUM_TS, CUM_TS), BF16))
    sel = jnp.asarray(_aug_selector(), BF16)
    nblk = SEQ // CUM_TS
    return pl.pallas_call(
        _cum_kernel,
        grid=(BATCH, nblk),
        in_specs=[
            pl.BlockSpec((CUM_TS, LANES), lambda b, i: (b * nblk + i, 0)),
            pl.BlockSpec((1, LANES), lambda b, i: (0, 0)),
            pl.BlockSpec((CUM_TS, CUM_TS), lambda b, i: (0, 0)),
            pl.BlockSpec((FOX_AUG * LANES, FOX_WIDTH), lambda b, i: (0, 0)),
        ],
        out_specs=pl.BlockSpec((CUM_TS, FOX_WIDTH), lambda b, i: (b * nblk + i, 0)),
        out_shape=jax.ShapeDtypeStruct((N_TOK, FOX_WIDTH), BF16),
        scratch_shapes=[pltpu.VMEM((1, LANES), F32)],
        compiler_params=_cparams(("arbitrary", "arbitrary")),
        name="forget_bias",
    )(f, b_forget_row, tri, sel)


SWA_TQ = WINDOW


def _swa_perm():
    perm = np.zeros((SWA_WIDTH,), np.int32)
    for t in range(2):
        for g in range(4):
            for e in range(2):
                head = (2 * t + e) * 4 + g
                base = (t * 4 + g) * LANES + e * HEAD_DIM
                perm[base:base + HEAD_DIM] = head * HEAD_DIM + np.arange(HEAD_DIM)
    return perm


def _swa_kernel(sinks_ref, q_ref, kc_ref, kp_ref, vc_ref, vp_ref, o_ref, bias_ref):
    i = pl.program_id(1)
    tq = SWA_TQ

    @pl.when((pl.program_id(0) == 0) & (i == 0))
    def _():
        row = lax.broadcasted_iota(jnp.int32, (tq, 2 * tq), 0)
        col = lax.broadcasted_iota(jnp.int32, (tq, 2 * tq), 1)
        dist = row + tq - col
        band = (dist >= 0) & (dist < WINDOW)
        distf = dist.astype(F32)
        for head in range(N_HEADS_SWA):
            slope2 = float(2.0 ** (-8.0 * (head + 1) / N_HEADS_SWA)) * LOG2E
            base = jnp.where(band, -slope2 * distf, -jnp.inf)
            bias_ref[1, head] = base
            bias_ref[0, head] = jnp.where(col >= tq, base, -jnp.inf)

    table = jnp.minimum(i, 1)
    lane = lax.broadcasted_iota(jnp.int32, (tq, LANES), 1)
    lo_half = lane < HEAD_DIM
    for t in range(2):
        kt = jnp.concatenate([kp_ref[:, t * LANES:(t + 1) * LANES],
                              kc_ref[:, t * LANES:(t + 1) * LANES]], axis=0)
        vt = jnp.concatenate([vp_ref[:, t * LANES:(t + 1) * LANES],
                              vc_ref[:, t * LANES:(t + 1) * LANES]], axis=0)
        heads = [(g, e) for g in range(4) for e in range(2)]
        scores = []
        for g, e in heads:
            tile = t * 4 + g
            qt = q_ref[:, tile * LANES:(tile + 1) * LANES]
            qm = jnp.where(lo_half if e == 0 else ~lo_half, qt, jnp.zeros_like(qt))
            scores.append(lax.dot_general(qm, kt, (((1,), (1,)), ((), ())),
                                          preferred_element_type=F32))
        probs, rdens = [], []
        for (g, e), s in zip(heads, scores):
            head = (2 * t + e) * 4 + g
            sink = sinks_ref[head] * LOG2E
            s = s + bias_ref[table, head]
            m = jnp.maximum(jnp.max(s, axis=-1, keepdims=True), sink)
            p = jnp.exp2(s - m)
            rdens.append(1.0 / (jnp.sum(p, axis=-1, keepdims=True) + jnp.exp2(sink - m)))
            probs.append(p.astype(BF16))
        outs = [jnp.dot(p, vt, preferred_element_type=F32) * r for p, r in zip(probs, rdens)]
        for g in range(4):
            tile = t * 4 + g
            o_ref[:, tile * LANES:(tile + 1) * LANES] = jnp.where(lo_half, outs[2 * g], outs[2 * g + 1]).astype(BF16)


def _swa(sinks, proj):
    nq = SEQ // SWA_TQ
    kblk = KA_COL // KV_WIDTH
    vblk = VA_COL // KV_WIDTH
    grid_spec = pltpu.PrefetchScalarGridSpec(
        num_scalar_prefetch=1,
        grid=(BATCH, nq),
        in_specs=[
            pl.BlockSpec((SWA_TQ, SWA_WIDTH), lambda b, i, s: (b * nq + i, 0)),
            pl.BlockSpec((SWA_TQ, KV_WIDTH), lambda b, i, s: (b * nq + i, kblk)),
            pl.BlockSpec((SWA_TQ, KV_WIDTH), lambda b, i, s: (b * nq + jnp.maximum(i - 1, 0), kblk)),
            pl.BlockSpec((SWA_TQ, KV_WIDTH), lambda b, i, s: (b * nq + i, vblk)),
            pl.BlockSpec((SWA_TQ, KV_WIDTH), lambda b, i, s: (b * nq + jnp.maximum(i - 1, 0), vblk)),
        ],
        out_specs=pl.BlockSpec((SWA_TQ, SWA_WIDTH), lambda b, i, s: (b * nq + i, 0)),
        scratch_shapes=[pltpu.VMEM((2, N_HEADS_SWA, SWA_TQ, 2 * SWA_TQ), F32)],
    )
    return pl.pallas_call(
        _swa_kernel,
        grid_spec=grid_spec,
        out_shape=jax.ShapeDtypeStruct((N_TOK, SWA_WIDTH), BF16),
        compiler_params=_cparams(("arbitrary", "arbitrary")),
        name="swa_attn",
    )(sinks, proj, proj, proj, proj, proj)


FOX_T = 512
FOX_ONES = 16
FOX_VROWS = HEAD_DIM + FOX_ONES


def _fox_kernel(q_ref, k_ref, aug_ref, vt_ref, o_ref, mask_ref, t_ref, acc_ref):
    qi = pl.program_id(2)
    t = FOX_T

    @pl.when(qi == 0)
    def _():
        kr = lax.broadcasted_iota(jnp.int32, (t, t), 0)
        qc = lax.broadcasted_iota(jnp.int32, (t, t), 1)
        mask_ref[...] = jnp.where(kr <= qc, 0.0, -jnp.inf)

    qf = q_ref[...].astype(F32).T
    drow = lax.broadcasted_iota(jnp.int32, (LANES, t), 0)
    qaug = []
    for hh in range(2):
        qh = jnp.where((drow >= HEAD_DIM * hh) & (drow < HEAD_DIM * (hh + 1)), qf, 0.0)
        ones = jnp.where((drow >= FOX_AUG * hh) & (drow < FOX_AUG * (hh + 1)), 1.0, 0.0)
        qaug.append(jnp.concatenate([qh, ones], axis=0).astype(BF16))
    acc_ref[...] = jnp.zeros_like(acc_ref)

    def stage_a(j, slot, diag):
        r0 = pl.multiple_of(j * t, t)
        kb = jnp.concatenate([k_ref[pl.ds(r0, t), :], aug_ref[pl.ds(r0, t), :]], axis=1)
        sts = [jnp.dot(kb, qaug[hh], preferred_element_type=F32) for hh in range(2)]
        mbs = []
        for hh in range(2):
            tt = sts[hh]
            if diag:
                tt = tt + mask_ref[...]
            t_ref[slot, hh] = tt
            mbs.append(jnp.max(tt, axis=0, keepdims=True))
        return tuple(mbs)

    def stage_b(jv, slot, ms, mbs):
        new, ps, alphas = [], [], []
        for hh in range(2):
            m_new = jnp.maximum(ms[hh], mbs[hh])
            alphas.append(jnp.exp2(ms[hh] - m_new))
            ps.append(jnp.exp2(t_ref[slot, hh] - m_new).astype(BF16))
            new.append(m_new)
        ones = jnp.ones((FOX_ONES, t), BF16)
        pvs = [jnp.dot(jnp.concatenate([vt_ref[0, 0, jv, hh * HEAD_DIM:(hh + 1) * HEAD_DIM, :], ones], axis=0),
                       ps[hh], preferred_element_type=F32)
               for hh in range(2)]
        for hh in range(2):
            acc_ref[hh] = alphas[hh] * acc_ref[hh] + pvs[hh]
        return tuple(new)

    ml0 = tuple(jnp.full((1, t), -jnp.inf, F32) for _ in range(2))
    mb0 = stage_a(qi, 0, True)

    def pair(j0, c):
        ml, mbs, jprev = c
        mb1 = stage_a(j0, 1, False)
        ml = stage_b(jprev, 0, ml, mbs)
        mb2 = stage_a(j0 + 1, 0, False)
        ml = stage_b(j0, 1, ml, mb1)
        return ml, mb2, j0 + 1

    def quad(ii, c):
        return pair(4 * ii + 2, pair(4 * ii, c))

    carry = lax.fori_loop(0, qi // 4, quad, (ml0, mb0, qi))
    carry = lax.cond(qi % 4 >= 2, lambda c: pair(4 * (qi // 4), c), lambda c: c, carry)
    ml, mbs, jprev = carry

    def odd_tail(c):
        ml, mbs, jprev = c
        mb1 = stage_a(qi - 1, 1, False)
        ml = stage_b(jprev, 0, ml, mbs)
        return stage_b(qi - 1, 1, ml, mb1)

    def even_tail(c):
        ml, mbs, jprev = c
        return stage_b(jprev, 0, ml, mbs)

    ml = lax.cond(qi % 2 == 1, odd_tail, even_tail, (ml, mbs, jprev))
    del ml
    ot = jnp.concatenate([acc_ref[hh, 0:HEAD_DIM, :] / acc_ref[hh, HEAD_DIM:HEAD_DIM + 1, :]
                          for hh in range(2)], axis=0)
    o_ref[...] = ot.T.astype(BF16)


def _fox(proj, aug, vt5):
    nq = SEQ // FOX_T
    npair = N_HEADS_FOX // 2
    return pl.pallas_call(
        _fox_kernel,
        grid=(BATCH, npair, nq),
        in_specs=[
            pl.BlockSpec((FOX_T, LANES), lambda b, h, i: (b * nq + i, QB_BLK + h)),
            pl.BlockSpec((SEQ, LANES), lambda b, h, i: (b, KB_BLK + h)),
            pl.BlockSpec((SEQ, LANES), lambda b, h, i: (b, h)),
            pl.BlockSpec((1, 1, nq, LANES, FOX_T), lambda b, h, i: (b, h, 0, 0, 0)),
        ],
        out_specs=pl.BlockSpec((FOX_T, LANES), lambda b, h, i: (b * nq + i, h)),
        out_shape=jax.ShapeDtypeStruct((N_TOK, FOX_WIDTH), BF16),
        scratch_shapes=[
            pltpu.VMEM((FOX_T, FOX_T), F32),
            pltpu.VMEM((2, 2, FOX_T, FOX_T), F32),
            pltpu.VMEM((2, FOX_VROWS, FOX_T), F32),
        ],
        compiler_params=_cparams(("arbitrary", "arbitrary", "arbitrary")),
        name="fox_attn",
    )(proj, proj, aug, vt5)


OUT_TM = 512
OUT_SUB = 256
SUBLANES = 8
PACKED_COLS = D_MODEL // 2
PACK_ROWS = PACKED_COLS // LANES
DMA_UNROLL = 8


def _outproj_kernel(oa_ref, ob_ref, x_ref, mod_ref, ga_ref, gb_ref, w_ref, gm_ref, wr2_ref, br_ref,
                    x1_ref, h2p_ref, ids_ref, gates_ref):
    subs = [slice(k * OUT_SUB, (k + 1) * OUT_SUB) for k in range(OUT_TM // OUT_SUB)]
    mixed = [jnp.concatenate([_rms(oa_ref[r, :].astype(F32)) * ga_ref[...],
                              _rms(ob_ref[r, :].astype(F32)) * gb_ref[...]], axis=1).astype(BF16) for r in subs]
    ys = [jnp.dot(m, w_ref[...], preferred_element_type=F32) for m in mixed]
    h2s = []
    for r, y in zip(subs, ys):
        x1 = x_ref[r, :] + mod_ref[0, 2:3, :] * y
        x1_ref[r, :] = x1
        h2s.append((_rms(x1) * gm_ref[...]) * (1.0 + mod_ref[0, 4:5, :]) + mod_ref[0, 3:4, :])
    hbs = [h2.astype(BF16) for h2 in h2s]
    for r, h2 in zip(subs, h2s):
        packed = _pack_bf16_pairs(h2)
        for s in range(PACK_ROWS):
            h2p_ref[r, s, :] = packed[:, s * LANES:(s + 1) * LANES]

    hls = [(h2 - hb.astype(F32)).astype(BF16) for h2, hb in zip(h2s, hbs)]
    r2s = [jnp.dot(hb, wr2_ref[...], preferred_element_type=F32) for hb in hbs]
    r3s = [jnp.dot(hl, wr2_ref[:, :LANES], preferred_element_type=F32) for hl in hls]
    for r, r2, r3 in zip(subs, r2s, r3s):
        logits = (r2[:, :LANES] + (r2[:, LANES:] + r3)) + br_ref[...]
        ids, gates = _route(logits)
        ids_ref[r, :] = ids
        gates_ref[r, :] = gates


def _route(logits):
    lane = lax.broadcasted_iota(jnp.int32, logits.shape, 1)
    neg = -jnp.inf
    is_g = lane < N_GROUPS
    gl = jnp.where(is_g, logits, neg)
    gmax = jnp.max(gl, axis=1, keepdims=True)
    gsel = jnp.min(jnp.where(gl == gmax, lane, LANES), axis=1, keepdims=True)
    gsum = jnp.sum(jnp.where(is_g, jnp.exp(gl - gmax), 0.0), axis=1, keepdims=True)
    g_val = 1.0 / gsum
    elane = lane - N_GROUPS
    in_sel = (elane >= 0) & (elane < N_EXPERTS) & ((elane >> 3) == gsel)
    ev = jnp.where(in_sel, logits, neg)
    t1 = jnp.max(ev, axis=1, keepdims=True)
    i1 = jnp.min(jnp.where(ev == t1, lane, LANES), axis=1, keepdims=True)
    ev2 = jnp.where(lane == i1, neg, ev)
    t2 = jnp.max(ev2, axis=1, keepdims=True)
    i2 = jnp.min(jnp.where(ev2 == t2, lane, LANES), axis=1, keepdims=True)
    e2 = jnp.exp(t2 - t1)
    den = 1.0 + e2
    w1 = (1.0 / den) * g_val
    w2 = (e2 / den) * g_val
    return (jnp.where(lane == 0, i1 - N_GROUPS, jnp.where(lane == 1, i2 - N_GROUPS, 0)),
            jnp.where(lane == 0, w1, jnp.where(lane == 1, w2, 0.0)))


def _outproj(oa, ob, x2, mod3, ga, gb, w_out, gm, wr, br):
    tiles_per_batch = SEQ // OUT_TM
    row = lambda i: (i, 0)
    const = lambda i: (0, 0)
    wr_hi = wr.astype(BF16)
    wr_lo = (wr - wr_hi.astype(F32)).astype(BF16)
    wr2 = jnp.concatenate([wr_hi, wr_lo], axis=1)
    return pl.pallas_call(
        _outproj_kernel,
        grid=(N_TOK // OUT_TM,),
        in_specs=[
            pl.BlockSpec((OUT_TM, SWA_WIDTH), row),
            pl.BlockSpec((OUT_TM, FOX_WIDTH), row),
            pl.BlockSpec((OUT_TM, D_MODEL), row),
            pl.BlockSpec((1, 6, D_MODEL), lambda i: (i // tiles_per_batch, 0, 0)),
            pl.BlockSpec((1, SWA_WIDTH), const),
            pl.BlockSpec((1, FOX_WIDTH), const),
            pl.BlockSpec((D_MODEL, D_MODEL), const),
            pl.BlockSpec((1, D_MODEL), const),
            pl.BlockSpec((D_MODEL, 2 * LANES), const),
            pl.BlockSpec((1, LANES), const),
        ],
        out_specs=[
            pl.BlockSpec((OUT_TM, D_MODEL), row),
            pl.BlockSpec((OUT_TM, PACK_ROWS, LANES), lambda i: (i, 0, 0)),
            pl.BlockSpec((OUT_TM, LANES), row),
            pl.BlockSpec((OUT_TM, LANES), row),
        ],
        out_shape=[
            jax.ShapeDtypeStruct((N_TOK, D_MODEL), F32),
            jax.ShapeDtypeStruct((N_TOK, PACK_ROWS, LANES), jnp.uint32),
            jax.ShapeDtypeStruct((N_TOK, LANES), jnp.int32),
            jax.ShapeDtypeStruct((N_TOK, LANES), F32),
        ],
        compiler_params=_cparams(("arbitrary",)),
        name="outproj_router",
    )(oa, ob, x2, mod3, ga, gb, w_out, gm, wr2, br)


N_BLOCKS = (N_TOK * TOP_K) // MOE_BLOCK + N_EXPERTS
P_ROWS = N_BLOCKS * MOE_BLOCK


def _pack_bf16_pairs(x):
    xb = x.astype(BF16).astype(F32)
    half = x.shape[1] // 2
    lo = lax.shift_right_logical(pltpu.bitcast(xb[:, :half], jnp.uint32), jnp.uint32(16))
    hi = pltpu.bitcast(xb[:, half:], jnp.uint32) & jnp.uint32(0xFFFF0000)
    return lo | hi


def _unpack_bf16_pairs(w):
    return (pltpu.bitcast(lax.shift_left(w, jnp.uint32(16)), F32),
            pltpu.bitcast(w & jnp.uint32(0xFFFF0000), F32))


def _expert_kernel(be_ref, tok_ref, nused_ref, nxt_ref, h2_hbm, wg_hbm, wu_hbm, wd_hbm, y_hbm, xbuf, sem,
                   wgf, wuf, wdf, wsem, wgb, wub, wdb, ystage, osem):
    i = pl.program_id(0)
    nused = nused_ref[0]

    def weight_copies(e):
        return (pltpu.make_async_copy(wg_hbm.at[e], wgf, wsem.at[0]),
                pltpu.make_async_copy(wu_hbm.at[e], wuf, wsem.at[1]),
                pltpu.make_async_copy(wd_hbm.at[e], wdf, wsem.at[2]))

    def issue(blk, slot):
        def body(g, _):
            for q in range(SUBLANES):
                tok = tok_ref[blk * MOE_BLOCK + g * SUBLANES + q]
                pltpu.make_async_copy(h2_hbm.at[tok], xbuf.at[slot, g, :, q, :], sem.at[slot]).start(
                    priority=q % 2)
            return 0
        lax.fori_loop(0, MOE_BLOCK // SUBLANES, body, 0)

    def out_copies(blk, slot):
        r0 = pl.multiple_of(blk * MOE_BLOCK, MOE_BLOCK)
        return [pltpu.make_async_copy(ystage.at[slot, :, pl.ds(s * LANES, LANES)],
                                      y_hbm.at[pl.ds(r0, MOE_BLOCK), s, :], osem.at[slot])
                for s in range(PACK_ROWS)]

    def out_wait(slot):
        pltpu.make_async_copy(ystage.at[slot], ystage.at[slot], osem.at[slot]).wait()

    def issue_inline(blk, slot):
        for r in range(MOE_BLOCK):
            tok = tok_ref[blk * MOE_BLOCK + r]
            pltpu.make_async_copy(h2_hbm.at[tok], xbuf.at[slot, r // SUBLANES, :, r % SUBLANES, :],
                                  sem.at[slot]).start(priority=r % 2)

    def gather_wait(slot):
        pltpu.make_async_copy(xbuf.at[slot], xbuf.at[slot], sem.at[slot]).wait()

    @pl.when(i == 0)
    def _():
        issue(0, 0)

    oslot = i % 2

    @pl.when(i >= 2)
    def _():
        out_wait(oslot)

    e = be_ref[i]
    new_expert = ((i == 0) | (e != be_ref[jnp.maximum(i - 1, 0)])) & (i < nused)

    @pl.when(i == 0)
    def _():
        for cp in weight_copies(e):
            cp.start()

    @pl.when(new_expert)
    def _():
        for cp in weight_copies(e):
            cp.wait()
        wgb[...] = wgf[...].astype(BF16)
        wub[...] = wuf[...].astype(BF16)
        wdb[...] = wdf[...].astype(BF16)

    @pl.when(new_expert & (nxt_ref[e] >= 0))
    def _():
        for cp in weight_copies(nxt_ref[e]):
            cp.start()

    @pl.when(i < nused)
    def _():
        slot = i % 2
        gather_wait(slot)
        gsub = MOE_SUB // SUBLANES
        xbs = []
        for k in range(MOE_BLOCK // MOE_SUB):
            xu = jnp.concatenate([xbuf[slot, k * gsub:(k + 1) * gsub, s].reshape(MOE_SUB, LANES)
                                  for s in range(PACK_ROWS)], axis=1)
            xbs.append(jnp.concatenate(_unpack_bf16_pairs(xu), axis=1).astype(BF16))
        issue_inline(i + 1, 1 - slot)
        gs = [jnp.dot(xb, wgb[...], preferred_element_type=F32) for xb in xbs]
        us = [jnp.dot(xb, wub[...], preferred_element_type=F32) for xb in xbs]
        hids = [(jax.nn.silu(g) * u).astype(BF16) for g, u in zip(gs, us)]
        ys = [jnp.dot(hid, wdb[...], preferred_element_type=F32) for hid in hids]
        for k, y in enumerate(ys):
            ystage[oslot, k * MOE_SUB:(k + 1) * MOE_SUB, :] = _pack_bf16_pairs(y)

    @pl.when(i >= nused)
    def _():
        gather_wait(i % 2)
        issue(i + 1, (i + 1) % 2)
        ystage[oslot] = jnp.zeros((MOE_BLOCK, PACKED_COLS), jnp.uint32)

    for cp in out_copies(i, oslot):
        cp.start()

    @pl.when(i == pl.num_programs(0) - 1)
    def _():
        gather_wait((i + 1) % 2)
        out_wait(oslot)
        out_wait(1 - oslot)


def _experts(block_e, row_tok, nused, next_used, h2p, wg, wu, wd):
    grid_spec = pltpu.PrefetchScalarGridSpec(
        num_scalar_prefetch=4,
        grid=(N_BLOCKS,),
        in_specs=[pl.BlockSpec(memory_space=pl.ANY)] * 4,
        out_specs=pl.BlockSpec(memory_space=pl.ANY),
        scratch_shapes=[pltpu.VMEM((2, MOE_BLOCK // SUBLANES, PACK_ROWS, SUBLANES, LANES), jnp.uint32),
                        pltpu.SemaphoreType.DMA((2,)),
                        pltpu.VMEM((D_MODEL, D_EXPERT), F32),
                        pltpu.VMEM((D_MODEL, D_EXPERT), F32),
                        pltpu.VMEM((D_EXPERT, D_MODEL), F32),
                        pltpu.SemaphoreType.DMA((3,)),
                        pltpu.VMEM((D_MODEL, D_EXPERT), BF16),
                        pltpu.VMEM((D_MODEL, D_EXPERT), BF16),
                        pltpu.VMEM((D_EXPERT, D_MODEL), BF16),
                        pltpu.VMEM((2, MOE_BLOCK, PACKED_COLS), jnp.uint32),
                        pltpu.SemaphoreType.DMA((2,))],
    )
    return pl.pallas_call(
        _expert_kernel,
        grid_spec=grid_spec,
        out_shape=jax.ShapeDtypeStruct((P_ROWS, PACK_ROWS, LANES), jnp.uint32),
        compiler_params=_cparams(("arbitrary",)),
        name="expert_ffn",
    )(block_e, row_tok, nused, next_used, h2p, wg, wu, wd)


CMB_TM = 128


def _combine_kernel(dest_ref, ys_hbm, x1_ref, gates_ref, mod_ref, fg_ref, o_ref, ybuf, sem):
    i = pl.program_id(0)
    n = pl.num_programs(0)
    groups = CMB_TM // SUBLANES

    def issue(tile, slot):
        def body(g, _):
            for q in range(SUBLANES):
                for k in range(TOP_K):
                    d = dest_ref[(tile * CMB_TM + g * SUBLANES + q) * TOP_K + k]
                    pltpu.make_async_copy(ys_hbm.at[d], ybuf.at[slot, k * groups + g, :, q, :],
                                          sem.at[slot]).start(priority=k)
            return 0
        lax.fori_loop(0, groups, body, 0)

    def issue_inline(tile, slot):
        for r in range(CMB_TM):
            for k in range(TOP_K):
                d = dest_ref[(tile * CMB_TM + r) * TOP_K + k]
                pltpu.make_async_copy(ys_hbm.at[d], ybuf.at[slot, k * groups + r // SUBLANES, :, r % SUBLANES, :],
                                      sem.at[slot]).start(priority=k)

    def gather_wait(slot):
        pltpu.make_async_copy(ybuf.at[slot], ybuf.at[slot], sem.at[slot]).wait()

    @pl.when(i == 0)
    def _():
        issue(0, 0)

    slot = i % 2
    gather_wait(slot)
    issue_inline(i + 1, 1 - slot)
    gts = gates_ref[...]
    w0 = gts[:, 0:1]
    w1 = gts[:, 1:2]
    half = D_MODEL // 2
    ssq = jnp.zeros((CMB_TM, 1), F32)
    for s in range(PACK_ROWS):
        y0 = _unpack_bf16_pairs(ybuf[slot, 0:groups, s].reshape(CMB_TM, LANES))
        y1 = _unpack_bf16_pairs(ybuf[slot, groups:2 * groups, s].reshape(CMB_TM, LANES))
        for part in range(2):
            cols = slice(part * half + s * LANES, part * half + (s + 1) * LANES)
            x2 = x1_ref[:, cols] + mod_ref[0, 5:6, cols] * (y0[part] * w0 + y1[part] * w1)
            ssq = ssq + jnp.sum(x2 * x2, axis=-1, keepdims=True)
            o_ref[:, cols] = x2
    o_ref[...] = (o_ref[...] * lax.rsqrt(ssq * (1.0 / D_MODEL) + EPS)) * fg_ref[...]

    @pl.when(i == n - 1)
    def _():
        gather_wait(1 - slot)


def _combine(dest, ys, x1, gates, mod3, final_g):
    tiles_per_batch = SEQ // CMB_TM
    dest = jnp.pad(dest, (0, TOP_K * CMB_TM))
    grid_spec = pltpu.PrefetchScalarGridSpec(
        num_scalar_prefetch=1,
        grid=(N_TOK // CMB_TM,),
        in_specs=[
            pl.BlockSpec(memory_space=pl.ANY),
            pl.BlockSpec((CMB_TM, D_MODEL), lambda i, d: (i, 0)),
            pl.BlockSpec((CMB_TM, LANES), lambda i, d: (i, 0)),
            pl.BlockSpec((1, 6, D_MODEL), lambda i, d: (i // tiles_per_batch, 0, 0)),
            pl.BlockSpec((1, D_MODEL), lambda i, d: (0, 0)),
        ],
        out_specs=pl.BlockSpec((CMB_TM, D_MODEL), lambda i, d: (i, 0)),
        scratch_shapes=[pltpu.VMEM((2, TOP_K * CMB_TM // SUBLANES, PACK_ROWS, SUBLANES, LANES), jnp.uint32),
                        pltpu.SemaphoreType.DMA((2,))],
    )
    return pl.pallas_call(
        _combine_kernel,
        grid_spec=grid_spec,
        out_shape=jax.ShapeDtypeStruct((N_TOK, D_MODEL), F32),
        compiler_params=_cparams(("arbitrary",)),
        name="combine_final",
    )(dest, ys, x1, gates, mod3, final_g)


def _routing_tables(eid):
    a = eid.shape[0]
    onehot = (eid[:, None] == jnp.arange(N_EXPERTS, dtype=jnp.int32)[None, :]).astype(jnp.int32)
    csum = jnp.cumsum(onehot, axis=0)
    rank = jnp.sum(onehot * csum, axis=1) - 1
    counts = csum[-1]
    padded = (counts + MOE_BLOCK - 1) // MOE_BLOCK * MOE_BLOCK
    pend = jnp.cumsum(padded)
    pstart = pend - padded
    dest = (pstart[eid] + rank).astype(jnp.int32)
    tok = jnp.arange(a, dtype=jnp.int32) // TOP_K
    row_tok = jnp.zeros((P_ROWS + MOE_BLOCK,), jnp.int32).at[dest].set(tok, unique_indices=True,
                                                                      mode='promise_in_bounds')
    blk_row = jnp.arange(N_BLOCKS, dtype=jnp.int32) * MOE_BLOCK
    block_e = jnp.minimum(jnp.sum((pend[None, :] <= blk_row[:, None]).astype(jnp.int32), axis=1),
                          N_EXPERTS - 1).astype(jnp.int32)
    nused = (pend[-1] // MOE_BLOCK).astype(jnp.int32).reshape(1)
    ids = jnp.arange(N_EXPERTS, dtype=jnp.int32)
    later_used = (ids[None, :] > ids[:, None]) & (counts[None, :] > 0)
    next_used = jnp.min(jnp.where(later_used, ids[None, :], N_EXPERTS), axis=1)
    next_used = jnp.where(next_used == N_EXPERTS, -1, next_used).astype(jnp.int32)
    return dest, row_tok, block_e, nused, next_used


def kernel(x, c, w_ada, b_ada, norm_mix_g, w_in, b_forget, sinks, out_norm_swa_g, out_norm_fox_g, w_out,
           norm_moe_g, w_group, b_group, w_expert, b_expert, w_gate, w_up, w_down, final_g):
    assert IN_TM == FOX_T
    x2 = x.reshape(N_TOK, D_MODEL)
    perm = _swa_perm()

    w_in0 = w_in[0]
    w_row = jnp.concatenate([w_in0[:, :SWA_WIDTH][:, perm], w_in0[:, SWA_WIDTH:VB_COL]], axis=1).astype(BF16)
    w_vt = w_in0[:, VB_COL:MAIN_COLS].astype(BF16)
    w_f = jnp.pad(w_in0[:, MAIN_COLS:], ((0, 0), (0, LANES - N_HEADS_FOX))).astype(BF16)
    colscale = jnp.concatenate([
        jnp.full((SWA_WIDTH,), HEAD_DIM ** -0.5 * LOG2E, F32),
        jnp.ones((2 * KV_WIDTH,), F32),
        jnp.full((FOX_WIDTH,), HEAD_DIM ** -0.5 * LOG2E, F32),
        jnp.ones((FOX_WIDTH,), F32)]).reshape(1, ROW_COLS)
    bf_row = jnp.pad(b_forget[0], (0, LANES - N_HEADS_FOX)).reshape(1, LANES)
    w_out_p = jnp.concatenate([w_out[0][:SWA_WIDTH][perm], w_out[0][SWA_WIDTH:]], axis=0).astype(BF16)
    ga = out_norm_swa_g[0][perm].reshape(1, SWA_WIDTH)
    gb = out_norm_fox_g[0].reshape(1, FOX_WIDTH)
    wr = jnp.pad(jnp.concatenate([w_group[0], w_expert[0]], axis=1),
                 ((0, 0), (0, LANES - N_GROUPS - N_EXPERTS)))
    br = jnp.pad(jnp.concatenate([b_group[0], b_expert[0]]), (0, LANES - N_GROUPS - N_EXPERTS)).reshape(1, LANES)
    wg, wu, wd = w_gate[0], w_up[0], w_down[0]

    mod = _adaln(c, w_ada[0], b_ada[0])
    mod3 = mod.reshape(BATCH, 6, D_MODEL)

    proj, f, vt5 = _inproj(x2, mod3, norm_mix_g[0].reshape(1, D_MODEL), w_row, w_f, colscale, w_vt)
    aug = _forget_bias(f, bf_row)

    o_a = _swa(sinks[0], proj)
    o_b = _fox(proj, aug, vt5)

    x1, h2, ids, gates = _outproj(o_a, o_b, x2, mod3, ga, gb, w_out_p,
                                  norm_moe_g[0].reshape(1, D_MODEL), wr, br)

    eid = ids[:, :TOP_K].reshape(-1)
    dest, row_tok, block_e, nused, next_used = _routing_tables(eid)
    ys = _experts(block_e, row_tok, nused, next_used, h2, wg, wu, wd)
    out = _combine(dest, ys, x1, gates, mod3, final_g.reshape(1, D_MODEL))
    return out.reshape(BATCH, SEQ, D_MODEL)
```

```python
import functools
import math

import numpy as np
import jax
import jax.numpy as jnp
from jax import lax
from jax.experimental import pallas as pl
from jax.experimental.pallas import tpu as pltpu

F32 = jnp.float32
BF16 = jnp.bfloat16

D_MODEL = 2048
BATCH = 2
SEQ = 8192
N_TOK = BATCH * SEQ
HEAD_DIM = 64
N_HEADS_SWA = 16
N_KV_SWA = 4
N_HEADS_FOX = 16
WINDOW = 128
SWA_WIDTH = N_HEADS_SWA * HEAD_DIM
KV_WIDTH = N_KV_SWA * HEAD_DIM
FOX_WIDTH = N_HEADS_FOX * HEAD_DIM
MAIN_COLS = SWA_WIDTH + 2 * KV_WIDTH + 3 * FOX_WIDTH
N_GROUPS = 4
EXPERTS_PER_GROUP = 8
N_EXPERTS = N_GROUPS * EXPERTS_PER_GROUP
TOP_K = 2
D_EXPERT = 512
MOE_BLOCK = 256
MOE_SUB = 128
EPS = 1e-6
LOG2E = math.log2(math.e)

LANES = 128
VMEM_LIMIT = 56 * 1024 * 1024

QA_BLK = 0
KA_COL = SWA_WIDTH
VA_COL = SWA_WIDTH + KV_WIDTH
QB_BLK = (SWA_WIDTH + 2 * KV_WIDTH) // LANES
KB_BLK = QB_BLK + FOX_WIDTH // LANES
VB_COL = SWA_WIDTH + 2 * KV_WIDTH + 2 * FOX_WIDTH
ROW_COLS = VB_COL


def _cparams(sem, vmem=VMEM_LIMIT):
    return pltpu.CompilerParams(dimension_semantics=sem, vmem_limit_bytes=vmem)


ADA_TN = 1024


def _adaln_kernel(cb_ref, w_ref, b_ref, o_ref):
    for b in range(BATCH):
        s = jax.nn.silu(cb_ref[b])
        cols = []
        for j in range(ADA_TN // LANES):
            prod = w_ref[:, j * LANES:(j + 1) * LANES] * s
            cols.append(jnp.sum(prod, axis=0, keepdims=True))
        o_ref[b:b + 1, :] = jnp.concatenate(cols, axis=1) + b_ref[...]


def _adaln(c, w_ada, b_ada):
    ncol = w_ada.shape[1]
    cb = jnp.broadcast_to(c[:, :, None], (BATCH, D_MODEL, LANES))
    return pl.pallas_call(
        _adaln_kernel,
        grid=(ncol // ADA_TN,),
        in_specs=[
            pl.BlockSpec((BATCH, D_MODEL, LANES), lambda j: (0, 0, 0)),
            pl.BlockSpec((D_MODEL, ADA_TN), lambda j: (0, j)),
            pl.BlockSpec((1, ADA_TN), lambda j: (0, j)),
        ],
        out_specs=pl.BlockSpec((BATCH, ADA_TN), lambda j: (0, j)),
        out_shape=jax.ShapeDtypeStruct((BATCH, ncol), F32),
        compiler_params=_cparams(("arbitrary",)),
        name="adaln",
    )(cb, w_ada, b_ada.reshape(1, ncol))


def _rms(x):
    return x * lax.rsqrt(jnp.mean(x * x, axis=-1, keepdims=True) + EPS)


IN_TM = 512
IN_TN = 1792
IN_NJ = ROW_COLS // IN_TN


def _inproj_kernel(x_ref, mod_ref, g_ref, w_ref, wf_ref, cs_ref, wvt_ref, o_ref, f_ref, vt_ref, h_ref):
    j = pl.program_id(1)

    @pl.when(j == 0)
    def _():
        y = _rms(x_ref[...])
        h = (y * g_ref[...]) * (1.0 + mod_ref[0, 1:2, :]) + mod_ref[0, 0:1, :]
        hb = h.astype(BF16)
        h_ref[...] = hb
        f_ref[...] = jnp.dot(hb, wf_ref[...], preferred_element_type=F32)

    @pl.when(j < IN_NJ)
    def _():
        acc = jnp.dot(h_ref[...], w_ref[...], preferred_element_type=F32)
        o_ref[...] = (acc * cs_ref[...]).astype(BF16)

    @pl.when(j == IN_NJ)
    def _():
        vt = lax.dot_general(wvt_ref[...], h_ref[...], (((0,), (1,)), ((), ())),
                             preferred_element_type=F32)
        vt_ref[0, :, 0] = vt.reshape(N_HEADS_FOX // 2, LANES, IN_TM).astype(BF16)


def _inproj(x2, mod3, g, w_row, w_f, colscale, w_vt):
    tiles_per_batch = SEQ // IN_TM
    last = IN_NJ - 1
    return pl.pallas_call(
        _inproj_kernel,
        grid=(N_TOK // IN_TM, IN_NJ + 1),
        in_specs=[
            pl.BlockSpec((IN_TM, D_MODEL), lambda i, j: (i, 0)),
            pl.BlockSpec((1, 6, D_MODEL), lambda i, j: (i // tiles_per_batch, 0, 0)),
            pl.BlockSpec((1, D_MODEL), lambda i, j: (0, 0)),
            pl.BlockSpec((D_MODEL, IN_TN), lambda i, j: (0, jnp.minimum(j, last))),
            pl.BlockSpec((D_MODEL, LANES), lambda i, j: (0, 0)),
            pl.BlockSpec((1, IN_TN), lambda i, j: (0, jnp.minimum(j, last))),
            pl.BlockSpec((D_MODEL, FOX_WIDTH), lambda i, j: (0, 0)),
        ],
        out_specs=[
            pl.BlockSpec((IN_TM, IN_TN), lambda i, j: (i, jnp.minimum(j, last))),
            pl.BlockSpec((IN_TM, LANES), lambda i, j: (i, 0)),
            pl.BlockSpec((1, N_HEADS_FOX // 2, 1, LANES, IN_TM),
                         lambda i, j: (i // tiles_per_batch, 0, i % tiles_per_batch, 0, 0)),
        ],
        out_shape=[
            jax.ShapeDtypeStruct((N_TOK, ROW_COLS), BF16),
            jax.ShapeDtypeStruct((N_TOK, LANES), F32),
            jax.ShapeDtypeStruct((BATCH, N_HEADS_FOX // 2, SEQ // IN_TM, LANES, IN_TM), BF16),
        ],
        scratch_shapes=[pltpu.VMEM((IN_TM, D_MODEL), BF16)],
        compiler_params=_cparams(("arbitrary", "arbitrary")),
        name="inproj",
    )(x2, mod3, g, w_row, w_f, colscale, w_vt)


CUM_TS = 512
FOX_AUG = 3


def _cum_kernel(f_ref, bf_ref, tri_ref, sel_ref, aug_ref, carry_ref):
    @pl.when(pl.program_id(1) == 0)
    def _():
        carry_ref[...] = jnp.zeros_like(carry_ref)

    lf = jax.nn.log_sigmoid(f_ref[...] + bf_ref[...])
    l_hi = lf.astype(BF16)
    l_r = lf - l_hi.astype(F32)
    l_mid = l_r.astype(BF16)
    l_lo = (l_r - l_mid.astype(F32)).astype(BF16)
    c3 = jnp.dot(tri_ref[...], jnp.concatenate([l_hi, l_mid, l_lo], axis=1), preferred_element_type=F32)
    cs = (c3[:, :LANES] + (c3[:, LANES:2 * LANES] + c3[:, 2 * LANES:])) + carry_ref[...]
    carry_ref[...] = cs[CUM_TS - 1:CUM_TS, :]
    bias = cs * (-LOG2E)
    hi = bias.astype(BF16)
    r1 = bias - hi.astype(F32)
    mid = r1.astype(BF16)
    lo = (r1 - mid.astype(F32)).astype(BF16)
    pieces = jnp.concatenate([hi, mid, lo], axis=1)
    aug_ref[...] = jnp.dot(pieces, sel_ref[...], preferred_element_type=F32).astype(BF16)


def _aug_selector():
    sel = np.zeros((FOX_AUG * LANES, FOX_WIDTH), np.float32)
    for head in range(N_HEADS_FOX):
        for piece in range(FOX_AUG):
            sel[piece * LANES + head, (head // 2) * LANES + FOX_AUG * (head % 2) + piece] = 1.0
    return sel


def _forget_bias(f, b_forget_row):
    tri = jnp.tril(jnp.ones((CUM_TS, CUM_TS), BF16))
    sel = jnp.asarray(_aug_selector(), BF16)
    nblk = SEQ // CUM_TS
    return pl.pallas_call(
        _cum_kernel,
        grid=(BATCH, nblk),
        in_specs=[
            pl.BlockSpec((CUM_TS, LANES), lambda b, i: (b * nblk + i, 0)),
            pl.BlockSpec((1, LANES), lambda b, i: (0, 0)),
            pl.BlockSpec((CUM_TS, CUM_TS), lambda b, i: (0, 0)),
            pl.BlockSpec((FOX_AUG * LANES, FOX_WIDTH), lambda b, i: (0, 0)),
        ],
        out_specs=pl.BlockSpec((CUM_TS, FOX_WIDTH), lambda b, i: (b * nblk + i, 0)),
        out_shape=jax.ShapeDtypeStruct((N_TOK, FOX_WIDTH), BF16),
        scratch_shapes=[pltpu.VMEM((1, LANES), F32)],
        compiler_params=_cparams(("arbitrary", "arbitrary")),
        name="forget_bias",
    )(f, b_forget_row, tri, sel)


SWA_TQ = WINDOW


def _swa_perm():
    perm = np.zeros((SWA_WIDTH,), np.int32)
    for t in range(2):
        for g in range(4):
            for e in range(2):
                head = (2 * t + e) * 4 + g
                base = (t * 4 + g) * LANES + e * HEAD_DIM
                perm[base:base + HEAD_DIM] = head * HEAD_DIM + np.arange(HEAD_DIM)
    return perm


def _swa_kernel(sinks_ref, q_ref, kc_ref, kp_ref, vc_ref, vp_ref, o_ref, bias_ref):
    i = pl.program_id(1)
    tq = SWA_TQ

    @pl.when((pl.program_id(0) == 0) & (i == 0))
    def _():
        row = lax.broadcasted_iota(jnp.int32, (tq, 2 * tq), 0)
        col = lax.broadcasted_iota(jnp.int32, (tq, 2 * tq), 1)
        dist = row + tq - col
        band = (dist >= 0) & (dist < WINDOW)
        distf = dist.astype(F32)
        for head in range(N_HEADS_SWA):
            slope2 = float(2.0 ** (-8.0 * (head + 1) / N_HEADS_SWA)) * LOG2E
            base = jnp.where(band, -slope2 * distf, -jnp.inf)
            bias_ref[1, head] = base
            bias_ref[0, head] = jnp.where(col >= tq, base, -jnp.inf)

    table = jnp.minimum(i, 1)
    lane = lax.broadcasted_iota(jnp.int32, (tq, LANES), 1)
    lo_half = lane < HEAD_DIM
    for t in range(2):
        kt = jnp.concatenate([kp_ref[:, t * LANES:(t + 1) * LANES],
                              kc_ref[:, t * LANES:(t + 1) * LANES]], axis=0)
        vt = jnp.concatenate([vp_ref[:, t * LANES:(t + 1) * LANES],
                              vc_ref[:, t * LANES:(t + 1) * LANES]], axis=0)
        heads = [(g, e) for g in range(4) for e in range(2)]
        scores = []
        for g, e in heads:
            tile = t * 4 + g
            qt = q_ref[:, tile * LANES:(tile + 1) * LANES]
            qm = jnp.where(lo_half if e == 0 else ~lo_half, qt, jnp.zeros_like(qt))
            scores.append(lax.dot_general(qm, kt, (((1,), (1,)), ((), ())),
                                          preferred_element_type=F32))
        probs, rdens = [], []
        for (g, e), s in zip(heads, scores):
            head = (2 * t + e) * 4 + g
            sink = sinks_ref[head] * LOG2E
            s = s + bias_ref[table, head]
            m = jnp.maximum(jnp.max(s, axis=-1, keepdims=True), sink)
            p = jnp.exp2(s - m)
            rdens.append(1.0 / (jnp.sum(p, axis=-1, keepdims=True) + jnp.exp2(sink - m)))
            probs.append(p.astype(BF16))
        outs = [jnp.dot(p, vt, preferred_element_type=F32) * r for p, r in zip(probs, rdens)]
        for g in range(4):
            tile = t * 4 + g
            o_ref[:, tile * LANES:(tile + 1) * LANES] = jnp.where(lo_half, outs[2 * g], outs[2 * g + 1]).astype(BF16)


def _swa(sinks, proj):
    nq = SEQ // SWA_TQ
    kblk = KA_COL // KV_WIDTH
    vblk = VA_COL // KV_WIDTH
    grid_spec = pltpu.PrefetchScalarGridSpec(
        num_scalar_prefetch=1,
        grid=(BATCH, nq),
        in_specs=[
            pl.BlockSpec((SWA_TQ, SWA_WIDTH), lambda b, i, s: (b * nq + i, 0)),
            pl.BlockSpec((SWA_TQ, KV_WIDTH), lambda b, i, s: (b * nq + i, kblk)),
            pl.BlockSpec((SWA_TQ, KV_WIDTH), lambda b, i, s: (b * nq + jnp.maximum(i - 1, 0), kblk)),
            pl.BlockSpec((SWA_TQ, KV_WIDTH), lambda b, i, s: (b * nq + i, vblk)),
            pl.BlockSpec((SWA_TQ, KV_WIDTH), lambda b, i, s: (b * nq + jnp.maximum(i - 1, 0), vblk)),
        ],
        out_specs=pl.BlockSpec((SWA_TQ, SWA_WIDTH), lambda b, i, s: (b * nq + i, 0)),
        scratch_shapes=[pltpu.VMEM((2, N_HEADS_SWA, SWA_TQ, 2 * SWA_TQ), F32)],
    )
    return pl.pallas_call(
        _swa_kernel,
        grid_spec=grid_spec,
        out_shape=jax.ShapeDtypeStruct((N_TOK, SWA_WIDTH), BF16),
        compiler_params=_cparams(("arbitrary", "arbitrary")),
        name="swa_attn",
    )(sinks, proj, proj, proj, proj, proj)


FOX_T = 512
FOX_ONES = 16
FOX_VROWS = HEAD_DIM + FOX_ONES


def _fox_kernel(q_ref, k_ref, aug_ref, vt_ref, o_ref, mask_ref, t_ref, acc_ref):
    qi = pl.program_id(2)
    t = FOX_T

    @pl.when(qi == 0)
    def _():
        kr = lax.broadcasted_iota(jnp.int32, (t, t), 0)
        qc = lax.broadcasted_iota(jnp.int32, (t, t), 1)
        mask_ref[...] = jnp.where(kr <= qc, 0.0, -jnp.inf)

    qf = q_ref[...].astype(F32).T
    drow = lax.broadcasted_iota(jnp.int32, (LANES, t), 0)
    qaug = []
    for hh in range(2):
        qh = jnp.where((drow >= HEAD_DIM * hh) & (drow < HEAD_DIM * (hh + 1)), qf, 0.0)
        ones = jnp.where((drow >= FOX_AUG * hh) & (drow < FOX_AUG * (hh + 1)), 1.0, 0.0)
        qaug.append(jnp.concatenate([qh, ones], axis=0).astype(BF16))
    acc_ref[...] = jnp.zeros_like(acc_ref)

    def stage_a(j, slot, diag):
        r0 = pl.multiple_of(j * t, t)
        kb = jnp.concatenate([k_ref[pl.ds(r0, t), :], aug_ref[pl.ds(r0, t), :]], axis=1)
        sts = [jnp.dot(kb, qaug[hh], preferred_element_type=F32) for hh in range(2)]
        mbs = []
        for hh in range(2):
            tt = sts[hh]
            if diag:
                tt = tt + mask_ref[...]
            t_ref[slot, hh] = tt
            mbs.append(jnp.max(tt, axis=0, keepdims=True))
        return tuple(mbs)

    def stage_b(jv, slot, ms, mbs):
        new, ps, alphas = [], [], []
        for hh in range(2):
            m_new = jnp.maximum(ms[hh], mbs[hh])
            alphas.append(jnp.exp2(ms[hh] - m_new))
            ps.append(jnp.exp2(t_ref[slot, hh] - m_new).astype(BF16))
            new.append(m_new)
        ones = jnp.ones((FOX_ONES, t), BF16)
        pvs = [jnp.dot(jnp.concatenate([vt_ref[0, 0, jv, hh * HEAD_DIM:(hh + 1) * HEAD_DIM, :], ones], axis=0),
                       ps[hh], preferred_element_type=F32)
               for hh in range(2)]
        for hh in range(2):
            acc_ref[hh] = alphas[hh] * acc_ref[hh] + pvs[hh]
        return tuple(new)

    ml0 = tuple(jnp.full((1, t), -jnp.inf, F32) for _ in range(2))
    mb0 = stage_a(qi, 0, True)

    def pair(j0, c):
        ml, mbs, jprev = c
        mb1 = stage_a(j0, 1, False)
        ml = stage_b(jprev, 0, ml, mbs)
        mb2 = stage_a(j0 + 1, 0, False)
        ml = stage_b(j0, 1, ml, mb1)
        return ml, mb2, j0 + 1

    def quad(ii, c):
        return pair(4 * ii + 2, pair(4 * ii, c))

    carry = lax.fori_loop(0, qi // 4, quad, (ml0, mb0, qi))
    carry = lax.cond(qi % 4 >= 2, lambda c: pair(4 * (qi // 4), c), lambda c: c, carry)
    ml, mbs, jprev = carry

    def odd_tail(c):
        ml, mbs, jprev = c
        mb1 = stage_a(qi - 1, 1, False)
        ml = stage_b(jprev, 0, ml, mbs)
        return stage_b(qi - 1, 1, ml, mb1)

    def even_tail(c):
        ml, mbs, jprev = c
        return stage_b(jprev, 0, ml, mbs)

    ml = lax.cond(qi % 2 == 1, odd_tail, even_tail, (ml, mbs, jprev))
    del ml
    ot = jnp.concatenate([acc_ref[hh, 0:HEAD_DIM, :] / acc_ref[hh, HEAD_DIM:HEAD_DIM + 1, :]
                          for hh in range(2)], axis=0)
    o_ref[...] = ot.T.astype(BF16)


def _fox(proj, aug, vt5):
    nq = SEQ // FOX_T
    npair = N_HEADS_FOX // 2
    return pl.pallas_call(
        _fox_kernel,
        grid=(BATCH, npair, nq),
        in_specs=[
            pl.BlockSpec((FOX_T, LANES), lambda b, h, i: (b * nq + i, QB_BLK + h)),
            pl.BlockSpec((SEQ, LANES), lambda b, h, i: (b, KB_BLK + h)),
            pl.BlockSpec((SEQ, LANES), lambda b, h, i: (b, h)),
            pl.BlockSpec((1, 1, nq, LANES, FOX_T), lambda b, h, i: (b, h, 0, 0, 0)),
        ],
        out_specs=pl.BlockSpec((FOX_T, LANES), lambda b, h, i: (b * nq + i, h)),
        out_shape=jax.ShapeDtypeStruct((N_TOK, FOX_WIDTH), BF16),
        scratch_shapes=[
            pltpu.VMEM((FOX_T, FOX_T), F32),
            pltpu.VMEM((2, 2, FOX_T, FOX_T), F32),
            pltpu.VMEM((2, FOX_VROWS, FOX_T), F32),
        ],
        compiler_params=_cparams(("arbitrary", "arbitrary", "arbitrary")),
        name="fox_attn",
    )(proj, proj, aug, vt5)


OUT_TM = 512
OUT_SUB = 256
SUBLANES = 8
PACKED_COLS = D_MODEL // 2
PACK_ROWS = PACKED_COLS // LANES
DMA_UNROLL = 8


def _outproj_kernel(oa_ref, ob_ref, x_ref, mod_ref, ga_ref, gb_ref, w_ref, gm_ref, wr2_ref, br_ref,
                    x1_ref, h2p_ref, ids_ref, gates_ref):
    subs = [slice(k * OUT_SUB, (k + 1) * OUT_SUB) for k in range(OUT_TM // OUT_SUB)]
    mixed = [jnp.concatenate([_rms(oa_ref[r, :].astype(F32)) * ga_ref[...],
                              _rms(ob_ref[r, :].astype(F32)) * gb_ref[...]], axis=1).astype(BF16) for r in subs]
    ys = [jnp.dot(m, w_ref[...], preferred_element_type=F32) for m in mixed]
    h2s = []
    for r, y in zip(subs, ys):
        x1 = x_ref[r, :] + mod_ref[0, 2:3, :] * y
        x1_ref[r, :] = x1
        h2s.append((_rms(x1) * gm_ref[...]) * (1.0 + mod_ref[0, 4:5, :]) + mod_ref[0, 3:4, :])
    hbs = [h2.astype(BF16) for h2 in h2s]
    for r, h2 in zip(subs, h2s):
        packed = _pack_bf16_pairs(h2)
        for s in range(PACK_ROWS):
            h2p_ref[r, s, :] = packed[:, s * LANES:(s + 1) * LANES]

    hls = [(h2 - hb.astype(F32)).astype(BF16) for h2, hb in zip(h2s, hbs)]
    r2s = [jnp.dot(hb, wr2_ref[...], preferred_element_type=F32) for hb in hbs]
    r3s = [jnp.dot(hl, wr2_ref[:, :LANES], preferred_element_type=F32) for hl in hls]
    for r, r2, r3 in zip(subs, r2s, r3s):
        logits = (r2[:, :LANES] + (r2[:, LANES:] + r3)) + br_ref[...]
        ids, gates = _route(logits)
        ids_ref[r, :] = ids
        gates_ref[r, :] = gates


def _route(logits):
    lane = lax.broadcasted_iota(jnp.int32, logits.shape, 1)
    neg = -jnp.inf
    is_g = lane < N_GROUPS
    gl = jnp.where(is_g, logits, neg)
    gmax = jnp.max(gl, axis=1, keepdims=True)
    gsel = jnp.min(jnp.where(gl == gmax, lane, LANES), axis=1, keepdims=True)
    gsum = jnp.sum(jnp.where(is_g, jnp.exp(gl - gmax), 0.0), axis=1, keepdims=True)
    g_val = 1.0 / gsum
    elane = lane - N_GROUPS
    in_sel = (elane >= 0) & (elane < N_EXPERTS) & ((elane >> 3) == gsel)
    ev = jnp.where(in_sel, logits, neg)
    t1 = jnp.max(ev, axis=1, keepdims=True)
    i1 = jnp.min(jnp.where(ev == t1, lane, LANES), axis=1, keepdims=True)
    ev2 = jnp.where(lane == i1, neg, ev)
    t2 = jnp.max(ev2, axis=1, keepdims=True)
    i2 = jnp.min(jnp.where(ev2 == t2, lane, LANES), axis=1, keepdims=True)
    e2 = jnp.exp(t2 - t1)
    den = 1.0 + e2
    w1 = (1.0 / den) * g_val
    w2 = (e2 / den) * g_val
    return (jnp.where(lane == 0, i1 - N_GROUPS, jnp.where(lane == 1, i2 - N_GROUPS, 0)),
            jnp.where(lane == 0, w1, jnp.where(lane == 1, w2, 0.0)))


def _outproj(oa, ob, x2, mod3, ga, gb, w_out, gm, wr, br):
    tiles_per_batch = SEQ // OUT_TM
    row = lambda i: (i, 0)
    const = lambda i: (0, 0)
    wr_hi = wr.astype(BF16)
    wr_lo = (wr - wr_hi.astype(F32)).astype(BF16)
    wr2 = jnp.concatenate([wr_hi, wr_lo], axis=1)
    return pl.pallas_call(
        _outproj_kernel,
        grid=(N_TOK // OUT_TM,),
        in_specs=[
            pl.BlockSpec((OUT_TM, SWA_WIDTH), row),
            pl.BlockSpec((OUT_TM, FOX_WIDTH), row),
            pl.BlockSpec((OUT_TM, D_MODEL), row),
            pl.BlockSpec((1, 6, D_MODEL), lambda i: (i // tiles_per_batch, 0, 0)),
            pl.BlockSpec((1, SWA_WIDTH), const),
            pl.BlockSpec((1, FOX_WIDTH), const),
            pl.BlockSpec((D_MODEL, D_MODEL), const),
            pl.BlockSpec((1, D_MODEL), const),
            pl.BlockSpec((D_MODEL, 2 * LANES), const),
            pl.BlockSpec((1, LANES), const),
        ],
        out_specs=[
            pl.BlockSpec((OUT_TM, D_MODEL), row),
            pl.BlockSpec((OUT_TM, PACK_ROWS, LANES), lambda i: (i, 0, 0)),
            pl.BlockSpec((OUT_TM, LANES), row),
            pl.BlockSpec((OUT_TM, LANES), row),
        ],
        out_shape=[
            jax.ShapeDtypeStruct((N_TOK, D_MODEL), F32),
            jax.ShapeDtypeStruct((N_TOK, PACK_ROWS, LANES), jnp.uint32),
            jax.ShapeDtypeStruct((N_TOK, LANES), jnp.int32),
            jax.ShapeDtypeStruct((N_TOK, LANES), F32),
        ],
        compiler_params=_cparams(("arbitrary",)),
        name="outproj_router",
    )(oa, ob, x2, mod3, ga, gb, w_out, gm, wr2, br)


N_BLOCKS = (N_TOK * TOP_K) // MOE_BLOCK + N_EXPERTS
P_ROWS = N_BLOCKS * MOE_BLOCK


def _pack_bf16_pairs(x):
    xb = x.astype(BF16).astype(F32)
    half = x.shape[1] // 2
    lo = lax.shift_right_logical(pltpu.bitcast(xb[:, :half], jnp.uint32), jnp.uint32(16))
    hi = pltpu.bitcast(xb[:, half:], jnp.uint32) & jnp.uint32(0xFFFF0000)
    return lo | hi


def _unpack_bf16_pairs(w):
    return (pltpu.bitcast(lax.shift_left(w, jnp.uint32(16)), F32),
            pltpu.bitcast(w & jnp.uint32(0xFFFF0000), F32))


def _expert_kernel(be_ref, tok_ref, nused_ref, nxt_ref, h2_hbm, wg_hbm, wu_hbm, wd_hbm, y_hbm, xbuf, sem,
                   wgf, wuf, wdf, wsem, wgb, wub, wdb, ystage, osem):
    i = pl.program_id(0)
    nused = nused_ref[0]

    def weight_copies(e):
        return (pltpu.make_async_copy(wg_hbm.at[e], wgf, wsem.at[0]),
                pltpu.make_async_copy(wu_hbm.at[e], wuf, wsem.at[1]),
                pltpu.make_async_copy(wd_hbm.at[e], wdf, wsem.at[2]))

    def issue(blk, slot):
        def body(g, _):
            for q in range(SUBLANES):
                tok = tok_ref[blk * MOE_BLOCK + g * SUBLANES + q]
                pltpu.make_async_copy(h2_hbm.at[tok], xbuf.at[slot, g, :, q, :], sem.at[slot]).start(
                    priority=q % 2)
            return 0
        lax.fori_loop(0, MOE_BLOCK // SUBLANES, body, 0)

    def out_copies(blk, slot):
        r0 = pl.multiple_of(blk * MOE_BLOCK, MOE_BLOCK)
        return [pltpu.make_async_copy(ystage.at[slot, :, pl.ds(s * LANES, LANES)],
                                      y_hbm.at[pl.ds(r0, MOE_BLOCK), s, :], osem.at[slot])
                for s in range(PACK_ROWS)]

    def out_wait(slot):
        pltpu.make_async_copy(ystage.at[slot], ystage.at[slot], osem.at[slot]).wait()

    def gather_wait(slot):
        pltpu.make_async_copy(xbuf.at[slot], xbuf.at[slot], sem.at[slot]).wait()

    @pl.when(i == 0)
    def _():
        issue(0, 0)

    @pl.when(i + 1 < nused)
    def _():
        issue(i + 1, (i + 1) % 2)

    oslot = i % 2

    @pl.when(i >= 2)
    def _():
        out_wait(oslot)

    e = be_ref[i]
    new_expert = ((i == 0) | (e != be_ref[jnp.maximum(i - 1, 0)])) & (i < nused)

    @pl.when(i == 0)
    def _():
        for cp in weight_copies(e):
            cp.start()

    @pl.when(new_expert)
    def _():
        for cp in weight_copies(e):
            cp.wait()
        wgb[...] = wgf[...].astype(BF16)
        wub[...] = wuf[...].astype(BF16)
        wdb[...] = wdf[...].astype(BF16)

    @pl.when(new_expert & (nxt_ref[e] >= 0))
    def _():
        for cp in weight_copies(nxt_ref[e]):
            cp.start()

    @pl.when(i < nused)
    def _():
        slot = i % 2
        gather_wait(slot)
        gsub = MOE_SUB // SUBLANES
        xbs = []
        for k in range(MOE_BLOCK // MOE_SUB):
            xu = jnp.concatenate([xbuf[slot, k * gsub:(k + 1) * gsub, s].reshape(MOE_SUB, LANES)
                                  for s in range(PACK_ROWS)], axis=1)
            xbs.append(jnp.concatenate(_unpack_bf16_pairs(xu), axis=1).astype(BF16))
        gs = [jnp.dot(xb, wgb[...], preferred_element_type=F32) for xb in xbs]
        us = [jnp.dot(xb, wub[...], preferred_element_type=F32) for xb in xbs]
        hids = [(jax.nn.silu(g) * u).astype(BF16) for g, u in zip(gs, us)]
        ys = [jnp.dot(hid, wdb[...], preferred_element_type=F32) for hid in hids]
        for k, y in enumerate(ys):
            ystage[oslot, k * MOE_SUB:(k + 1) * MOE_SUB, :] = _pack_bf16_pairs(y)

    @pl.when(i >= nused)
    def _():
        ystage[oslot] = jnp.zeros((MOE_BLOCK, PACKED_COLS), jnp.uint32)

    for cp in out_copies(i, oslot):
        cp.start()

    @pl.when(i == pl.num_programs(0) - 1)
    def _():
        out_wait(oslot)
        out_wait(1 - oslot)


def _experts(block_e, row_tok, nused, next_used, h2p, wg, wu, wd):
    grid_spec = pltpu.PrefetchScalarGridSpec(
        num_scalar_prefetch=4,
        grid=(N_BLOCKS,),
        in_specs=[pl.BlockSpec(memory_space=pl.ANY)] * 4,
        out_specs=pl.BlockSpec(memory_space=pl.ANY),
        scratch_shapes=[pltpu.VMEM((2, MOE_BLOCK // SUBLANES, PACK_ROWS, SUBLANES, LANES), jnp.uint32),
                        pltpu.SemaphoreType.DMA((2,)),
                        pltpu.VMEM((D_MODEL, D_EXPERT), F32),
                        pltpu.VMEM((D_MODEL, D_EXPERT), F32),
                        pltpu.VMEM((D_EXPERT, D_MODEL), F32),
                        pltpu.SemaphoreType.DMA((3,)),
                        pltpu.VMEM((D_MODEL, D_EXPERT), BF16),
                        pltpu.VMEM((D_MODEL, D_EXPERT), BF16),
                        pltpu.VMEM((D_EXPERT, D_MODEL), BF16),
                        pltpu.VMEM((2, MOE_BLOCK, PACKED_COLS), jnp.uint32),
                        pltpu.SemaphoreType.DMA((2,))],
    )
    return pl.pallas_call(
        _expert_kernel,
        grid_spec=grid_spec,
        out_shape=jax.ShapeDtypeStruct((P_ROWS, PACK_ROWS, LANES), jnp.uint32),
        compiler_params=_cparams(("arbitrary",)),
        name="expert_ffn",
    )(block_e, row_tok, nused, next_used, h2p, wg, wu, wd)


CMB_TM = 256


def _combine_kernel(dest_ref, ys_hbm, x1_ref, gates_ref, mod_ref, fg_ref, o_ref, ybuf, sem):
    i = pl.program_id(0)
    n = pl.num_programs(0)
    groups = CMB_TM // SUBLANES

    def issue(tile, slot):
        def body(g, _):
            for q in range(SUBLANES):
                for k in range(TOP_K):
                    d = dest_ref[(tile * CMB_TM + g * SUBLANES + q) * TOP_K + k]
                    pltpu.make_async_copy(ys_hbm.at[d], ybuf.at[slot, k * groups + g, :, q, :],
                                          sem.at[slot]).start(priority=k)
            return 0
        lax.fori_loop(0, groups, body, 0)

    @pl.when(i == 0)
    def _():
        issue(0, 0)

    @pl.when(i + 1 < n)
    def _():
        issue(i + 1, (i + 1) % 2)

    slot = i % 2
    pltpu.make_async_copy(ybuf.at[slot], ybuf.at[slot], sem.at[slot]).wait()
    gts = gates_ref[...]
    w0 = gts[:, 0:1]
    w1 = gts[:, 1:2]
    half = D_MODEL // 2
    ssq = jnp.zeros((CMB_TM, 1), F32)
    for s in range(PACK_ROWS):
        y0 = _unpack_bf16_pairs(ybuf[slot, 0:groups, s].reshape(CMB_TM, LANES))
        y1 = _unpack_bf16_pairs(ybuf[slot, groups:2 * groups, s].reshape(CMB_TM, LANES))
        for part in range(2):
            cols = slice(part * half + s * LANES, part * half + (s + 1) * LANES)
            x2 = x1_ref[:, cols] + mod_ref[0, 5:6, cols] * (y0[part] * w0 + y1[part] * w1)
            ssq = ssq + jnp.sum(x2 * x2, axis=-1, keepdims=True)
            o_ref[:, cols] = x2
    o_ref[...] = (o_ref[...] * lax.rsqrt(ssq * (1.0 / D_MODEL) + EPS)) * fg_ref[...]


def _combine(dest, ys, x1, gates, mod3, final_g):
    tiles_per_batch = SEQ // CMB_TM
    grid_spec = pltpu.PrefetchScalarGridSpec(
        num_scalar_prefetch=1,
        grid=(N_TOK // CMB_TM,),
        in_specs=[
            pl.BlockSpec(memory_space=pl.ANY),
            pl.BlockSpec((CMB_TM, D_MODEL), lambda i, d: (i, 0)),
            pl.BlockSpec((CMB_TM, LANES), lambda i, d: (i, 0)),
            pl.BlockSpec((1, 6, D_MODEL), lambda i, d: (i // tiles_per_batch, 0, 0)),
            pl.BlockSpec((1, D_MODEL), lambda i, d: (0, 0)),
        ],
        out_specs=pl.BlockSpec((CMB_TM, D_MODEL), lambda i, d: (i, 0)),
        scratch_shapes=[pltpu.VMEM((2, TOP_K * CMB_TM // SUBLANES, PACK_ROWS, SUBLANES, LANES), jnp.uint32),
                        pltpu.SemaphoreType.DMA((2,))],
    )
    return pl.pallas_call(
        _combine_kernel,
        grid_spec=grid_spec,
        out_shape=jax.ShapeDtypeStruct((N_TOK, D_MODEL), F32),
        compiler_params=_cparams(("arbitrary",)),
        name="combine_final",
    )(dest, ys, x1, gates, mod3, final_g)


def _routing_tables(eid):
    a = eid.shape[0]
    onehot = (eid[:, None] == jnp.arange(N_EXPERTS, dtype=jnp.int32)[None, :]).astype(jnp.int32)
    csum = jnp.cumsum(onehot, axis=0)
    rank = jnp.sum(onehot * csum, axis=1) - 1
    counts = csum[-1]
    padded = (counts + MOE_BLOCK - 1) // MOE_BLOCK * MOE_BLOCK
    pend = jnp.cumsum(padded)
    pstart = pend - padded
    dest = (pstart[eid] + rank).astype(jnp.int32)
    blk_row = jnp.arange(N_BLOCKS, dtype=jnp.int32) * MOE_BLOCK
    block_e = jnp.minimum(jnp.sum((pend[None, :] <= blk_row[:, None]).astype(jnp.int32), axis=1),
                          N_EXPERTS - 1).astype(jnp.int32)
    tok_sorted = (jnp.argsort(eid, stable=True) // TOP_K).astype(jnp.int32)
    starts = jnp.cumsum(counts) - counts
    in_blk = jnp.arange(MOE_BLOCK, dtype=jnp.int32)[None, :]
    k = (blk_row - pstart[block_e])[:, None] + in_blk
    src = jnp.clip(starts[block_e][:, None] + k, 0, a - 1)
    row_tok = jnp.where(k < counts[block_e][:, None], tok_sorted[src], 0).reshape(P_ROWS).astype(jnp.int32)
    nused = (pend[-1] // MOE_BLOCK).astype(jnp.int32).reshape(1)
    ids = jnp.arange(N_EXPERTS, dtype=jnp.int32)
    later_used = (ids[None, :] > ids[:, None]) & (counts[None, :] > 0)
    next_used = jnp.min(jnp.where(later_used, ids[None, :], N_EXPERTS), axis=1)
    next_used = jnp.where(next_used == N_EXPERTS, -1, next_used).astype(jnp.int32)
    return dest, row_tok, block_e, nused, next_used


def kernel(x, c, w_ada, b_ada, norm_mix_g, w_in, b_forget, sinks, out_norm_swa_g, out_norm_fox_g, w_out,
           norm_moe_g, w_group, b_group, w_expert, b_expert, w_gate, w_up, w_down, final_g):
    assert IN_TM == FOX_T
    x2 = x.reshape(N_TOK, D_MODEL)
    perm = _swa_perm()

    w_in0 = w_in[0]
    w_row = jnp.concatenate([w_in0[:, :SWA_WIDTH][:, perm], w_in0[:, SWA_WIDTH:VB_COL]], axis=1).astype(BF16)
    w_vt = w_in0[:, VB_COL:MAIN_COLS].astype(BF16)
    w_f = jnp.pad(w_in0[:, MAIN_COLS:], ((0, 0), (0, LANES - N_HEADS_FOX))).astype(BF16)
    colscale = jnp.concatenate([
        jnp.full((SWA_WIDTH,), HEAD_DIM ** -0.5 * LOG2E, F32),
        jnp.ones((2 * KV_WIDTH,), F32),
        jnp.full((FOX_WIDTH,), HEAD_DIM ** -0.5 * LOG2E, F32),
        jnp.ones((FOX_WIDTH,), F32)]).reshape(1, ROW_COLS)
    bf_row = jnp.pad(b_forget[0], (0, LANES - N_HEADS_FOX)).reshape(1, LANES)
    w_out_p = jnp.concatenate([w_out[0][:SWA_WIDTH][perm], w_out[0][SWA_WIDTH:]], axis=0).astype(BF16)
    ga = out_norm_swa_g[0][perm].reshape(1, SWA_WIDTH)
    gb = out_norm_fox_g[0].reshape(1, FOX_WIDTH)
    wr = jnp.pad(jnp.concatenate([w_group[0], w_expert[0]], axis=1),
                 ((0, 0), (0, LANES - N_GROUPS - N_EXPERTS)))
    br = jnp.pad(jnp.concatenate([b_group[0], b_expert[0]]), (0, LANES - N_GROUPS - N_EXPERTS)).reshape(1, LANES)
    wg, wu, wd = w_gate[0], w_up[0], w_down[0]

    mod = _adaln(c, w_ada[0], b_ada[0])
    mod3 = mod.reshape(BATCH, 6, D_MODEL)

    proj, f, vt5 = _inproj(x2, mod3, norm_mix_g[0].reshape(1, D_MODEL), w_row, w_f, colscale, w_vt)
    aug = _forget_bias(f, bf_row)

    o_a = _swa(sinks[0], proj)
    o_b = _fox(proj, aug, vt5)

    x1, h2, ids, gates = _outproj(o_a, o_b, x2, mod3, ga, gb, w_out_p,
                                  norm_moe_g[0].reshape(1, D_MODEL), wr, br)

    eid = ids[:, :TOP_K].reshape(-1)
    dest, row_tok, block_e, nused, next_used = _routing_tables(eid)
    ys = _experts(block_e, row_tok, nused, next_used, h2, wg, wu, wd)
    out = _combine(dest, ys, x1, gates, mod3, final_g.reshape(1, D_MODEL))
    return out.reshape(BATCH, SEQ, D_MODEL)
```

```python
import functools
import math

import numpy as np
import jax
import jax.numpy as jnp
from jax import lax
from jax.experimental import pallas as pl
from jax.experimental.pallas import tpu as pltpu

F32 = jnp.float32
BF16 = jnp.bfloat16

D_MODEL = 2048
BATCH = 2
SEQ = 8192
N_TOK = BATCH * SEQ
HEAD_DIM = 64
N_HEADS_SWA = 16
N_KV_SWA = 4
N_HEADS_FOX = 16
WINDOW = 128
SWA_WIDTH = N_HEADS_SWA * HEAD_DIM
KV_WIDTH = N_KV_SWA * HEAD_DIM
FOX_WIDTH = N_HEADS_FOX * HEAD_DIM
MAIN_COLS = SWA_WIDTH + 2 * KV_WIDTH + 3 * FOX_WIDTH
N_GROUPS = 4
EXPERTS_PER_GROUP = 8
N_EXPERTS = N_GROUPS * EXPERTS_PER_GROUP
TOP_K = 2
D_EXPERT = 512
MOE_BLOCK = 256
MOE_SUB = 128
EPS = 1e-6
LOG2E = math.log2(math.e)

LANES = 128
VMEM_LIMIT = 56 * 1024 * 1024

QA_BLK = 0
KA_COL = SWA_WIDTH
VA_COL = SWA_WIDTH + KV_WIDTH
QB_BLK = (SWA_WIDTH + 2 * KV_WIDTH) // LANES
KB_BLK = QB_BLK + FOX_WIDTH // LANES
VB_COL = SWA_WIDTH + 2 * KV_WIDTH + 2 * FOX_WIDTH
ROW_COLS = VB_COL


def _cparams(sem, vmem=VMEM_LIMIT):
    return pltpu.CompilerParams(dimension_semantics=sem, vmem_limit_bytes=vmem)


ADA_TN = 1024


def _adaln_kernel(cb_ref, w_ref, b_ref, o_ref):
    for b in range(BATCH):
        s = jax.nn.silu(cb_ref[b])
        cols = []
        for j in range(ADA_TN // LANES):
            prod = w_ref[:, j * LANES:(j + 1) * LANES] * s
            cols.append(jnp.sum(prod, axis=0, keepdims=True))
        o_ref[b:b + 1, :] = jnp.concatenate(cols, axis=1) + b_ref[...]


def _adaln(c, w_ada, b_ada):
    ncol = w_ada.shape[1]
    cb = jnp.broadcast_to(c[:, :, None], (BATCH, D_MODEL, LANES))
    return pl.pallas_call(
        _adaln_kernel,
        grid=(ncol // ADA_TN,),
        in_specs=[
            pl.BlockSpec((BATCH, D_MODEL, LANES), lambda j: (0, 0, 0)),
            pl.BlockSpec((D_MODEL, ADA_TN), lambda j: (0, j)),
            pl.BlockSpec((1, ADA_TN), lambda j: (0, j)),
        ],
        out_specs=pl.BlockSpec((BATCH, ADA_TN), lambda j: (0, j)),
        out_shape=jax.ShapeDtypeStruct((BATCH, ncol), F32),
        compiler_params=_cparams(("arbitrary",)),
        name="adaln",
    )(cb, w_ada, b_ada.reshape(1, ncol))


def _rms(x):
    return x * lax.rsqrt(jnp.mean(x * x, axis=-1, keepdims=True) + EPS)


IN_TM = 512
IN_TN = 1792
IN_NJ = ROW_COLS // IN_TN


def _inproj_kernel(x_ref, mod_ref, g_ref, w_ref, wf_ref, cs_ref, wvt_ref, o_ref, f_ref, vt_ref, h_ref):
    j = pl.program_id(1)

    @pl.when(j == 0)
    def _():
        y = _rms(x_ref[...])
        h = (y * g_ref[...]) * (1.0 + mod_ref[0, 1:2, :]) + mod_ref[0, 0:1, :]
        hb = h.astype(BF16)
        h_ref[...] = hb
        f_ref[...] = jnp.dot(hb, wf_ref[...], preferred_element_type=F32)

    @pl.when(j < IN_NJ)
    def _():
        acc = jnp.dot(h_ref[...], w_ref[...], preferred_element_type=F32)
        o_ref[...] = (acc * cs_ref[...]).astype(BF16)

    @pl.when(j == IN_NJ)
    def _():
        vt = lax.dot_general(wvt_ref[...], h_ref[...], (((0,), (1,)), ((), ())),
                             preferred_element_type=F32)
        vt_ref[0, :, 0] = vt.reshape(N_HEADS_FOX // 2, LANES, IN_TM).astype(BF16)


def _inproj(x2, mod3, g, w_row, w_f, colscale, w_vt):
    tiles_per_batch = SEQ // IN_TM
    last = IN_NJ - 1
    return pl.pallas_call(
        _inproj_kernel,
        grid=(N_TOK // IN_TM, IN_NJ + 1),
        in_specs=[
            pl.BlockSpec((IN_TM, D_MODEL), lambda i, j: (i, 0)),
            pl.BlockSpec((1, 6, D_MODEL), lambda i, j: (i // tiles_per_batch, 0, 0)),
            pl.BlockSpec((1, D_MODEL), lambda i, j: (0, 0)),
            pl.BlockSpec((D_MODEL, IN_TN), lambda i, j: (0, jnp.minimum(j, last))),
            pl.BlockSpec((D_MODEL, LANES), lambda i, j: (0, 0)),
            pl.BlockSpec((1, IN_TN), lambda i, j: (0, jnp.minimum(j, last))),
            pl.BlockSpec((D_MODEL, FOX_WIDTH), lambda i, j: (0, 0)),
        ],
        out_specs=[
            pl.BlockSpec((IN_TM, IN_TN), lambda i, j: (i, jnp.minimum(j, last))),
            pl.BlockSpec((IN_TM, LANES), lambda i, j: (i, 0)),
            pl.BlockSpec((1, N_HEADS_FOX // 2, 1, LANES, IN_TM),
                         lambda i, j: (i // tiles_per_batch, 0, i % tiles_per_batch, 0, 0)),
        ],
        out_shape=[
            jax.ShapeDtypeStruct((N_TOK, ROW_COLS), BF16),
            jax.ShapeDtypeStruct((N_TOK, LANES), F32),
            jax.ShapeDtypeStruct((BATCH, N_HEADS_FOX // 2, SEQ // IN_TM, LANES, IN_TM), BF16),
        ],
        scratch_shapes=[pltpu.VMEM((IN_TM, D_MODEL), BF16)],
        compiler_params=_cparams(("arbitrary", "arbitrary")),
        name="inproj",
    )(x2, mod3, g, w_row, w_f, colscale, w_vt)


CUM_TS = 512
FOX_AUG = 3


def _cum_kernel(f_ref, bf_ref, tri_ref, sel_ref, aug_ref, carry_ref):
    @pl.when(pl.program_id(1) == 0)
    def _():
        carry_ref[...] = jnp.zeros_like(carry_ref)

    lf = jax.nn.log_sigmoid(f_ref[...] + bf_ref[...])
    l_hi = lf.astype(BF16)
    l_r = lf - l_hi.astype(F32)
    l_mid = l_r.astype(BF16)
    l_lo = (l_r - l_mid.astype(F32)).astype(BF16)
    c3 = jnp.dot(tri_ref[...], jnp.concatenate([l_hi, l_mid, l_lo], axis=1), preferred_element_type=F32)
    cs = (c3[:, :LANES] + (c3[:, LANES:2 * LANES] + c3[:, 2 * LANES:])) + carry_ref[...]
    carry_ref[...] = cs[CUM_TS - 1:CUM_TS, :]
    bias = cs * (-LOG2E)
    hi = bias.astype(BF16)
    r1 = bias - hi.astype(F32)
    mid = r1.astype(BF16)
    lo = (r1 - mid.astype(F32)).astype(BF16)
    pieces = jnp.concatenate([hi, mid, lo], axis=1)
    aug_ref[...] = jnp.dot(pieces, sel_ref[...], preferred_element_type=F32).astype(BF16)


def _aug_selector():
    sel = np.zeros((FOX_AUG * LANES, FOX_WIDTH), np.float32)
    for head in range(N_HEADS_FOX):
        for piece in range(FOX_AUG):
            sel[piece * LANES + head, (head // 2) * LANES + FOX_AUG * (head % 2) + piece] = 1.0
    return sel


def _forget_bias(f, b_forget_row):
    tri = jnp.tril(jnp.ones((CUM_TS, CUM_TS), BF16))
    sel = jnp.asarray(_aug_selector(), BF16)
    nblk = SEQ // CUM_TS
    return pl.pallas_call(
        _cum_kernel,
        grid=(BATCH, nblk),
        in_specs=[
            pl.BlockSpec((CUM_TS, LANES), lambda b, i: (b * nblk + i, 0)),
            pl.BlockSpec((1, LANES), lambda b, i: (0, 0)),
            pl.BlockSpec((CUM_TS, CUM_TS), lambda b, i: (0, 0)),
            pl.BlockSpec((FOX_AUG * LANES, FOX_WIDTH), lambda b, i: (0, 0)),
        ],
        out_specs=pl.BlockSpec((CUM_TS, FOX_WIDTH), lambda b, i: (b * nblk + i, 0)),
        out_shape=jax.ShapeDtypeStruct((N_TOK, FOX_WIDTH), BF16),
        scratch_shapes=[pltpu.VMEM((1, LANES), F32)],
        compiler_params=_cparams(("arbitrary", "arbitrary")),
        name="forget_bias",
    )(f, b_forget_row, tri, sel)


SWA_TQ = WINDOW


def _swa_perm():
    perm = np.zeros((SWA_WIDTH,), np.int32)
    for t in range(2):
        for g in range(4):
            for e in range(2):
                head = (2 * t + e) * 4 + g
                base = (t * 4 + g) * LANES + e * HEAD_DIM
                perm[base:base + HEAD_DIM] = head * HEAD_DIM + np.arange(HEAD_DIM)
    return perm


def _swa_kernel(sinks_ref, q_ref, kc_ref, kp_ref, vc_ref, vp_ref, o_ref, bias_ref):
    i = pl.program_id(1)
    tq = SWA_TQ

    @pl.when((pl.program_id(0) == 0) & (i == 0))
    def _():
        row = lax.broadcasted_iota(jnp.int32, (tq, 2 * tq), 0)
        col = lax.broadcasted_iota(jnp.int32, (tq, 2 * tq), 1)
        dist = row + tq - col
        band = (dist >= 0) & (dist < WINDOW)
        distf = dist.astype(F32)
        for head in range(N_HEADS_SWA):
            slope2 = float(2.0 ** (-8.0 * (head + 1) / N_HEADS_SWA)) * LOG2E
            base = jnp.where(band, -slope2 * distf, -jnp.inf)
            bias_ref[1, head] = base
            bias_ref[0, head] = jnp.where(col >= tq, base, -jnp.inf)

    table = jnp.minimum(i, 1)
    lane = lax.broadcasted_iota(jnp.int32, (tq, LANES), 1)
    lo_half = lane < HEAD_DIM
    for t in range(2):
        kt = jnp.concatenate([kp_ref[:, t * LANES:(t + 1) * LANES],
                              kc_ref[:, t * LANES:(t + 1) * LANES]], axis=0)
        vt = jnp.concatenate([vp_ref[:, t * LANES:(t + 1) * LANES],
                              vc_ref[:, t * LANES:(t + 1) * LANES]], axis=0)
        heads = [(g, e) for g in range(4) for e in range(2)]
        scores = []
        for g, e in heads:
            tile = t * 4 + g
            qt = q_ref[:, tile * LANES:(tile + 1) * LANES]
            qm = jnp.where(lo_half if e == 0 else ~lo_half, qt, jnp.zeros_like(qt))
            scores.append(lax.dot_general(qm, kt, (((1,), (1,)), ((), ())),
                                          preferred_element_type=F32))
        probs, rdens = [], []
        for (g, e), s in zip(heads, scores):
            head = (2 * t + e) * 4 + g
            sink = sinks_ref[head] * LOG2E
            s = s + bias_ref[table, head]
            m = jnp.maximum(jnp.max(s, axis=-1, keepdims=True), sink)
            p = jnp.exp2(s - m)
            rdens.append(1.0 / (jnp.sum(p, axis=-1, keepdims=True) + jnp.exp2(sink - m)))
            probs.append(p.astype(BF16))
        outs = [jnp.dot(p, vt, preferred_element_type=F32) * r for p, r in zip(probs, rdens)]
        for g in range(4):
            tile = t * 4 + g
            o_ref[:, tile * LANES:(tile + 1) * LANES] = jnp.where(lo_half, outs[2 * g], outs[2 * g + 1]).astype(BF16)


def _swa(sinks, proj):
    nq = SEQ // SWA_TQ
    kblk = KA_COL // KV_WIDTH
    vblk = VA_COL // KV_WIDTH
    grid_spec = pltpu.PrefetchScalarGridSpec(
        num_scalar_prefetch=1,
        grid=(BATCH, nq),
        in_specs=[
            pl.BlockSpec((SWA_TQ, SWA_WIDTH), lambda b, i, s: (b * nq + i, 0)),
            pl.BlockSpec((SWA_TQ, KV_WIDTH), lambda b, i, s: (b * nq + i, kblk)),
            pl.BlockSpec((SWA_TQ, KV_WIDTH), lambda b, i, s: (b * nq + jnp.maximum(i - 1, 0), kblk)),
            pl.BlockSpec((SWA_TQ, KV_WIDTH), lambda b, i, s: (b * nq + i, vblk)),
            pl.BlockSpec((SWA_TQ, KV_WIDTH), lambda b, i, s: (b * nq + jnp.maximum(i - 1, 0), vblk)),
        ],
        out_specs=pl.BlockSpec((SWA_TQ, SWA_WIDTH), lambda b, i, s: (b * nq + i, 0)),
        scratch_shapes=[pltpu.VMEM((2, N_HEADS_SWA, SWA_TQ, 2 * SWA_TQ), F32)],
    )
    return pl.pallas_call(
        _swa_kernel,
        grid_spec=grid_spec,
        out_shape=jax.ShapeDtypeStruct((N_TOK, SWA_WIDTH), BF16),
        compiler_params=_cparams(("arbitrary", "arbitrary")),
        name="swa_attn",
    )(sinks, proj, proj, proj, proj, proj)


FOX_T = 512
FOX_ONES = 16
FOX_VROWS = HEAD_DIM + FOX_ONES


def _fox_kernel(q_ref, k_ref, aug_ref, vt_ref, o_ref, mask_ref, t_ref, acc_ref):
    qi = pl.program_id(2)
    t = FOX_T

    @pl.when(qi == 0)
    def _():
        kr = lax.broadcasted_iota(jnp.int32, (t, t), 0)
        qc = lax.broadcasted_iota(jnp.int32, (t, t), 1)
        mask_ref[...] = jnp.where(kr <= qc, 0.0, -jnp.inf)

    qf = q_ref[...].astype(F32).T
    drow = lax.broadcasted_iota(jnp.int32, (LANES, t), 0)
    qaug = []
    for hh in range(2):
        qh = jnp.where((drow >= HEAD_DIM * hh) & (drow < HEAD_DIM * (hh + 1)), qf, 0.0)
        ones = jnp.where((drow >= FOX_AUG * hh) & (drow < FOX_AUG * (hh + 1)), 1.0, 0.0)
        qaug.append(jnp.concatenate([qh, ones], axis=0).astype(BF16))
    acc_ref[...] = jnp.zeros_like(acc_ref)

    def stage_a(j, slot, diag):
        r0 = pl.multiple_of(j * t, t)
        kb = jnp.concatenate([k_ref[pl.ds(r0, t), :], aug_ref[pl.ds(r0, t), :]], axis=1)
        sts = [jnp.dot(kb, qaug[hh], preferred_element_type=F32) for hh in range(2)]
        mbs = []
        for hh in range(2):
            tt = sts[hh]
            if diag:
                tt = tt + mask_ref[...]
            t_ref[slot, hh] = tt
            mbs.append(jnp.max(tt, axis=0, keepdims=True))
        return tuple(mbs)

    def stage_b(jv, slot, ms, mbs):
        new, ps, alphas = [], [], []
        for hh in range(2):
            m_new = jnp.maximum(ms[hh], mbs[hh])
            alphas.append(jnp.exp2(ms[hh] - m_new))
            ps.append(jnp.exp2(t_ref[slot, hh] - m_new).astype(BF16))
            new.append(m_new)
        ones = jnp.ones((FOX_ONES, t), BF16)
        pvs = [jnp.dot(jnp.concatenate([vt_ref[0, 0, jv, hh * HEAD_DIM:(hh + 1) * HEAD_DIM, :], ones], axis=0),
                       ps[hh], preferred_element_type=F32)
               for hh in range(2)]
        for hh in range(2):
            acc_ref[hh] = alphas[hh] * acc_ref[hh] + pvs[hh]
        return tuple(new)

    ml0 = tuple(jnp.full((1, t), -jnp.inf, F32) for _ in range(2))
    mb0 = stage_a(qi, 0, True)

    def pair(j0, c):
        ml, mbs, jprev = c
        mb1 = stage_a(j0, 1, False)
        ml = stage_b(jprev, 0, ml, mbs)
        mb2 = stage_a(j0 + 1, 0, False)
        ml = stage_b(j0, 1, ml, mb1)
        return ml, mb2, j0 + 1

    def quad(ii, c):
        return pair(4 * ii + 2, pair(4 * ii, c))

    carry = lax.fori_loop(0, qi // 4, quad, (ml0, mb0, qi))
    base = 4 * (qi // 4)

    def tail(rem):
        def run(c):
            ml, mb_prev, prev = c
            for k in range(rem):
                mb_new = stage_a(base + k, (k + 1) % 2, False)
                ml = stage_b(prev, k % 2, ml, mb_prev)
                prev, mb_prev = base + k, mb_new
            stage_b(prev, rem % 2, ml, mb_prev)
            ot = jnp.concatenate([acc_ref[hh, 0:HEAD_DIM, :] / acc_ref[hh, HEAD_DIM:HEAD_DIM + 1, :]
                                  for hh in range(2)], axis=0)
            o_ref[...] = ot.T.astype(BF16)
            return jnp.int32(0)
        return run

    lax.switch(qi % 4, [tail(rem) for rem in range(4)], carry)


def _fox(proj, aug, vt5):
    nq = SEQ // FOX_T
    npair = N_HEADS_FOX // 2
    return pl.pallas_call(
        _fox_kernel,
        grid=(BATCH, npair, nq),
        in_specs=[
            pl.BlockSpec((FOX_T, LANES), lambda b, h, i: (b * nq + i, QB_BLK + h)),
            pl.BlockSpec((SEQ, LANES), lambda b, h, i: (b, KB_BLK + h)),
            pl.BlockSpec((SEQ, LANES), lambda b, h, i: (b, h)),
            pl.BlockSpec((1, 1, nq, LANES, FOX_T), lambda b, h, i: (b, h, 0, 0, 0)),
        ],
        out_specs=pl.BlockSpec((FOX_T, LANES), lambda b, h, i: (b * nq + i, h)),
        out_shape=jax.ShapeDtypeStruct((N_TOK, FOX_WIDTH), BF16),
        scratch_shapes=[
            pltpu.VMEM((FOX_T, FOX_T), F32),
            pltpu.VMEM((2, 2, FOX_T, FOX_T), F32),
            pltpu.VMEM((2, FOX_VROWS, FOX_T), F32),
        ],
        compiler_params=_cparams(("arbitrary", "arbitrary", "arbitrary")),
        name="fox_attn",
    )(proj, proj, aug, vt5)


OUT_TM = 512
OUT_SUB = 256
SUBLANES = 8
PACKED_COLS = D_MODEL // 2
PACK_ROWS = PACKED_COLS // LANES
DMA_UNROLL = 8


def _outproj_kernel(oa_ref, ob_ref, x_ref, mod_ref, ga_ref, gb_ref, w_ref, gm_ref, wr2_ref, br_ref,
                    x1_ref, h2p_ref, ids_ref, gates_ref):
    subs = [slice(k * OUT_SUB, (k + 1) * OUT_SUB) for k in range(OUT_TM // OUT_SUB)]
    mixed = [jnp.concatenate([_rms(oa_ref[r, :].astype(F32)) * ga_ref[...],
                              _rms(ob_ref[r, :].astype(F32)) * gb_ref[...]], axis=1).astype(BF16) for r in subs]
    ys = [jnp.dot(m, w_ref[...], preferred_element_type=F32) for m in mixed]
    h2s = []
    for r, y in zip(subs, ys):
        x1 = x_ref[r, :] + mod_ref[0, 2:3, :] * y
        x1_ref[r, :] = x1
        h2s.append((_rms(x1) * gm_ref[...]) * (1.0 + mod_ref[0, 4:5, :]) + mod_ref[0, 3:4, :])
    hbs = [h2.astype(BF16) for h2 in h2s]
    for r, h2 in zip(subs, h2s):
        packed = _pack_bf16_pairs(h2)
        for s in range(PACK_ROWS):
            h2p_ref[r, s, :] = packed[:, s * LANES:(s + 1) * LANES]

    hls = [(h2 - hb.astype(F32)).astype(BF16) for h2, hb in zip(h2s, hbs)]
    r2s = [jnp.dot(hb, wr2_ref[...], preferred_element_type=F32) for hb in hbs]
    r3s = [jnp.dot(hl, wr2_ref[:, :LANES], preferred_element_type=F32) for hl in hls]
    for r, r2, r3 in zip(subs, r2s, r3s):
        logits = (r2[:, :LANES] + (r2[:, LANES:] + r3)) + br_ref[...]
        ids, gates = _route(logits)
        ids_ref[r, :] = ids
        gates_ref[r, :] = gates


def _route(logits):
    lane = lax.broadcasted_iota(jnp.int32, logits.shape, 1)
    neg = -jnp.inf
    is_g = lane < N_GROUPS
    gl = jnp.where(is_g, logits, neg)
    gmax = jnp.max(gl, axis=1, keepdims=True)
    gsel = jnp.min(jnp.where(gl == gmax, lane, LANES), axis=1, keepdims=True)
    gsum = jnp.sum(jnp.where(is_g, jnp.exp(gl - gmax), 0.0), axis=1, keepdims=True)
    g_val = 1.0 / gsum
    elane = lane - N_GROUPS
    in_sel = (elane >= 0) & (elane < N_EXPERTS) & ((elane >> 3) == gsel)
    ev = jnp.where(in_sel, logits, neg)
    t1 = jnp.max(ev, axis=1, keepdims=True)
    i1 = jnp.min(jnp.where(ev == t1, lane, LANES), axis=1, keepdims=True)
    ev2 = jnp.where(lane == i1, neg, ev)
    t2 = jnp.max(ev2, axis=1, keepdims=True)
    i2 = jnp.min(jnp.where(ev2 == t2, lane, LANES), axis=1, keepdims=True)
    e2 = jnp.exp(t2 - t1)
    den = 1.0 + e2
    w1 = (1.0 / den) * g_val
    w2 = (e2 / den) * g_val
    return (jnp.where(lane == 0, i1 - N_GROUPS, jnp.where(lane == 1, i2 - N_GROUPS, 0)),
            jnp.where(lane == 0, w1, jnp.where(lane == 1, w2, 0.0)))


def _outproj(oa, ob, x2, mod3, ga, gb, w_out, gm, wr, br):
    tiles_per_batch = SEQ // OUT_TM
    row = lambda i: (i, 0)
    const = lambda i: (0, 0)
    wr_hi = wr.astype(BF16)
    wr_lo = (wr - wr_hi.astype(F32)).astype(BF16)
    wr2 = jnp.concatenate([wr_hi, wr_lo], axis=1)
    return pl.pallas_call(
        _outproj_kernel,
        grid=(N_TOK // OUT_TM,),
        in_specs=[
            pl.BlockSpec((OUT_TM, SWA_WIDTH), row),
            pl.BlockSpec((OUT_TM, FOX_WIDTH), row),
            pl.BlockSpec((OUT_TM, D_MODEL), row),
            pl.BlockSpec((1, 6, D_MODEL), lambda i: (i // tiles_per_batch, 0, 0)),
            pl.BlockSpec((1, SWA_WIDTH), const),
            pl.BlockSpec((1, FOX_WIDTH), const),
            pl.BlockSpec((D_MODEL, D_MODEL), const),
            pl.BlockSpec((1, D_MODEL), const),
            pl.BlockSpec((D_MODEL, 2 * LANES), const),
            pl.BlockSpec((1, LANES), const),
        ],
        out_specs=[
            pl.BlockSpec((OUT_TM, D_MODEL), row),
            pl.BlockSpec((OUT_TM, PACK_ROWS, LANES), lambda i: (i, 0, 0)),
            pl.BlockSpec((OUT_TM, LANES), row),
            pl.BlockSpec((OUT_TM, LANES), row),
        ],
        out_shape=[
            jax.ShapeDtypeStruct((N_TOK, D_MODEL), F32),
            jax.ShapeDtypeStruct((N_TOK, PACK_ROWS, LANES), jnp.uint32),
            jax.ShapeDtypeStruct((N_TOK, LANES), jnp.int32),
            jax.ShapeDtypeStruct((N_TOK, LANES), F32),
        ],
        compiler_params=_cparams(("arbitrary",)),
        name="outproj_router",
    )(oa, ob, x2, mod3, ga, gb, w_out, gm, wr2, br)


N_BLOCKS = (N_TOK * TOP_K) // MOE_BLOCK + N_EXPERTS
P_ROWS = N_BLOCKS * MOE_BLOCK


def _pack_bf16_pairs(x):
    xb = x.astype(BF16).astype(F32)
    half = x.shape[1] // 2
    lo = lax.shift_right_logical(pltpu.bitcast(xb[:, :half], jnp.uint32), jnp.uint32(16))
    hi = pltpu.bitcast(xb[:, half:], jnp.uint32) & jnp.uint32(0xFFFF0000)
    return lo | hi


def _unpack_bf16_pairs(w):
    return (pltpu.bitcast(lax.shift_left(w, jnp.uint32(16)), F32),
            pltpu.bitcast(w & jnp.uint32(0xFFFF0000), F32))


def _expert_kernel(be_ref, tok_ref, nused_ref, nxt_ref, h2_hbm, wg_hbm, wu_hbm, wd_hbm, y_hbm, xbuf, sem,
                   wgf, wuf, wdf, wsem, wgb, wub, wdb, ystage, osem):
    i = pl.program_id(0)
    nused = nused_ref[0]

    def weight_copies(e):
        return (pltpu.make_async_copy(wg_hbm.at[e], wgf, wsem.at[0]),
                pltpu.make_async_copy(wu_hbm.at[e], wuf, wsem.at[1]),
                pltpu.make_async_copy(wd_hbm.at[e], wdf, wsem.at[2]))

    def issue(blk, slot):
        def body(g, _):
            for q in range(SUBLANES):
                tok = tok_ref[blk * MOE_BLOCK + g * SUBLANES + q]
                pltpu.make_async_copy(h2_hbm.at[tok], xbuf.at[slot, g, :, q, :], sem.at[slot]).start(
                    priority=q % 2)
            return 0
        lax.fori_loop(0, MOE_BLOCK // SUBLANES, body, 0)

    def out_copies(blk, slot):
        r0 = pl.multiple_of(blk * MOE_BLOCK, MOE_BLOCK)
        return [pltpu.make_async_copy(ystage.at[slot, :, pl.ds(s * LANES, LANES)],
                                      y_hbm.at[pl.ds(r0, MOE_BLOCK), s, :], osem.at[slot])
                for s in range(PACK_ROWS)]

    def out_wait(slot):
        pltpu.make_async_copy(ystage.at[slot], ystage.at[slot], osem.at[slot]).wait()

    def gather_wait(slot):
        pltpu.make_async_copy(xbuf.at[slot], xbuf.at[slot], sem.at[slot]).wait()

    @pl.when(i == 0)
    def _():
        issue(0, 0)

    @pl.when(i + 1 < nused)
    def _():
        issue(i + 1, (i + 1) % 2)

    oslot = i % 2

    @pl.when(i >= 2)
    def _():
        out_wait(oslot)

    e = be_ref[i]
    new_expert = ((i == 0) | (e != be_ref[jnp.maximum(i - 1, 0)])) & (i < nused)

    @pl.when(i == 0)
    def _():
        for cp in weight_copies(e):
            cp.start()

    @pl.when(new_expert)
    def _():
        for cp in weight_copies(e):
            cp.wait()
        wgb[...] = wgf[...].astype(BF16)
        wub[...] = wuf[...].astype(BF16)
        wdb[...] = wdf[...].astype(BF16)

    @pl.when(new_expert & (nxt_ref[e] >= 0))
    def _():
        for cp in weight_copies(nxt_ref[e]):
            cp.start()

    @pl.when(i < nused)
    def _():
        slot = i % 2
        gather_wait(slot)
        gsub = MOE_SUB // SUBLANES
        xbs = []
        for k in range(MOE_BLOCK // MOE_SUB):
            xu = jnp.concatenate([xbuf[slot, k * gsub:(k + 1) * gsub, s].reshape(MOE_SUB, LANES)
                                  for s in range(PACK_ROWS)], axis=1)
            xbs.append(jnp.concatenate(_unpack_bf16_pairs(xu), axis=1).astype(BF16))
        gs = [jnp.dot(xb, wgb[...], preferred_element_type=F32) for xb in xbs]
        us = [jnp.dot(xb, wub[...], preferred_element_type=F32) for xb in xbs]
        hids = [(jax.nn.silu(g) * u).astype(BF16) for g, u in zip(gs, us)]
        ys = [jnp.dot(hid, wdb[...], preferred_element_type=F32) for hid in hids]
        for k, y in enumerate(ys):
            ystage[oslot, k * MOE_SUB:(k + 1) * MOE_SUB, :] = _pack_bf16_pairs(y)

    @pl.when(i >= nused)
    def _():
        ystage[oslot] = jnp.zeros((MOE_BLOCK, PACKED_COLS), jnp.uint32)

    for cp in out_copies(i, oslot):
        cp.start()

    @pl.when(i == pl.num_programs(0) - 1)
    def _():
        out_wait(oslot)
        out_wait(1 - oslot)


def _experts(block_e, row_tok, nused, next_used, h2p, wg, wu, wd):
    grid_spec = pltpu.PrefetchScalarGridSpec(
        num_scalar_prefetch=4,
        grid=(N_BLOCKS,),
        in_specs=[pl.BlockSpec(memory_space=pl.ANY)] * 4,
        out_specs=pl.BlockSpec(memory_space=pl.ANY),
        scratch_shapes=[pltpu.VMEM((2, MOE_BLOCK // SUBLANES, PACK_ROWS, SUBLANES, LANES), jnp.uint32),
                        pltpu.SemaphoreType.DMA((2,)),
                        pltpu.VMEM((D_MODEL, D_EXPERT), F32),
                        pltpu.VMEM((D_MODEL, D_EXPERT), F32),
                        pltpu.VMEM((D_EXPERT, D_MODEL), F32),
                        pltpu.SemaphoreType.DMA((3,)),
                        pltpu.VMEM((D_MODEL, D_EXPERT), BF16),
                        pltpu.VMEM((D_MODEL, D_EXPERT), BF16),
                        pltpu.VMEM((D_EXPERT, D_MODEL), BF16),
                        pltpu.VMEM((2, MOE_BLOCK, PACKED_COLS), jnp.uint32),
                        pltpu.SemaphoreType.DMA((2,))],
    )
    return pl.pallas_call(
        _expert_kernel,
        grid_spec=grid_spec,
        out_shape=jax.ShapeDtypeStruct((P_ROWS, PACK_ROWS, LANES), jnp.uint32),
        compiler_params=_cparams(("arbitrary",)),
        name="expert_ffn",
    )(block_e, row_tok, nused, next_used, h2p, wg, wu, wd)


CMB_TM = 256


def _combine_kernel(dest_ref, ys_hbm, x1_ref, gates_ref, mod_ref, fg_ref, o_ref, ybuf, sem):
    i = pl.program_id(0)
    n = pl.num_programs(0)
    groups = CMB_TM // SUBLANES

    def issue(tile, slot):
        def body(g, _):
            for q in range(SUBLANES):
                for k in range(TOP_K):
                    d = dest_ref[(tile * CMB_TM + g * SUBLANES + q) * TOP_K + k]
                    pltpu.make_async_copy(ys_hbm.at[d], ybuf.at[slot, k * groups + g, :, q, :],
                                          sem.at[slot]).start(priority=k)
            return 0
        lax.fori_loop(0, groups, body, 0)

    @pl.when(i == 0)
    def _():
        issue(0, 0)

    @pl.when(i + 1 < n)
    def _():
        issue(i + 1, (i + 1) % 2)

    slot = i % 2
    pltpu.make_async_copy(ybuf.at[slot], ybuf.at[slot], sem.at[slot]).wait()
    gts = gates_ref[...]
    w0 = gts[:, 0:1]
    w1 = gts[:, 1:2]
    half = D_MODEL // 2
    ssq = jnp.zeros((CMB_TM, 1), F32)
    for s in range(PACK_ROWS):
        y0 = _unpack_bf16_pairs(ybuf[slot, 0:groups, s].reshape(CMB_TM, LANES))
        y1 = _unpack_bf16_pairs(ybuf[slot, groups:2 * groups, s].reshape(CMB_TM, LANES))
        for part in range(2):
            cols = slice(part * half + s * LANES, part * half + (s + 1) * LANES)
            x2 = x1_ref[:, cols] + mod_ref[0, 5:6, cols] * (y0[part] * w0 + y1[part] * w1)
            ssq = ssq + jnp.sum(x2 * x2, axis=-1, keepdims=True)
            o_ref[:, cols] = x2
    o_ref[...] = (o_ref[...] * lax.rsqrt(ssq * (1.0 / D_MODEL) + EPS)) * fg_ref[...]


def _combine(dest, ys, x1, gates, mod3, final_g):
    tiles_per_batch = SEQ // CMB_TM
    grid_spec = pltpu.PrefetchScalarGridSpec(
        num_scalar_prefetch=1,
        grid=(N_TOK // CMB_TM,),
        in_specs=[
            pl.BlockSpec(memory_space=pl.ANY),
            pl.BlockSpec((CMB_TM, D_MODEL), lambda i, d: (i, 0)),
            pl.BlockSpec((CMB_TM, LANES), lambda i, d: (i, 0)),
            pl.BlockSpec((1, 6, D_MODEL), lambda i, d: (i // tiles_per_batch, 0, 0)),
            pl.BlockSpec((1, D_MODEL), lambda i, d: (0, 0)),
        ],
        out_specs=pl.BlockSpec((CMB_TM, D_MODEL), lambda i, d: (i, 0)),
        scratch_shapes=[pltpu.VMEM((2, TOP_K * CMB_TM // SUBLANES, PACK_ROWS, SUBLANES, LANES), jnp.uint32),
                        pltpu.SemaphoreType.DMA((2,))],
    )
    return pl.pallas_call(
        _combine_kernel,
        grid_spec=grid_spec,
        out_shape=jax.ShapeDtypeStruct((N_TOK, D_MODEL), F32),
        compiler_params=_cparams(("arbitrary",)),
        name="combine_final",
    )(dest, ys, x1, gates, mod3, final_g)


def _routing_tables(eid):
    a = eid.shape[0]
    onehot = (eid[:, None] == jnp.arange(N_EXPERTS, dtype=jnp.int32)[None, :]).astype(jnp.int32)
    csum = jnp.cumsum(onehot, axis=0)
    rank = jnp.sum(onehot * csum, axis=1) - 1
    counts = csum[-1]
    padded = (counts + MOE_BLOCK - 1) // MOE_BLOCK * MOE_BLOCK
    pend = jnp.cumsum(padded)
    pstart = pend - padded
    dest = (pstart[eid] + rank).astype(jnp.int32)
    blk_row = jnp.arange(N_BLOCKS, dtype=jnp.int32) * MOE_BLOCK
    block_e = jnp.minimum(jnp.sum((pend[None, :] <= blk_row[:, None]).astype(jnp.int32), axis=1),
                          N_EXPERTS - 1).astype(jnp.int32)
    tok_sorted = (jnp.argsort(eid, stable=True) // TOP_K).astype(jnp.int32)
    starts = jnp.cumsum(counts) - counts
    in_blk = jnp.arange(MOE_BLOCK, dtype=jnp.int32)[None, :]
    k = (blk_row - pstart[block_e])[:, None] + in_blk
    src = jnp.clip(starts[block_e][:, None] + k, 0, a - 1)
    row_tok = jnp.where(k < counts[block_e][:, None], tok_sorted[src], 0).reshape(P_ROWS).astype(jnp.int32)
    nused = (pend[-1] // MOE_BLOCK).astype(jnp.int32).reshape(1)
    ids = jnp.arange(N_EXPERTS, dtype=jnp.int32)
    later_used = (ids[None, :] > ids[:, None]) & (counts[None, :] > 0)
    next_used = jnp.min(jnp.where(later_used, ids[None, :], N_EXPERTS), axis=1)
    next_used = jnp.where(next_used == N_EXPERTS, -1, next_used).astype(jnp.int32)
    return dest, row_tok, block_e, nused, next_used


def kernel(x, c, w_ada, b_ada, norm_mix_g, w_in, b_forget, sinks, out_norm_swa_g, out_norm_fox_g, w_out,
           norm_moe_g, w_group, b_group, w_expert, b_expert, w_gate, w_up, w_down, final_g):
    assert IN_TM == FOX_T
    x2 = x.reshape(N_TOK, D_MODEL)
    perm = _swa_perm()

    w_in0 = w_in[0]
    w_row = jnp.concatenate([w_in0[:, :SWA_WIDTH][:, perm], w_in0[:, SWA_WIDTH:VB_COL]], axis=1).astype(BF16)
    w_vt = w_in0[:, VB_COL:MAIN_COLS].astype(BF16)
    w_f = jnp.pad(w_in0[:, MAIN_COLS:], ((0, 0), (0, LANES - N_HEADS_FOX))).astype(BF16)
    colscale = jnp.concatenate([
        jnp.full((SWA_WIDTH,), HEAD_DIM ** -0.5 * LOG2E, F32),
        jnp.ones((2 * KV_WIDTH,), F32),
        jnp.full((FOX_WIDTH,), HEAD_DIM ** -0.5 * LOG2E, F32),
        jnp.ones((FOX_WIDTH,), F32)]).reshape(1, ROW_COLS)
    bf_row = jnp.pad(b_forget[0], (0, LANES - N_HEADS_FOX)).reshape(1, LANES)
    w_out_p = jnp.concatenate([w_out[0][:SWA_WIDTH][perm], w_out[0][SWA_WIDTH:]], axis=0).astype(BF16)
    ga = out_norm_swa_g[0][perm].reshape(1, SWA_WIDTH)
    gb = out_norm_fox_g[0].reshape(1, FOX_WIDTH)
    wr = jnp.pad(jnp.concatenate([w_group[0], w_expert[0]], axis=1),
                 ((0, 0), (0, LANES - N_GROUPS - N_EXPERTS)))
    br = jnp.pad(jnp.concatenate([b_group[0], b_expert[0]]), (0, LANES - N_GROUPS - N_EXPERTS)).reshape(1, LANES)
    wg, wu, wd = w_gate[0], w_up[0], w_down[0]

    mod = _adaln(c, w_ada[0], b_ada[0])
    mod3 = mod.reshape(BATCH, 6, D_MODEL)

    proj, f, vt5 = _inproj(x2, mod3, norm_mix_g[0].reshape(1, D_MODEL), w_row, w_f, colscale, w_vt)
    aug = _forget_bias(f, bf_row)

    o_a = _swa(sinks[0], proj)
    o_b = _fox(proj, aug, vt5)

    x1, h2, ids, gates = _outproj(o_a, o_b, x2, mod3, ga, gb, w_out_p,
                                  norm_moe_g[0].reshape(1, D_MODEL), wr, br)

    eid = ids[:, :TOP_K].reshape(-1)
    dest, row_tok, block_e, nused, next_used = _routing_tables(eid)
    ys = _experts(block_e, row_tok, nused, next_used, h2, wg, wu, wd)
    out = _combine(dest, ys, x1, gates, mod3, final_g.reshape(1, D_MODEL))
    return out.reshape(BATCH, SEQ, D_MODEL)
```

```python
import functools
import math

import numpy as np
import jax
import jax.numpy as jnp
from jax import lax
from jax.experimental import pallas as pl
from jax.experimental.pallas import tpu as pltpu

F32 = jnp.float32
BF16 = jnp.bfloat16

D_MODEL = 2048
BATCH = 2
SEQ = 8192
N_TOK = BATCH * SEQ
HEAD_DIM = 64
N_HEADS_SWA = 16
N_KV_SWA = 4
N_HEADS_FOX = 16
WINDOW = 128
SWA_WIDTH = N_HEADS_SWA * HEAD_DIM
KV_WIDTH = N_KV_SWA * HEAD_DIM
FOX_WIDTH = N_HEADS_FOX * HEAD_DIM
MAIN_COLS = SWA_WIDTH + 2 * KV_WIDTH + 3 * FOX_WIDTH
N_GROUPS = 4
EXPERTS_PER_GROUP = 8
N_EXPERTS = N_GROUPS * EXPERTS_PER_GROUP
TOP_K = 2
D_EXPERT = 512
MOE_BLOCK = 256
MOE_SUB = 128
EPS = 1e-6
LOG2E = math.log2(math.e)

LANES = 128
VMEM_LIMIT = 56 * 1024 * 1024

QA_BLK = 0
KA_COL = SWA_WIDTH
VA_COL = SWA_WIDTH + KV_WIDTH
QB_BLK = (SWA_WIDTH + 2 * KV_WIDTH) // LANES
KB_BLK = QB_BLK + FOX_WIDTH // LANES
VB_COL = SWA_WIDTH + 2 * KV_WIDTH + 2 * FOX_WIDTH
ROW_COLS = VB_COL


def _cparams(sem, vmem=VMEM_LIMIT):
    return pltpu.CompilerParams(dimension_semantics=sem, vmem_limit_bytes=vmem)


ADA_TN = 1024


def _adaln_kernel(cb_ref, w_ref, b_ref, o_ref):
    for b in range(BATCH):
        s = jax.nn.silu(cb_ref[b])
        cols = []
        for j in range(ADA_TN // LANES):
            prod = w_ref[:, j * LANES:(j + 1) * LANES] * s
            cols.append(jnp.sum(prod, axis=0, keepdims=True))
        o_ref[b:b + 1, :] = jnp.concatenate(cols, axis=1) + b_ref[...]


def _adaln(c, w_ada, b_ada):
    ncol = w_ada.shape[1]
    cb = jnp.broadcast_to(c[:, :, None], (BATCH, D_MODEL, LANES))
    return pl.pallas_call(
        _adaln_kernel,
        grid=(ncol // ADA_TN,),
        in_specs=[
            pl.BlockSpec((BATCH, D_MODEL, LANES), lambda j: (0, 0, 0)),
            pl.BlockSpec((D_MODEL, ADA_TN), lambda j: (0, j)),
            pl.BlockSpec((1, ADA_TN), lambda j: (0, j)),
        ],
        out_specs=pl.BlockSpec((BATCH, ADA_TN), lambda j: (0, j)),
        out_shape=jax.ShapeDtypeStruct((BATCH, ncol), F32),
        compiler_params=_cparams(("arbitrary",)),
        name="adaln",
    )(cb, w_ada, b_ada.reshape(1, ncol))


def _rms(x):
    return x * lax.rsqrt(jnp.mean(x * x, axis=-1, keepdims=True) + EPS)


IN_TM = 512
IN_TN = 1792
IN_NJ = ROW_COLS // IN_TN


def _inproj_kernel(x_ref, mod_ref, g_ref, w_ref, wf_ref, cs_ref, wvt_ref, o_ref, f_ref, vt_ref, h_ref):
    j = pl.program_id(1)

    @pl.when(j == 0)
    def _():
        y = _rms(x_ref[...])
        h = (y * g_ref[...]) * (1.0 + mod_ref[0, 1:2, :]) + mod_ref[0, 0:1, :]
        hb = h.astype(BF16)
        h_ref[...] = hb
        f_ref[...] = jnp.dot(hb, wf_ref[...], preferred_element_type=F32)

    @pl.when(j < IN_NJ)
    def _():
        acc = jnp.dot(h_ref[...], w_ref[...], preferred_element_type=F32)
        o_ref[...] = (acc * cs_ref[...]).astype(BF16)

    @pl.when(j == IN_NJ)
    def _():
        vt = lax.dot_general(wvt_ref[...], h_ref[...], (((0,), (1,)), ((), ())),
                             preferred_element_type=F32)
        vt_ref[0, :, 0] = vt.reshape(N_HEADS_FOX // 2, LANES, IN_TM).astype(BF16)


def _inproj(x2, mod3, g, w_row, w_f, colscale, w_vt):
    tiles_per_batch = SEQ // IN_TM
    last = IN_NJ - 1
    return pl.pallas_call(
        _inproj_kernel,
        grid=(N_TOK // IN_TM, IN_NJ + 1),
        in_specs=[
            pl.BlockSpec((IN_TM, D_MODEL), lambda i, j: (i, 0)),
            pl.BlockSpec((1, 6, D_MODEL), lambda i, j: (i // tiles_per_batch, 0, 0)),
            pl.BlockSpec((1, D_MODEL), lambda i, j: (0, 0)),
            pl.BlockSpec((D_MODEL, IN_TN), lambda i, j: (0, jnp.minimum(j, last))),
            pl.BlockSpec((D_MODEL, LANES), lambda i, j: (0, 0)),
            pl.BlockSpec((1, IN_TN), lambda i, j: (0, jnp.minimum(j, last))),
            pl.BlockSpec((D_MODEL, FOX_WIDTH), lambda i, j: (0, 0)),
        ],
        out_specs=[
            pl.BlockSpec((IN_TM, IN_TN), lambda i, j: (i, jnp.minimum(j, last))),
            pl.BlockSpec((IN_TM, LANES), lambda i, j: (i, 0)),
            pl.BlockSpec((1, N_HEADS_FOX // 2, 1, LANES, IN_TM),
                         lambda i, j: (i // tiles_per_batch, 0, i % tiles_per_batch, 0, 0)),
        ],
        out_shape=[
            jax.ShapeDtypeStruct((N_TOK, ROW_COLS), BF16),
            jax.ShapeDtypeStruct((N_TOK, LANES), F32),
            jax.ShapeDtypeStruct((BATCH, N_HEADS_FOX // 2, SEQ // IN_TM, LANES, IN_TM), BF16),
        ],
        scratch_shapes=[pltpu.VMEM((IN_TM, D_MODEL), BF16)],
        compiler_params=_cparams(("arbitrary", "arbitrary")),
        name="inproj",
    )(x2, mod3, g, w_row, w_f, colscale, w_vt)


CUM_TS = 512
FOX_AUG = 3


def _cum_kernel(f_ref, bf_ref, tri_ref, sel_ref, aug_ref, carry_ref):
    @pl.when(pl.program_id(1) == 0)
    def _():
        carry_ref[...] = jnp.zeros_like(carry_ref)

    lf = jax.nn.log_sigmoid(f_ref[...] + bf_ref[...])
    l_hi = lf.astype(BF16)
    l_r = lf - l_hi.astype(F32)
    l_mid = l_r.astype(BF16)
    l_lo = (l_r - l_mid.astype(F32)).astype(BF16)
    c3 = jnp.dot(tri_ref[...], jnp.concatenate([l_hi, l_mid, l_lo], axis=1), preferred_element_type=F32)
    cs = (c3[:, :LANES] + (c3[:, LANES:2 * LANES] + c3[:, 2 * LANES:])) + carry_ref[...]
    carry_ref[...] = cs[CUM_TS - 1:CUM_TS, :]
    bias = cs * (-LOG2E)
    hi = bias.astype(BF16)
    r1 = bias - hi.astype(F32)
    mid = r1.astype(BF16)
    lo = (r1 - mid.astype(F32)).astype(BF16)
    pieces = jnp.concatenate([hi, mid, lo], axis=1)
    aug_ref[...] = jnp.dot(pieces, sel_ref[...], preferred_element_type=F32).astype(BF16)


def _aug_selector():
    sel = np.zeros((FOX_AUG * LANES, FOX_WIDTH), np.float32)
    for head in range(N_HEADS_FOX):
        for piece in range(FOX_AUG):
            sel[piece * LANES + head, (head // 2) * LANES + FOX_AUG * (head % 2) + piece] = 1.0
    return sel


def _forget_bias(f, b_forget_row):
    tri = jnp.tril(jnp.ones((CUM_TS, CUM_TS), BF16))
    sel = jnp.asarray(_aug_selector(), BF16)
    nblk = SEQ // CUM_TS
    return pl.pallas_call(
        _cum_kernel,
        grid=(BATCH, nblk),
        in_specs=[
            pl.BlockSpec((CUM_TS, LANES), lambda b, i: (b * nblk + i, 0)),
            pl.BlockSpec((1, LANES), lambda b, i: (0, 0)),
            pl.BlockSpec((CUM_TS, CUM_TS), lambda b, i: (0, 0)),
            pl.BlockSpec((FOX_AUG * LANES, FOX_WIDTH), lambda b, i: (0, 0)),
        ],
        out_specs=pl.BlockSpec((CUM_TS, FOX_WIDTH), lambda b, i: (b * nblk + i, 0)),
        out_shape=jax.ShapeDtypeStruct((N_TOK, FOX_WIDTH), BF16),
        scratch_shapes=[pltpu.VMEM((1, LANES), F32)],
        compiler_params=_cparams(("arbitrary", "arbitrary")),
        name="forget_bias",
    )(f, b_forget_row, tri, sel)


SWA_TQ = WINDOW


def _swa_head_order(w, axis):
    w = jnp.moveaxis(w, axis, 0)
    rest = w.shape[1:]
    w = w.reshape((2, 2, 4, HEAD_DIM) + rest).swapaxes(1, 2).reshape((SWA_WIDTH,) + rest)
    return jnp.moveaxis(w, 0, axis)


def _swa_kernel(sinks_ref, q_ref, kc_ref, kp_ref, vc_ref, vp_ref, o_ref, bias_ref):
    i = pl.program_id(1)
    tq = SWA_TQ

    @pl.when((pl.program_id(0) == 0) & (i == 0))
    def _():
        row = lax.broadcasted_iota(jnp.int32, (tq, 2 * tq), 0)
        col = lax.broadcasted_iota(jnp.int32, (tq, 2 * tq), 1)
        dist = row + tq - col
        band = (dist >= 0) & (dist < WINDOW)
        distf = dist.astype(F32)
        for head in range(N_HEADS_SWA):
            slope2 = float(2.0 ** (-8.0 * (head + 1) / N_HEADS_SWA)) * LOG2E
            base = jnp.where(band, -slope2 * distf, -jnp.inf)
            bias_ref[1, head] = base
            bias_ref[0, head] = jnp.where(col >= tq, base, -jnp.inf)

    table = jnp.minimum(i, 1)
    lane = lax.broadcasted_iota(jnp.int32, (tq, LANES), 1)
    lo_half = lane < HEAD_DIM
    for t in range(2):
        kt = jnp.concatenate([kp_ref[:, t * LANES:(t + 1) * LANES],
                              kc_ref[:, t * LANES:(t + 1) * LANES]], axis=0)
        vt = jnp.concatenate([vp_ref[:, t * LANES:(t + 1) * LANES],
                              vc_ref[:, t * LANES:(t + 1) * LANES]], axis=0)
        heads = [(g, e) for g in range(4) for e in range(2)]
        scores = []
        for g, e in heads:
            tile = t * 4 + g
            qt = q_ref[:, tile * LANES:(tile + 1) * LANES]
            qm = jnp.where(lo_half if e == 0 else ~lo_half, qt, jnp.zeros_like(qt))
            scores.append(lax.dot_general(qm, kt, (((1,), (1,)), ((), ())),
                                          preferred_element_type=F32))
        probs, rdens = [], []
        for (g, e), s in zip(heads, scores):
            head = (2 * t + e) * 4 + g
            sink = sinks_ref[head] * LOG2E
            s = s + bias_ref[table, head]
            m = jnp.maximum(jnp.max(s, axis=-1, keepdims=True), sink)
            p = jnp.exp2(s - m)
            rdens.append(1.0 / (jnp.sum(p, axis=-1, keepdims=True) + jnp.exp2(sink - m)))
            probs.append(p.astype(BF16))
        outs = [jnp.dot(p, vt, preferred_element_type=F32) * r for p, r in zip(probs, rdens)]
        for g in range(4):
            tile = t * 4 + g
            o_ref[:, tile * LANES:(tile + 1) * LANES] = jnp.where(lo_half, outs[2 * g], outs[2 * g + 1]).astype(BF16)


def _swa(sinks, proj):
    nq = SEQ // SWA_TQ
    kblk = KA_COL // KV_WIDTH
    vblk = VA_COL // KV_WIDTH
    grid_spec = pltpu.PrefetchScalarGridSpec(
        num_scalar_prefetch=1,
        grid=(BATCH, nq),
        in_specs=[
            pl.BlockSpec((SWA_TQ, SWA_WIDTH), lambda b, i, s: (b * nq + i, 0)),
            pl.BlockSpec((SWA_TQ, KV_WIDTH), lambda b, i, s: (b * nq + i, kblk)),
            pl.BlockSpec((SWA_TQ, KV_WIDTH), lambda b, i, s: (b * nq + jnp.maximum(i - 1, 0), kblk)),
            pl.BlockSpec((SWA_TQ, KV_WIDTH), lambda b, i, s: (b * nq + i, vblk)),
            pl.BlockSpec((SWA_TQ, KV_WIDTH), lambda b, i, s: (b * nq + jnp.maximum(i - 1, 0), vblk)),
        ],
        out_specs=pl.BlockSpec((SWA_TQ, SWA_WIDTH), lambda b, i, s: (b * nq + i, 0)),
        scratch_shapes=[pltpu.VMEM((2, N_HEADS_SWA, SWA_TQ, 2 * SWA_TQ), F32)],
    )
    return pl.pallas_call(
        _swa_kernel,
        grid_spec=grid_spec,
        out_shape=jax.ShapeDtypeStruct((N_TOK, SWA_WIDTH), BF16),
        compiler_params=_cparams(("arbitrary", "arbitrary")),
        name="swa_attn",
    )(sinks, proj, proj, proj, proj, proj)


FOX_T = 512
FOX_ONES = 16
FOX_VROWS = HEAD_DIM + FOX_ONES


def _fox_kernel(q_ref, k_ref, aug_ref, vt_ref, o_ref, mask_ref, t_ref, acc_ref):
    qi = pl.program_id(2)
    t = FOX_T

    @pl.when(qi == 0)
    def _():
        kr = lax.broadcasted_iota(jnp.int32, (t, t), 0)
        qc = lax.broadcasted_iota(jnp.int32, (t, t), 1)
        mask_ref[...] = jnp.where(kr <= qc, 0.0, -jnp.inf)

    qf = q_ref[...].astype(F32).T
    drow = lax.broadcasted_iota(jnp.int32, (LANES, t), 0)
    qaug = []
    for hh in range(2):
        qh = jnp.where((drow >= HEAD_DIM * hh) & (drow < HEAD_DIM * (hh + 1)), qf, 0.0)
        ones = jnp.where((drow >= FOX_AUG * hh) & (drow < FOX_AUG * (hh + 1)), 1.0, 0.0)
        qaug.append(jnp.concatenate([qh, ones], axis=0).astype(BF16))
    acc_ref[...] = jnp.zeros_like(acc_ref)

    def stage_a(j, slot, diag):
        r0 = pl.multiple_of(j * t, t)
        kb = jnp.concatenate([k_ref[pl.ds(r0, t), :], aug_ref[pl.ds(r0, t), :]], axis=1)
        sts = [jnp.dot(kb, qaug[hh], preferred_element_type=F32) for hh in range(2)]
        mbs = []
        for hh in range(2):
            tt = sts[hh]
            if diag:
                tt = tt + mask_ref[...]
            t_ref[slot, hh] = tt
            mbs.append(jnp.max(tt, axis=0, keepdims=True))
        return tuple(mbs)

    def stage_b(jv, slot, ms, mbs):
        new, ps, alphas = [], [], []
        for hh in range(2):
            m_new = jnp.maximum(ms[hh], mbs[hh])
            alphas.append(jnp.exp2(ms[hh] - m_new))
            ps.append(jnp.exp2(t_ref[slot, hh] - m_new).astype(BF16))
            new.append(m_new)
        ones = jnp.ones((FOX_ONES, t), BF16)
        pvs = [jnp.dot(jnp.concatenate([vt_ref[0, 0, jv, hh * HEAD_DIM:(hh + 1) * HEAD_DIM, :], ones], axis=0),
                       ps[hh], preferred_element_type=F32)
               for hh in range(2)]
        for hh in range(2):
            acc_ref[hh] = alphas[hh] * acc_ref[hh] + pvs[hh]
        return tuple(new)

    ml0 = tuple(jnp.full((1, t), -jnp.inf, F32) for _ in range(2))
    mb0 = stage_a(qi, 0, True)

    def pair(j0, c):
        ml, mbs, jprev = c
        mb1 = stage_a(j0, 1, False)
        ml = stage_b(jprev, 0, ml, mbs)
        mb2 = stage_a(j0 + 1, 0, False)
        ml = stage_b(j0, 1, ml, mb1)
        return ml, mb2, j0 + 1

    def quad(ii, c):
        return pair(4 * ii + 2, pair(4 * ii, c))

    carry = lax.fori_loop(0, qi // 4, quad, (ml0, mb0, qi))
    base = 4 * (qi // 4)

    def tail(rem):
        def run(c):
            ml, mb_prev, prev = c
            for k in range(rem):
                mb_new = stage_a(base + k, (k + 1) % 2, False)
                ml = stage_b(prev, k % 2, ml, mb_prev)
                prev, mb_prev = base + k, mb_new
            stage_b(prev, rem % 2, ml, mb_prev)
            ot = jnp.concatenate([acc_ref[hh, 0:HEAD_DIM, :] / acc_ref[hh, HEAD_DIM:HEAD_DIM + 1, :]
                                  for hh in range(2)], axis=0)
            o_ref[...] = ot.T.astype(BF16)
            return jnp.int32(0)
        return run

    lax.switch(qi % 4, [tail(rem) for rem in range(4)], carry)


def _fox(proj, aug, vt5):
    nq = SEQ // FOX_T
    npair = N_HEADS_FOX // 2
    return pl.pallas_call(
        _fox_kernel,
        grid=(BATCH, npair, nq),
        in_specs=[
            pl.BlockSpec((FOX_T, LANES), lambda b, h, i: (b * nq + i, QB_BLK + h)),
            pl.BlockSpec((SEQ, LANES), lambda b, h, i: (b, KB_BLK + h)),
            pl.BlockSpec((SEQ, LANES), lambda b, h, i: (b, h)),
            pl.BlockSpec((1, 1, nq, LANES, FOX_T), lambda b, h, i: (b, h, 0, 0, 0)),
        ],
        out_specs=pl.BlockSpec((FOX_T, LANES), lambda b, h, i: (b * nq + i, h)),
        out_shape=jax.ShapeDtypeStruct((N_TOK, FOX_WIDTH), BF16),
        scratch_shapes=[
            pltpu.VMEM((FOX_T, FOX_T), F32),
            pltpu.VMEM((2, 2, FOX_T, FOX_T), F32),
            pltpu.VMEM((2, FOX_VROWS, FOX_T), F32),
        ],
        compiler_params=_cparams(("arbitrary", "arbitrary", "arbitrary")),
        name="fox_attn",
    )(proj, proj, aug, vt5)


OUT_TM = 512
OUT_SUB = 256
SUBLANES = 8
PACKED_COLS = D_MODEL // 2
PACK_ROWS = PACKED_COLS // LANES
DMA_UNROLL = 8


def _outproj_kernel(oa_ref, ob_ref, x_ref, mod_ref, ga_ref, gb_ref, w_ref, gm_ref, wr2_ref, br_ref,
                    x1_ref, h2p_ref, ids_ref, gates_ref):
    subs = [slice(k * OUT_SUB, (k + 1) * OUT_SUB) for k in range(OUT_TM // OUT_SUB)]
    mixed = [jnp.concatenate([_rms(oa_ref[r, :].astype(F32)) * ga_ref[...],
                              _rms(ob_ref[r, :].astype(F32)) * gb_ref[...]], axis=1).astype(BF16) for r in subs]
    ys = [jnp.dot(m, w_ref[...], preferred_element_type=F32) for m in mixed]
    h2s = []
    for r, y in zip(subs, ys):
        x1 = x_ref[r, :] + mod_ref[0, 2:3, :] * y
        x1_ref[r, :] = x1
        h2s.append((_rms(x1) * gm_ref[...]) * (1.0 + mod_ref[0, 4:5, :]) + mod_ref[0, 3:4, :])
    hbs = [h2.astype(BF16) for h2 in h2s]
    for r, h2 in zip(subs, h2s):
        packed = _pack_bf16_pairs(h2)
        for s in range(PACK_ROWS):
            h2p_ref[r, s, :] = packed[:, s * LANES:(s + 1) * LANES]

    hls = [(h2 - hb.astype(F32)).astype(BF16) for h2, hb in zip(h2s, hbs)]
    r2s = [jnp.dot(hb, wr2_ref[...], preferred_element_type=F32) for hb in hbs]
    r3s = [jnp.dot(hl, wr2_ref[:, :LANES], preferred_element_type=F32) for hl in hls]
    for r, r2, r3 in zip(subs, r2s, r3s):
        logits = (r2[:, :LANES] + (r2[:, LANES:] + r3)) + br_ref[...]
        ids, gates = _route(logits)
        ids_ref[r, :] = ids
        gates_ref[r, :] = gates


def _route(logits):
    lane = lax.broadcasted_iota(jnp.int32, logits.shape, 1)
    neg = -jnp.inf
    is_g = lane < N_GROUPS
    gl = jnp.where(is_g, logits, neg)
    gmax = jnp.max(gl, axis=1, keepdims=True)
    gsel = jnp.min(jnp.where(gl == gmax, lane, LANES), axis=1, keepdims=True)
    gsum = jnp.sum(jnp.where(is_g, jnp.exp(gl - gmax), 0.0), axis=1, keepdims=True)
    g_val = 1.0 / gsum
    elane = lane - N_GROUPS
    in_sel = (elane >= 0) & (elane < N_EXPERTS) & ((elane >> 3) == gsel)
    ev = jnp.where(in_sel, logits, neg)
    t1 = jnp.max(ev, axis=1, keepdims=True)
    i1 = jnp.min(jnp.where(ev == t1, lane, LANES), axis=1, keepdims=True)
    ev2 = jnp.where(lane == i1, neg, ev)
    t2 = jnp.max(ev2, axis=1, keepdims=True)
    i2 = jnp.min(jnp.where(ev2 == t2, lane, LANES), axis=1, keepdims=True)
    e2 = jnp.exp(t2 - t1)
    den = 1.0 + e2
    w1 = (1.0 / den) * g_val
    w2 = (e2 / den) * g_val
    return (jnp.where(lane == 0, i1 - N_GROUPS, jnp.where(lane == 1, i2 - N_GROUPS, 0)),
            jnp.where(lane == 0, w1, jnp.where(lane == 1, w2, 0.0)))


def _outproj(oa, ob, x2, mod3, ga, gb, w_out, gm, wr, br):
    tiles_per_batch = SEQ // OUT_TM
    row = lambda i: (i, 0)
    const = lambda i: (0, 0)
    wr_hi = wr.astype(BF16)
    wr_lo = (wr - wr_hi.astype(F32)).astype(BF16)
    wr2 = jnp.concatenate([wr_hi, wr_lo], axis=1)
    return pl.pallas_call(
        _outproj_kernel,
        grid=(N_TOK // OUT_TM,),
        in_specs=[
            pl.BlockSpec((OUT_TM, SWA_WIDTH), row),
            pl.BlockSpec((OUT_TM, FOX_WIDTH), row),
            pl.BlockSpec((OUT_TM, D_MODEL), row),
            pl.BlockSpec((1, 6, D_MODEL), lambda i: (i // tiles_per_batch, 0, 0)),
            pl.BlockSpec((1, SWA_WIDTH), const),
            pl.BlockSpec((1, FOX_WIDTH), const),
            pl.BlockSpec((D_MODEL, D_MODEL), const),
            pl.BlockSpec((1, D_MODEL), const),
            pl.BlockSpec((D_MODEL, 2 * LANES), const),
            pl.BlockSpec((1, LANES), const),
        ],
        out_specs=[
            pl.BlockSpec((OUT_TM, D_MODEL), row),
            pl.BlockSpec((OUT_TM, PACK_ROWS, LANES), lambda i: (i, 0, 0)),
            pl.BlockSpec((OUT_TM, LANES), row),
            pl.BlockSpec((OUT_TM, LANES), row),
        ],
        out_shape=[
            jax.ShapeDtypeStruct((N_TOK, D_MODEL), F32),
            jax.ShapeDtypeStruct((N_TOK, PACK_ROWS, LANES), jnp.uint32),
            jax.ShapeDtypeStruct((N_TOK, LANES), jnp.int32),
            jax.ShapeDtypeStruct((N_TOK, LANES), F32),
        ],
        compiler_params=_cparams(("arbitrary",)),
        name="outproj_router",
    )(oa, ob, x2, mod3, ga, gb, w_out, gm, wr2, br)


N_BLOCKS = (N_TOK * TOP_K) // MOE_BLOCK + N_EXPERTS
P_ROWS = N_BLOCKS * MOE_BLOCK


def _pack_bf16_pairs(x):
    xb = x.astype(BF16).astype(F32)
    half = x.shape[1] // 2
    lo = lax.shift_right_logical(pltpu.bitcast(xb[:, :half], jnp.uint32), jnp.uint32(16))
    hi = pltpu.bitcast(xb[:, half:], jnp.uint32) & jnp.uint32(0xFFFF0000)
    return lo | hi


def _unpack_bf16_pairs(w):
    return (pltpu.bitcast(lax.shift_left(w, jnp.uint32(16)), F32),
            pltpu.bitcast(w & jnp.uint32(0xFFFF0000), F32))


def _expert_kernel(be_ref, tok_ref, src_ref, nused_ref, nxt_ref, h2_hbm, wg_hbm, wu_hbm, wd_hbm, y_hbm, xbuf, sem,
                   wgf, wuf, wdf, wsem, wgb, wub, wdb, ystage, osem):
    i = pl.program_id(0)
    nused = nused_ref[0]

    def weight_copies(e):
        return (pltpu.make_async_copy(wg_hbm.at[e], wgf, wsem.at[0]),
                pltpu.make_async_copy(wu_hbm.at[e], wuf, wsem.at[1]),
                pltpu.make_async_copy(wd_hbm.at[e], wdf, wsem.at[2]))

    def issue(blk, slot):
        def body(g, _):
            for q in range(SUBLANES):
                tok = tok_ref[jnp.minimum(src_ref[blk] + g * SUBLANES + q, N_TOK * TOP_K - 1)]
                pltpu.make_async_copy(h2_hbm.at[tok], xbuf.at[slot, g, :, q, :], sem.at[slot]).start(
                    priority=q % 2)
            return 0
        lax.fori_loop(0, MOE_BLOCK // SUBLANES, body, 0)

    def out_copies(blk, slot):
        r0 = pl.multiple_of(blk * MOE_BLOCK, MOE_BLOCK)
        return [pltpu.make_async_copy(ystage.at[slot, :, pl.ds(s * LANES, LANES)],
                                      y_hbm.at[pl.ds(r0, MOE_BLOCK), s, :], osem.at[slot])
                for s in range(PACK_ROWS)]

    def out_wait(slot):
        pltpu.make_async_copy(ystage.at[slot], ystage.at[slot], osem.at[slot]).wait()

    def gather_wait(slot):
        pltpu.make_async_copy(xbuf.at[slot], xbuf.at[slot], sem.at[slot]).wait()

    @pl.when(i == 0)
    def _():
        issue(0, 0)

    @pl.when(i + 1 < nused)
    def _():
        issue(i + 1, (i + 1) % 2)

    oslot = i % 2

    @pl.when(i >= 2)
    def _():
        out_wait(oslot)

    e = be_ref[i]
    new_expert = ((i == 0) | (e != be_ref[jnp.maximum(i - 1, 0)])) & (i < nused)

    @pl.when(i == 0)
    def _():
        for cp in weight_copies(e):
            cp.start()

    @pl.when(new_expert)
    def _():
        for cp in weight_copies(e):
            cp.wait()
        wgb[...] = wgf[...].astype(BF16)
        wub[...] = wuf[...].astype(BF16)
        wdb[...] = wdf[...].astype(BF16)

    @pl.when(new_expert & (nxt_ref[e] >= 0))
    def _():
        for cp in weight_copies(nxt_ref[e]):
            cp.start()

    @pl.when(i < nused)
    def _():
        slot = i % 2
        gather_wait(slot)
        gsub = MOE_SUB // SUBLANES
        xbs = []
        for k in range(MOE_BLOCK // MOE_SUB):
            xu = jnp.concatenate([xbuf[slot, k * gsub:(k + 1) * gsub, s].reshape(MOE_SUB, LANES)
                                  for s in range(PACK_ROWS)], axis=1)
            xbs.append(jnp.concatenate(_unpack_bf16_pairs(xu), axis=1).astype(BF16))
        gs = [jnp.dot(xb, wgb[...], preferred_element_type=F32) for xb in xbs]
        us = [jnp.dot(xb, wub[...], preferred_element_type=F32) for xb in xbs]
        hids = [(jax.nn.silu(g) * u).astype(BF16) for g, u in zip(gs, us)]
        ys = [jnp.dot(hid, wdb[...], preferred_element_type=F32) for hid in hids]
        for k, y in enumerate(ys):
            ystage[oslot, k * MOE_SUB:(k + 1) * MOE_SUB, :] = _pack_bf16_pairs(y)

    @pl.when(i >= nused)
    def _():
        ystage[oslot] = jnp.zeros((MOE_BLOCK, PACKED_COLS), jnp.uint32)

    for cp in out_copies(i, oslot):
        cp.start()

    @pl.when(i == pl.num_programs(0) - 1)
    def _():
        out_wait(oslot)
        out_wait(1 - oslot)


def _experts(block_e, tok_sorted, blk_src, nused, next_used, h2p, wg, wu, wd):
    grid_spec = pltpu.PrefetchScalarGridSpec(
        num_scalar_prefetch=5,
        grid=(N_BLOCKS,),
        in_specs=[pl.BlockSpec(memory_space=pl.ANY)] * 4,
        out_specs=pl.BlockSpec(memory_space=pl.ANY),
        scratch_shapes=[pltpu.VMEM((2, MOE_BLOCK // SUBLANES, PACK_ROWS, SUBLANES, LANES), jnp.uint32),
                        pltpu.SemaphoreType.DMA((2,)),
                        pltpu.VMEM((D_MODEL, D_EXPERT), F32),
                        pltpu.VMEM((D_MODEL, D_EXPERT), F32),
                        pltpu.VMEM((D_EXPERT, D_MODEL), F32),
                        pltpu.SemaphoreType.DMA((3,)),
                        pltpu.VMEM((D_MODEL, D_EXPERT), BF16),
                        pltpu.VMEM((D_MODEL, D_EXPERT), BF16),
                        pltpu.VMEM((D_EXPERT, D_MODEL), BF16),
                        pltpu.VMEM((2, MOE_BLOCK, PACKED_COLS), jnp.uint32),
                        pltpu.SemaphoreType.DMA((2,))],
    )
    return pl.pallas_call(
        _expert_kernel,
        grid_spec=grid_spec,
        out_shape=jax.ShapeDtypeStruct((P_ROWS, PACK_ROWS, LANES), jnp.uint32),
        compiler_params=_cparams(("arbitrary",)),
        name="expert_ffn",
    )(block_e, tok_sorted, blk_src, nused, next_used, h2p, wg, wu, wd)


CMB_TM = 256


def _combine_kernel(dest_ref, ys_hbm, x1_ref, gates_ref, mod_ref, fg_ref, o_ref, ybuf, sem):
    i = pl.program_id(0)
    n = pl.num_programs(0)
    groups = CMB_TM // SUBLANES

    def issue(tile, slot):
        def body(g, _):
            for q in range(SUBLANES):
                for k in range(TOP_K):
                    d = dest_ref[(tile * CMB_TM + g * SUBLANES + q) * TOP_K + k]
                    pltpu.make_async_copy(ys_hbm.at[d], ybuf.at[slot, k * groups + g, :, q, :],
                                          sem.at[slot]).start(priority=k)
            return 0
        lax.fori_loop(0, groups, body, 0)

    @pl.when(i == 0)
    def _():
        issue(0, 0)

    @pl.when(i + 1 < n)
    def _():
        issue(i + 1, (i + 1) % 2)

    slot = i % 2
    pltpu.make_async_copy(ybuf.at[slot], ybuf.at[slot], sem.at[slot]).wait()
    gts = gates_ref[...]
    w0 = gts[:, 0:1]
    w1 = gts[:, 1:2]
    half = D_MODEL // 2
    ssq = jnp.zeros((CMB_TM, 1), F32)
    for s in range(PACK_ROWS):
        y0 = _unpack_bf16_pairs(ybuf[slot, 0:groups, s].reshape(CMB_TM, LANES))
        y1 = _unpack_bf16_pairs(ybuf[slot, groups:2 * groups, s].reshape(CMB_TM, LANES))
        for part in range(2):
            cols = slice(part * half + s * LANES, part * half + (s + 1) * LANES)
            x2 = x1_ref[:, cols] + mod_ref[0, 5:6, cols] * (y0[part] * w0 + y1[part] * w1)
            ssq = ssq + jnp.sum(x2 * x2, axis=-1, keepdims=True)
            o_ref[:, cols] = x2
    o_ref[...] = (o_ref[...] * lax.rsqrt(ssq * (1.0 / D_MODEL) + EPS)) * fg_ref[...]


def _combine(dest, ys, x1, gates, mod3, final_g):
    tiles_per_batch = SEQ // CMB_TM
    grid_spec = pltpu.PrefetchScalarGridSpec(
        num_scalar_prefetch=1,
        grid=(N_TOK // CMB_TM,),
        in_specs=[
            pl.BlockSpec(memory_space=pl.ANY),
            pl.BlockSpec((CMB_TM, D_MODEL), lambda i, d: (i, 0)),
            pl.BlockSpec((CMB_TM, LANES), lambda i, d: (i, 0)),
            pl.BlockSpec((1, 6, D_MODEL), lambda i, d: (i // tiles_per_batch, 0, 0)),
            pl.BlockSpec((1, D_MODEL), lambda i, d: (0, 0)),
        ],
        out_specs=pl.BlockSpec((CMB_TM, D_MODEL), lambda i, d: (i, 0)),
        scratch_shapes=[pltpu.VMEM((2, TOP_K * CMB_TM // SUBLANES, PACK_ROWS, SUBLANES, LANES), jnp.uint32),
                        pltpu.SemaphoreType.DMA((2,))],
    )
    return pl.pallas_call(
        _combine_kernel,
        grid_spec=grid_spec,
        out_shape=jax.ShapeDtypeStruct((N_TOK, D_MODEL), F32),
        compiler_params=_cparams(("arbitrary",)),
        name="combine_final",
    )(dest, ys, x1, gates, mod3, final_g)


def _routing_tables(eid):
    ids = jnp.arange(N_EXPERTS, dtype=jnp.int32)
    onehot = (eid[:, None] == ids[None, :]).astype(jnp.int32)
    csum = jnp.cumsum(onehot, axis=0)
    counts = csum[-1]
    padded = (counts + MOE_BLOCK - 1) // MOE_BLOCK * MOE_BLOCK
    pend = jnp.cumsum(padded)
    pstart = pend - padded
    dest = (jnp.sum(onehot * (csum + pstart[None, :]), axis=1) - 1).astype(jnp.int32)
    blk_row = jnp.arange(N_BLOCKS, dtype=jnp.int32) * MOE_BLOCK
    block_e = jnp.minimum(jnp.sum((pend[None, :] <= blk_row[:, None]).astype(jnp.int32), axis=1),
                          N_EXPERTS - 1).astype(jnp.int32)
    tok_sorted = (jnp.argsort(eid, stable=True) // TOP_K).astype(jnp.int32)
    starts = jnp.cumsum(counts) - counts
    is_e = (block_e[:, None] == ids[None, :]).astype(jnp.int32)
    blk_src = (blk_row + jnp.sum(is_e * (starts - pstart)[None, :], axis=1)).astype(jnp.int32)
    nused = (pend[-1] // MOE_BLOCK).astype(jnp.int32).reshape(1)
    later_used = (ids[None, :] > ids[:, None]) & (counts[None, :] > 0)
    next_used = jnp.min(jnp.where(later_used, ids[None, :], N_EXPERTS), axis=1)
    next_used = jnp.where(next_used == N_EXPERTS, -1, next_used).astype(jnp.int32)
    return dest, tok_sorted, blk_src, block_e, nused, next_used


def kernel(x, c, w_ada, b_ada, norm_mix_g, w_in, b_forget, sinks, out_norm_swa_g, out_norm_fox_g, w_out,
           norm_moe_g, w_group, b_group, w_expert, b_expert, w_gate, w_up, w_down, final_g):
    assert IN_TM == FOX_T
    x2 = x.reshape(N_TOK, D_MODEL)

    w_in0 = w_in[0]
    w_row = jnp.concatenate([_swa_head_order(w_in0[:, :SWA_WIDTH], 1), w_in0[:, SWA_WIDTH:VB_COL]],
                            axis=1).astype(BF16)
    w_vt = w_in0[:, VB_COL:MAIN_COLS].astype(BF16)
    w_f = jnp.pad(w_in0[:, MAIN_COLS:], ((0, 0), (0, LANES - N_HEADS_FOX))).astype(BF16)
    colscale = jnp.concatenate([
        jnp.full((SWA_WIDTH,), HEAD_DIM ** -0.5 * LOG2E, F32),
        jnp.ones((2 * KV_WIDTH,), F32),
        jnp.full((FOX_WIDTH,), HEAD_DIM ** -0.5 * LOG2E, F32),
        jnp.ones((FOX_WIDTH,), F32)]).reshape(1, ROW_COLS)
    bf_row = jnp.pad(b_forget[0], (0, LANES - N_HEADS_FOX)).reshape(1, LANES)
    w_out_p = jnp.concatenate([_swa_head_order(w_out[0][:SWA_WIDTH], 0), w_out[0][SWA_WIDTH:]],
                              axis=0).astype(BF16)
    ga = _swa_head_order(out_norm_swa_g[0], 0).reshape(1, SWA_WIDTH)
    gb = out_norm_fox_g[0].reshape(1, FOX_WIDTH)
    wr = jnp.pad(jnp.concatenate([w_group[0], w_expert[0]], axis=1),
                 ((0, 0), (0, LANES - N_GROUPS - N_EXPERTS)))
    br = jnp.pad(jnp.concatenate([b_group[0], b_expert[0]]), (0, LANES - N_GROUPS - N_EXPERTS)).reshape(1, LANES)
    wg, wu, wd = w_gate[0], w_up[0], w_down[0]

    mod = _adaln(c, w_ada[0], b_ada[0])
    mod3 = mod.reshape(BATCH, 6, D_MODEL)

    proj, f, vt5 = _inproj(x2, mod3, norm_mix_g[0].reshape(1, D_MODEL), w_row, w_f, colscale, w_vt)
    aug = _forget_bias(f, bf_row)

    o_a = _swa(sinks[0], proj)
    o_b = _fox(proj, aug, vt5)

    x1, h2, ids, gates = _outproj(o_a, o_b, x2, mod3, ga, gb, w_out_p,
                                  norm_moe_g[0].reshape(1, D_MODEL), wr, br)

    eid = ids[:, :TOP_K].reshape(-1)
    dest, tok_sorted, blk_src, block_e, nused, next_used = _routing_tables(eid)
    ys = _experts(block_e, tok_sorted, blk_src, nused, next_used, h2, wg, wu, wd)
    out = _combine(dest, ys, x1, gates, mod3, final_g.reshape(1, D_MODEL))
    return out.reshape(BATCH, SEQ, D_MODEL)
```

```python
import functools
import math

import numpy as np
import jax
import jax.numpy as jnp
from jax import lax
from jax.experimental import pallas as pl
from jax.experimental.pallas import tpu as pltpu

F32 = jnp.float32
BF16 = jnp.bfloat16

D_MODEL = 2048
BATCH = 2
SEQ = 8192
N_TOK = BATCH * SEQ
HEAD_DIM = 64
N_HEADS_SWA = 16
N_KV_SWA = 4
N_HEADS_FOX = 16
WINDOW = 128
SWA_WIDTH = N_HEADS_SWA * HEAD_DIM
KV_WIDTH = N_KV_SWA * HEAD_DIM
FOX_WIDTH = N_HEADS_FOX * HEAD_DIM
MAIN_COLS = SWA_WIDTH + 2 * KV_WIDTH + 3 * FOX_WIDTH
N_GROUPS = 4
EXPERTS_PER_GROUP = 8
N_EXPERTS = N_GROUPS * EXPERTS_PER_GROUP
TOP_K = 2
D_EXPERT = 512
MOE_BLOCK = 256
MOE_SUB = 128
EPS = 1e-6
LOG2E = math.log2(math.e)

LANES = 128
VMEM_LIMIT = 56 * 1024 * 1024

QA_BLK = 0
KA_COL = SWA_WIDTH
VA_COL = SWA_WIDTH + KV_WIDTH
QB_BLK = (SWA_WIDTH + 2 * KV_WIDTH) // LANES
KB_BLK = QB_BLK + FOX_WIDTH // LANES
VB_COL = SWA_WIDTH + 2 * KV_WIDTH + 2 * FOX_WIDTH
ROW_COLS = VB_COL


def _cparams(sem, vmem=VMEM_LIMIT):
    return pltpu.CompilerParams(dimension_semantics=sem, vmem_limit_bytes=vmem)


ADA_TN = 1024


def _adaln_kernel(cb_ref, w_ref, b_ref, o_ref):
    for b in range(BATCH):
        s = jax.nn.silu(cb_ref[b])
        cols = []
        for j in range(ADA_TN // LANES):
            prod = w_ref[:, j * LANES:(j + 1) * LANES] * s
            cols.append(jnp.sum(prod, axis=0, keepdims=True))
        o_ref[b:b + 1, :] = jnp.concatenate(cols, axis=1) + b_ref[...]


def _adaln(c, w_ada, b_ada):
    ncol = w_ada.shape[1]
    cb = jnp.broadcast_to(c[:, :, None], (BATCH, D_MODEL, LANES))
    return pl.pallas_call(
        _adaln_kernel,
        grid=(ncol // ADA_TN,),
        in_specs=[
            pl.BlockSpec((BATCH, D_MODEL, LANES), lambda j: (0, 0, 0)),
            pl.BlockSpec((D_MODEL, ADA_TN), lambda j: (0, j)),
            pl.BlockSpec((1, ADA_TN), lambda j: (0, j)),
        ],
        out_specs=pl.BlockSpec((BATCH, ADA_TN), lambda j: (0, j)),
        out_shape=jax.ShapeDtypeStruct((BATCH, ncol), F32),
        compiler_params=_cparams(("arbitrary",)),
        name="adaln",
    )(cb, w_ada, b_ada.reshape(1, ncol))


def _rms(x):
    return x * lax.rsqrt(jnp.mean(x * x, axis=-1, keepdims=True) + EPS)


IN_TM = 512
IN_TN = 1792
IN_NJ = ROW_COLS // IN_TN


def _inproj_kernel(x_ref, mod_ref, g_ref, w_ref, wf_ref, cs_ref, wvt_ref, o_ref, f_ref, vt_ref, h_ref):
    j = pl.program_id(1)

    @pl.when(j == 0)
    def _():
        y = _rms(x_ref[...])
        h = (y * g_ref[...]) * (1.0 + mod_ref[0, 1:2, :]) + mod_ref[0, 0:1, :]
        hb = h.astype(BF16)
        h_ref[...] = hb
        f_ref[...] = jnp.dot(hb, wf_ref[...], preferred_element_type=F32)

    @pl.when(j < IN_NJ)
    def _():
        acc = jnp.dot(h_ref[...], w_ref[...], preferred_element_type=F32)
        o_ref[...] = (acc * cs_ref[...]).astype(BF16)

    @pl.when(j == IN_NJ)
    def _():
        vt = lax.dot_general(wvt_ref[...], h_ref[...], (((0,), (1,)), ((), ())),
                             preferred_element_type=F32)
        vt_ref[0, :, 0] = vt.reshape(N_HEADS_FOX // 2, LANES, IN_TM).astype(BF16)


def _inproj(x2, mod3, g, w_row, w_f, colscale, w_vt):
    tiles_per_batch = SEQ // IN_TM
    last = IN_NJ - 1
    return pl.pallas_call(
        _inproj_kernel,
        grid=(N_TOK // IN_TM, IN_NJ + 1),
        in_specs=[
            pl.BlockSpec((IN_TM, D_MODEL), lambda i, j: (i, 0)),
            pl.BlockSpec((1, 6, D_MODEL), lambda i, j: (i // tiles_per_batch, 0, 0)),
            pl.BlockSpec((1, D_MODEL), lambda i, j: (0, 0)),
            pl.BlockSpec((D_MODEL, IN_TN), lambda i, j: (0, jnp.minimum(j, last))),
            pl.BlockSpec((D_MODEL, LANES), lambda i, j: (0, 0)),
            pl.BlockSpec((1, IN_TN), lambda i, j: (0, jnp.minimum(j, last))),
            pl.BlockSpec((D_MODEL, FOX_WIDTH), lambda i, j: (0, 0)),
        ],
        out_specs=[
            pl.BlockSpec((IN_TM, IN_TN), lambda i, j: (i, jnp.minimum(j, last))),
            pl.BlockSpec((IN_TM, LANES), lambda i, j: (i, 0)),
            pl.BlockSpec((1, N_HEADS_FOX // 2, 1, LANES, IN_TM),
                         lambda i, j: (i // tiles_per_batch, 0, i % tiles_per_batch, 0, 0)),
        ],
        out_shape=[
            jax.ShapeDtypeStruct((N_TOK, ROW_COLS), BF16),
            jax.ShapeDtypeStruct((N_TOK, LANES), F32),
            jax.ShapeDtypeStruct((BATCH, N_HEADS_FOX // 2, SEQ // IN_TM, LANES, IN_TM), BF16),
        ],
        scratch_shapes=[pltpu.VMEM((IN_TM, D_MODEL), BF16)],
        compiler_params=_cparams(("arbitrary", "arbitrary")),
        name="inproj",
    )(x2, mod3, g, w_row, w_f, colscale, w_vt)


CUM_TS = 512
FOX_AUG = 3


def _cum_kernel(f_ref, bf_ref, tri_ref, sel_ref, aug_ref, carry_ref):
    @pl.when(pl.program_id(1) == 0)
    def _():
        carry_ref[...] = jnp.zeros_like(carry_ref)

    lf = jax.nn.log_sigmoid(f_ref[...] + bf_ref[...])
    l_hi = lf.astype(BF16)
    l_r = lf - l_hi.astype(F32)
    l_mid = l_r.astype(BF16)
    l_lo = (l_r - l_mid.astype(F32)).astype(BF16)
    c3 = jnp.dot(tri_ref[...], jnp.concatenate([l_hi, l_mid, l_lo], axis=1), preferred_element_type=F32)
    cs = (c3[:, :LANES] + (c3[:, LANES:2 * LANES] + c3[:, 2 * LANES:])) + carry_ref[...]
    carry_ref[...] = cs[CUM_TS - 1:CUM_TS, :]
    bias = cs * (-LOG2E)
    hi = bias.astype(BF16)
    r1 = bias - hi.astype(F32)
    mid = r1.astype(BF16)
    lo = (r1 - mid.astype(F32)).astype(BF16)
    pieces = jnp.concatenate([hi, mid, lo], axis=1)
    aug_ref[...] = jnp.dot(pieces, sel_ref[...], preferred_element_type=F32).astype(BF16)


def _aug_selector():
    sel = np.zeros((FOX_AUG * LANES, FOX_WIDTH), np.float32)
    for head in range(N_HEADS_FOX):
        for piece in range(FOX_AUG):
            sel[piece * LANES + head, (head // 2) * LANES + FOX_AUG * (head % 2) + piece] = 1.0
    return sel


def _forget_bias(f, b_forget_row):
    tri = jnp.tril(jnp.ones((CUM_TS, CUM_TS), BF16))
    sel = jnp.asarray(_aug_selector(), BF16)
    nblk = SEQ // CUM_TS
    return pl.pallas_call(
        _cum_kernel,
        grid=(BATCH, nblk),
        in_specs=[
            pl.BlockSpec((CUM_TS, LANES), lambda b, i: (b * nblk + i, 0)),
            pl.BlockSpec((1, LANES), lambda b, i: (0, 0)),
            pl.BlockSpec((CUM_TS, CUM_TS), lambda b, i: (0, 0)),
            pl.BlockSpec((FOX_AUG * LANES, FOX_WIDTH), lambda b, i: (0, 0)),
        ],
        out_specs=pl.BlockSpec((CUM_TS, FOX_WIDTH), lambda b, i: (b * nblk + i, 0)),
        out_shape=jax.ShapeDtypeStruct((N_TOK, FOX_WIDTH), BF16),
        scratch_shapes=[pltpu.VMEM((1, LANES), F32)],
        compiler_params=_cparams(("arbitrary", "arbitrary")),
        name="forget_bias",
    )(f, b_forget_row, tri, sel)


SWA_TQ = WINDOW


def _swa_kernel(sinks_ref, q_ref, kc_ref, kp_ref, vc_ref, vp_ref, o_ref, bias_ref):
    i = pl.program_id(1)
    tq = SWA_TQ

    @pl.when((pl.program_id(0) == 0) & (i == 0))
    def _():
        row = lax.broadcasted_iota(jnp.int32, (tq, 2 * tq), 0)
        col = lax.broadcasted_iota(jnp.int32, (tq, 2 * tq), 1)
        dist = row + tq - col
        band = (dist >= 0) & (dist < WINDOW)
        distf = dist.astype(F32)
        for head in range(N_HEADS_SWA):
            slope2 = float(2.0 ** (-8.0 * (head + 1) / N_HEADS_SWA)) * LOG2E
            base = jnp.where(band, -slope2 * distf, -jnp.inf)
            bias_ref[1, head] = base
            bias_ref[0, head] = jnp.where(col >= tq, base, -jnp.inf)

    table = jnp.minimum(i, 1)
    lane = lax.broadcasted_iota(jnp.int32, (tq, LANES), 1)
    lo_half = lane < HEAD_DIM
    for t in range(2):
        kt = jnp.concatenate([kp_ref[:, t * LANES:(t + 1) * LANES],
                              kc_ref[:, t * LANES:(t + 1) * LANES]], axis=0)
        vt = jnp.concatenate([vp_ref[:, t * LANES:(t + 1) * LANES],
                              vc_ref[:, t * LANES:(t + 1) * LANES]], axis=0)
        heads = [(g, e) for g in range(4) for e in range(2)]
        scores = []
        for g, e in heads:
            tile, f = t * 4 + g, g // 2
            qt = q_ref[:, tile * LANES:(tile + 1) * LANES]
            if e != f:
                qt = jnp.concatenate([qt[:, HEAD_DIM:], qt[:, :HEAD_DIM]], axis=1)
            qm = jnp.where(lo_half if f == 0 else ~lo_half, qt, jnp.zeros_like(qt))
            scores.append(lax.dot_general(qm, kt, (((1,), (1,)), ((), ())),
                                          preferred_element_type=F32))
        probs, rdens = [], []
        for (g, e), s in zip(heads, scores):
            head = 2 * (t * 4 + g) + e
            sink = sinks_ref[head] * LOG2E
            s = s + bias_ref[table, head]
            m = jnp.maximum(jnp.max(s, axis=-1, keepdims=True), sink)
            p = jnp.exp2(s - m)
            rdens.append(1.0 / (jnp.sum(p, axis=-1, keepdims=True) + jnp.exp2(sink - m)))
            probs.append(p.astype(BF16))
        outs = [jnp.dot(p, vt, preferred_element_type=F32) * r for p, r in zip(probs, rdens)]
        for g in range(4):
            tile, f = t * 4 + g, g // 2
            same, other = outs[2 * g + f], pltpu.roll(outs[2 * g + 1 - f], HEAD_DIM, axis=1)
            o_ref[:, tile * LANES:(tile + 1) * LANES] = jnp.where(lo_half == (f == 0), same, other).astype(BF16)


def _swa(sinks, proj):
    nq = SEQ // SWA_TQ
    kblk = KA_COL // KV_WIDTH
    vblk = VA_COL // KV_WIDTH
    grid_spec = pltpu.PrefetchScalarGridSpec(
        num_scalar_prefetch=1,
        grid=(BATCH, nq),
        in_specs=[
            pl.BlockSpec((SWA_TQ, SWA_WIDTH), lambda b, i, s: (b * nq + i, 0)),
            pl.BlockSpec((SWA_TQ, KV_WIDTH), lambda b, i, s: (b * nq + i, kblk)),
            pl.BlockSpec((SWA_TQ, KV_WIDTH), lambda b, i, s: (b * nq + jnp.maximum(i - 1, 0), kblk)),
            pl.BlockSpec((SWA_TQ, KV_WIDTH), lambda b, i, s: (b * nq + i, vblk)),
            pl.BlockSpec((SWA_TQ, KV_WIDTH), lambda b, i, s: (b * nq + jnp.maximum(i - 1, 0), vblk)),
        ],
        out_specs=pl.BlockSpec((SWA_TQ, SWA_WIDTH), lambda b, i, s: (b * nq + i, 0)),
        scratch_shapes=[pltpu.VMEM((2, N_HEADS_SWA, SWA_TQ, 2 * SWA_TQ), F32)],
    )
    return pl.pallas_call(
        _swa_kernel,
        grid_spec=grid_spec,
        out_shape=jax.ShapeDtypeStruct((N_TOK, SWA_WIDTH), BF16),
        compiler_params=_cparams(("arbitrary", "arbitrary")),
        name="swa_attn",
    )(sinks, proj, proj, proj, proj, proj)


FOX_T = 512
FOX_ONES = 16
FOX_VROWS = HEAD_DIM + FOX_ONES


def _fox_kernel(q_ref, k_ref, aug_ref, vt_ref, o_ref, mask_ref, t_ref, acc_ref):
    qi = pl.program_id(2)
    t = FOX_T

    @pl.when(qi == 0)
    def _():
        kr = lax.broadcasted_iota(jnp.int32, (t, t), 0)
        qc = lax.broadcasted_iota(jnp.int32, (t, t), 1)
        mask_ref[...] = jnp.where(kr <= qc, 0.0, -jnp.inf)

    qf = q_ref[...].astype(F32).T
    drow = lax.broadcasted_iota(jnp.int32, (LANES, t), 0)
    qaug = []
    for hh in range(2):
        qh = jnp.where((drow >= HEAD_DIM * hh) & (drow < HEAD_DIM * (hh + 1)), qf, 0.0)
        ones = jnp.where((drow >= FOX_AUG * hh) & (drow < FOX_AUG * (hh + 1)), 1.0, 0.0)
        qaug.append(jnp.concatenate([qh, ones], axis=0).astype(BF16))
    acc_ref[...] = jnp.zeros_like(acc_ref)

    def stage_a(j, slot, diag):
        r0 = pl.multiple_of(j * t, t)
        kb = jnp.concatenate([k_ref[pl.ds(r0, t), :], aug_ref[pl.ds(r0, t), :]], axis=1)
        sts = [jnp.dot(kb, qaug[hh], preferred_element_type=F32) for hh in range(2)]
        mbs = []
        for hh in range(2):
            tt = sts[hh]
            if diag:
                tt = tt + mask_ref[...]
            t_ref[slot, hh] = tt
            mbs.append(jnp.max(tt, axis=0, keepdims=True))
        return tuple(mbs)

    def stage_b(jv, slot, ms, mbs):
        new, ps, alphas = [], [], []
        for hh in range(2):
            m_new = jnp.maximum(ms[hh], mbs[hh])
            alphas.append(jnp.exp2(ms[hh] - m_new))
            ps.append(jnp.exp2(t_ref[slot, hh] - m_new).astype(BF16))
            new.append(m_new)
        ones = jnp.ones((FOX_ONES, t), BF16)
        pvs = [jnp.dot(jnp.concatenate([vt_ref[0, 0, jv, hh * HEAD_DIM:(hh + 1) * HEAD_DIM, :], ones], axis=0),
                       ps[hh], preferred_element_type=F32)
               for hh in range(2)]
        for hh in range(2):
            acc_ref[hh] = alphas[hh] * acc_ref[hh] + pvs[hh]
        return tuple(new)

    ml0 = tuple(jnp.full((1, t), -jnp.inf, F32) for _ in range(2))
    mb0 = stage_a(qi, 0, True)

    def pair(j0, c):
        ml, mbs, jprev = c
        mb1 = stage_a(j0, 1, False)
        ml = stage_b(jprev, 0, ml, mbs)
        mb2 = stage_a(j0 + 1, 0, False)
        ml = stage_b(j0, 1, ml, mb1)
        return ml, mb2, j0 + 1

    def quad(ii, c):
        return pair(4 * ii + 2, pair(4 * ii, c))

    carry = lax.fori_loop(0, qi // 4, quad, (ml0, mb0, qi))
    base = 4 * (qi // 4)

    def tail(rem):
        def run(c):
            ml, mb_prev, prev = c
            for k in range(rem):
                mb_new = stage_a(base + k, (k + 1) % 2, False)
                ml = stage_b(prev, k % 2, ml, mb_prev)
                prev, mb_prev = base + k, mb_new
            stage_b(prev, rem % 2, ml, mb_prev)
            ot = jnp.concatenate([acc_ref[hh, 0:HEAD_DIM, :] / acc_ref[hh, HEAD_DIM:HEAD_DIM + 1, :]
                                  for hh in range(2)], axis=0)
            o_ref[...] = ot.T.astype(BF16)
            return jnp.int32(0)
        return run

    lax.switch(qi % 4, [tail(rem) for rem in range(4)], carry)


def _fox(proj, aug, vt5):
    nq = SEQ // FOX_T
    npair = N_HEADS_FOX // 2
    return pl.pallas_call(
        _fox_kernel,
        grid=(BATCH, npair, nq),
        in_specs=[
            pl.BlockSpec((FOX_T, LANES), lambda b, h, i: (b * nq + i, QB_BLK + h)),
            pl.BlockSpec((SEQ, LANES), lambda b, h, i: (b, KB_BLK + h)),
            pl.BlockSpec((SEQ, LANES), lambda b, h, i: (b, h)),
            pl.BlockSpec((1, 1, nq, LANES, FOX_T), lambda b, h, i: (b, h, 0, 0, 0)),
        ],
        out_specs=pl.BlockSpec((FOX_T, LANES), lambda b, h, i: (b * nq + i, h)),
        out_shape=jax.ShapeDtypeStruct((N_TOK, FOX_WIDTH), BF16),
        scratch_shapes=[
            pltpu.VMEM((FOX_T, FOX_T), F32),
            pltpu.VMEM((2, 2, FOX_T, FOX_T), F32),
            pltpu.VMEM((2, FOX_VROWS, FOX_T), F32),
        ],
        compiler_params=_cparams(("arbitrary", "arbitrary", "arbitrary")),
        name="fox_attn",
    )(proj, proj, aug, vt5)


OUT_TM = 512
OUT_SUB = 256
SUBLANES = 8
PACKED_COLS = D_MODEL // 2
PACK_ROWS = PACKED_COLS // LANES
DMA_UNROLL = 8


def _outproj_kernel(oa_ref, ob_ref, x_ref, mod_ref, ga_ref, gb_ref, w_ref, gm_ref, wr2_ref, br_ref,
                    x1_ref, h2p_ref, ids_ref, gates_ref):
    subs = [slice(k * OUT_SUB, (k + 1) * OUT_SUB) for k in range(OUT_TM // OUT_SUB)]
    mixed = [jnp.concatenate([_rms(oa_ref[r, :].astype(F32)) * ga_ref[...],
                              _rms(ob_ref[r, :].astype(F32)) * gb_ref[...]], axis=1).astype(BF16) for r in subs]
    ys = [jnp.dot(m, w_ref[...], preferred_element_type=F32) for m in mixed]
    h2s = []
    for r, y in zip(subs, ys):
        x1 = x_ref[r, :] + mod_ref[0, 2:3, :] * y
        x1_ref[r, :] = x1
        h2s.append((_rms(x1) * gm_ref[...]) * (1.0 + mod_ref[0, 4:5, :]) + mod_ref[0, 3:4, :])
    hbs = [h2.astype(BF16) for h2 in h2s]
    for r, h2 in zip(subs, h2s):
        packed = _pack_bf16_pairs(h2)
        for s in range(PACK_ROWS):
            h2p_ref[r, s, :] = packed[:, s * LANES:(s + 1) * LANES]

    hls = [(h2 - hb.astype(F32)).astype(BF16) for h2, hb in zip(h2s, hbs)]
    r2s = [jnp.dot(hb, wr2_ref[...], preferred_element_type=F32) for hb in hbs]
    r3s = [jnp.dot(hl, wr2_ref[:, :LANES], preferred_element_type=F32) for hl in hls]
    for r, r2, r3 in zip(subs, r2s, r3s):
        logits = (r2[:, :LANES] + (r2[:, LANES:] + r3)) + br_ref[...]
        ids, gates = _route(logits)
        ids_ref[r, :] = ids
        gates_ref[r, :] = gates


def _route(logits):
    lane = lax.broadcasted_iota(jnp.int32, logits.shape, 1)
    neg = -jnp.inf
    is_g = lane < N_GROUPS
    gl = jnp.where(is_g, logits, neg)
    gmax = jnp.max(gl, axis=1, keepdims=True)
    gsel = jnp.min(jnp.where(gl == gmax, lane, LANES), axis=1, keepdims=True)
    gsum = jnp.sum(jnp.where(is_g, jnp.exp(gl - gmax), 0.0), axis=1, keepdims=True)
    g_val = 1.0 / gsum
    elane = lane - N_GROUPS
    in_sel = (elane >= 0) & (elane < N_EXPERTS) & ((elane >> 3) == gsel)
    ev = jnp.where(in_sel, logits, neg)
    t1 = jnp.max(ev, axis=1, keepdims=True)
    i1 = jnp.min(jnp.where(ev == t1, lane, LANES), axis=1, keepdims=True)
    ev2 = jnp.where(lane == i1, neg, ev)
    t2 = jnp.max(ev2, axis=1, keepdims=True)
    i2 = jnp.min(jnp.where(ev2 == t2, lane, LANES), axis=1, keepdims=True)
    e2 = jnp.exp(t2 - t1)
    den = 1.0 + e2
    w1 = (1.0 / den) * g_val
    w2 = (e2 / den) * g_val
    return (jnp.where(lane == 0, i1 - N_GROUPS, jnp.where(lane == 1, i2 - N_GROUPS, 0)),
            jnp.where(lane == 0, w1, jnp.where(lane == 1, w2, 0.0)))


def _outproj(oa, ob, x2, mod3, ga, gb, w_out, gm, wr, br):
    tiles_per_batch = SEQ // OUT_TM
    row = lambda i: (i, 0)
    const = lambda i: (0, 0)
    wr_hi = wr.astype(BF16)
    wr_lo = (wr - wr_hi.astype(F32)).astype(BF16)
    wr2 = jnp.concatenate([wr_hi, wr_lo], axis=1)
    return pl.pallas_call(
        _outproj_kernel,
        grid=(N_TOK // OUT_TM,),
        in_specs=[
            pl.BlockSpec((OUT_TM, SWA_WIDTH), row),
            pl.BlockSpec((OUT_TM, FOX_WIDTH), row),
            pl.BlockSpec((OUT_TM, D_MODEL), row),
            pl.BlockSpec((1, 6, D_MODEL), lambda i: (i // tiles_per_batch, 0, 0)),
            pl.BlockSpec((1, SWA_WIDTH), const),
            pl.BlockSpec((1, FOX_WIDTH), const),
            pl.BlockSpec((D_MODEL, D_MODEL), const),
            pl.BlockSpec((1, D_MODEL), const),
            pl.BlockSpec((D_MODEL, 2 * LANES), const),
            pl.BlockSpec((1, LANES), const),
        ],
        out_specs=[
            pl.BlockSpec((OUT_TM, D_MODEL), row),
            pl.BlockSpec((OUT_TM, PACK_ROWS, LANES), lambda i: (i, 0, 0)),
            pl.BlockSpec((OUT_TM, LANES), row),
            pl.BlockSpec((OUT_TM, LANES), row),
        ],
        out_shape=[
            jax.ShapeDtypeStruct((N_TOK, D_MODEL), F32),
            jax.ShapeDtypeStruct((N_TOK, PACK_ROWS, LANES), jnp.uint32),
            jax.ShapeDtypeStruct((N_TOK, LANES), jnp.int32),
            jax.ShapeDtypeStruct((N_TOK, LANES), F32),
        ],
        compiler_params=_cparams(("arbitrary",)),
        name="outproj_router",
    )(oa, ob, x2, mod3, ga, gb, w_out, gm, wr2, br)


N_BLOCKS = (N_TOK * TOP_K) // MOE_BLOCK + N_EXPERTS
P_ROWS = N_BLOCKS * MOE_BLOCK


def _pack_bf16_pairs(x):
    xb = x.astype(BF16).astype(F32)
    half = x.shape[1] // 2
    lo = lax.shift_right_logical(pltpu.bitcast(xb[:, :half], jnp.uint32), jnp.uint32(16))
    hi = pltpu.bitcast(xb[:, half:], jnp.uint32) & jnp.uint32(0xFFFF0000)
    return lo | hi


def _unpack_bf16_pairs(w):
    return (pltpu.bitcast(lax.shift_left(w, jnp.uint32(16)), F32),
            pltpu.bitcast(w & jnp.uint32(0xFFFF0000), F32))


def _expert_kernel(be_ref, tok_ref, src_ref, nused_ref, nxt_ref, h2_hbm, wg_hbm, wu_hbm, wd_hbm, y_hbm, xbuf, sem,
                   wgf, wuf, wdf, wsem, wgb, wub, wdb, ystage, osem):
    i = pl.program_id(0)
    nused = nused_ref[0]

    def weight_copies(e):
        return (pltpu.make_async_copy(wg_hbm.at[e], wgf, wsem.at[0]),
                pltpu.make_async_copy(wu_hbm.at[e], wuf, wsem.at[1]),
                pltpu.make_async_copy(wd_hbm.at[e], wdf, wsem.at[2]))

    def issue(blk, slot):
        def body(g, _):
            for q in range(SUBLANES):
                tok = tok_ref[jnp.minimum(src_ref[blk] + g * SUBLANES + q, N_TOK * TOP_K - 1)]
                pltpu.make_async_copy(h2_hbm.at[tok], xbuf.at[slot, g, :, q, :], sem.at[slot]).start(
                    priority=q % 2)
            return 0
        lax.fori_loop(0, MOE_BLOCK // SUBLANES, body, 0)

    def out_copies(blk, slot):
        r0 = pl.multiple_of(blk * MOE_BLOCK, MOE_BLOCK)
        return [pltpu.make_async_copy(ystage.at[slot, :, pl.ds(s * LANES, LANES)],
                                      y_hbm.at[pl.ds(r0, MOE_BLOCK), s, :], osem.at[slot])
                for s in range(PACK_ROWS)]

    def out_wait(slot):
        pltpu.make_async_copy(ystage.at[slot], ystage.at[slot], osem.at[slot]).wait()

    def gather_wait(slot):
        pltpu.make_async_copy(xbuf.at[slot], xbuf.at[slot], sem.at[slot]).wait()

    @pl.when(i == 0)
    def _():
        issue(0, 0)

    @pl.when(i + 1 < nused)
    def _():
        issue(i + 1, (i + 1) % 2)

    oslot = i % 2

    @pl.when(i >= 2)
    def _():
        out_wait(oslot)

    e = be_ref[i]
    new_expert = ((i == 0) | (e != be_ref[jnp.maximum(i - 1, 0)])) & (i < nused)

    @pl.when(i == 0)
    def _():
        for cp in weight_copies(e):
            cp.start()

    @pl.when(new_expert)
    def _():
        for cp in weight_copies(e):
            cp.wait()
        wgb[...] = wgf[...].astype(BF16)
        wub[...] = wuf[...].astype(BF16)
        wdb[...] = wdf[...].astype(BF16)

    @pl.when(new_expert & (nxt_ref[e] >= 0))
    def _():
        for cp in weight_copies(nxt_ref[e]):
            cp.start()

    @pl.when(i < nused)
    def _():
        slot = i % 2
        gather_wait(slot)
        gsub = MOE_SUB // SUBLANES
        xbs = []
        for k in range(MOE_BLOCK // MOE_SUB):
            xu = jnp.concatenate([xbuf[slot, k * gsub:(k + 1) * gsub, s].reshape(MOE_SUB, LANES)
                                  for s in range(PACK_ROWS)], axis=1)
            xbs.append(jnp.concatenate(_unpack_bf16_pairs(xu), axis=1).astype(BF16))
        gs = [jnp.dot(xb, wgb[...], preferred_element_type=F32) for xb in xbs]
        us = [jnp.dot(xb, wub[...], preferred_element_type=F32) for xb in xbs]
        hids = [(jax.nn.silu(g) * u).astype(BF16) for g, u in zip(gs, us)]
        ys = [jnp.dot(hid, wdb[...], preferred_element_type=F32) for hid in hids]
        for k, y in enumerate(ys):
            ystage[oslot, k * MOE_SUB:(k + 1) * MOE_SUB, :] = _pack_bf16_pairs(y)

    @pl.when(i >= nused)
    def _():
        ystage[oslot] = jnp.zeros((MOE_BLOCK, PACKED_COLS), jnp.uint32)

    for cp in out_copies(i, oslot):
        cp.start()

    @pl.when(i == pl.num_programs(0) - 1)
    def _():
        out_wait(oslot)
        out_wait(1 - oslot)


def _experts(block_e, tok_sorted, blk_src, nused, next_used, h2p, wg, wu, wd):
    grid_spec = pltpu.PrefetchScalarGridSpec(
        num_scalar_prefetch=5,
        grid=(N_BLOCKS,),
        in_specs=[pl.BlockSpec(memory_space=pl.ANY)] * 4,
        out_specs=pl.BlockSpec(memory_space=pl.ANY),
        scratch_shapes=[pltpu.VMEM((2, MOE_BLOCK // SUBLANES, PACK_ROWS, SUBLANES, LANES), jnp.uint32),
                        pltpu.SemaphoreType.DMA((2,)),
                        pltpu.VMEM((D_MODEL, D_EXPERT), F32),
                        pltpu.VMEM((D_MODEL, D_EXPERT), F32),
                        pltpu.VMEM((D_EXPERT, D_MODEL), F32),
                        pltpu.SemaphoreType.DMA((3,)),
                        pltpu.VMEM((D_MODEL, D_EXPERT), BF16),
                        pltpu.VMEM((D_MODEL, D_EXPERT), BF16),
                        pltpu.VMEM((D_EXPERT, D_MODEL), BF16),
                        pltpu.VMEM((2, MOE_BLOCK, PACKED_COLS), jnp.uint32),
                        pltpu.SemaphoreType.DMA((2,))],
    )
    return pl.pallas_call(
        _expert_kernel,
        grid_spec=grid_spec,
        out_shape=jax.ShapeDtypeStruct((P_ROWS, PACK_ROWS, LANES), jnp.uint32),
        compiler_params=_cparams(("arbitrary",)),
        name="expert_ffn",
    )(block_e, tok_sorted, blk_src, nused, next_used, h2p, wg, wu, wd)


CMB_TM = 256


def _combine_kernel(dest_ref, ys_hbm, x1_ref, gates_ref, mod_ref, fg_ref, o_ref, ybuf, sem):
    i = pl.program_id(0)
    n = pl.num_programs(0)
    groups = CMB_TM // SUBLANES

    def issue(tile, slot):
        def body(g, _):
            for q in range(SUBLANES):
                for k in range(TOP_K):
                    d = dest_ref[(tile * CMB_TM + g * SUBLANES + q) * TOP_K + k]
                    pltpu.make_async_copy(ys_hbm.at[d], ybuf.at[slot, k * groups + g, :, q, :],
                                          sem.at[slot]).start(priority=k)
            return 0
        lax.fori_loop(0, groups, body, 0)

    @pl.when(i == 0)
    def _():
        issue(0, 0)

    @pl.when(i + 1 < n)
    def _():
        issue(i + 1, (i + 1) % 2)

    slot = i % 2
    pltpu.make_async_copy(ybuf.at[slot], ybuf.at[slot], sem.at[slot]).wait()
    gts = gates_ref[...]
    w0 = gts[:, 0:1]
    w1 = gts[:, 1:2]
    half = D_MODEL // 2
    ssq = jnp.zeros((CMB_TM, 1), F32)
    for s in range(PACK_ROWS):
        y0 = _unpack_bf16_pairs(ybuf[slot, 0:groups, s].reshape(CMB_TM, LANES))
        y1 = _unpack_bf16_pairs(ybuf[slot, groups:2 * groups, s].reshape(CMB_TM, LANES))
        for part in range(2):
            cols = slice(part * half + s * LANES, part * half + (s + 1) * LANES)
            x2 = x1_ref[:, cols] + mod_ref[0, 5:6, cols] * (y0[part] * w0 + y1[part] * w1)
            ssq = ssq + jnp.sum(x2 * x2, axis=-1, keepdims=True)
            o_ref[:, cols] = x2
    o_ref[...] = (o_ref[...] * lax.rsqrt(ssq * (1.0 / D_MODEL) + EPS)) * fg_ref[...]


def _combine(dest, ys, x1, gates, mod3, final_g):
    tiles_per_batch = SEQ // CMB_TM
    grid_spec = pltpu.PrefetchScalarGridSpec(
        num_scalar_prefetch=1,
        grid=(N_TOK // CMB_TM,),
        in_specs=[
            pl.BlockSpec(memory_space=pl.ANY),
            pl.BlockSpec((CMB_TM, D_MODEL), lambda i, d: (i, 0)),
            pl.BlockSpec((CMB_TM, LANES), lambda i, d: (i, 0)),
            pl.BlockSpec((1, 6, D_MODEL), lambda i, d: (i // tiles_per_batch, 0, 0)),
            pl.BlockSpec((1, D_MODEL), lambda i, d: (0, 0)),
        ],
        out_specs=pl.BlockSpec((CMB_TM, D_MODEL), lambda i, d: (i, 0)),
        scratch_shapes=[pltpu.VMEM((2, TOP_K * CMB_TM // SUBLANES, PACK_ROWS, SUBLANES, LANES), jnp.uint32),
                        pltpu.SemaphoreType.DMA((2,))],
    )
    return pl.pallas_call(
        _combine_kernel,
        grid_spec=grid_spec,
        out_shape=jax.ShapeDtypeStruct((N_TOK, D_MODEL), F32),
        compiler_params=_cparams(("arbitrary",)),
        name="combine_final",
    )(dest, ys, x1, gates, mod3, final_g)


def _routing_tables(eid):
    ids = jnp.arange(N_EXPERTS, dtype=jnp.int32)
    onehot = (eid[:, None] == ids[None, :]).astype(jnp.int32)
    csum = jnp.cumsum(onehot, axis=0)
    counts = csum[-1]
    padded = (counts + MOE_BLOCK - 1) // MOE_BLOCK * MOE_BLOCK
    pend = jnp.cumsum(padded)
    pstart = pend - padded
    dest = (jnp.sum(onehot * (csum + pstart[None, :]), axis=1) - 1).astype(jnp.int32)
    blk_row = jnp.arange(N_BLOCKS, dtype=jnp.int32) * MOE_BLOCK
    block_e = jnp.minimum(jnp.sum((pend[None, :] <= blk_row[:, None]).astype(jnp.int32), axis=1),
                          N_EXPERTS - 1).astype(jnp.int32)
    tok_sorted = (jnp.argsort(eid, stable=True) // TOP_K).astype(jnp.int32)
    starts = jnp.cumsum(counts) - counts
    is_e = (block_e[:, None] == ids[None, :]).astype(jnp.int32)
    blk_src = (blk_row + jnp.sum(is_e * (starts - pstart)[None, :], axis=1)).astype(jnp.int32)
    nused = (pend[-1] // MOE_BLOCK).astype(jnp.int32).reshape(1)
    later_used = (ids[None, :] > ids[:, None]) & (counts[None, :] > 0)
    next_used = jnp.min(jnp.where(later_used, ids[None, :], N_EXPERTS), axis=1)
    next_used = jnp.where(next_used == N_EXPERTS, -1, next_used).astype(jnp.int32)
    return dest, tok_sorted, blk_src, block_e, nused, next_used


def kernel(x, c, w_ada, b_ada, norm_mix_g, w_in, b_forget, sinks, out_norm_swa_g, out_norm_fox_g, w_out,
           norm_moe_g, w_group, b_group, w_expert, b_expert, w_gate, w_up, w_down, final_g):
    assert IN_TM == FOX_T
    x2 = x.reshape(N_TOK, D_MODEL)

    w_in0 = w_in[0]
    w_row = w_in0[:, :VB_COL].astype(BF16)
    w_vt = w_in0[:, VB_COL:MAIN_COLS].astype(BF16)
    w_f = jnp.pad(w_in0[:, MAIN_COLS:], ((0, 0), (0, LANES - N_HEADS_FOX))).astype(BF16)
    colscale = jnp.concatenate([
        jnp.full((SWA_WIDTH,), HEAD_DIM ** -0.5 * LOG2E, F32),
        jnp.ones((2 * KV_WIDTH,), F32),
        jnp.full((FOX_WIDTH,), HEAD_DIM ** -0.5 * LOG2E, F32),
        jnp.ones((FOX_WIDTH,), F32)]).reshape(1, ROW_COLS)
    bf_row = jnp.pad(b_forget[0], (0, LANES - N_HEADS_FOX)).reshape(1, LANES)
    w_out_p = w_out[0].astype(BF16)
    ga = out_norm_swa_g[0].reshape(1, SWA_WIDTH)
    gb = out_norm_fox_g[0].reshape(1, FOX_WIDTH)
    wr = jnp.pad(jnp.concatenate([w_group[0], w_expert[0]], axis=1),
                 ((0, 0), (0, LANES - N_GROUPS - N_EXPERTS)))
    br = jnp.pad(jnp.concatenate([b_group[0], b_expert[0]]), (0, LANES - N_GROUPS - N_EXPERTS)).reshape(1, LANES)
    wg, wu, wd = w_gate[0], w_up[0], w_down[0]

    mod = _adaln(c, w_ada[0], b_ada[0])
    mod3 = mod.reshape(BATCH, 6, D_MODEL)

    proj, f, vt5 = _inproj(x2, mod3, norm_mix_g[0].reshape(1, D_MODEL), w_row, w_f, colscale, w_vt)
    aug = _forget_bias(f, bf_row)

    o_a = _swa(sinks[0], proj)
    o_b = _fox(proj, aug, vt5)

    x1, h2, ids, gates = _outproj(o_a, o_b, x2, mod3, ga, gb, w_out_p,
                                  norm_moe_g[0].reshape(1, D_MODEL), wr, br)

    eid = ids[:, :TOP_K].reshape(-1)
    dest, tok_sorted, blk_src, block_e, nused, next_used = _routing_tables(eid)
    ys = _experts(block_e, tok_sorted, blk_src, nused, next_used, h2, wg, wu, wd)
    out = _combine(dest, ys, x1, gates, mod3, final_g.reshape(1, D_MODEL))
    return out.reshape(BATCH, SEQ, D_MODEL)
```

```python
import functools
import math

import numpy as np
import jax
import jax.numpy as jnp
from jax import lax
from jax.experimental import pallas as pl
from jax.experimental.pallas import tpu as pltpu

F32 = jnp.float32
BF16 = jnp.bfloat16

D_MODEL = 2048
BATCH = 2
SEQ = 8192
N_TOK = BATCH * SEQ
HEAD_DIM = 64
N_HEADS_SWA = 16
N_KV_SWA = 4
N_HEADS_FOX = 16
WINDOW = 128
SWA_WIDTH = N_HEADS_SWA * HEAD_DIM
KV_WIDTH = N_KV_SWA * HEAD_DIM
FOX_WIDTH = N_HEADS_FOX * HEAD_DIM
MAIN_COLS = SWA_WIDTH + 2 * KV_WIDTH + 3 * FOX_WIDTH
N_GROUPS = 4
EXPERTS_PER_GROUP = 8
N_EXPERTS = N_GROUPS * EXPERTS_PER_GROUP
TOP_K = 2
D_EXPERT = 512
MOE_BLOCK = 256
MOE_SUB = 128
EPS = 1e-6
LOG2E = math.log2(math.e)

LANES = 128
VMEM_LIMIT = 56 * 1024 * 1024

QA_BLK = 0
KA_COL = SWA_WIDTH
VA_COL = SWA_WIDTH + KV_WIDTH
QB_BLK = (SWA_WIDTH + 2 * KV_WIDTH) // LANES
KB_BLK = QB_BLK + FOX_WIDTH // LANES
VB_COL = SWA_WIDTH + 2 * KV_WIDTH + 2 * FOX_WIDTH
ROW_COLS = VB_COL


def _cparams(sem, vmem=VMEM_LIMIT):
    return pltpu.CompilerParams(dimension_semantics=sem, vmem_limit_bytes=vmem)


ADA_TN = 1024


def _adaln_kernel(cb_ref, w_ref, b_ref, o_ref):
    for b in range(BATCH):
        s = jax.nn.silu(cb_ref[b])
        cols = []
        for j in range(ADA_TN // LANES):
            prod = w_ref[:, j * LANES:(j + 1) * LANES] * s
            cols.append(jnp.sum(prod, axis=0, keepdims=True))
        o_ref[b:b + 1, :] = jnp.concatenate(cols, axis=1) + b_ref[...]


def _adaln(c, w_ada, b_ada):
    ncol = w_ada.shape[1]
    cb = jnp.broadcast_to(c[:, :, None], (BATCH, D_MODEL, LANES))
    return pl.pallas_call(
        _adaln_kernel,
        grid=(ncol // ADA_TN,),
        in_specs=[
            pl.BlockSpec((BATCH, D_MODEL, LANES), lambda j: (0, 0, 0)),
            pl.BlockSpec((D_MODEL, ADA_TN), lambda j: (0, j)),
            pl.BlockSpec((1, ADA_TN), lambda j: (0, j)),
        ],
        out_specs=pl.BlockSpec((BATCH, ADA_TN), lambda j: (0, j)),
        out_shape=jax.ShapeDtypeStruct((BATCH, ncol), F32),
        compiler_params=_cparams(("arbitrary",)),
        name="adaln",
    )(cb, w_ada, b_ada.reshape(1, ncol))


def _rms(x):
    return x * lax.rsqrt(jnp.mean(x * x, axis=-1, keepdims=True) + EPS)


IN_TM = 512
IN_TN = 1792
IN_NJ = ROW_COLS // IN_TN


def _inproj_kernel(x_ref, mod_ref, g_ref, w_ref, wf_ref, cs_ref, wvt_ref, o_ref, f_ref, vt_ref, h_ref):
    j = pl.program_id(1)

    @pl.when(j == 0)
    def _():
        y = _rms(x_ref[...])
        h = (y * g_ref[...]) * (1.0 + mod_ref[0, 1:2, :]) + mod_ref[0, 0:1, :]
        hb = h.astype(BF16)
        h_ref[...] = hb
        f_ref[...] = jnp.dot(hb, wf_ref[...], preferred_element_type=F32)

    @pl.when(j < IN_NJ)
    def _():
        acc = jnp.dot(h_ref[...], w_ref[...], preferred_element_type=F32)
        o_ref[...] = (acc * cs_ref[...]).astype(BF16)

    @pl.when(j == IN_NJ)
    def _():
        vt = lax.dot_general(wvt_ref[...], h_ref[...], (((0,), (1,)), ((), ())),
                             preferred_element_type=F32)
        vt_ref[0, :, 0] = vt.reshape(N_HEADS_FOX // 2, LANES, IN_TM).astype(BF16)


def _inproj(x2, mod3, g, w_row, w_f, colscale, w_vt):
    tiles_per_batch = SEQ // IN_TM
    last = IN_NJ - 1
    return pl.pallas_call(
        _inproj_kernel,
        grid=(N_TOK // IN_TM, IN_NJ + 1),
        in_specs=[
            pl.BlockSpec((IN_TM, D_MODEL), lambda i, j: (i, 0)),
            pl.BlockSpec((1, 6, D_MODEL), lambda i, j: (i // tiles_per_batch, 0, 0)),
            pl.BlockSpec((1, D_MODEL), lambda i, j: (0, 0)),
            pl.BlockSpec((D_MODEL, IN_TN), lambda i, j: (0, jnp.minimum(j, last))),
            pl.BlockSpec((D_MODEL, LANES), lambda i, j: (0, 0)),
            pl.BlockSpec((1, IN_TN), lambda i, j: (0, jnp.minimum(j, last))),
            pl.BlockSpec((D_MODEL, FOX_WIDTH), lambda i, j: (0, 0)),
        ],
        out_specs=[
            pl.BlockSpec((IN_TM, IN_TN), lambda i, j: (i, jnp.minimum(j, last))),
            pl.BlockSpec((IN_TM, LANES), lambda i, j: (i, 0)),
            pl.BlockSpec((1, N_HEADS_FOX // 2, 1, LANES, IN_TM),
                         lambda i, j: (i // tiles_per_batch, 0, i % tiles_per_batch, 0, 0)),
        ],
        out_shape=[
            jax.ShapeDtypeStruct((N_TOK, ROW_COLS), BF16),
            jax.ShapeDtypeStruct((N_TOK, LANES), F32),
            jax.ShapeDtypeStruct((BATCH, N_HEADS_FOX // 2, SEQ // IN_TM, LANES, IN_TM), BF16),
        ],
        scratch_shapes=[pltpu.VMEM((IN_TM, D_MODEL), BF16)],
        compiler_params=_cparams(("arbitrary", "arbitrary")),
        name="inproj",
    )(x2, mod3, g, w_row, w_f, colscale, w_vt)


CUM_TS = 512
FOX_AUG = 3


def _cum_kernel(f_ref, bf_ref, tri_ref, sel_ref, aug_ref, carry_ref):
    @pl.when(pl.program_id(1) == 0)
    def _():
        carry_ref[...] = jnp.zeros_like(carry_ref)

    lf = jax.nn.log_sigmoid(f_ref[...] + bf_ref[...])
    l_hi = lf.astype(BF16)
    l_r = lf - l_hi.astype(F32)
    l_mid = l_r.astype(BF16)
    l_lo = (l_r - l_mid.astype(F32)).astype(BF16)
    c3 = jnp.dot(tri_ref[...], jnp.concatenate([l_hi, l_mid, l_lo], axis=1), preferred_element_type=F32)
    cs = (c3[:, :LANES] + (c3[:, LANES:2 * LANES] + c3[:, 2 * LANES:])) + carry_ref[...]
    carry_ref[...] = cs[CUM_TS - 1:CUM_TS, :]
    bias = cs * (-LOG2E)
    hi = bias.astype(BF16)
    r1 = bias - hi.astype(F32)
    mid = r1.astype(BF16)
    lo = (r1 - mid.astype(F32)).astype(BF16)
    pieces = jnp.concatenate([hi, mid, lo], axis=1)
    aug_ref[...] = jnp.dot(pieces, sel_ref[...], preferred_element_type=F32).astype(BF16)


def _aug_selector():
    sel = np.zeros((FOX_AUG * LANES, FOX_WIDTH), np.float32)
    for head in range(N_HEADS_FOX):
        for piece in range(FOX_AUG):
            sel[piece * LANES + head, (head // 2) * LANES + FOX_AUG * (head % 2) + piece] = 1.0
    return sel


def _forget_bias(f, b_forget_row):
    tri = jnp.tril(jnp.ones((CUM_TS, CUM_TS), BF16))
    sel = jnp.asarray(_aug_selector(), BF16)
    nblk = SEQ // CUM_TS
    return pl.pallas_call(
        _cum_kernel,
        grid=(BATCH, nblk),
        in_specs=[
            pl.BlockSpec((CUM_TS, LANES), lambda b, i: (b * nblk + i, 0)),
            pl.BlockSpec((1, LANES), lambda b, i: (0, 0)),
            pl.BlockSpec((CUM_TS, CUM_TS), lambda b, i: (0, 0)),
            pl.BlockSpec((FOX_AUG * LANES, FOX_WIDTH), lambda b, i: (0, 0)),
        ],
        out_specs=pl.BlockSpec((CUM_TS, FOX_WIDTH), lambda b, i: (b * nblk + i, 0)),
        out_shape=jax.ShapeDtypeStruct((N_TOK, FOX_WIDTH), BF16),
        scratch_shapes=[pltpu.VMEM((1, LANES), F32)],
        compiler_params=_cparams(("arbitrary", "arbitrary")),
        name="forget_bias",
    )(f, b_forget_row, tri, sel)


SWA_TQ = WINDOW
SWA_GROUP = 4


def _swa_kernel(sinks_ref, q_ref, kc_ref, kp_ref, vc_ref, vp_ref, o_ref, bias_ref):
    i = pl.program_id(1)
    tq = SWA_TQ

    @pl.when((pl.program_id(0) == 0) & (i == 0))
    def _():
        row = lax.broadcasted_iota(jnp.int32, (tq, 2 * tq), 0)
        col = lax.broadcasted_iota(jnp.int32, (tq, 2 * tq), 1)
        dist = row + tq - col
        band = (dist >= 0) & (dist < WINDOW)
        distf = dist.astype(F32)
        for head in range(N_HEADS_SWA):
            slope2 = float(2.0 ** (-8.0 * (head + 1) / N_HEADS_SWA)) * LOG2E
            base = jnp.where(band, -slope2 * distf, -jnp.inf)
            bias_ref[1, head] = base
            bias_ref[0, head] = jnp.where(col >= tq, base, -jnp.inf)

    table = jnp.minimum(i, 1)
    lane = lax.broadcasted_iota(jnp.int32, (tq, LANES), 1)
    lo_half = lane < HEAD_DIM
    for t in range(2):
        kt = jnp.concatenate([kp_ref[:, t * LANES:(t + 1) * LANES],
                              kc_ref[:, t * LANES:(t + 1) * LANES]], axis=0)
        vt = jnp.concatenate([vp_ref[:, t * LANES:(t + 1) * LANES],
                              vc_ref[:, t * LANES:(t + 1) * LANES]], axis=0)
        for g0 in range(0, 4, SWA_GROUP):
            heads = [(g, e) for g in range(g0, g0 + SWA_GROUP) for e in range(2)]
            scores = []
            for g, e in heads:
                tile, f = t * 4 + g, g // 2
                qt = q_ref[:, tile * LANES:(tile + 1) * LANES]
                if e != f:
                    qt = jnp.concatenate([qt[:, HEAD_DIM:], qt[:, :HEAD_DIM]], axis=1)
                qm = jnp.where(lo_half if f == 0 else ~lo_half, qt, jnp.zeros_like(qt))
                scores.append(lax.dot_general(qm, kt, (((1,), (1,)), ((), ())),
                                              preferred_element_type=F32))
            probs, rdens = [], []
            for (g, e), s in zip(heads, scores):
                head = 2 * (t * 4 + g) + e
                sink = sinks_ref[head] * LOG2E
                s = s + bias_ref[table, head]
                m = jnp.maximum(jnp.max(s, axis=-1, keepdims=True), sink)
                p = jnp.exp2(s - m)
                rdens.append(1.0 / (jnp.sum(p, axis=-1, keepdims=True) + jnp.exp2(sink - m)))
                probs.append(p.astype(BF16))
            outs = [jnp.dot(p, vt, preferred_element_type=F32) * r for p, r in zip(probs, rdens)]
            for k, g in enumerate(range(g0, g0 + SWA_GROUP)):
                tile, f = t * 4 + g, g // 2
                same, other = outs[2 * k + f], pltpu.roll(outs[2 * k + 1 - f], HEAD_DIM, axis=1)
                o_ref[:, tile * LANES:(tile + 1) * LANES] = jnp.where(lo_half == (f == 0), same,
                                                                      other).astype(BF16)


def _swa(sinks, proj):
    nq = SEQ // SWA_TQ
    kblk = KA_COL // KV_WIDTH
    vblk = VA_COL // KV_WIDTH
    grid_spec = pltpu.PrefetchScalarGridSpec(
        num_scalar_prefetch=1,
        grid=(BATCH, nq),
        in_specs=[
            pl.BlockSpec((SWA_TQ, SWA_WIDTH), lambda b, i, s: (b * nq + i, 0)),
            pl.BlockSpec((SWA_TQ, KV_WIDTH), lambda b, i, s: (b * nq + i, kblk)),
            pl.BlockSpec((SWA_TQ, KV_WIDTH), lambda b, i, s: (b * nq + jnp.maximum(i - 1, 0), kblk)),
            pl.BlockSpec((SWA_TQ, KV_WIDTH), lambda b, i, s: (b * nq + i, vblk)),
            pl.BlockSpec((SWA_TQ, KV_WIDTH), lambda b, i, s: (b * nq + jnp.maximum(i - 1, 0), vblk)),
        ],
        out_specs=pl.BlockSpec((SWA_TQ, SWA_WIDTH), lambda b, i, s: (b * nq + i, 0)),
        scratch_shapes=[pltpu.VMEM((2, N_HEADS_SWA, SWA_TQ, 2 * SWA_TQ), F32)],
    )
    return pl.pallas_call(
        _swa_kernel,
        grid_spec=grid_spec,
        out_shape=jax.ShapeDtypeStruct((N_TOK, SWA_WIDTH), BF16),
        compiler_params=_cparams(("arbitrary", "arbitrary")),
        name="swa_attn",
    )(sinks, proj, proj, proj, proj, proj)


FOX_T = 512
FOX_ONES = 16
FOX_VROWS = HEAD_DIM + FOX_ONES


def _fox_kernel(q_ref, k_ref, aug_ref, vt_ref, o_ref, mask_ref, t_ref, acc_ref):
    qi = pl.program_id(2)
    t = FOX_T

    @pl.when(qi == 0)
    def _():
        kr = lax.broadcasted_iota(jnp.int32, (t, t), 0)
        qc = lax.broadcasted_iota(jnp.int32, (t, t), 1)
        mask_ref[...] = jnp.where(kr <= qc, 0.0, -jnp.inf)

    qf = q_ref[...].astype(F32).T
    drow = lax.broadcasted_iota(jnp.int32, (LANES, t), 0)
    qaug = []
    for hh in range(2):
        qh = jnp.where((drow >= HEAD_DIM * hh) & (drow < HEAD_DIM * (hh + 1)), qf, 0.0)
        ones = jnp.where((drow >= FOX_AUG * hh) & (drow < FOX_AUG * (hh + 1)), 1.0, 0.0)
        qaug.append(jnp.concatenate([qh, ones], axis=0).astype(BF16))
    acc_ref[...] = jnp.zeros_like(acc_ref)

    def stage_a(j, slot, diag):
        r0 = pl.multiple_of(j * t, t)
        kb = jnp.concatenate([k_ref[pl.ds(r0, t), :], aug_ref[pl.ds(r0, t), :]], axis=1)
        sts = [jnp.dot(kb, qaug[hh], preferred_element_type=F32) for hh in range(2)]
        mbs = []
        for hh in range(2):
            tt = sts[hh]
            if diag:
                tt = tt + mask_ref[...]
            t_ref[slot, hh] = tt
            mbs.append(jnp.max(tt, axis=0, keepdims=True))
        return tuple(mbs)

    def stage_b(jv, slot, ms, mbs):
        new, ps, alphas = [], [], []
        for hh in range(2):
            m_new = jnp.maximum(ms[hh], mbs[hh])
            alphas.append(jnp.exp2(ms[hh] - m_new))
            ps.append(jnp.exp2(t_ref[slot, hh] - m_new).astype(BF16))
            new.append(m_new)
        ones = jnp.ones((FOX_ONES, t), BF16)
        pvs = [jnp.dot(jnp.concatenate([vt_ref[0, 0, jv, hh * HEAD_DIM:(hh + 1) * HEAD_DIM, :], ones], axis=0),
                       ps[hh], preferred_element_type=F32)
               for hh in range(2)]
        for hh in range(2):
            acc_ref[hh] = alphas[hh] * acc_ref[hh] + pvs[hh]
        return tuple(new)

    ml0 = tuple(jnp.full((1, t), -jnp.inf, F32) for _ in range(2))
    mb0 = stage_a(qi, 0, True)

    def pair(j0, c):
        ml, mbs, jprev = c
        mb1 = stage_a(j0, 1, False)
        ml = stage_b(jprev, 0, ml, mbs)
        mb2 = stage_a(j0 + 1, 0, False)
        ml = stage_b(j0, 1, ml, mb1)
        return ml, mb2, j0 + 1

    def quad(ii, c):
        return pair(4 * ii + 2, pair(4 * ii, c))

    carry = lax.fori_loop(0, qi // 4, quad, (ml0, mb0, qi))
    base = 4 * (qi // 4)

    def tail(rem):
        def run(c):
            ml, mb_prev, prev = c
            for k in range(rem):
                mb_new = stage_a(base + k, (k + 1) % 2, False)
                ml = stage_b(prev, k % 2, ml, mb_prev)
                prev, mb_prev = base + k, mb_new
            stage_b(prev, rem % 2, ml, mb_prev)
            ot = jnp.concatenate([acc_ref[hh, 0:HEAD_DIM, :] / acc_ref[hh, HEAD_DIM:HEAD_DIM + 1, :]
                                  for hh in range(2)], axis=0)
            o_ref[...] = ot.T.astype(BF16)
            return jnp.int32(0)
        return run

    lax.switch(qi % 4, [tail(rem) for rem in range(4)], carry)


def _fox(proj, aug, vt5):
    nq = SEQ // FOX_T
    npair = N_HEADS_FOX // 2
    return pl.pallas_call(
        _fox_kernel,
        grid=(BATCH, npair, nq),
        in_specs=[
            pl.BlockSpec((FOX_T, LANES), lambda b, h, i: (b * nq + i, QB_BLK + h)),
            pl.BlockSpec((SEQ, LANES), lambda b, h, i: (b, KB_BLK + h)),
            pl.BlockSpec((SEQ, LANES), lambda b, h, i: (b, h)),
            pl.BlockSpec((1, 1, nq, LANES, FOX_T), lambda b, h, i: (b, h, 0, 0, 0)),
        ],
        out_specs=pl.BlockSpec((FOX_T, LANES), lambda b, h, i: (b * nq + i, h)),
        out_shape=jax.ShapeDtypeStruct((N_TOK, FOX_WIDTH), BF16),
        scratch_shapes=[
            pltpu.VMEM((FOX_T, FOX_T), F32),
            pltpu.VMEM((2, 2, FOX_T, FOX_T), F32),
            pltpu.VMEM((2, FOX_VROWS, FOX_T), F32),
        ],
        compiler_params=_cparams(("arbitrary", "arbitrary", "arbitrary")),
        name="fox_attn",
    )(proj, proj, aug, vt5)


OUT_TM = 512
OUT_SUB = 256
SUBLANES = 8
PACKED_COLS = D_MODEL // 2
PACK_ROWS = PACKED_COLS // LANES
DMA_UNROLL = 8


def _outproj_kernel(oa_ref, ob_ref, x_ref, mod_ref, ga_ref, gb_ref, w_ref, gm_ref, wr2_ref, br_ref,
                    x1_ref, h2p_ref, ids_ref, gates_ref):
    subs = [slice(k * OUT_SUB, (k + 1) * OUT_SUB) for k in range(OUT_TM // OUT_SUB)]
    mixed = [jnp.concatenate([_rms(oa_ref[r, :].astype(F32)) * ga_ref[...],
                              _rms(ob_ref[r, :].astype(F32)) * gb_ref[...]], axis=1).astype(BF16) for r in subs]
    ys = [jnp.dot(m, w_ref[...], preferred_element_type=F32) for m in mixed]
    h2s = []
    for r, y in zip(subs, ys):
        x1 = x_ref[r, :] + mod_ref[0, 2:3, :] * y
        x1_ref[r, :] = x1
        h2s.append((_rms(x1) * gm_ref[...]) * (1.0 + mod_ref[0, 4:5, :]) + mod_ref[0, 3:4, :])
    hbs = [h2.astype(BF16) for h2 in h2s]
    for r, h2 in zip(subs, h2s):
        packed = _pack_bf16_pairs(h2)
        for s in range(PACK_ROWS):
            h2p_ref[r, s, :] = packed[:, s * LANES:(s + 1) * LANES]

    hls = [(h2 - hb.astype(F32)).astype(BF16) for h2, hb in zip(h2s, hbs)]
    r2s = [jnp.dot(hb, wr2_ref[...], preferred_element_type=F32) for hb in hbs]
    r3s = [jnp.dot(hl, wr2_ref[:, :LANES], preferred_element_type=F32) for hl in hls]
    for r, r2, r3 in zip(subs, r2s, r3s):
        logits = (r2[:, :LANES] + (r2[:, LANES:] + r3)) + br_ref[...]
        ids, gates = _route(logits)
        ids_ref[r, :] = ids
        gates_ref[r, :] = gates


def _route(logits):
    lane = lax.broadcasted_iota(jnp.int32, logits.shape, 1)
    neg = -jnp.inf
    is_g = lane < N_GROUPS
    gl = jnp.where(is_g, logits, neg)
    gmax = jnp.max(gl, axis=1, keepdims=True)
    gsel = jnp.min(jnp.where(gl == gmax, lane, LANES), axis=1, keepdims=True)
    gsum = jnp.sum(jnp.where(is_g, jnp.exp(gl - gmax), 0.0), axis=1, keepdims=True)
    g_val = 1.0 / gsum
    elane = lane - N_GROUPS
    in_sel = (elane >= 0) & (elane < N_EXPERTS) & ((elane >> 3) == gsel)
    ev = jnp.where(in_sel, logits, neg)
    t1 = jnp.max(ev, axis=1, keepdims=True)
    i1 = jnp.min(jnp.where(ev == t1, lane, LANES), axis=1, keepdims=True)
    ev2 = jnp.where(lane == i1, neg, ev)
    t2 = jnp.max(ev2, axis=1, keepdims=True)
    i2 = jnp.min(jnp.where(ev2 == t2, lane, LANES), axis=1, keepdims=True)
    e2 = jnp.exp(t2 - t1)
    den = 1.0 + e2
    w1 = (1.0 / den) * g_val
    w2 = (e2 / den) * g_val
    return (jnp.where(lane == 0, i1 - N_GROUPS, jnp.where(lane == 1, i2 - N_GROUPS, 0)),
            jnp.where(lane == 0, w1, jnp.where(lane == 1, w2, 0.0)))


def _outproj(oa, ob, x2, mod3, ga, gb, w_out, gm, wr, br):
    tiles_per_batch = SEQ // OUT_TM
    row = lambda i: (i, 0)
    const = lambda i: (0, 0)
    wr_hi = wr.astype(BF16)
    wr_lo = (wr - wr_hi.astype(F32)).astype(BF16)
    wr2 = jnp.concatenate([wr_hi, wr_lo], axis=1)
    return pl.pallas_call(
        _outproj_kernel,
        grid=(N_TOK // OUT_TM,),
        in_specs=[
            pl.BlockSpec((OUT_TM, SWA_WIDTH), row),
            pl.BlockSpec((OUT_TM, FOX_WIDTH), row),
            pl.BlockSpec((OUT_TM, D_MODEL), row),
            pl.BlockSpec((1, 6, D_MODEL), lambda i: (i // tiles_per_batch, 0, 0)),
            pl.BlockSpec((1, SWA_WIDTH), const),
            pl.BlockSpec((1, FOX_WIDTH), const),
            pl.BlockSpec((D_MODEL, D_MODEL), const),
            pl.BlockSpec((1, D_MODEL), const),
            pl.BlockSpec((D_MODEL, 2 * LANES), const),
            pl.BlockSpec((1, LANES), const),
        ],
        out_specs=[
            pl.BlockSpec((OUT_TM, D_MODEL), row),
            pl.BlockSpec((OUT_TM, PACK_ROWS, LANES), lambda i: (i, 0, 0)),
            pl.BlockSpec((OUT_TM, LANES), row),
            pl.BlockSpec((OUT_TM, LANES), row),
        ],
        out_shape=[
            jax.ShapeDtypeStruct((N_TOK, D_MODEL), F32),
            jax.ShapeDtypeStruct((N_TOK, PACK_ROWS, LANES), jnp.uint32),
            jax.ShapeDtypeStruct((N_TOK, LANES), jnp.int32),
            jax.ShapeDtypeStruct((N_TOK, LANES), F32),
        ],
        compiler_params=_cparams(("arbitrary",)),
        name="outproj_router",
    )(oa, ob, x2, mod3, ga, gb, w_out, gm, wr2, br)


N_BLOCKS = (N_TOK * TOP_K) // MOE_BLOCK + N_EXPERTS
P_ROWS = N_BLOCKS * MOE_BLOCK


def _pack_bf16_pairs(x):
    xb = x.astype(BF16).astype(F32)
    half = x.shape[1] // 2
    lo = lax.shift_right_logical(pltpu.bitcast(xb[:, :half], jnp.uint32), jnp.uint32(16))
    hi = pltpu.bitcast(xb[:, half:], jnp.uint32) & jnp.uint32(0xFFFF0000)
    return lo | hi


def _unpack_bf16_pairs(w):
    return (pltpu.bitcast(lax.shift_left(w, jnp.uint32(16)), F32),
            pltpu.bitcast(w & jnp.uint32(0xFFFF0000), F32))


def _expert_kernel(be_ref, tok_ref, src_ref, nused_ref, nxt_ref, h2_hbm, wg_hbm, wu_hbm, wd_hbm, y_hbm, xbuf, sem,
                   wgf, wuf, wdf, wsem, wgb, wub, wdb, ystage, osem):
    i = pl.program_id(0)
    nused = nused_ref[0]

    def weight_copies(e):
        return (pltpu.make_async_copy(wg_hbm.at[e], wgf, wsem.at[0]),
                pltpu.make_async_copy(wu_hbm.at[e], wuf, wsem.at[1]),
                pltpu.make_async_copy(wd_hbm.at[e], wdf, wsem.at[2]))

    def issue(blk, slot):
        def body(g, _):
            for q in range(SUBLANES):
                tok = tok_ref[jnp.minimum(src_ref[blk] + g * SUBLANES + q, N_TOK * TOP_K - 1)]
                pltpu.make_async_copy(h2_hbm.at[tok], xbuf.at[slot, g, :, q, :], sem.at[slot]).start(
                    priority=q % 2)
            return 0
        lax.fori_loop(0, MOE_BLOCK // SUBLANES, body, 0)

    def out_copies(blk, slot):
        r0 = pl.multiple_of(blk * MOE_BLOCK, MOE_BLOCK)
        return [pltpu.make_async_copy(ystage.at[slot, :, pl.ds(s * LANES, LANES)],
                                      y_hbm.at[pl.ds(r0, MOE_BLOCK), s, :], osem.at[slot])
                for s in range(PACK_ROWS)]

    def out_wait(slot):
        pltpu.make_async_copy(ystage.at[slot], ystage.at[slot], osem.at[slot]).wait()

    def gather_wait(slot):
        pltpu.make_async_copy(xbuf.at[slot], xbuf.at[slot], sem.at[slot]).wait()

    @pl.when(i == 0)
    def _():
        issue(0, 0)

    @pl.when(i + 1 < nused)
    def _():
        issue(i + 1, (i + 1) % 2)

    oslot = i % 2

    @pl.when(i >= 2)
    def _():
        out_wait(oslot)

    e = be_ref[i]
    new_expert = ((i == 0) | (e != be_ref[jnp.maximum(i - 1, 0)])) & (i < nused)

    @pl.when(i == 0)
    def _():
        for cp in weight_copies(e):
            cp.start()

    @pl.when(new_expert)
    def _():
        for cp in weight_copies(e):
            cp.wait()
        wgb[...] = wgf[...].astype(BF16)
        wub[...] = wuf[...].astype(BF16)
        wdb[...] = wdf[...].astype(BF16)

    @pl.when(new_expert & (nxt_ref[e] >= 0))
    def _():
        for cp in weight_copies(nxt_ref[e]):
            cp.start()

    @pl.when(i < nused)
    def _():
        slot = i % 2
        gather_wait(slot)
        gsub = MOE_SUB // SUBLANES
        xbs = []
        for k in range(MOE_BLOCK // MOE_SUB):
            xu = jnp.concatenate([xbuf[slot, k * gsub:(k + 1) * gsub, s].reshape(MOE_SUB, LANES)
                                  for s in range(PACK_ROWS)], axis=1)
            xbs.append(jnp.concatenate(_unpack_bf16_pairs(xu), axis=1).astype(BF16))
        gs = [jnp.dot(xb, wgb[...], preferred_element_type=F32) for xb in xbs]
        us = [jnp.dot(xb, wub[...], preferred_element_type=F32) for xb in xbs]
        hids = [(jax.nn.silu(g) * u).astype(BF16) for g, u in zip(gs, us)]
        ys = [jnp.dot(hid, wdb[...], preferred_element_type=F32) for hid in hids]
        for k, y in enumerate(ys):
            ystage[oslot, k * MOE_SUB:(k + 1) * MOE_SUB, :] = _pack_bf16_pairs(y)

    @pl.when(i >= nused)
    def _():
        ystage[oslot] = jnp.zeros((MOE_BLOCK, PACKED_COLS), jnp.uint32)

    for cp in out_copies(i, oslot):
        cp.start()

    @pl.when(i == pl.num_programs(0) - 1)
    def _():
        out_wait(oslot)
        out_wait(1 - oslot)


def _experts(block_e, tok_sorted, blk_src, nused, next_used, h2p, wg, wu, wd):
    grid_spec = pltpu.PrefetchScalarGridSpec(
        num_scalar_prefetch=5,
        grid=(N_BLOCKS,),
        in_specs=[pl.BlockSpec(memory_space=pl.ANY)] * 4,
        out_specs=pl.BlockSpec(memory_space=pl.ANY),
        scratch_shapes=[pltpu.VMEM((2, MOE_BLOCK // SUBLANES, PACK_ROWS, SUBLANES, LANES), jnp.uint32),
                        pltpu.SemaphoreType.DMA((2,)),
                        pltpu.VMEM((D_MODEL, D_EXPERT), F32),
                        pltpu.VMEM((D_MODEL, D_EXPERT), F32),
                        pltpu.VMEM((D_EXPERT, D_MODEL), F32),
                        pltpu.SemaphoreType.DMA((3,)),
                        pltpu.VMEM((D_MODEL, D_EXPERT), BF16),
                        pltpu.VMEM((D_MODEL, D_EXPERT), BF16),
                        pltpu.VMEM((D_EXPERT, D_MODEL), BF16),
                        pltpu.VMEM((2, MOE_BLOCK, PACKED_COLS), jnp.uint32),
                        pltpu.SemaphoreType.DMA((2,))],
    )
    return pl.pallas_call(
        _expert_kernel,
        grid_spec=grid_spec,
        out_shape=jax.ShapeDtypeStruct((P_ROWS, PACK_ROWS, LANES), jnp.uint32),
        compiler_params=_cparams(("arbitrary",)),
        name="expert_ffn",
    )(block_e, tok_sorted, blk_src, nused, next_used, h2p, wg, wu, wd)


CMB_TM = 256


def _combine_kernel(dest_ref, ys_hbm, x1_ref, gates_ref, mod_ref, fg_ref, o_ref, ybuf, sem):
    i = pl.program_id(0)
    n = pl.num_programs(0)
    groups = CMB_TM // SUBLANES

    def issue(tile, slot):
        def body(g, _):
            for q in range(SUBLANES):
                for k in range(TOP_K):
                    d = dest_ref[(tile * CMB_TM + g * SUBLANES + q) * TOP_K + k]
                    pltpu.make_async_copy(ys_hbm.at[d], ybuf.at[slot, k * groups + g, :, q, :],
                                          sem.at[slot]).start(priority=k)
            return 0
        lax.fori_loop(0, groups, body, 0)

    @pl.when(i == 0)
    def _():
        issue(0, 0)

    @pl.when(i + 1 < n)
    def _():
        issue(i + 1, (i + 1) % 2)

    slot = i % 2
    pltpu.make_async_copy(ybuf.at[slot], ybuf.at[slot], sem.at[slot]).wait()
    gts = gates_ref[...]
    w0 = gts[:, 0:1]
    w1 = gts[:, 1:2]
    half = D_MODEL // 2
    ssq = jnp.zeros((CMB_TM, 1), F32)
    for s in range(PACK_ROWS):
        y0 = _unpack_bf16_pairs(ybuf[slot, 0:groups, s].reshape(CMB_TM, LANES))
        y1 = _unpack_bf16_pairs(ybuf[slot, groups:2 * groups, s].reshape(CMB_TM, LANES))
        for part in range(2):
            cols = slice(part * half + s * LANES, part * half + (s + 1) * LANES)
            x2 = x1_ref[:, cols] + mod_ref[0, 5:6, cols] * (y0[part] * w0 + y1[part] * w1)
            ssq = ssq + jnp.sum(x2 * x2, axis=-1, keepdims=True)
            o_ref[:, cols] = x2
    o_ref[...] = (o_ref[...] * lax.rsqrt(ssq * (1.0 / D_MODEL) + EPS)) * fg_ref[...]


def _combine(dest, ys, x1, gates, mod3, final_g):
    tiles_per_batch = SEQ // CMB_TM
    grid_spec = pltpu.PrefetchScalarGridSpec(
        num_scalar_prefetch=1,
        grid=(N_TOK // CMB_TM,),
        in_specs=[
            pl.BlockSpec(memory_space=pl.ANY),
            pl.BlockSpec((CMB_TM, D_MODEL), lambda i, d: (i, 0)),
            pl.BlockSpec((CMB_TM, LANES), lambda i, d: (i, 0)),
            pl.BlockSpec((1, 6, D_MODEL), lambda i, d: (i // tiles_per_batch, 0, 0)),
            pl.BlockSpec((1, D_MODEL), lambda i, d: (0, 0)),
        ],
        out_specs=pl.BlockSpec((CMB_TM, D_MODEL), lambda i, d: (i, 0)),
        scratch_shapes=[pltpu.VMEM((2, TOP_K * CMB_TM // SUBLANES, PACK_ROWS, SUBLANES, LANES), jnp.uint32),
                        pltpu.SemaphoreType.DMA((2,))],
    )
    return pl.pallas_call(
        _combine_kernel,
        grid_spec=grid_spec,
        out_shape=jax.ShapeDtypeStruct((N_TOK, D_MODEL), F32),
        compiler_params=_cparams(("arbitrary",)),
        name="combine_final",
    )(dest, ys, x1, gates, mod3, final_g)


def _routing_tables(eid):
    ids = jnp.arange(N_EXPERTS, dtype=jnp.int32)
    onehot = (eid[None, :] == ids[:, None]).astype(jnp.int32)
    csum = jnp.cumsum(onehot, axis=1)
    counts = csum[:, -1]
    padded = (counts + MOE_BLOCK - 1) // MOE_BLOCK * MOE_BLOCK
    pend = jnp.cumsum(padded)
    pstart = pend - padded
    dest = (jnp.sum(onehot * (csum + pstart[:, None]), axis=0) - 1).astype(jnp.int32)
    blk_row = jnp.arange(N_BLOCKS, dtype=jnp.int32) * MOE_BLOCK
    block_e = jnp.minimum(jnp.sum((pend[None, :] <= blk_row[:, None]).astype(jnp.int32), axis=1),
                          N_EXPERTS - 1).astype(jnp.int32)
    tok_sorted = (jnp.argsort(eid, stable=True) // TOP_K).astype(jnp.int32)
    starts = jnp.cumsum(counts) - counts
    is_e = (block_e[:, None] == ids[None, :]).astype(jnp.int32)
    blk_src = (blk_row + jnp.sum(is_e * (starts - pstart)[None, :], axis=1)).astype(jnp.int32)
    nused = (pend[-1] // MOE_BLOCK).astype(jnp.int32).reshape(1)
    later_used = (ids[None, :] > ids[:, None]) & (counts[None, :] > 0)
    next_used = jnp.min(jnp.where(later_used, ids[None, :], N_EXPERTS), axis=1)
    next_used = jnp.where(next_used == N_EXPERTS, -1, next_used).astype(jnp.int32)
    return dest, tok_sorted, blk_src, block_e, nused, next_used


def kernel(x, c, w_ada, b_ada, norm_mix_g, w_in, b_forget, sinks, out_norm_swa_g, out_norm_fox_g, w_out,
           norm_moe_g, w_group, b_group, w_expert, b_expert, w_gate, w_up, w_down, final_g):
    assert IN_TM == FOX_T
    x2 = x.reshape(N_TOK, D_MODEL)

    w_in0 = w_in[0]
    w_row = w_in0[:, :VB_COL].astype(BF16)
    w_vt = w_in0[:, VB_COL:MAIN_COLS].astype(BF16)
    w_f = jnp.pad(w_in0[:, MAIN_COLS:], ((0, 0), (0, LANES - N_HEADS_FOX))).astype(BF16)
    colscale = jnp.concatenate([
        jnp.full((SWA_WIDTH,), HEAD_DIM ** -0.5 * LOG2E, F32),
        jnp.ones((2 * KV_WIDTH,), F32),
        jnp.full((FOX_WIDTH,), HEAD_DIM ** -0.5 * LOG2E, F32),
        jnp.ones((FOX_WIDTH,), F32)]).reshape(1, ROW_COLS)
    bf_row = jnp.pad(b_forget[0], (0, LANES - N_HEADS_FOX)).reshape(1, LANES)
    w_out_p = w_out[0].astype(BF16)
    ga = out_norm_swa_g[0].reshape(1, SWA_WIDTH)
    gb = out_norm_fox_g[0].reshape(1, FOX_WIDTH)
    wr = jnp.pad(jnp.concatenate([w_group[0], w_expert[0]], axis=1),
                 ((0, 0), (0, LANES - N_GROUPS - N_EXPERTS)))
    br = jnp.pad(jnp.concatenate([b_group[0], b_expert[0]]), (0, LANES - N_GROUPS - N_EXPERTS)).reshape(1, LANES)
    wg, wu, wd = w_gate[0], w_up[0], w_down[0]

    mod = _adaln(c, w_ada[0], b_ada[0])
    mod3 = mod.reshape(BATCH, 6, D_MODEL)

    proj, f, vt5 = _inproj(x2, mod3, norm_mix_g[0].reshape(1, D_MODEL), w_row, w_f, colscale, w_vt)
    aug = _forget_bias(f, bf_row)

    o_a = _swa(sinks[0], proj)
    o_b = _fox(proj, aug, vt5)

    x1, h2, ids, gates = _outproj(o_a, o_b, x2, mod3, ga, gb, w_out_p,
                                  norm_moe_g[0].reshape(1, D_MODEL), wr, br)

    eid = ids[:, :TOP_K].reshape(-1)
    dest, tok_sorted, blk_src, block_e, nused, next_used = _routing_tables(eid)
    ys = _experts(block_e, tok_sorted, blk_src, nused, next_used, h2, wg, wu, wd)
    out = _combine(dest, ys, x1, gates, mod3, final_g.reshape(1, D_MODEL))
    return out.reshape(BATCH, SEQ, D_MODEL)
```

```python
import math

import numpy as np
import jax
import jax.numpy as jnp
from jax import lax
from jax.experimental import pallas as pl
from jax.experimental.pallas import tpu as pltpu

F32 = jnp.float32
BF16 = jnp.bfloat16

D_MODEL = 2048
BATCH = 2
SEQ = 8192
N_TOK = BATCH * SEQ
HEAD_DIM = 64
N_HEADS_SWA = 16
N_KV_SWA = 4
N_HEADS_FOX = 16
WINDOW = 128
SWA_WIDTH = N_HEADS_SWA * HEAD_DIM
KV_WIDTH = N_KV_SWA * HEAD_DIM
FOX_WIDTH = N_HEADS_FOX * HEAD_DIM
MAIN_COLS = SWA_WIDTH + 2 * KV_WIDTH + 3 * FOX_WIDTH
N_GROUPS = 4
EXPERTS_PER_GROUP = 8
N_EXPERTS = N_GROUPS * EXPERTS_PER_GROUP
TOP_K = 2
D_EXPERT = 512
MOE_BLOCK = 256
MOE_SUB = 128
EPS = 1e-6
LOG2E = math.log2(math.e)

LANES = 128
SUBLANES = 8
VMEM_LIMIT = 56 * 1024 * 1024

KA_COL = SWA_WIDTH
VA_COL = SWA_WIDTH + KV_WIDTH
QB_BLK = (SWA_WIDTH + 2 * KV_WIDTH) // LANES
KB_BLK = QB_BLK + FOX_WIDTH // LANES
VB_COL = SWA_WIDTH + 2 * KV_WIDTH + 2 * FOX_WIDTH
ROW_COLS = VB_COL


def _cparams(sem, vmem=VMEM_LIMIT):
    return pltpu.CompilerParams(dimension_semantics=sem, vmem_limit_bytes=vmem)


ADA_TN = 2048


def _adaln_kernel(cb_ref, w_ref, b_ref, o_ref):
    for b in range(BATCH):
        s = jax.nn.silu(cb_ref[b])
        cols = []
        for j in range(ADA_TN // LANES):
            prod = w_ref[:, j * LANES:(j + 1) * LANES] * s
            cols.append(jnp.sum(prod, axis=0, keepdims=True))
        o_ref[b:b + 1, :] = jnp.concatenate(cols, axis=1) + b_ref[...]


def _adaln(c, w_ada, b_ada):
    ncol = w_ada.shape[1]
    cb = jnp.broadcast_to(c[:, :, None], (BATCH, D_MODEL, LANES))
    return pl.pallas_call(
        _adaln_kernel,
        grid=(ncol // ADA_TN,),
        in_specs=[
            pl.BlockSpec((BATCH, D_MODEL, LANES), lambda j: (0, 0, 0)),
            pl.BlockSpec((D_MODEL, ADA_TN), lambda j: (0, j)),
            pl.BlockSpec((1, ADA_TN), lambda j: (0, j)),
        ],
        out_specs=pl.BlockSpec((BATCH, ADA_TN), lambda j: (0, j)),
        out_shape=jax.ShapeDtypeStruct((BATCH, ncol), F32),
        compiler_params=_cparams(("arbitrary",)),
        name="adaln",
    )(cb, w_ada, b_ada.reshape(1, ncol))


def _rms(x):
    return x * lax.rsqrt(jnp.mean(x * x, axis=-1, keepdims=True) + EPS)


IN_TM = 512
IN_TN = 1792
IN_NJ = ROW_COLS // IN_TN


def _inproj_kernel(x_ref, mod_ref, g_ref, w_ref, wf_ref, cs_ref, wvt_ref, o_ref, f_ref, vt_ref, h_ref):
    j = pl.program_id(1)

    @pl.when(j == 0)
    def _():
        y = _rms(x_ref[...])
        h = (y * g_ref[...]) * (1.0 + mod_ref[0, 1:2, :]) + mod_ref[0, 0:1, :]
        hb = h.astype(BF16)
        h_ref[...] = hb
        f_ref[...] = jnp.dot(hb, wf_ref[...], preferred_element_type=F32)

    @pl.when(j < IN_NJ)
    def _():
        acc = jnp.dot(h_ref[...], w_ref[...], preferred_element_type=F32)
        o_ref[...] = (acc * cs_ref[...]).astype(BF16)

    @pl.when(j == IN_NJ)
    def _():
        vt = lax.dot_general(wvt_ref[...], h_ref[...], (((0,), (1,)), ((), ())),
                             preferred_element_type=F32)
        vt_ref[0, :, 0] = vt.reshape(N_HEADS_FOX // 2, LANES, IN_TM).astype(BF16)


def _inproj(x2, mod3, g, w_row, w_f, colscale, w_vt):
    tiles_per_batch = SEQ // IN_TM
    last = IN_NJ - 1
    return pl.pallas_call(
        _inproj_kernel,
        grid=(N_TOK // IN_TM, IN_NJ + 1),
        in_specs=[
            pl.BlockSpec((IN_TM, D_MODEL), lambda i, j: (i, 0)),
            pl.BlockSpec((1, 6, D_MODEL), lambda i, j: (i // tiles_per_batch, 0, 0)),
            pl.BlockSpec((1, D_MODEL), lambda i, j: (0, 0)),
            pl.BlockSpec((D_MODEL, IN_TN), lambda i, j: (0, jnp.minimum(j, last))),
            pl.BlockSpec((D_MODEL, LANES), lambda i, j: (0, 0)),
            pl.BlockSpec((1, IN_TN), lambda i, j: (0, jnp.minimum(j, last))),
            pl.BlockSpec((D_MODEL, FOX_WIDTH), lambda i, j: (0, 0)),
        ],
        out_specs=[
            pl.BlockSpec((IN_TM, IN_TN), lambda i, j: (i, jnp.minimum(j, last))),
            pl.BlockSpec((IN_TM, LANES), lambda i, j: (i, 0)),
            pl.BlockSpec((1, N_HEADS_FOX // 2, 1, LANES, IN_TM),
                         lambda i, j: (i // tiles_per_batch, 0, i % tiles_per_batch, 0, 0)),
        ],
        out_shape=[
            jax.ShapeDtypeStruct((N_TOK, ROW_COLS), BF16),
            jax.ShapeDtypeStruct((N_TOK, LANES), F32),
            jax.ShapeDtypeStruct((BATCH, N_HEADS_FOX // 2, SEQ // IN_TM, LANES, IN_TM), BF16),
        ],
        scratch_shapes=[pltpu.VMEM((IN_TM, D_MODEL), BF16)],
        compiler_params=_cparams(("arbitrary", "arbitrary")),
        name="inproj",
    )(x2, mod3, g, w_row, w_f, colscale, w_vt)


CUM_TS = 512
FOX_AUG = 3


def _cum_kernel(f_ref, bf_ref, tri_ref, sel_ref, aug_ref, carry_ref):
    @pl.when(pl.program_id(1) == 0)
    def _():
        carry_ref[...] = jnp.zeros_like(carry_ref)

    lf = jax.nn.log_sigmoid(f_ref[...] + bf_ref[...])
    l_hi = lf.astype(BF16)
    l_r = lf - l_hi.astype(F32)
    l_mid = l_r.astype(BF16)
    l_lo = (l_r - l_mid.astype(F32)).astype(BF16)
    c3 = jnp.dot(tri_ref[...], jnp.concatenate([l_hi, l_mid, l_lo], axis=1), preferred_element_type=F32)
    cs = (c3[:, :LANES] + (c3[:, LANES:2 * LANES] + c3[:, 2 * LANES:])) + carry_ref[...]
    carry_ref[...] = cs[CUM_TS - 1:CUM_TS, :]
    bias = cs * (-LOG2E)
    hi = bias.astype(BF16)
    r1 = bias - hi.astype(F32)
    mid = r1.astype(BF16)
    lo = (r1 - mid.astype(F32)).astype(BF16)
    pieces = jnp.concatenate([hi, mid, lo], axis=1)
    aug_ref[...] = jnp.dot(pieces, sel_ref[...], preferred_element_type=F32).astype(BF16)


def _aug_selector():
    sel = np.zeros((FOX_AUG * LANES, FOX_WIDTH), np.float32)
    for head in range(N_HEADS_FOX):
        for piece in range(FOX_AUG):
            sel[piece * LANES + head, (head // 2) * LANES + FOX_AUG * (head % 2) + piece] = 1.0
    return sel


def _forget_bias(f, b_forget_row):
    tri = jnp.tril(jnp.ones((CUM_TS, CUM_TS), BF16))
    sel = jnp.asarray(_aug_selector(), BF16)
    nblk = SEQ // CUM_TS
    return pl.pallas_call(
        _cum_kernel,
        grid=(BATCH, nblk),
        in_specs=[
            pl.BlockSpec((CUM_TS, LANES), lambda b, i: (b * nblk + i, 0)),
            pl.BlockSpec((1, LANES), lambda b, i: (0, 0)),
            pl.BlockSpec((CUM_TS, CUM_TS), lambda b, i: (0, 0)),
            pl.BlockSpec((FOX_AUG * LANES, FOX_WIDTH), lambda b, i: (0, 0)),
        ],
        out_specs=pl.BlockSpec((CUM_TS, FOX_WIDTH), lambda b, i: (b * nblk + i, 0)),
        out_shape=jax.ShapeDtypeStruct((N_TOK, FOX_WIDTH), BF16),
        scratch_shapes=[pltpu.VMEM((1, LANES), F32)],
        compiler_params=_cparams(("arbitrary", "arbitrary")),
        name="forget_bias",
    )(f, b_forget_row, tri, sel)


SWA_TQ = WINDOW
SWA_GROUP = 4


def _swa_kernel(sinks_ref, q_ref, kc_ref, kp_ref, vc_ref, vp_ref, o_ref, bias_ref):
    i = pl.program_id(1)
    tq = SWA_TQ

    @pl.when((pl.program_id(0) == 0) & (i == 0))
    def _():
        row = lax.broadcasted_iota(jnp.int32, (tq, 2 * tq), 0)
        col = lax.broadcasted_iota(jnp.int32, (tq, 2 * tq), 1)
        dist = row + tq - col
        band = (dist >= 0) & (dist < WINDOW)
        distf = dist.astype(F32)
        for head in range(N_HEADS_SWA):
            slope2 = float(2.0 ** (-8.0 * (head + 1) / N_HEADS_SWA)) * LOG2E
            base = jnp.where(band, -slope2 * distf, -jnp.inf)
            bias_ref[1, head] = base
            bias_ref[0, head] = jnp.where(col >= tq, base, -jnp.inf)

    table = jnp.minimum(i, 1)
    lane = lax.broadcasted_iota(jnp.int32, (tq, LANES), 1)
    lo_half = lane < HEAD_DIM
    for t in range(2):
        kt = jnp.concatenate([kp_ref[:, t * LANES:(t + 1) * LANES],
                              kc_ref[:, t * LANES:(t + 1) * LANES]], axis=0)
        vt = jnp.concatenate([vp_ref[:, t * LANES:(t + 1) * LANES],
                              vc_ref[:, t * LANES:(t + 1) * LANES]], axis=0)
        for g0 in range(0, 4, SWA_GROUP):
            heads = [(g, e) for g in range(g0, g0 + SWA_GROUP) for e in range(2)]
            scores = []
            for g, e in heads:
                tile, f = t * 4 + g, g // 2
                qt = q_ref[:, tile * LANES:(tile + 1) * LANES]
                if e != f:
                    qt = jnp.concatenate([qt[:, HEAD_DIM:], qt[:, :HEAD_DIM]], axis=1)
                qm = jnp.where(lo_half if f == 0 else ~lo_half, qt, jnp.zeros_like(qt))
                scores.append(lax.dot_general(qm, kt, (((1,), (1,)), ((), ())),
                                              preferred_element_type=F32))
            probs, rdens = [], []
            for (g, e), s in zip(heads, scores):
                head = 2 * (t * 4 + g) + e
                sink = sinks_ref[head] * LOG2E
                s = s + bias_ref[table, head]
                m = jnp.maximum(jnp.max(s, axis=-1, keepdims=True), sink)
                p = jnp.exp2(s - m)
                rdens.append(1.0 / (jnp.sum(p, axis=-1, keepdims=True) + jnp.exp2(sink - m)))
                probs.append(p.astype(BF16))
            outs = [jnp.dot(p, vt, preferred_element_type=F32) * r for p, r in zip(probs, rdens)]
            for k, g in enumerate(range(g0, g0 + SWA_GROUP)):
                tile, f = t * 4 + g, g // 2
                same, other = outs[2 * k + f], pltpu.roll(outs[2 * k + 1 - f], HEAD_DIM, axis=1)
                o_ref[:, tile * LANES:(tile + 1) * LANES] = jnp.where(lo_half == (f == 0), same,
                                                                      other).astype(BF16)


def _swa(sinks, proj):
    nq = SEQ // SWA_TQ
    kblk = KA_COL // KV_WIDTH
    vblk = VA_COL // KV_WIDTH
    grid_spec = pltpu.PrefetchScalarGridSpec(
        num_scalar_prefetch=1,
        grid=(BATCH, nq),
        in_specs=[
            pl.BlockSpec((SWA_TQ, SWA_WIDTH), lambda b, i, s: (b * nq + i, 0)),
            pl.BlockSpec((SWA_TQ, KV_WIDTH), lambda b, i, s: (b * nq + i, kblk)),
            pl.BlockSpec((SWA_TQ, KV_WIDTH), lambda b, i, s: (b * nq + jnp.maximum(i - 1, 0), kblk)),
            pl.BlockSpec((SWA_TQ, KV_WIDTH), lambda b, i, s: (b * nq + i, vblk)),
            pl.BlockSpec((SWA_TQ, KV_WIDTH), lambda b, i, s: (b * nq + jnp.maximum(i - 1, 0), vblk)),
        ],
        out_specs=pl.BlockSpec((SWA_TQ, SWA_WIDTH), lambda b, i, s: (b * nq + i, 0)),
        scratch_shapes=[pltpu.VMEM((2, N_HEADS_SWA, SWA_TQ, 2 * SWA_TQ), F32)],
    )
    return pl.pallas_call(
        _swa_kernel,
        grid_spec=grid_spec,
        out_shape=jax.ShapeDtypeStruct((N_TOK, SWA_WIDTH), BF16),
        compiler_params=_cparams(("arbitrary", "arbitrary")),
        name="swa_attn",
    )(sinks, proj, proj, proj, proj, proj)


FOX_T = 512
FOX_ONES = 16
FOX_VROWS = HEAD_DIM + FOX_ONES


def _fox_kernel(q_ref, k_ref, aug_ref, vt_ref, o_ref, mask_ref, t_ref, acc_ref):
    qi = pl.program_id(2)
    t = FOX_T

    @pl.when(qi == 0)
    def _():
        kr = lax.broadcasted_iota(jnp.int32, (t, t), 0)
        qc = lax.broadcasted_iota(jnp.int32, (t, t), 1)
        mask_ref[...] = jnp.where(kr <= qc, 0.0, -jnp.inf)

    qf = q_ref[...].astype(F32).T
    drow = lax.broadcasted_iota(jnp.int32, (LANES, t), 0)
    qaug = []
    for hh in range(2):
        qh = jnp.where((drow >= HEAD_DIM * hh) & (drow < HEAD_DIM * (hh + 1)), qf, 0.0)
        ones = jnp.where((drow >= FOX_AUG * hh) & (drow < FOX_AUG * (hh + 1)), 1.0, 0.0)
        qaug.append(jnp.concatenate([qh, ones], axis=0).astype(BF16))
    acc_ref[...] = jnp.zeros_like(acc_ref)

    def stage_a(j, slot, diag):
        r0 = pl.multiple_of(j * t, t)
        kb = jnp.concatenate([k_ref[pl.ds(r0, t), :], aug_ref[pl.ds(r0, t), :]], axis=1)
        sts = [jnp.dot(kb, qaug[hh], preferred_element_type=F32) for hh in range(2)]
        mbs = []
        for hh in range(2):
            tt = sts[hh]
            if diag:
                tt = tt + mask_ref[...]
            t_ref[slot, hh] = tt
            mbs.append(jnp.max(tt, axis=0, keepdims=True))
        return tuple(mbs)

    def stage_b(jv, slot, ms, mbs):
        new, ps, alphas = [], [], []
        for hh in range(2):
            m_new = jnp.maximum(ms[hh], mbs[hh])
            alphas.append(jnp.exp2(ms[hh] - m_new))
            ps.append(jnp.exp2(t_ref[slot, hh] - m_new).astype(BF16))
            new.append(m_new)
        ones = jnp.ones((FOX_ONES, t), BF16)
        pvs = [jnp.dot(jnp.concatenate([vt_ref[0, 0, jv, hh * HEAD_DIM:(hh + 1) * HEAD_DIM, :], ones], axis=0),
                       ps[hh], preferred_element_type=F32)
               for hh in range(2)]
        for hh in range(2):
            acc_ref[hh] = alphas[hh] * acc_ref[hh] + pvs[hh]
        return tuple(new)

    ml0 = tuple(jnp.full((1, t), -jnp.inf, F32) for _ in range(2))
    mb0 = stage_a(qi, 0, True)

    def pair(j0, c):
        ml, mbs, jprev = c
        mb1 = stage_a(j0, 1, False)
        ml = stage_b(jprev, 0, ml, mbs)
        mb2 = stage_a(j0 + 1, 0, False)
        ml = stage_b(j0, 1, ml, mb1)
        return ml, mb2, j0 + 1

    def quad(ii, c):
        return pair(4 * ii + 2, pair(4 * ii, c))

    carry = lax.fori_loop(0, qi // 4, quad, (ml0, mb0, qi))
    base = 4 * (qi // 4)

    def tail(rem):
        def run(c):
            ml, mb_prev, prev = c
            for k in range(rem):
                mb_new = stage_a(base + k, (k + 1) % 2, False)
                ml = stage_b(prev, k % 2, ml, mb_prev)
                prev, mb_prev = base + k, mb_new
            stage_b(prev, rem % 2, ml, mb_prev)
            ot = jnp.concatenate([acc_ref[hh, 0:HEAD_DIM, :] / acc_ref[hh, HEAD_DIM:HEAD_DIM + 1, :]
                                  for hh in range(2)], axis=0)
            o_ref[...] = ot.T.astype(BF16)
            return jnp.int32(0)
        return run

    lax.switch(qi % 4, [tail(rem) for rem in range(4)], carry)


def _fox(proj, aug, vt5):
    nq = SEQ // FOX_T
    npair = N_HEADS_FOX // 2
    return pl.pallas_call(
        _fox_kernel,
        grid=(BATCH, npair, nq),
        in_specs=[
            pl.BlockSpec((FOX_T, LANES), lambda b, h, i: (b * nq + i, QB_BLK + h)),
            pl.BlockSpec((SEQ, LANES), lambda b, h, i: (b, KB_BLK + h)),
            pl.BlockSpec((SEQ, LANES), lambda b, h, i: (b, h)),
            pl.BlockSpec((1, 1, nq, LANES, FOX_T), lambda b, h, i: (b, h, 0, 0, 0)),
        ],
        out_specs=pl.BlockSpec((FOX_T, LANES), lambda b, h, i: (b * nq + i, h)),
        out_shape=jax.ShapeDtypeStruct((N_TOK, FOX_WIDTH), BF16),
        scratch_shapes=[
            pltpu.VMEM((FOX_T, FOX_T), F32),
            pltpu.VMEM((2, 2, FOX_T, FOX_T), F32),
            pltpu.VMEM((2, FOX_VROWS, FOX_T), F32),
        ],
        compiler_params=_cparams(("arbitrary", "arbitrary", "arbitrary")),
        name="fox_attn",
    )(proj, proj, aug, vt5)


OUT_TM = 512
OUT_SUB = 256
PACKED_COLS = D_MODEL // 2
PACK_ROWS = PACKED_COLS // LANES


def _outproj_kernel(oa_ref, ob_ref, x_ref, mod_ref, ga_ref, gb_ref, w_ref, gm_ref, wr2_ref, br_ref,
                    x1_ref, h2p_ref, ids_ref, gates_ref):
    subs = [slice(k * OUT_SUB, (k + 1) * OUT_SUB) for k in range(OUT_TM // OUT_SUB)]
    mixed = [jnp.concatenate([_rms(oa_ref[r, :].astype(F32)) * ga_ref[...],
                              _rms(ob_ref[r, :].astype(F32)) * gb_ref[...]], axis=1).astype(BF16) for r in subs]
    ys = [jnp.dot(m, w_ref[...], preferred_element_type=F32) for m in mixed]
    h2s = []
    for r, y in zip(subs, ys):
        x1 = x_ref[r, :] + mod_ref[0, 2:3, :] * y
        x1_ref[r, :] = x1
        h2s.append((_rms(x1) * gm_ref[...]) * (1.0 + mod_ref[0, 4:5, :]) + mod_ref[0, 3:4, :])
    hbs = [h2.astype(BF16) for h2 in h2s]
    for r, h2 in zip(subs, h2s):
        packed = _pack_bf16_pairs(h2)
        for s in range(PACK_ROWS):
            h2p_ref[r, s, :] = packed[:, s * LANES:(s + 1) * LANES]

    hls = [(h2 - hb.astype(F32)).astype(BF16) for h2, hb in zip(h2s, hbs)]
    r2s = [jnp.dot(hb, wr2_ref[...], preferred_element_type=F32) for hb in hbs]
    r3s = [jnp.dot(hl, wr2_ref[:, :LANES], preferred_element_type=F32) for hl in hls]
    for r, r2, r3 in zip(subs, r2s, r3s):
        logits = (r2[:, :LANES] + (r2[:, LANES:] + r3)) + br_ref[...]
        ids, gates = _route(logits)
        ids_ref[r, :] = ids
        gates_ref[r, :] = gates


def _route(logits):
    lane = lax.broadcasted_iota(jnp.int32, logits.shape, 1)
    neg = -jnp.inf
    is_g = lane < N_GROUPS
    gl = jnp.where(is_g, logits, neg)
    gmax = jnp.max(gl, axis=1, keepdims=True)
    gsel = jnp.min(jnp.where(gl == gmax, lane, LANES), axis=1, keepdims=True)
    gsum = jnp.sum(jnp.where(is_g, jnp.exp(gl - gmax), 0.0), axis=1, keepdims=True)
    g_val = 1.0 / gsum
    elane = lane - N_GROUPS
    in_sel = (elane >= 0) & (elane < N_EXPERTS) & ((elane >> 3) == gsel)
    ev = jnp.where(in_sel, logits, neg)
    t1 = jnp.max(ev, axis=1, keepdims=True)
    i1 = jnp.min(jnp.where(ev == t1, lane, LANES), axis=1, keepdims=True)
    ev2 = jnp.where(lane == i1, neg, ev)
    t2 = jnp.max(ev2, axis=1, keepdims=True)
    i2 = jnp.min(jnp.where(ev2 == t2, lane, LANES), axis=1, keepdims=True)
    e2 = jnp.exp(t2 - t1)
    den = 1.0 + e2
    w1 = (1.0 / den) * g_val
    w2 = (e2 / den) * g_val
    return (jnp.where(lane == 0, i1 - N_GROUPS, jnp.where(lane == 1, i2 - N_GROUPS, 0)),
            jnp.where(lane == 0, w1, jnp.where(lane == 1, w2, 0.0)))


def _outproj(oa, ob, x2, mod3, ga, gb, w_out, gm, wr, br):
    tiles_per_batch = SEQ // OUT_TM
    row = lambda i: (i, 0)
    const = lambda i: (0, 0)
    wr_hi = wr.astype(BF16)
    wr_lo = (wr - wr_hi.astype(F32)).astype(BF16)
    wr2 = jnp.concatenate([wr_hi, wr_lo], axis=1)
    return pl.pallas_call(
        _outproj_kernel,
        grid=(N_TOK // OUT_TM,),
        in_specs=[
            pl.BlockSpec((OUT_TM, SWA_WIDTH), row),
            pl.BlockSpec((OUT_TM, FOX_WIDTH), row),
            pl.BlockSpec((OUT_TM, D_MODEL), row),
            pl.BlockSpec((1, 6, D_MODEL), lambda i: (i // tiles_per_batch, 0, 0)),
            pl.BlockSpec((1, SWA_WIDTH), const),
            pl.BlockSpec((1, FOX_WIDTH), const),
            pl.BlockSpec((D_MODEL, D_MODEL), const),
            pl.BlockSpec((1, D_MODEL), const),
            pl.BlockSpec((D_MODEL, 2 * LANES), const),
            pl.BlockSpec((1, LANES), const),
        ],
        out_specs=[
            pl.BlockSpec((OUT_TM, D_MODEL), row),
            pl.BlockSpec((OUT_TM, PACK_ROWS, LANES), lambda i: (i, 0, 0)),
            pl.BlockSpec((OUT_TM, LANES), row),
            pl.BlockSpec((OUT_TM, LANES), row),
        ],
        out_shape=[
            jax.ShapeDtypeStruct((N_TOK, D_MODEL), F32),
            jax.ShapeDtypeStruct((N_TOK, PACK_ROWS, LANES), jnp.uint32),
            jax.ShapeDtypeStruct((N_TOK, LANES), jnp.int32),
            jax.ShapeDtypeStruct((N_TOK, LANES), F32),
        ],
        compiler_params=_cparams(("arbitrary",)),
        name="outproj_router",
    )(oa, ob, x2, mod3, ga, gb, w_out, gm, wr2, br)


N_BLOCKS = (N_TOK * TOP_K) // MOE_BLOCK + N_EXPERTS
P_ROWS = N_BLOCKS * MOE_BLOCK


def _pack_bf16_pairs(x):
    xb = x.astype(BF16).astype(F32)
    half = x.shape[1] // 2
    lo = lax.shift_right_logical(pltpu.bitcast(xb[:, :half], jnp.uint32), jnp.uint32(16))
    hi = pltpu.bitcast(xb[:, half:], jnp.uint32) & jnp.uint32(0xFFFF0000)
    return lo | hi


def _unpack_bf16_pairs(w):
    return (pltpu.bitcast(lax.shift_left(w, jnp.uint32(16)), F32),
            pltpu.bitcast(w & jnp.uint32(0xFFFF0000), F32))


def _expert_kernel(be_ref, tok_ref, src_ref, nused_ref, nxt_ref, h2_hbm, wg_hbm, wu_hbm, wd_hbm, y_hbm, xbuf, sem,
                   wgf, wuf, wdf, wsem, wgb, wub, wdb, ystage, osem):
    i = pl.program_id(0)
    nused = nused_ref[0]

    def weight_copies(e):
        return (pltpu.make_async_copy(wg_hbm.at[e], wgf, wsem.at[0]),
                pltpu.make_async_copy(wu_hbm.at[e], wuf, wsem.at[1]),
                pltpu.make_async_copy(wd_hbm.at[e], wdf, wsem.at[2]))

    def issue(blk, slot):
        def body(g, _):
            for q in range(SUBLANES):
                tok = tok_ref[jnp.minimum(src_ref[blk] + g * SUBLANES + q, N_TOK * TOP_K - 1)]
                pltpu.make_async_copy(h2_hbm.at[tok], xbuf.at[slot, g, :, q, :], sem.at[slot]).start(
                    priority=q % 2)
            return 0
        lax.fori_loop(0, MOE_BLOCK // SUBLANES, body, 0)

    def out_copies(blk, slot):
        r0 = pl.multiple_of(blk * MOE_BLOCK, MOE_BLOCK)
        return [pltpu.make_async_copy(ystage.at[slot, :, pl.ds(s * LANES, LANES)],
                                      y_hbm.at[pl.ds(r0, MOE_BLOCK), s, :], osem.at[slot])
                for s in range(PACK_ROWS)]

    def out_wait(slot):
        pltpu.make_async_copy(ystage.at[slot], ystage.at[slot], osem.at[slot]).wait()

    def gather_wait(slot):
        pltpu.make_async_copy(xbuf.at[slot], xbuf.at[slot], sem.at[slot]).wait()

    @pl.when(i == 0)
    def _():
        issue(0, 0)

    @pl.when(i + 1 < nused)
    def _():
        issue(i + 1, (i + 1) % 2)

    oslot = i % 2

    @pl.when(i >= 2)
    def _():
        out_wait(oslot)

    e = be_ref[i]
    new_expert = ((i == 0) | (e != be_ref[jnp.maximum(i - 1, 0)])) & (i < nused)

    @pl.when(i == 0)
    def _():
        for cp in weight_copies(e):
            cp.start()

    @pl.when(new_expert)
    def _():
        for cp in weight_copies(e):
            cp.wait()
        wgb[...] = wgf[...].astype(BF16)
        wub[...] = wuf[...].astype(BF16)
        wdb[...] = wdf[...].astype(BF16)

    @pl.when(new_expert & (nxt_ref[e] >= 0))
    def _():
        for cp in weight_copies(nxt_ref[e]):
            cp.start()

    @pl.when(i < nused)
    def _():
        slot = i % 2
        gather_wait(slot)
        gsub = MOE_SUB // SUBLANES
        xbs = []
        for k in range(MOE_BLOCK // MOE_SUB):
            xu = jnp.concatenate([xbuf[slot, k * gsub:(k + 1) * gsub, s].reshape(MOE_SUB, LANES)
                                  for s in range(PACK_ROWS)], axis=1)
            xbs.append(jnp.concatenate(_unpack_bf16_pairs(xu), axis=1).astype(BF16))
        gs = [jnp.dot(xb, wgb[...], preferred_element_type=F32) for xb in xbs]
        us = [jnp.dot(xb, wub[...], preferred_element_type=F32) for xb in xbs]
        hids = [(jax.nn.silu(g) * u).astype(BF16) for g, u in zip(gs, us)]
        ys = [jnp.dot(hid, wdb[...], preferred_element_type=F32) for hid in hids]
        for k, y in enumerate(ys):
            ystage[oslot, k * MOE_SUB:(k + 1) * MOE_SUB, :] = _pack_bf16_pairs(y)

    @pl.when(i >= nused)
    def _():
        ystage[oslot] = jnp.zeros((MOE_BLOCK, PACKED_COLS), jnp.uint32)

    for cp in out_copies(i, oslot):
        cp.start()

    @pl.when(i == pl.num_programs(0) - 1)
    def _():
        out_wait(oslot)
        out_wait(1 - oslot)


def _experts(block_e, tok_sorted, blk_src, nused, next_used, h2p, wg, wu, wd):
    grid_spec = pltpu.PrefetchScalarGridSpec(
        num_scalar_prefetch=5,
        grid=(N_BLOCKS,),
        in_specs=[pl.BlockSpec(memory_space=pl.ANY)] * 4,
        out_specs=pl.BlockSpec(memory_space=pl.ANY),
        scratch_shapes=[pltpu.VMEM((2, MOE_BLOCK // SUBLANES, PACK_ROWS, SUBLANES, LANES), jnp.uint32),
                        pltpu.SemaphoreType.DMA((2,)),
                        pltpu.VMEM((D_MODEL, D_EXPERT), F32),
                        pltpu.VMEM((D_MODEL, D_EXPERT), F32),
                        pltpu.VMEM((D_EXPERT, D_MODEL), F32),
                        pltpu.SemaphoreType.DMA((3,)),
                        pltpu.VMEM((D_MODEL, D_EXPERT), BF16),
                        pltpu.VMEM((D_MODEL, D_EXPERT), BF16),
                        pltpu.VMEM((D_EXPERT, D_MODEL), BF16),
                        pltpu.VMEM((2, MOE_BLOCK, PACKED_COLS), jnp.uint32),
                        pltpu.SemaphoreType.DMA((2,))],
    )
    return pl.pallas_call(
        _expert_kernel,
        grid_spec=grid_spec,
        out_shape=jax.ShapeDtypeStruct((P_ROWS, PACK_ROWS, LANES), jnp.uint32),
        compiler_params=_cparams(("arbitrary",)),
        name="expert_ffn",
    )(block_e, tok_sorted, blk_src, nused, next_used, h2p, wg, wu, wd)


CMB_TM = 256


def _combine_kernel(dest_ref, ys_hbm, x1_ref, gates_ref, mod_ref, fg_ref, o_ref, ybuf, sem):
    i = pl.program_id(0)
    n = pl.num_programs(0)
    groups = CMB_TM // SUBLANES

    def issue(tile, slot):
        def body(g, _):
            for q in range(SUBLANES):
                for k in range(TOP_K):
                    d = dest_ref[(tile * CMB_TM + g * SUBLANES + q) * TOP_K + k]
                    pltpu.make_async_copy(ys_hbm.at[d], ybuf.at[slot, k * groups + g, :, q, :],
                                          sem.at[slot]).start(priority=k)
            return 0
        lax.fori_loop(0, groups, body, 0)

    @pl.when(i == 0)
    def _():
        issue(0, 0)

    @pl.when(i + 1 < n)
    def _():
        issue(i + 1, (i + 1) % 2)

    slot = i % 2
    pltpu.make_async_copy(ybuf.at[slot], ybuf.at[slot], sem.at[slot]).wait()
    gts = gates_ref[...]
    w0 = gts[:, 0:1]
    w1 = gts[:, 1:2]
    half = D_MODEL // 2
    ssq = jnp.zeros((CMB_TM, 1), F32)
    for s in range(PACK_ROWS):
        y0 = _unpack_bf16_pairs(ybuf[slot, 0:groups, s].reshape(CMB_TM, LANES))
        y1 = _unpack_bf16_pairs(ybuf[slot, groups:2 * groups, s].reshape(CMB_TM, LANES))
        for part in range(2):
            cols = slice(part * half + s * LANES, part * half + (s + 1) * LANES)
            x2 = x1_ref[:, cols] + mod_ref[0, 5:6, cols] * (y0[part] * w0 + y1[part] * w1)
            ssq = ssq + jnp.sum(x2 * x2, axis=-1, keepdims=True)
            o_ref[:, cols] = x2
    o_ref[...] = (o_ref[...] * lax.rsqrt(ssq * (1.0 / D_MODEL) + EPS)) * fg_ref[...]


def _combine(dest, ys, x1, gates, mod3, final_g):
    tiles_per_batch = SEQ // CMB_TM
    grid_spec = pltpu.PrefetchScalarGridSpec(
        num_scalar_prefetch=1,
        grid=(N_TOK // CMB_TM,),
        in_specs=[
            pl.BlockSpec(memory_space=pl.ANY),
            pl.BlockSpec((CMB_TM, D_MODEL), lambda i, d: (i, 0)),
            pl.BlockSpec((CMB_TM, LANES), lambda i, d: (i, 0)),
            pl.BlockSpec((1, 6, D_MODEL), lambda i, d: (i // tiles_per_batch, 0, 0)),
            pl.BlockSpec((1, D_MODEL), lambda i, d: (0, 0)),
        ],
        out_specs=pl.BlockSpec((CMB_TM, D_MODEL), lambda i, d: (i, 0)),
        scratch_shapes=[pltpu.VMEM((2, TOP_K * CMB_TM // SUBLANES, PACK_ROWS, SUBLANES, LANES), jnp.uint32),
                        pltpu.SemaphoreType.DMA((2,))],
    )
    return pl.pallas_call(
        _combine_kernel,
        grid_spec=grid_spec,
        out_shape=jax.ShapeDtypeStruct((N_TOK, D_MODEL), F32),
        compiler_params=_cparams(("arbitrary",)),
        name="combine_final",
    )(dest, ys, x1, gates, mod3, final_g)


def _routing_tables(eid):
    ids = jnp.arange(N_EXPERTS, dtype=jnp.int32)
    onehot = (eid[None, :] == ids[:, None]).astype(jnp.int32)
    csum = jnp.cumsum(onehot, axis=1)
    counts = csum[:, -1]
    padded = (counts + MOE_BLOCK - 1) // MOE_BLOCK * MOE_BLOCK
    pend = jnp.cumsum(padded)
    pstart = pend - padded
    dest = (jnp.sum(onehot * (csum + pstart[:, None]), axis=0) - 1).astype(jnp.int32)
    blk_row = jnp.arange(N_BLOCKS, dtype=jnp.int32) * MOE_BLOCK
    block_e = jnp.minimum(jnp.sum((pend[None, :] <= blk_row[:, None]).astype(jnp.int32), axis=1),
                          N_EXPERTS - 1).astype(jnp.int32)
    tok_sorted = (jnp.argsort(eid, stable=True) // TOP_K).astype(jnp.int32)
    starts = jnp.cumsum(counts) - counts
    is_e = (block_e[:, None] == ids[None, :]).astype(jnp.int32)
    blk_src = (blk_row + jnp.sum(is_e * (starts - pstart)[None, :], axis=1)).astype(jnp.int32)
    nused = (pend[-1] // MOE_BLOCK).astype(jnp.int32).reshape(1)
    later_used = (ids[None, :] > ids[:, None]) & (counts[None, :] > 0)
    next_used = jnp.min(jnp.where(later_used, ids[None, :], N_EXPERTS), axis=1)
    next_used = jnp.where(next_used == N_EXPERTS, -1, next_used).astype(jnp.int32)
    return dest, tok_sorted, blk_src, block_e, nused, next_used


def kernel(x, c, w_ada, b_ada, norm_mix_g, w_in, b_forget, sinks, out_norm_swa_g, out_norm_fox_g, w_out,
           norm_moe_g, w_group, b_group, w_expert, b_expert, w_gate, w_up, w_down, final_g):
    assert IN_TM == FOX_T
    x2 = x.reshape(N_TOK, D_MODEL)

    w_in0 = w_in[0]
    w_row = w_in0[:, :VB_COL].astype(BF16)
    w_vt = w_in0[:, VB_COL:MAIN_COLS].astype(BF16)
    w_f = jnp.pad(w_in0[:, MAIN_COLS:], ((0, 0), (0, LANES - N_HEADS_FOX))).astype(BF16)
    colscale = jnp.concatenate([
        jnp.full((SWA_WIDTH,), HEAD_DIM ** -0.5 * LOG2E, F32),
        jnp.ones((2 * KV_WIDTH,), F32),
        jnp.full((FOX_WIDTH,), HEAD_DIM ** -0.5 * LOG2E, F32),
        jnp.ones((FOX_WIDTH,), F32)]).reshape(1, ROW_COLS)
    bf_row = jnp.pad(b_forget[0], (0, LANES - N_HEADS_FOX)).reshape(1, LANES)
    w_out_p = w_out[0].astype(BF16)
    ga = out_norm_swa_g[0].reshape(1, SWA_WIDTH)
    gb = out_norm_fox_g[0].reshape(1, FOX_WIDTH)
    wr = jnp.pad(jnp.concatenate([w_group[0], w_expert[0]], axis=1),
                 ((0, 0), (0, LANES - N_GROUPS - N_EXPERTS)))
    br = jnp.pad(jnp.concatenate([b_group[0], b_expert[0]]), (0, LANES - N_GROUPS - N_EXPERTS)).reshape(1, LANES)
    wg, wu, wd = w_gate[0], w_up[0], w_down[0]

    mod = _adaln(c, w_ada[0], b_ada[0])
    mod3 = mod.reshape(BATCH, 6, D_MODEL)

    proj, f, vt5 = _inproj(x2, mod3, norm_mix_g[0].reshape(1, D_MODEL), w_row, w_f, colscale, w_vt)
    aug = _forget_bias(f, bf_row)

    o_a = _swa(sinks[0], proj)
    o_b = _fox(proj, aug, vt5)

    x1, h2, ids, gates = _outproj(o_a, o_b, x2, mod3, ga, gb, w_out_p,
                                  norm_moe_g[0].reshape(1, D_MODEL), wr, br)

    eid = ids[:, :TOP_K].reshape(-1)
    dest, tok_sorted, blk_src, block_e, nused, next_used = _routing_tables(eid)
    ys = _experts(block_e, tok_sorted, blk_src, nused, next_used, h2, wg, wu, wd)
    out = _combine(dest, ys, x1, gates, mod3, final_g.reshape(1, D_MODEL))
    return out.reshape(BATCH, SEQ, D_MODEL)
```

```python
import math

import numpy as np
import jax
import jax.numpy as jnp
from jax import lax
from jax.experimental import pallas as pl
from jax.experimental.pallas import tpu as pltpu

F32 = jnp.float32
BF16 = jnp.bfloat16

D_MODEL = 2048
BATCH = 2
SEQ = 8192
N_TOK = BATCH * SEQ
HEAD_DIM = 64
N_HEADS_SWA = 16
N_KV_SWA = 4
N_HEADS_FOX = 16
WINDOW = 128
SWA_WIDTH = N_HEADS_SWA * HEAD_DIM
KV_WIDTH = N_KV_SWA * HEAD_DIM
FOX_WIDTH = N_HEADS_FOX * HEAD_DIM
MAIN_COLS = SWA_WIDTH + 2 * KV_WIDTH + 3 * FOX_WIDTH
N_GROUPS = 4
EXPERTS_PER_GROUP = 8
N_EXPERTS = N_GROUPS * EXPERTS_PER_GROUP
TOP_K = 2
D_EXPERT = 512
MOE_BLOCK = 256
MOE_SUB = 128
EPS = 1e-6
LOG2E = math.log2(math.e)

LANES = 128
SUBLANES = 8
VMEM_LIMIT = 56 * 1024 * 1024

KA_COL = SWA_WIDTH
VA_COL = SWA_WIDTH + KV_WIDTH
QB_BLK = (SWA_WIDTH + 2 * KV_WIDTH) // LANES
KB_BLK = QB_BLK + FOX_WIDTH // LANES
VB_COL = SWA_WIDTH + 2 * KV_WIDTH + 2 * FOX_WIDTH
ROW_COLS = VB_COL


def _cparams(sem, vmem=VMEM_LIMIT):
    return pltpu.CompilerParams(dimension_semantics=sem, vmem_limit_bytes=vmem)


ADA_TN = 1024


def _adaln_kernel(cb_ref, w_ref, b_ref, o_ref):
    for b in range(BATCH):
        s = jax.nn.silu(cb_ref[b])
        cols = []
        for j in range(ADA_TN // LANES):
            prod = w_ref[:, j * LANES:(j + 1) * LANES] * s
            cols.append(jnp.sum(prod, axis=0, keepdims=True))
        o_ref[b:b + 1, :] = jnp.concatenate(cols, axis=1) + b_ref[...]


def _adaln(c, w_ada, b_ada):
    ncol = w_ada.shape[1]
    cb = jnp.broadcast_to(c[:, :, None], (BATCH, D_MODEL, LANES))
    return pl.pallas_call(
        _adaln_kernel,
        grid=(ncol // ADA_TN,),
        in_specs=[
            pl.BlockSpec((BATCH, D_MODEL, LANES), lambda j: (0, 0, 0)),
            pl.BlockSpec((D_MODEL, ADA_TN), lambda j: (0, j)),
            pl.BlockSpec((1, ADA_TN), lambda j: (0, j)),
        ],
        out_specs=pl.BlockSpec((BATCH, ADA_TN), lambda j: (0, j)),
        out_shape=jax.ShapeDtypeStruct((BATCH, ncol), F32),
        compiler_params=_cparams(("arbitrary",)),
        name="adaln",
    )(cb, w_ada, b_ada.reshape(1, ncol))


def _rms(x):
    return x * lax.rsqrt(jnp.mean(x * x, axis=-1, keepdims=True) + EPS)


IN_TM = 512
IN_TN = 1792
IN_NJ = ROW_COLS // IN_TN


def _inproj_kernel(x_ref, mod_ref, g_ref, w_ref, wf_ref, cs_ref, wvt_ref, o_ref, f_ref, vt_ref, h_ref):
    j = pl.program_id(1)

    @pl.when(j == 0)
    def _():
        y = _rms(x_ref[...])
        h = (y * g_ref[...]) * (1.0 + mod_ref[0, 1:2, :]) + mod_ref[0, 0:1, :]
        hb = h.astype(BF16)
        h_ref[...] = hb
        f_ref[...] = jnp.dot(hb, wf_ref[...], preferred_element_type=F32)

    @pl.when(j < IN_NJ)
    def _():
        acc = jnp.dot(h_ref[...], w_ref[...], preferred_element_type=F32)
        o_ref[...] = (acc * cs_ref[...]).astype(BF16)

    @pl.when(j == IN_NJ)
    def _():
        vt = lax.dot_general(wvt_ref[...], h_ref[...], (((0,), (1,)), ((), ())),
                             preferred_element_type=F32)
        vt_ref[0, :, 0] = vt.reshape(N_HEADS_FOX // 2, LANES, IN_TM).astype(BF16)


def _inproj(x2, mod3, g, w_row, w_f, colscale, w_vt):
    tiles_per_batch = SEQ // IN_TM
    last = IN_NJ - 1
    return pl.pallas_call(
        _inproj_kernel,
        grid=(N_TOK // IN_TM, IN_NJ + 1),
        in_specs=[
            pl.BlockSpec((IN_TM, D_MODEL), lambda i, j: (i, 0)),
            pl.BlockSpec((1, 6, D_MODEL), lambda i, j: (i // tiles_per_batch, 0, 0)),
            pl.BlockSpec((1, D_MODEL), lambda i, j: (0, 0)),
            pl.BlockSpec((D_MODEL, IN_TN), lambda i, j: (0, jnp.minimum(j, last))),
            pl.BlockSpec((D_MODEL, LANES), lambda i, j: (0, 0)),
            pl.BlockSpec((1, IN_TN), lambda i, j: (0, jnp.minimum(j, last))),
            pl.BlockSpec((D_MODEL, FOX_WIDTH), lambda i, j: (0, 0)),
        ],
        out_specs=[
            pl.BlockSpec((IN_TM, IN_TN), lambda i, j: (i, jnp.minimum(j, last))),
            pl.BlockSpec((IN_TM, LANES), lambda i, j: (i, 0)),
            pl.BlockSpec((1, N_HEADS_FOX // 2, 1, LANES, IN_TM),
                         lambda i, j: (i // tiles_per_batch, 0, i % tiles_per_batch, 0, 0)),
        ],
        out_shape=[
            jax.ShapeDtypeStruct((N_TOK, ROW_COLS), BF16),
            jax.ShapeDtypeStruct((N_TOK, LANES), F32),
            jax.ShapeDtypeStruct((BATCH, N_HEADS_FOX // 2, SEQ // IN_TM, LANES, IN_TM), BF16),
        ],
        scratch_shapes=[pltpu.VMEM((IN_TM, D_MODEL), BF16)],
        compiler_params=_cparams(("arbitrary", "arbitrary")),
        name="inproj",
    )(x2, mod3, g, w_row, w_f, colscale, w_vt)


CUM_TS = 512
FOX_AUG = 3


def _cum_kernel(f_ref, bf_ref, tri_ref, sel_ref, aug_ref, carry_ref):
    @pl.when(pl.program_id(1) == 0)
    def _():
        carry_ref[...] = jnp.zeros_like(carry_ref)

    lf = jax.nn.log_sigmoid(f_ref[...] + bf_ref[...])
    l_hi = lf.astype(BF16)
    l_r = lf - l_hi.astype(F32)
    l_mid = l_r.astype(BF16)
    l_lo = (l_r - l_mid.astype(F32)).astype(BF16)
    c3 = jnp.dot(tri_ref[...], jnp.concatenate([l_hi, l_mid, l_lo], axis=1), preferred_element_type=F32)
    cs = (c3[:, :LANES] + (c3[:, LANES:2 * LANES] + c3[:, 2 * LANES:])) + carry_ref[...]
    carry_ref[...] = cs[CUM_TS - 1:CUM_TS, :]
    bias = cs * (-LOG2E)
    hi = bias.astype(BF16)
    r1 = bias - hi.astype(F32)
    mid = r1.astype(BF16)
    lo = (r1 - mid.astype(F32)).astype(BF16)
    pieces = jnp.concatenate([hi, mid, lo], axis=1)
    aug_ref[...] = jnp.dot(pieces, sel_ref[...], preferred_element_type=F32).astype(BF16)


def _aug_selector():
    sel = np.zeros((FOX_AUG * LANES, FOX_WIDTH), np.float32)
    for head in range(N_HEADS_FOX):
        for piece in range(FOX_AUG):
            sel[piece * LANES + head, (head // 2) * LANES + FOX_AUG * (head % 2) + piece] = 1.0
    return sel


def _forget_bias(f, b_forget_row):
    tri = jnp.tril(jnp.ones((CUM_TS, CUM_TS), BF16))
    sel = jnp.asarray(_aug_selector(), BF16)
    nblk = SEQ // CUM_TS
    return pl.pallas_call(
        _cum_kernel,
        grid=(BATCH, nblk),
        in_specs=[
            pl.BlockSpec((CUM_TS, LANES), lambda b, i: (b * nblk + i, 0)),
            pl.BlockSpec((1, LANES), lambda b, i: (0, 0)),
            pl.BlockSpec((CUM_TS, CUM_TS), lambda b, i: (0, 0)),
            pl.BlockSpec((FOX_AUG * LANES, FOX_WIDTH), lambda b, i: (0, 0)),
        ],
        out_specs=pl.BlockSpec((CUM_TS, FOX_WIDTH), lambda b, i: (b * nblk + i, 0)),
        out_shape=jax.ShapeDtypeStruct((N_TOK, FOX_WIDTH), BF16),
        scratch_shapes=[pltpu.VMEM((1, LANES), F32)],
        compiler_params=_cparams(("arbitrary", "arbitrary")),
        name="forget_bias",
    )(f, b_forget_row, tri, sel)


SWA_TQ = WINDOW
SWA_GROUP = 4


def _swa_kernel(sinks_ref, q_ref, kc_ref, kp_ref, vc_ref, vp_ref, o_ref, bias_ref):
    i = pl.program_id(1)
    tq = SWA_TQ

    @pl.when((pl.program_id(0) == 0) & (i == 0))
    def _():
        row = lax.broadcasted_iota(jnp.int32, (tq, 2 * tq), 0)
        col = lax.broadcasted_iota(jnp.int32, (tq, 2 * tq), 1)
        dist = row + tq - col
        band = (dist >= 0) & (dist < WINDOW)
        distf = dist.astype(F32)
        for head in range(N_HEADS_SWA):
            slope2 = float(2.0 ** (-8.0 * (head + 1) / N_HEADS_SWA)) * LOG2E
            base = jnp.where(band, -slope2 * distf, -jnp.inf)
            bias_ref[1, head] = base
            bias_ref[0, head] = jnp.where(col >= tq, base, -jnp.inf)

    table = jnp.minimum(i, 1)
    lane = lax.broadcasted_iota(jnp.int32, (tq, LANES), 1)
    lo_half = lane < HEAD_DIM
    for t in range(2):
        kt = jnp.concatenate([kp_ref[:, t * LANES:(t + 1) * LANES],
                              kc_ref[:, t * LANES:(t + 1) * LANES]], axis=0)
        vt = jnp.concatenate([vp_ref[:, t * LANES:(t + 1) * LANES],
                              vc_ref[:, t * LANES:(t + 1) * LANES]], axis=0)
        for g0 in range(0, 4, SWA_GROUP):
            heads = [(g, e) for g in range(g0, g0 + SWA_GROUP) for e in range(2)]
            scores = []
            for g, e in heads:
                tile, f = t * 4 + g, g // 2
                qt = q_ref[:, tile * LANES:(tile + 1) * LANES]
                if e != f:
                    qt = jnp.concatenate([qt[:, HEAD_DIM:], qt[:, :HEAD_DIM]], axis=1)
                qm = jnp.where(lo_half if f == 0 else ~lo_half, qt, jnp.zeros_like(qt))
                scores.append(lax.dot_general(qm, kt, (((1,), (1,)), ((), ())),
                                              preferred_element_type=F32))
            probs, rdens = [], []
            for (g, e), s in zip(heads, scores):
                head = 2 * (t * 4 + g) + e
                sink = sinks_ref[head] * LOG2E
                s = s + bias_ref[table, head]
                m = jnp.maximum(jnp.max(s, axis=-1, keepdims=True), sink)
                p = jnp.exp2(s - m)
                rdens.append(1.0 / (jnp.sum(p, axis=-1, keepdims=True) + jnp.exp2(sink - m)))
                probs.append(p.astype(BF16))
            outs = [jnp.dot(p, vt, preferred_element_type=F32) * r for p, r in zip(probs, rdens)]
            for k, g in enumerate(range(g0, g0 + SWA_GROUP)):
                tile, f = t * 4 + g, g // 2
                same, other = outs[2 * k + f], pltpu.roll(outs[2 * k + 1 - f], HEAD_DIM, axis=1)
                o_ref[:, tile * LANES:(tile + 1) * LANES] = jnp.where(lo_half == (f == 0), same,
                                                                      other).astype(BF16)


def _swa(sinks, proj):
    nq = SEQ // SWA_TQ
    kblk = KA_COL // KV_WIDTH
    vblk = VA_COL // KV_WIDTH
    grid_spec = pltpu.PrefetchScalarGridSpec(
        num_scalar_prefetch=1,
        grid=(BATCH, nq),
        in_specs=[
            pl.BlockSpec((SWA_TQ, SWA_WIDTH), lambda b, i, s: (b * nq + i, 0)),
            pl.BlockSpec((SWA_TQ, KV_WIDTH), lambda b, i, s: (b * nq + i, kblk)),
            pl.BlockSpec((SWA_TQ, KV_WIDTH), lambda b, i, s: (b * nq + jnp.maximum(i - 1, 0), kblk)),
            pl.BlockSpec((SWA_TQ, KV_WIDTH), lambda b, i, s: (b * nq + i, vblk)),
            pl.BlockSpec((SWA_TQ, KV_WIDTH), lambda b, i, s: (b * nq + jnp.maximum(i - 1, 0), vblk)),
        ],
        out_specs=pl.BlockSpec((SWA_TQ, SWA_WIDTH), lambda b, i, s: (b * nq + i, 0)),
        scratch_shapes=[pltpu.VMEM((2, N_HEADS_SWA, SWA_TQ, 2 * SWA_TQ), F32)],
    )
    return pl.pallas_call(
        _swa_kernel,
        grid_spec=grid_spec,
        out_shape=jax.ShapeDtypeStruct((N_TOK, SWA_WIDTH), BF16),
        compiler_params=_cparams(("arbitrary", "arbitrary")),
        name="swa_attn",
    )(sinks, proj, proj, proj, proj, proj)


FOX_T = 512
FOX_ONES = 16
FOX_VROWS = HEAD_DIM + FOX_ONES


def _fox_kernel(q_ref, k_ref, aug_ref, vt_ref, o_ref, mask_ref, t_ref, acc_ref):
    qi = pl.program_id(2)
    t = FOX_T

    @pl.when(qi == 0)
    def _():
        kr = lax.broadcasted_iota(jnp.int32, (t, t), 0)
        qc = lax.broadcasted_iota(jnp.int32, (t, t), 1)
        mask_ref[...] = jnp.where(kr <= qc, 0.0, -jnp.inf)

    qf = q_ref[...].astype(F32).T
    drow = lax.broadcasted_iota(jnp.int32, (LANES, t), 0)
    qaug = []
    for hh in range(2):
        qh = jnp.where((drow >= HEAD_DIM * hh) & (drow < HEAD_DIM * (hh + 1)), qf, 0.0)
        ones = jnp.where((drow >= FOX_AUG * hh) & (drow < FOX_AUG * (hh + 1)), 1.0, 0.0)
        qaug.append(jnp.concatenate([qh, ones], axis=0).astype(BF16))
    acc_ref[...] = jnp.zeros_like(acc_ref)

    def stage_a(j, slot, diag):
        r0 = pl.multiple_of(j * t, t)
        kb = jnp.concatenate([k_ref[pl.ds(r0, t), :], aug_ref[pl.ds(r0, t), :]], axis=1)
        sts = [jnp.dot(kb, qaug[hh], preferred_element_type=F32) for hh in range(2)]
        mbs = []
        for hh in range(2):
            tt = sts[hh]
            if diag:
                tt = tt + mask_ref[...]
            t_ref[slot, hh] = tt
            mbs.append(jnp.max(tt, axis=0, keepdims=True))
        return tuple(mbs)

    def stage_b(jv, slot, ms, mbs):
        new, ps, alphas = [], [], []
        for hh in range(2):
            m_new = jnp.maximum(ms[hh], mbs[hh])
            alphas.append(jnp.exp2(ms[hh] - m_new))
            ps.append(jnp.exp2(t_ref[slot, hh] - m_new).astype(BF16))
            new.append(m_new)
        ones = jnp.ones((FOX_ONES, t), BF16)
        pvs = [jnp.dot(jnp.concatenate([vt_ref[0, 0, jv, hh * HEAD_DIM:(hh + 1) * HEAD_DIM, :], ones], axis=0),
                       ps[hh], preferred_element_type=F32)
               for hh in range(2)]
        for hh in range(2):
            acc_ref[hh] = alphas[hh] * acc_ref[hh] + pvs[hh]
        return tuple(new)

    ml0 = tuple(jnp.full((1, t), -jnp.inf, F32) for _ in range(2))
    mb0 = stage_a(qi, 0, True)

    def pair(j0, c):
        ml, mbs, jprev = c
        mb1 = stage_a(j0, 1, False)
        ml = stage_b(jprev, 0, ml, mbs)
        mb2 = stage_a(j0 + 1, 0, False)
        ml = stage_b(j0, 1, ml, mb1)
        return ml, mb2, j0 + 1

    def quad(ii, c):
        return pair(4 * ii + 2, pair(4 * ii, c))

    carry = lax.fori_loop(0, qi // 4, quad, (ml0, mb0, qi))
    base = 4 * (qi // 4)

    def tail(rem):
        def run(c):
            ml, mb_prev, prev = c
            for k in range(rem):
                mb_new = stage_a(base + k, (k + 1) % 2, False)
                ml = stage_b(prev, k % 2, ml, mb_prev)
                prev, mb_prev = base + k, mb_new
            stage_b(prev, rem % 2, ml, mb_prev)
            ot = jnp.concatenate([acc_ref[hh, 0:HEAD_DIM, :] / acc_ref[hh, HEAD_DIM:HEAD_DIM + 1, :]
                                  for hh in range(2)], axis=0)
            o_ref[...] = ot.T.astype(BF16)
            return jnp.int32(0)
        return run

    lax.switch(qi % 4, [tail(rem) for rem in range(4)], carry)


def _fox(proj, aug, vt5):
    nq = SEQ // FOX_T
    npair = N_HEADS_FOX // 2
    return pl.pallas_call(
        _fox_kernel,
        grid=(BATCH, npair, nq),
        in_specs=[
            pl.BlockSpec((FOX_T, LANES), lambda b, h, i: (b * nq + i, QB_BLK + h)),
            pl.BlockSpec((SEQ, LANES), lambda b, h, i: (b, KB_BLK + h)),
            pl.BlockSpec((SEQ, LANES), lambda b, h, i: (b, h)),
            pl.BlockSpec((1, 1, nq, LANES, FOX_T), lambda b, h, i: (b, h, 0, 0, 0)),
        ],
        out_specs=pl.BlockSpec((FOX_T, LANES), lambda b, h, i: (b * nq + i, h)),
        out_shape=jax.ShapeDtypeStruct((N_TOK, FOX_WIDTH), BF16),
        scratch_shapes=[
            pltpu.VMEM((FOX_T, FOX_T), F32),
            pltpu.VMEM((2, 2, FOX_T, FOX_T), F32),
            pltpu.VMEM((2, FOX_VROWS, FOX_T), F32),
        ],
        compiler_params=_cparams(("arbitrary", "arbitrary", "arbitrary")),
        name="fox_attn",
    )(proj, proj, aug, vt5)


OUT_TM = 512
OUT_SUB = 256
PACKED_COLS = D_MODEL // 2
PACK_ROWS = PACKED_COLS // LANES


def _outproj_kernel(oa_ref, ob_ref, x_ref, mod_ref, ga_ref, gb_ref, w_ref, gm_ref, wr2_ref, br_ref,
                    x1_ref, h2p_ref, ids_ref, gates_ref):
    subs = [slice(k * OUT_SUB, (k + 1) * OUT_SUB) for k in range(OUT_TM // OUT_SUB)]
    mixed = [jnp.concatenate([_rms(oa_ref[r, :].astype(F32)) * ga_ref[...],
                              _rms(ob_ref[r, :].astype(F32)) * gb_ref[...]], axis=1).astype(BF16) for r in subs]
    ys = [jnp.dot(m, w_ref[...], preferred_element_type=F32) for m in mixed]
    h2s = []
    for r, y in zip(subs, ys):
        x1 = x_ref[r, :] + mod_ref[0, 2:3, :] * y
        x1_ref[r, :] = x1
        h2s.append((_rms(x1) * gm_ref[...]) * (1.0 + mod_ref[0, 4:5, :]) + mod_ref[0, 3:4, :])
    hbs = [h2.astype(BF16) for h2 in h2s]
    for r, h2 in zip(subs, h2s):
        packed = _pack_bf16_pairs(h2)
        for s in range(PACK_ROWS):
            h2p_ref[r, s, :] = packed[:, s * LANES:(s + 1) * LANES]

    hls = [(h2 - hb.astype(F32)).astype(BF16) for h2, hb in zip(h2s, hbs)]
    r2s = [jnp.dot(hb, wr2_ref[...], preferred_element_type=F32) for hb in hbs]
    r3s = [jnp.dot(hl, wr2_ref[:, :LANES], preferred_element_type=F32) for hl in hls]
    for r, r2, r3 in zip(subs, r2s, r3s):
        logits = (r2[:, :LANES] + (r2[:, LANES:] + r3)) + br_ref[...]
        ids, gates = _route(logits)
        ids_ref[r, :] = ids
        gates_ref[r, :] = gates


def _route(logits):
    lane = lax.broadcasted_iota(jnp.int32, logits.shape, 1)
    neg = -jnp.inf
    is_g = lane < N_GROUPS
    gl = jnp.where(is_g, logits, neg)
    gmax = jnp.max(gl, axis=1, keepdims=True)
    gsel = jnp.min(jnp.where(gl == gmax, lane, LANES), axis=1, keepdims=True)
    gsum = jnp.sum(jnp.where(is_g, jnp.exp(gl - gmax), 0.0), axis=1, keepdims=True)
    g_val = 1.0 / gsum
    elane = lane - N_GROUPS
    in_sel = (elane >= 0) & (elane < N_EXPERTS) & ((elane >> 3) == gsel)
    ev = jnp.where(in_sel, logits, neg)
    t1 = jnp.max(ev, axis=1, keepdims=True)
    i1 = jnp.min(jnp.where(ev == t1, lane, LANES), axis=1, keepdims=True)
    ev2 = jnp.where(lane == i1, neg, ev)
    t2 = jnp.max(ev2, axis=1, keepdims=True)
    i2 = jnp.min(jnp.where(ev2 == t2, lane, LANES), axis=1, keepdims=True)
    e2 = jnp.exp(t2 - t1)
    den = 1.0 + e2
    w1 = (1.0 / den) * g_val
    w2 = (e2 / den) * g_val
    return (jnp.where(lane == 0, i1 - N_GROUPS, jnp.where(lane == 1, i2 - N_GROUPS, 0)),
            jnp.where(lane == 0, w1, jnp.where(lane == 1, w2, 0.0)))


def _outproj(oa, ob, x2, mod3, ga, gb, w_out, gm, wr, br):
    tiles_per_batch = SEQ // OUT_TM
    row = lambda i: (i, 0)
    const = lambda i: (0, 0)
    wr_hi = wr.astype(BF16)
    wr_lo = (wr - wr_hi.astype(F32)).astype(BF16)
    wr2 = jnp.concatenate([wr_hi, wr_lo], axis=1)
    return pl.pallas_call(
        _outproj_kernel,
        grid=(N_TOK // OUT_TM,),
        in_specs=[
            pl.BlockSpec((OUT_TM, SWA_WIDTH), row),
            pl.BlockSpec((OUT_TM, FOX_WIDTH), row),
            pl.BlockSpec((OUT_TM, D_MODEL), row),
            pl.BlockSpec((1, 6, D_MODEL), lambda i: (i // tiles_per_batch, 0, 0)),
            pl.BlockSpec((1, SWA_WIDTH), const),
            pl.BlockSpec((1, FOX_WIDTH), const),
            pl.BlockSpec((D_MODEL, D_MODEL), const),
            pl.BlockSpec((1, D_MODEL), const),
            pl.BlockSpec((D_MODEL, 2 * LANES), const),
            pl.BlockSpec((1, LANES), const),
        ],
        out_specs=[
            pl.BlockSpec((OUT_TM, D_MODEL), row),
            pl.BlockSpec((OUT_TM, PACK_ROWS, LANES), lambda i: (i, 0, 0)),
            pl.BlockSpec((OUT_TM, LANES), row),
            pl.BlockSpec((OUT_TM, LANES), row),
        ],
        out_shape=[
            jax.ShapeDtypeStruct((N_TOK, D_MODEL), F32),
            jax.ShapeDtypeStruct((N_TOK, PACK_ROWS, LANES), jnp.uint32),
            jax.ShapeDtypeStruct((N_TOK, LANES), jnp.int32),
            jax.ShapeDtypeStruct((N_TOK, LANES), F32),
        ],
        compiler_params=_cparams(("arbitrary",)),
        name="outproj_router",
    )(oa, ob, x2, mod3, ga, gb, w_out, gm, wr2, br)


N_BLOCKS = (N_TOK * TOP_K) // MOE_BLOCK + N_EXPERTS
P_ROWS = N_BLOCKS * MOE_BLOCK


def _pack_bf16_pairs(x):
    xb = x.astype(BF16).astype(F32)
    half = x.shape[1] // 2
    lo = lax.shift_right_logical(pltpu.bitcast(xb[:, :half], jnp.uint32), jnp.uint32(16))
    hi = pltpu.bitcast(xb[:, half:], jnp.uint32) & jnp.uint32(0xFFFF0000)
    return lo | hi


def _unpack_bf16_pairs(w):
    return (pltpu.bitcast(lax.shift_left(w, jnp.uint32(16)), F32),
            pltpu.bitcast(w & jnp.uint32(0xFFFF0000), F32))


def _expert_kernel(be_ref, tok_ref, src_ref, nused_ref, nxt_ref, h2_hbm, wg_hbm, wu_hbm, wd_hbm, y_hbm, xbuf, sem,
                   wgf, wuf, wdf, wsem, wgb, wub, wdb, ystage, osem):
    i = pl.program_id(0)
    nused = nused_ref[0]

    def weight_copies(e):
        return (pltpu.make_async_copy(wg_hbm.at[e], wgf, wsem.at[0]),
                pltpu.make_async_copy(wu_hbm.at[e], wuf, wsem.at[1]),
                pltpu.make_async_copy(wd_hbm.at[e], wdf, wsem.at[2]))

    def issue(blk, slot):
        def body(g, _):
            for q in range(SUBLANES):
                tok = tok_ref[jnp.minimum(src_ref[blk] + g * SUBLANES + q, N_TOK * TOP_K - 1)]
                pltpu.make_async_copy(h2_hbm.at[tok], xbuf.at[slot, g, :, q, :], sem.at[slot]).start(
                    priority=q % 2)
            return 0
        lax.fori_loop(0, MOE_BLOCK // SUBLANES, body, 0)

    def out_copies(blk, slot):
        r0 = pl.multiple_of(blk * MOE_BLOCK, MOE_BLOCK)
        return [pltpu.make_async_copy(ystage.at[slot, :, pl.ds(s * LANES, LANES)],
                                      y_hbm.at[pl.ds(r0, MOE_BLOCK), s, :], osem.at[slot])
                for s in range(PACK_ROWS)]

    def out_wait(slot):
        pltpu.make_async_copy(ystage.at[slot], ystage.at[slot], osem.at[slot]).wait()

    def gather_wait(slot):
        pltpu.make_async_copy(xbuf.at[slot], xbuf.at[slot], sem.at[slot]).wait()

    @pl.when(i == 0)
    def _():
        issue(0, 0)

    @pl.when(i + 1 < nused)
    def _():
        issue(i + 1, (i + 1) % 2)

    oslot = i % 2

    @pl.when(i >= 2)
    def _():
        out_wait(oslot)

    e = be_ref[i]
    new_expert = ((i == 0) | (e != be_ref[jnp.maximum(i - 1, 0)])) & (i < nused)

    @pl.when(i == 0)
    def _():
        for cp in weight_copies(e):
            cp.start()

    @pl.when(new_expert)
    def _():
        for cp in weight_copies(e):
            cp.wait()
        wgb[...] = wgf[...].astype(BF16)
        wub[...] = wuf[...].astype(BF16)
        wdb[...] = wdf[...].astype(BF16)

    @pl.when(new_expert & (nxt_ref[e] >= 0))
    def _():
        for cp in weight_copies(nxt_ref[e]):
            cp.start()

    @pl.when(i < nused)
    def _():
        slot = i % 2
        gather_wait(slot)
        gsub = MOE_SUB // SUBLANES
        xbs = []
        for k in range(MOE_BLOCK // MOE_SUB):
            xu = jnp.concatenate([xbuf[slot, k * gsub:(k + 1) * gsub, s].reshape(MOE_SUB, LANES)
                                  for s in range(PACK_ROWS)], axis=1)
            xbs.append(jnp.concatenate(_unpack_bf16_pairs(xu), axis=1).astype(BF16))
        gs = [jnp.dot(xb, wgb[...], preferred_element_type=F32) for xb in xbs]
        us = [jnp.dot(xb, wub[...], preferred_element_type=F32) for xb in xbs]
        hids = [(jax.nn.silu(g) * u).astype(BF16) for g, u in zip(gs, us)]
        ys = [jnp.dot(hid, wdb[...], preferred_element_type=F32) for hid in hids]
        for k, y in enumerate(ys):
            ystage[oslot, k * MOE_SUB:(k + 1) * MOE_SUB, :] = _pack_bf16_pairs(y)

    @pl.when(i >= nused)
    def _():
        ystage[oslot] = jnp.zeros((MOE_BLOCK, PACKED_COLS), jnp.uint32)

    for cp in out_copies(i, oslot):
        cp.start()

    @pl.when(i == pl.num_programs(0) - 1)
    def _():
        out_wait(oslot)
        out_wait(1 - oslot)


def _experts(block_e, tok_sorted, blk_src, nused, next_used, h2p, wg, wu, wd):
    grid_spec = pltpu.PrefetchScalarGridSpec(
        num_scalar_prefetch=5,
        grid=(N_BLOCKS,),
        in_specs=[pl.BlockSpec(memory_space=pl.ANY)] * 4,
        out_specs=pl.BlockSpec(memory_space=pl.ANY),
        scratch_shapes=[pltpu.VMEM((2, MOE_BLOCK // SUBLANES, PACK_ROWS, SUBLANES, LANES), jnp.uint32),
                        pltpu.SemaphoreType.DMA((2,)),
                        pltpu.VMEM((D_MODEL, D_EXPERT), F32),
                        pltpu.VMEM((D_MODEL, D_EXPERT), F32),
                        pltpu.VMEM((D_EXPERT, D_MODEL), F32),
                        pltpu.SemaphoreType.DMA((3,)),
                        pltpu.VMEM((D_MODEL, D_EXPERT), BF16),
                        pltpu.VMEM((D_MODEL, D_EXPERT), BF16),
                        pltpu.VMEM((D_EXPERT, D_MODEL), BF16),
                        pltpu.VMEM((2, MOE_BLOCK, PACKED_COLS), jnp.uint32),
                        pltpu.SemaphoreType.DMA((2,))],
    )
    return pl.pallas_call(
        _expert_kernel,
        grid_spec=grid_spec,
        out_shape=jax.ShapeDtypeStruct((P_ROWS, PACK_ROWS, LANES), jnp.uint32),
        compiler_params=_cparams(("arbitrary",)),
        name="expert_ffn",
    )(block_e, tok_sorted, blk_src, nused, next_used, h2p, wg, wu, wd)


CMB_TM = 256


def _combine_kernel(dest_ref, ys_hbm, x1_ref, gates_ref, mod_ref, fg_ref, o_ref, ybuf, sem):
    i = pl.program_id(0)
    n = pl.num_programs(0)
    groups = CMB_TM // SUBLANES

    def issue(tile, slot):
        def body(g, _):
            for q in range(SUBLANES):
                for k in range(TOP_K):
                    d = dest_ref[(tile * CMB_TM + g * SUBLANES + q) * TOP_K + k]
                    pltpu.make_async_copy(ys_hbm.at[d], ybuf.at[slot, k * groups + g, :, q, :],
                                          sem.at[slot]).start(priority=k)
            return 0
        lax.fori_loop(0, groups, body, 0)

    @pl.when(i == 0)
    def _():
        issue(0, 0)

    @pl.when(i + 1 < n)
    def _():
        issue(i + 1, (i + 1) % 2)

    slot = i % 2
    pltpu.make_async_copy(ybuf.at[slot], ybuf.at[slot], sem.at[slot]).wait()
    gts = gates_ref[...]
    w0 = gts[:, 0:1]
    w1 = gts[:, 1:2]
    half = D_MODEL // 2
    ssq = jnp.zeros((CMB_TM, 1), F32)
    for s in range(PACK_ROWS):
        y0 = _unpack_bf16_pairs(ybuf[slot, 0:groups, s].reshape(CMB_TM, LANES))
        y1 = _unpack_bf16_pairs(ybuf[slot, groups:2 * groups, s].reshape(CMB_TM, LANES))
        for part in range(2):
            cols = slice(part * half + s * LANES, part * half + (s + 1) * LANES)
            x2 = x1_ref[:, cols] + mod_ref[0, 5:6, cols] * (y0[part] * w0 + y1[part] * w1)
            ssq = ssq + jnp.sum(x2 * x2, axis=-1, keepdims=True)
            o_ref[:, cols] = x2
    o_ref[...] = (o_ref[...] * lax.rsqrt(ssq * (1.0 / D_MODEL) + EPS)) * fg_ref[...]


def _combine(dest, ys, x1, gates, mod3, final_g):
    tiles_per_batch = SEQ // CMB_TM
    grid_spec = pltpu.PrefetchScalarGridSpec(
        num_scalar_prefetch=1,
        grid=(N_TOK // CMB_TM,),
        in_specs=[
            pl.BlockSpec(memory_space=pl.ANY),
            pl.BlockSpec((CMB_TM, D_MODEL), lambda i, d: (i, 0)),
            pl.BlockSpec((CMB_TM, LANES), lambda i, d: (i, 0)),
            pl.BlockSpec((1, 6, D_MODEL), lambda i, d: (i // tiles_per_batch, 0, 0)),
            pl.BlockSpec((1, D_MODEL), lambda i, d: (0, 0)),
        ],
        out_specs=pl.BlockSpec((CMB_TM, D_MODEL), lambda i, d: (i, 0)),
        scratch_shapes=[pltpu.VMEM((2, TOP_K * CMB_TM // SUBLANES, PACK_ROWS, SUBLANES, LANES), jnp.uint32),
                        pltpu.SemaphoreType.DMA((2,))],
    )
    return pl.pallas_call(
        _combine_kernel,
        grid_spec=grid_spec,
        out_shape=jax.ShapeDtypeStruct((N_TOK, D_MODEL), F32),
        compiler_params=_cparams(("arbitrary",)),
        name="combine_final",
    )(dest, ys, x1, gates, mod3, final_g)


def _routing_tables(eid):
    ids = jnp.arange(N_EXPERTS, dtype=jnp.int32)
    onehot = (eid[None, :] == ids[:, None]).astype(jnp.int32)
    csum = jnp.cumsum(onehot, axis=1)
    counts = csum[:, -1]
    padded = (counts + MOE_BLOCK - 1) // MOE_BLOCK * MOE_BLOCK
    pend = jnp.cumsum(padded)
    pstart = pend - padded
    dest = (jnp.sum(onehot * (csum + pstart[:, None]), axis=0) - 1).astype(jnp.int32)
    blk_row = jnp.arange(N_BLOCKS, dtype=jnp.int32) * MOE_BLOCK
    block_e = jnp.minimum(jnp.sum((pend[None, :] <= blk_row[:, None]).astype(jnp.int32), axis=1),
                          N_EXPERTS - 1).astype(jnp.int32)
    tok_sorted = (jnp.argsort(eid, stable=True) // TOP_K).astype(jnp.int32)
    starts = jnp.cumsum(counts) - counts
    is_e = (block_e[:, None] == ids[None, :]).astype(jnp.int32)
    blk_src = (blk_row + jnp.sum(is_e * (starts - pstart)[None, :], axis=1)).astype(jnp.int32)
    nused = (pend[-1] // MOE_BLOCK).astype(jnp.int32).reshape(1)
    later_used = (ids[None, :] > ids[:, None]) & (counts[None, :] > 0)
    next_used = jnp.min(jnp.where(later_used, ids[None, :], N_EXPERTS), axis=1)
    next_used = jnp.where(next_used == N_EXPERTS, -1, next_used).astype(jnp.int32)
    return dest, tok_sorted, blk_src, block_e, nused, next_used


def kernel(x, c, w_ada, b_ada, norm_mix_g, w_in, b_forget, sinks, out_norm_swa_g, out_norm_fox_g, w_out,
           norm_moe_g, w_group, b_group, w_expert, b_expert, w_gate, w_up, w_down, final_g):
    assert IN_TM == FOX_T
    x2 = x.reshape(N_TOK, D_MODEL)

    w_in0 = w_in[0]
    w_row = w_in0[:, :VB_COL].astype(BF16)
    w_vt = w_in0[:, VB_COL:MAIN_COLS].astype(BF16)
    w_f = jnp.pad(w_in0[:, MAIN_COLS:], ((0, 0), (0, LANES - N_HEADS_FOX))).astype(BF16)
    colscale = jnp.concatenate([
        jnp.full((SWA_WIDTH,), HEAD_DIM ** -0.5 * LOG2E, F32),
        jnp.ones((2 * KV_WIDTH,), F32),
        jnp.full((FOX_WIDTH,), HEAD_DIM ** -0.5 * LOG2E, F32),
        jnp.ones((FOX_WIDTH,), F32)]).reshape(1, ROW_COLS)
    bf_row = jnp.pad(b_forget[0], (0, LANES - N_HEADS_FOX)).reshape(1, LANES)
    w_out_p = w_out[0].astype(BF16)
    ga = out_norm_swa_g[0].reshape(1, SWA_WIDTH)
    gb = out_norm_fox_g[0].reshape(1, FOX_WIDTH)
    wr = jnp.pad(jnp.concatenate([w_group[0], w_expert[0]], axis=1),
                 ((0, 0), (0, LANES - N_GROUPS - N_EXPERTS)))
    br = jnp.pad(jnp.concatenate([b_group[0], b_expert[0]]), (0, LANES - N_GROUPS - N_EXPERTS)).reshape(1, LANES)
    wg, wu, wd = w_gate[0], w_up[0], w_down[0]

    mod = _adaln(c, w_ada[0], b_ada[0])
    mod3 = mod.reshape(BATCH, 6, D_MODEL)

    proj, f, vt5 = _inproj(x2, mod3, norm_mix_g[0].reshape(1, D_MODEL), w_row, w_f, colscale, w_vt)
    aug = _forget_bias(f, bf_row)

    o_a = _swa(sinks[0], proj)
    o_b = _fox(proj, aug, vt5)

    x1, h2, ids, gates = _outproj(o_a, o_b, x2, mod3, ga, gb, w_out_p,
                                  norm_moe_g[0].reshape(1, D_MODEL), wr, br)

    eid = ids[:, :TOP_K].reshape(-1)
    dest, tok_sorted, blk_src, block_e, nused, next_used = _routing_tables(eid)
    ys = _experts(block_e, tok_sorted, blk_src, nused, next_used, h2, wg, wu, wd)
    out = _combine(dest, ys, x1, gates, mod3, final_g.reshape(1, D_MODEL))
    return out.reshape(BATCH, SEQ, D_MODEL)
```

```python
import functools
import math

import numpy as np
import jax
import jax.numpy as jnp
from jax import lax
from jax.experimental import pallas as pl
from jax.experimental.pallas import tpu as pltpu

F32 = jnp.float32
BF16 = jnp.bfloat16

D_MODEL = 2048
BATCH = 2
SEQ = 8192
N_TOK = BATCH * SEQ
HEAD_DIM = 64
N_HEADS_SWA = 16
N_KV_SWA = 4
N_HEADS_FOX = 16
WINDOW = 128
SWA_WIDTH = N_HEADS_SWA * HEAD_DIM
KV_WIDTH = N_KV_SWA * HEAD_DIM
FOX_WIDTH = N_HEADS_FOX * HEAD_DIM
MAIN_COLS = SWA_WIDTH + 2 * KV_WIDTH + 3 * FOX_WIDTH
N_GROUPS = 4
EXPERTS_PER_GROUP = 8
N_EXPERTS = N_GROUPS * EXPERTS_PER_GROUP
TOP_K = 2
D_EXPERT = 512
MOE_BLOCK = 256
MOE_SUB = 128
EPS = 1e-6
LOG2E = math.log2(math.e)

LANES = 128
VMEM_LIMIT = 56 * 1024 * 1024

QA_BLK = 0
KA_COL = SWA_WIDTH
VA_COL = SWA_WIDTH + KV_WIDTH
QB_BLK = (SWA_WIDTH + 2 * KV_WIDTH) // LANES
KB_BLK = QB_BLK + FOX_WIDTH // LANES
VB_COL = SWA_WIDTH + 2 * KV_WIDTH + 2 * FOX_WIDTH
ROW_COLS = VB_COL


def _cparams(sem, vmem=VMEM_LIMIT):
    return pltpu.CompilerParams(dimension_semantics=sem, vmem_limit_bytes=vmem)


ADA_TN = 1024


def _adaln_kernel(cb_ref, w_ref, b_ref, o_ref):
    for b in range(BATCH):
        s = jax.nn.silu(cb_ref[b])
        cols = []
        for j in range(ADA_TN // LANES):
            prod = w_ref[:, j * LANES:(j + 1) * LANES] * s
            cols.append(jnp.sum(prod, axis=0, keepdims=True))
        o_ref[b:b + 1, :] = jnp.concatenate(cols, axis=1) + b_ref[...]


def _adaln(c, w_ada, b_ada):
    ncol = w_ada.shape[1]
    cb = jnp.broadcast_to(c[:, :, None], (BATCH, D_MODEL, LANES))
    return pl.pallas_call(
        _adaln_kernel,
        grid=(ncol // ADA_TN,),
        in_specs=[
            pl.BlockSpec((BATCH, D_MODEL, LANES), lambda j: (0, 0, 0)),
            pl.BlockSpec((D_MODEL, ADA_TN), lambda j: (0, j)),
            pl.BlockSpec((1, ADA_TN), lambda j: (0, j)),
        ],
        out_specs=pl.BlockSpec((BATCH, ADA_TN), lambda j: (0, j)),
        out_shape=jax.ShapeDtypeStruct((BATCH, ncol), F32),
        compiler_params=_cparams(("arbitrary",)),
        name="adaln",
    )(cb, w_ada, b_ada.reshape(1, ncol))


def _rms(x):
    return x * lax.rsqrt(jnp.mean(x * x, axis=-1, keepdims=True) + EPS)


IN_TM = 512
IN_TN = 1792
IN_NJ = ROW_COLS // IN_TN


def _inproj_kernel(x_ref, mod_ref, g_ref, w_ref, wf_ref, cs_ref, wvt_ref, o_ref, f_ref, vt_ref, h_ref):
    j = pl.program_id(1)

    @pl.when(j == 0)
    def _():
        y = _rms(x_ref[...])
        h = (y * g_ref[...]) * (1.0 + mod_ref[0, 1:2, :]) + mod_ref[0, 0:1, :]
        hb = h.astype(BF16)
        h_ref[...] = hb
        f_ref[...] = jnp.dot(hb, wf_ref[...], preferred_element_type=F32)

    @pl.when(j < IN_NJ)
    def _():
        acc = jnp.dot(h_ref[...], w_ref[...], preferred_element_type=F32)
        o_ref[...] = (acc * cs_ref[...]).astype(BF16)

    @pl.when(j == IN_NJ)
    def _():
        vt = lax.dot_general(wvt_ref[...], h_ref[...], (((0,), (1,)), ((), ())),
                             preferred_element_type=F32)
        vt_ref[0, :, 0] = vt.reshape(N_HEADS_FOX // 2, LANES, IN_TM).astype(BF16)


def _inproj(x2, mod3, g, w_row, w_f, colscale, w_vt):
    tiles_per_batch = SEQ // IN_TM
    last = IN_NJ - 1
    return pl.pallas_call(
        _inproj_kernel,
        grid=(N_TOK // IN_TM, IN_NJ + 1),
        in_specs=[
            pl.BlockSpec((IN_TM, D_MODEL), lambda i, j: (i, 0)),
            pl.BlockSpec((1, 6, D_MODEL), lambda i, j: (i // tiles_per_batch, 0, 0)),
            pl.BlockSpec((1, D_MODEL), lambda i, j: (0, 0)),
            pl.BlockSpec((D_MODEL, IN_TN), lambda i, j: (0, jnp.minimum(j, last))),
            pl.BlockSpec((D_MODEL, LANES), lambda i, j: (0, 0)),
            pl.BlockSpec((1, IN_TN), lambda i, j: (0, jnp.minimum(j, last))),
            pl.BlockSpec((D_MODEL, FOX_WIDTH), lambda i, j: (0, 0)),
        ],
        out_specs=[
            pl.BlockSpec((IN_TM, IN_TN), lambda i, j: (i, jnp.minimum(j, last))),
            pl.BlockSpec((IN_TM, LANES), lambda i, j: (i, 0)),
            pl.BlockSpec((1, N_HEADS_FOX // 2, 1, LANES, IN_TM),
                         lambda i, j: (i // tiles_per_batch, 0, i % tiles_per_batch, 0, 0)),
        ],
        out_shape=[
            jax.ShapeDtypeStruct((N_TOK, ROW_COLS), BF16),
            jax.ShapeDtypeStruct((N_TOK, LANES), F32),
            jax.ShapeDtypeStruct((BATCH, N_HEADS_FOX // 2, SEQ // IN_TM, LANES, IN_TM), BF16),
        ],
        scratch_shapes=[pltpu.VMEM((IN_TM, D_MODEL), BF16)],
        compiler_params=_cparams(("arbitrary", "arbitrary")),
        name="inproj",
    )(x2, mod3, g, w_row, w_f, colscale, w_vt)


CUM_TS = 512
FOX_AUG = 3


def _cum_kernel(f_ref, bf_ref, tri_ref, sel_ref, aug_ref, carry_ref):
    @pl.when(pl.program_id(1) == 0)
    def _():
        carry_ref[...] = jnp.zeros_like(carry_ref)

    lf = jax.nn.log_sigmoid(f_ref[...] + bf_ref[...])
    l_hi = lf.astype(BF16)
    l_r = lf - l_hi.astype(F32)
    l_mid = l_r.astype(BF16)
    l_lo = (l_r - l_mid.astype(F32)).astype(BF16)
    c3 = jnp.dot(tri_ref[...], jnp.concatenate([l_hi, l_mid, l_lo], axis=1), preferred_element_type=F32)
    cs = (c3[:, :LANES] + (c3[:, LANES:2 * LANES] + c3[:, 2 * LANES:])) + carry_ref[...]
    carry_ref[...] = cs[CUM_TS - 1:CUM_TS, :]
    bias = cs * (-LOG2E)
    hi = bias.astype(BF16)
    r1 = bias - hi.astype(F32)
    mid = r1.astype(BF16)
    lo = (r1 - mid.astype(F32)).astype(BF16)
    pieces = jnp.concatenate([hi, mid, lo], axis=1)
    aug_ref[...] = jnp.dot(pieces, sel_ref[...], preferred_element_type=F32).astype(BF16)


def _aug_selector():
    sel = np.zeros((FOX_AUG * LANES, FOX_WIDTH), np.float32)
    for head in range(N_HEADS_FOX):
        for piece in range(FOX_AUG):
            sel[piece * LANES + head, (head // 2) * LANES + FOX_AUG * (head % 2) + piece] = 1.0
    return sel


def _forget_bias(f, b_forget_row):
    tri = jnp.tril(jnp.ones((CUM_TS, CUM_TS), BF16))
    sel = jnp.asarray(_aug_selector(), BF16)
    nblk = SEQ // CUM_TS
    return pl.pallas_call(
        _cum_kernel,
        grid=(BATCH, nblk),
        in_specs=[
            pl.BlockSpec((CUM_TS, LANES), lambda b, i: (b * nblk + i, 0)),
            pl.BlockSpec((1, LANES), lambda b, i: (0, 0)),
            pl.BlockSpec((CUM_TS, CUM_TS), lambda b, i: (0, 0)),
            pl.BlockSpec((FOX_AUG * LANES, FOX_WIDTH), lambda b, i: (0, 0)),
        ],
        out_specs=pl.BlockSpec((CUM_TS, FOX_WIDTH), lambda b, i: (b * nblk + i, 0)),
        out_shape=jax.ShapeDtypeStruct((N_TOK, FOX_WIDTH), BF16),
        scratch_shapes=[pltpu.VMEM((1, LANES), F32)],
        compiler_params=_cparams(("arbitrary", "arbitrary")),
        name="forget_bias",
    )(f, b_forget_row, tri, sel)


SWA_TQ = WINDOW
SWA_GROUP = 4


def _swa_kernel(sinks_ref, q_ref, kc_ref, kp_ref, vc_ref, vp_ref, o_ref, bias_ref):
    i = pl.program_id(1)
    tq = SWA_TQ

    @pl.when((pl.program_id(0) == 0) & (i == 0))
    def _():
        row = lax.broadcasted_iota(jnp.int32, (tq, 2 * tq), 0)
        col = lax.broadcasted_iota(jnp.int32, (tq, 2 * tq), 1)
        dist = row + tq - col
        band = (dist >= 0) & (dist < WINDOW)
        distf = dist.astype(F32)
        for head in range(N_HEADS_SWA):
            slope2 = float(2.0 ** (-8.0 * (head + 1) / N_HEADS_SWA)) * LOG2E
            base = jnp.where(band, -slope2 * distf, -jnp.inf)
            bias_ref[1, head] = base
            bias_ref[0, head] = jnp.where(col >= tq, base, -jnp.inf)

    table = jnp.minimum(i, 1)
    lane = lax.broadcasted_iota(jnp.int32, (tq, LANES), 1)
    lo_half = lane < HEAD_DIM
    for t in range(2):
        kt = jnp.concatenate([kp_ref[:, t * LANES:(t + 1) * LANES],
                              kc_ref[:, t * LANES:(t + 1) * LANES]], axis=0)
        vt = jnp.concatenate([vp_ref[:, t * LANES:(t + 1) * LANES],
                              vc_ref[:, t * LANES:(t + 1) * LANES]], axis=0)
        for g0 in range(0, 4, SWA_GROUP):
            heads = [(g, e) for g in range(g0, g0 + SWA_GROUP) for e in range(2)]
            scores = []
            for g, e in heads:
                tile, f = t * 4 + g, g // 2
                qt = q_ref[:, tile * LANES:(tile + 1) * LANES]
                if e != f:
                    qt = jnp.concatenate([qt[:, HEAD_DIM:], qt[:, :HEAD_DIM]], axis=1)
                qm = jnp.where(lo_half if f == 0 else ~lo_half, qt, jnp.zeros_like(qt))
                scores.append(lax.dot_general(qm, kt, (((1,), (1,)), ((), ())),
                                              preferred_element_type=F32))
            probs, rdens = [], []
            for (g, e), s in zip(heads, scores):
                head = 2 * (t * 4 + g) + e
                sink = sinks_ref[head] * LOG2E
                s = s + bias_ref[table, head]
                m = jnp.maximum(jnp.max(s, axis=-1, keepdims=True), sink)
                p = jnp.exp2(s - m)
                rdens.append(1.0 / (jnp.sum(p, axis=-1, keepdims=True) + jnp.exp2(sink - m)))
                probs.append(p.astype(BF16))
            outs = [jnp.dot(p, vt, preferred_element_type=F32) * r for p, r in zip(probs, rdens)]
            for k, g in enumerate(range(g0, g0 + SWA_GROUP)):
                tile, f = t * 4 + g, g // 2
                same, other = outs[2 * k + f], pltpu.roll(outs[2 * k + 1 - f], HEAD_DIM, axis=1)
                o_ref[:, tile * LANES:(tile + 1) * LANES] = jnp.where(lo_half == (f == 0), same,
                                                                      other).astype(BF16)


def _swa(sinks, proj):
    nq = SEQ // SWA_TQ
    kblk = KA_COL // KV_WIDTH
    vblk = VA_COL // KV_WIDTH
    grid_spec = pltpu.PrefetchScalarGridSpec(
        num_scalar_prefetch=1,
        grid=(BATCH, nq),
        in_specs=[
            pl.BlockSpec((SWA_TQ, SWA_WIDTH), lambda b, i, s: (b * nq + i, 0)),
            pl.BlockSpec((SWA_TQ, KV_WIDTH), lambda b, i, s: (b * nq + i, kblk)),
            pl.BlockSpec((SWA_TQ, KV_WIDTH), lambda b, i, s: (b * nq + jnp.maximum(i - 1, 0), kblk)),
            pl.BlockSpec((SWA_TQ, KV_WIDTH), lambda b, i, s: (b * nq + i, vblk)),
            pl.BlockSpec((SWA_TQ, KV_WIDTH), lambda b, i, s: (b * nq + jnp.maximum(i - 1, 0), vblk)),
        ],
        out_specs=pl.BlockSpec((SWA_TQ, SWA_WIDTH), lambda b, i, s: (b * nq + i, 0)),
        scratch_shapes=[pltpu.VMEM((2, N_HEADS_SWA, SWA_TQ, 2 * SWA_TQ), F32)],
    )
    return pl.pallas_call(
        _swa_kernel,
        grid_spec=grid_spec,
        out_shape=jax.ShapeDtypeStruct((N_TOK, SWA_WIDTH), BF16),
        compiler_params=_cparams(("arbitrary", "arbitrary")),
        name="swa_attn",
    )(sinks, proj, proj, proj, proj, proj)


FOX_T = 512
FOX_ONES = 16
FOX_VROWS = HEAD_DIM + FOX_ONES


def _fox_kernel(q_ref, k_ref, aug_ref, vt_ref, o_ref, mask_ref, t_ref, acc_ref):
    qi = pl.program_id(2)
    t = FOX_T

    @pl.when(qi == 0)
    def _():
        kr = lax.broadcasted_iota(jnp.int32, (t, t), 0)
        qc = lax.broadcasted_iota(jnp.int32, (t, t), 1)
        mask_ref[...] = jnp.where(kr <= qc, 0.0, -jnp.inf)

    qf = q_ref[...].astype(F32).T
    drow = lax.broadcasted_iota(jnp.int32, (LANES, t), 0)
    qaug = []
    for hh in range(2):
        qh = jnp.where((drow >= HEAD_DIM * hh) & (drow < HEAD_DIM * (hh + 1)), qf, 0.0)
        ones = jnp.where((drow >= FOX_AUG * hh) & (drow < FOX_AUG * (hh + 1)), 1.0, 0.0)
        qaug.append(jnp.concatenate([qh, ones], axis=0).astype(BF16))
    acc_ref[...] = jnp.zeros_like(acc_ref)

    def stage_a(j, slot, diag):
        r0 = pl.multiple_of(j * t, t)
        kb = jnp.concatenate([k_ref[pl.ds(r0, t), :], aug_ref[pl.ds(r0, t), :]], axis=1)
        sts = [jnp.dot(kb, qaug[hh], preferred_element_type=F32) for hh in range(2)]
        mbs = []
        for hh in range(2):
            tt = sts[hh]
            if diag:
                tt = tt + mask_ref[...]
            t_ref[slot, hh] = tt
            mbs.append(jnp.max(tt, axis=0, keepdims=True))
        return tuple(mbs)

    def stage_b(jv, slot, ms, mbs):
        new, ps, alphas = [], [], []
        for hh in range(2):
            m_new = jnp.maximum(ms[hh], mbs[hh])
            alphas.append(jnp.exp2(ms[hh] - m_new))
            ps.append(jnp.exp2(t_ref[slot, hh] - m_new).astype(BF16))
            new.append(m_new)
        ones = jnp.ones((FOX_ONES, t), BF16)
        pvs = [jnp.dot(jnp.concatenate([vt_ref[0, 0, jv, hh * HEAD_DIM:(hh + 1) * HEAD_DIM, :], ones], axis=0),
                       ps[hh], preferred_element_type=F32)
               for hh in range(2)]
        for hh in range(2):
            acc_ref[hh] = alphas[hh] * acc_ref[hh] + pvs[hh]
        return tuple(new)

    ml0 = tuple(jnp.full((1, t), -jnp.inf, F32) for _ in range(2))
    mb0 = stage_a(qi, 0, True)

    def pair(j0, c):
        ml, mbs, jprev = c
        mb1 = stage_a(j0, 1, False)
        ml = stage_b(jprev, 0, ml, mbs)
        mb2 = stage_a(j0 + 1, 0, False)
        ml = stage_b(j0, 1, ml, mb1)
        return ml, mb2, j0 + 1

    def quad(ii, c):
        return pair(4 * ii + 2, pair(4 * ii, c))

    carry = lax.fori_loop(0, qi // 4, quad, (ml0, mb0, qi))
    base = 4 * (qi // 4)

    def tail(rem):
        def run(c):
            ml, mb_prev, prev = c
            for k in range(rem):
                mb_new = stage_a(base + k, (k + 1) % 2, False)
                ml = stage_b(prev, k % 2, ml, mb_prev)
                prev, mb_prev = base + k, mb_new
            stage_b(prev, rem % 2, ml, mb_prev)
            ot = jnp.concatenate([acc_ref[hh, 0:HEAD_DIM, :] / acc_ref[hh, HEAD_DIM:HEAD_DIM + 1, :]
                                  for hh in range(2)], axis=0)
            o_ref[...] = ot.T.astype(BF16)
            return jnp.int32(0)
        return run

    lax.switch(qi % 4, [tail(rem) for rem in range(4)], carry)


def _fox(proj, aug, vt5):
    nq = SEQ // FOX_T
    npair = N_HEADS_FOX // 2
    return pl.pallas_call(
        _fox_kernel,
        grid=(BATCH, npair, nq),
        in_specs=[
            pl.BlockSpec((FOX_T, LANES), lambda b, h, i: (b * nq + i, QB_BLK + h)),
            pl.BlockSpec((SEQ, LANES), lambda b, h, i: (b, KB_BLK + h)),
            pl.BlockSpec((SEQ, LANES), lambda b, h, i: (b, h)),
            pl.BlockSpec((1, 1, nq, LANES, FOX_T), lambda b, h, i: (b, h, 0, 0, 0)),
        ],
        out_specs=pl.BlockSpec((FOX_T, LANES), lambda b, h, i: (b * nq + i, h)),
        out_shape=jax.ShapeDtypeStruct((N_TOK, FOX_WIDTH), BF16),
        scratch_shapes=[
            pltpu.VMEM((FOX_T, FOX_T), F32),
            pltpu.VMEM((2, 2, FOX_T, FOX_T), F32),
            pltpu.VMEM((2, FOX_VROWS, FOX_T), F32),
        ],
        compiler_params=_cparams(("arbitrary", "arbitrary", "arbitrary")),
        name="fox_attn",
    )(proj, proj, aug, vt5)


OUT_TM = 512
OUT_SUB = 128
SUBLANES = 8
PACKED_COLS = D_MODEL // 2
PACK_ROWS = PACKED_COLS // LANES
DMA_UNROLL = 8


def _outproj_kernel(oa_ref, ob_ref, x_ref, mod_ref, ga_ref, gb_ref, w_ref, gm_ref, wr2_ref, br_ref,
                    x1_ref, h2p_ref, ids_ref, gates_ref):
    subs = [slice(k * OUT_SUB, (k + 1) * OUT_SUB) for k in range(OUT_TM // OUT_SUB)]
    mixed = [jnp.concatenate([_rms(oa_ref[r, :].astype(F32)) * ga_ref[...],
                              _rms(ob_ref[r, :].astype(F32)) * gb_ref[...]], axis=1).astype(BF16) for r in subs]
    ys = [jnp.dot(m, w_ref[...], preferred_element_type=F32) for m in mixed]
    h2s = []
    for r, y in zip(subs, ys):
        x1 = x_ref[r, :] + mod_ref[0, 2:3, :] * y
        x1_ref[r, :] = x1
        h2s.append((_rms(x1) * gm_ref[...]) * (1.0 + mod_ref[0, 4:5, :]) + mod_ref[0, 3:4, :])
    hbs = [h2.astype(BF16) for h2 in h2s]
    for r, h2 in zip(subs, h2s):
        packed = _pack_bf16_pairs(h2)
        for s in range(PACK_ROWS):
            h2p_ref[r, s, :] = packed[:, s * LANES:(s + 1) * LANES]

    hls = [(h2 - hb.astype(F32)).astype(BF16) for h2, hb in zip(h2s, hbs)]
    r2s = [jnp.dot(hb, wr2_ref[...], preferred_element_type=F32) for hb in hbs]
    r3s = [jnp.dot(hl, wr2_ref[:, :LANES], preferred_element_type=F32) for hl in hls]
    for r, r2, r3 in zip(subs, r2s, r3s):
        logits = (r2[:, :LANES] + (r2[:, LANES:] + r3)) + br_ref[...]
        ids, gates = _route(logits)
        ids_ref[r, :] = ids
        gates_ref[r, :] = gates


def _route(logits):
    lane = lax.broadcasted_iota(jnp.int32, logits.shape, 1)
    neg = -jnp.inf
    is_g = lane < N_GROUPS
    gl = jnp.where(is_g, logits, neg)
    gmax = jnp.max(gl, axis=1, keepdims=True)
    gsel = jnp.min(jnp.where(gl == gmax, lane, LANES), axis=1, keepdims=True)
    gsum = jnp.sum(jnp.where(is_g, jnp.exp(gl - gmax), 0.0), axis=1, keepdims=True)
    g_val = 1.0 / gsum
    elane = lane - N_GROUPS
    in_sel = (elane >= 0) & (elane < N_EXPERTS) & ((elane >> 3) == gsel)
    ev = jnp.where(in_sel, logits, neg)
    t1 = jnp.max(ev, axis=1, keepdims=True)
    i1 = jnp.min(jnp.where(ev == t1, lane, LANES), axis=1, keepdims=True)
    ev2 = jnp.where(lane == i1, neg, ev)
    t2 = jnp.max(ev2, axis=1, keepdims=True)
    i2 = jnp.min(jnp.where(ev2 == t2, lane, LANES), axis=1, keepdims=True)
    e2 = jnp.exp(t2 - t1)
    den = 1.0 + e2
    w1 = (1.0 / den) * g_val
    w2 = (e2 / den) * g_val
    return (jnp.where(lane == 0, i1 - N_GROUPS, jnp.where(lane == 1, i2 - N_GROUPS, 0)),
            jnp.where(lane == 0, w1, jnp.where(lane == 1, w2, 0.0)))


def _outproj(oa, ob, x2, mod3, ga, gb, w_out, gm, wr, br):
    tiles_per_batch = SEQ // OUT_TM
    row = lambda i: (i, 0)
    const = lambda i: (0, 0)
    wr_hi = wr.astype(BF16)
    wr_lo = (wr - wr_hi.astype(F32)).astype(BF16)
    wr2 = jnp.concatenate([wr_hi, wr_lo], axis=1)
    return pl.pallas_call(
        _outproj_kernel,
        grid=(N_TOK // OUT_TM,),
        in_specs=[
            pl.BlockSpec((OUT_TM, SWA_WIDTH), row),
            pl.BlockSpec((OUT_TM, FOX_WIDTH), row),
            pl.BlockSpec((OUT_TM, D_MODEL), row),
            pl.BlockSpec((1, 6, D_MODEL), lambda i: (i // tiles_per_batch, 0, 0)),
            pl.BlockSpec((1, SWA_WIDTH), const),
            pl.BlockSpec((1, FOX_WIDTH), const),
            pl.BlockSpec((D_MODEL, D_MODEL), const),
            pl.BlockSpec((1, D_MODEL), const),
            pl.BlockSpec((D_MODEL, 2 * LANES), const),
            pl.BlockSpec((1, LANES), const),
        ],
        out_specs=[
            pl.BlockSpec((OUT_TM, D_MODEL), row),
            pl.BlockSpec((OUT_TM, PACK_ROWS, LANES), lambda i: (i, 0, 0)),
            pl.BlockSpec((OUT_TM, LANES), row),
            pl.BlockSpec((OUT_TM, LANES), row),
        ],
        out_shape=[
            jax.ShapeDtypeStruct((N_TOK, D_MODEL), F32),
            jax.ShapeDtypeStruct((N_TOK, PACK_ROWS, LANES), jnp.uint32),
            jax.ShapeDtypeStruct((N_TOK, LANES), jnp.int32),
            jax.ShapeDtypeStruct((N_TOK, LANES), F32),
        ],
        compiler_params=_cparams(("arbitrary",)),
        name="outproj_router",
    )(oa, ob, x2, mod3, ga, gb, w_out, gm, wr2, br)


N_BLOCKS = (N_TOK * TOP_K) // MOE_BLOCK + N_EXPERTS
P_ROWS = N_BLOCKS * MOE_BLOCK


def _pack_bf16_pairs(x):
    xb = x.astype(BF16).astype(F32)
    half = x.shape[1] // 2
    lo = lax.shift_right_logical(pltpu.bitcast(xb[:, :half], jnp.uint32), jnp.uint32(16))
    hi = pltpu.bitcast(xb[:, half:], jnp.uint32) & jnp.uint32(0xFFFF0000)
    return lo | hi


def _unpack_bf16_pairs(w):
    return (pltpu.bitcast(lax.shift_left(w, jnp.uint32(16)), F32),
            pltpu.bitcast(w & jnp.uint32(0xFFFF0000), F32))


def _expert_kernel(be_ref, tok_ref, src_ref, nused_ref, nxt_ref, h2_hbm, wg_hbm, wu_hbm, wd_hbm, y_hbm, xbuf, sem,
                   wgf, wuf, wdf, wsem, wgb, wub, wdb, ystage, osem):
    i = pl.program_id(0)
    nused = nused_ref[0]

    def weight_copies(e):
        return (pltpu.make_async_copy(wg_hbm.at[e], wgf, wsem.at[0]),
                pltpu.make_async_copy(wu_hbm.at[e], wuf, wsem.at[1]),
                pltpu.make_async_copy(wd_hbm.at[e], wdf, wsem.at[2]))

    def issue(blk, slot):
        def body(g, _):
            for q in range(SUBLANES):
                tok = tok_ref[jnp.minimum(src_ref[blk] + g * SUBLANES + q, N_TOK * TOP_K - 1)]
                pltpu.make_async_copy(h2_hbm.at[tok], xbuf.at[slot, g, :, q, :], sem.at[slot]).start(
                    priority=q % 2)
            return 0
        lax.fori_loop(0, MOE_BLOCK // SUBLANES, body, 0)

    def out_copies(blk, slot):
        r0 = pl.multiple_of(blk * MOE_BLOCK, MOE_BLOCK)
        return [pltpu.make_async_copy(ystage.at[slot, :, pl.ds(s * LANES, LANES)],
                                      y_hbm.at[pl.ds(r0, MOE_BLOCK), s, :], osem.at[slot])
                for s in range(PACK_ROWS)]

    def out_wait(slot):
        pltpu.make_async_copy(ystage.at[slot], ystage.at[slot], osem.at[slot]).wait()

    def gather_wait(slot):
        pltpu.make_async_copy(xbuf.at[slot], xbuf.at[slot], sem.at[slot]).wait()

    @pl.when(i == 0)
    def _():
        issue(0, 0)

    @pl.when(i + 1 < nused)
    def _():
        issue(i + 1, (i + 1) % 2)

    oslot = i % 2

    @pl.when(i >= 2)
    def _():
        out_wait(oslot)

    e = be_ref[i]
    new_expert = ((i == 0) | (e != be_ref[jnp.maximum(i - 1, 0)])) & (i < nused)

    @pl.when(i == 0)
    def _():
        for cp in weight_copies(e):
            cp.start()

    @pl.when(new_expert)
    def _():
        for cp in weight_copies(e):
            cp.wait()
        wgb[...] = wgf[...].astype(BF16)
        wub[...] = wuf[...].astype(BF16)
        wdb[...] = wdf[...].astype(BF16)

    @pl.when(new_expert & (nxt_ref[e] >= 0))
    def _():
        for cp in weight_copies(nxt_ref[e]):
            cp.start()

    @pl.when(i < nused)
    def _():
        slot = i % 2
        gather_wait(slot)
        gsub = MOE_SUB // SUBLANES
        xbs = []
        for k in range(MOE_BLOCK // MOE_SUB):
            xu = jnp.concatenate([xbuf[slot, k * gsub:(k + 1) * gsub, s].reshape(MOE_SUB, LANES)
                                  for s in range(PACK_ROWS)], axis=1)
            xbs.append(jnp.concatenate(_unpack_bf16_pairs(xu), axis=1).astype(BF16))
        gs = [jnp.dot(xb, wgb[...], preferred_element_type=F32) for xb in xbs]
        us = [jnp.dot(xb, wub[...], preferred_element_type=F32) for xb in xbs]
        hids = [(jax.nn.silu(g) * u).astype(BF16) for g, u in zip(gs, us)]
        ys = [jnp.dot(hid, wdb[...], preferred_element_type=F32) for hid in hids]
        for k, y in enumerate(ys):
            ystage[oslot, k * MOE_SUB:(k + 1) * MOE_SUB, :] = _pack_bf16_pairs(y)

    @pl.when(i >= nused)
    def _():
        ystage[oslot] = jnp.zeros((MOE_BLOCK, PACKED_COLS), jnp.uint32)

    for cp in out_copies(i, oslot):
        cp.start()

    @pl.when(i == pl.num_programs(0) - 1)
    def _():
        out_wait(oslot)
        out_wait(1 - oslot)


def _experts(block_e, tok_sorted, blk_src, nused, next_used, h2p, wg, wu, wd):
    grid_spec = pltpu.PrefetchScalarGridSpec(
        num_scalar_prefetch=5,
        grid=(N_BLOCKS,),
        in_specs=[pl.BlockSpec(memory_space=pl.ANY)] * 4,
        out_specs=pl.BlockSpec(memory_space=pl.ANY),
        scratch_shapes=[pltpu.VMEM((2, MOE_BLOCK // SUBLANES, PACK_ROWS, SUBLANES, LANES), jnp.uint32),
                        pltpu.SemaphoreType.DMA((2,)),
                        pltpu.VMEM((D_MODEL, D_EXPERT), F32),
                        pltpu.VMEM((D_MODEL, D_EXPERT), F32),
                        pltpu.VMEM((D_EXPERT, D_MODEL), F32),
                        pltpu.SemaphoreType.DMA((3,)),
                        pltpu.VMEM((D_MODEL, D_EXPERT), BF16),
                        pltpu.VMEM((D_MODEL, D_EXPERT), BF16),
                        pltpu.VMEM((D_EXPERT, D_MODEL), BF16),
                        pltpu.VMEM((2, MOE_BLOCK, PACKED_COLS), jnp.uint32),
                        pltpu.SemaphoreType.DMA((2,))],
    )
    return pl.pallas_call(
        _expert_kernel,
        grid_spec=grid_spec,
        out_shape=jax.ShapeDtypeStruct((P_ROWS, PACK_ROWS, LANES), jnp.uint32),
        compiler_params=_cparams(("arbitrary",)),
        name="expert_ffn",
    )(block_e, tok_sorted, blk_src, nused, next_used, h2p, wg, wu, wd)


CMB_TM = 512


def _combine_kernel(dest_ref, ys_hbm, x1_ref, gates_ref, mod_ref, fg_ref, o_ref, ybuf, sem):
    i = pl.program_id(0)
    n = pl.num_programs(0)
    groups = CMB_TM // SUBLANES

    def issue(tile, slot):
        def body(g, _):
            for q in range(SUBLANES):
                for k in range(TOP_K):
                    d = dest_ref[(tile * CMB_TM + g * SUBLANES + q) * TOP_K + k]
                    pltpu.make_async_copy(ys_hbm.at[d], ybuf.at[slot, k * groups + g, :, q, :],
                                          sem.at[slot]).start(priority=k)
            return 0
        lax.fori_loop(0, groups, body, 0)

    @pl.when(i == 0)
    def _():
        issue(0, 0)

    @pl.when(i + 1 < n)
    def _():
        issue(i + 1, (i + 1) % 2)

    slot = i % 2
    pltpu.make_async_copy(ybuf.at[slot], ybuf.at[slot], sem.at[slot]).wait()
    gts = gates_ref[...]
    w0 = gts[:, 0:1]
    w1 = gts[:, 1:2]
    half = D_MODEL // 2
    ssq = jnp.zeros((CMB_TM, 1), F32)
    for s in range(PACK_ROWS):
        y0 = _unpack_bf16_pairs(ybuf[slot, 0:groups, s].reshape(CMB_TM, LANES))
        y1 = _unpack_bf16_pairs(ybuf[slot, groups:2 * groups, s].reshape(CMB_TM, LANES))
        for part in range(2):
            cols = slice(part * half + s * LANES, part * half + (s + 1) * LANES)
            x2 = x1_ref[:, cols] + mod_ref[0, 5:6, cols] * (y0[part] * w0 + y1[part] * w1)
            ssq = ssq + jnp.sum(x2 * x2, axis=-1, keepdims=True)
            o_ref[:, cols] = x2
    o_ref[...] = (o_ref[...] * lax.rsqrt(ssq * (1.0 / D_MODEL) + EPS)) * fg_ref[...]


def _combine(dest, ys, x1, gates, mod3, final_g):
    tiles_per_batch = SEQ // CMB_TM
    grid_spec = pltpu.PrefetchScalarGridSpec(
        num_scalar_prefetch=1,
        grid=(N_TOK // CMB_TM,),
        in_specs=[
            pl.BlockSpec(memory_space=pl.ANY),
            pl.BlockSpec((CMB_TM, D_MODEL), lambda i, d: (i, 0)),
            pl.BlockSpec((CMB_TM, LANES), lambda i, d: (i, 0)),
            pl.BlockSpec((1, 6, D_MODEL), lambda i, d: (i // tiles_per_batch, 0, 0)),
            pl.BlockSpec((1, D_MODEL), lambda i, d: (0, 0)),
        ],
        out_specs=pl.BlockSpec((CMB_TM, D_MODEL), lambda i, d: (i, 0)),
        scratch_shapes=[pltpu.VMEM((2, TOP_K * CMB_TM // SUBLANES, PACK_ROWS, SUBLANES, LANES), jnp.uint32),
                        pltpu.SemaphoreType.DMA((2,))],
    )
    return pl.pallas_call(
        _combine_kernel,
        grid_spec=grid_spec,
        out_shape=jax.ShapeDtypeStruct((N_TOK, D_MODEL), F32),
        compiler_params=_cparams(("arbitrary",)),
        name="combine_final",
    )(dest, ys, x1, gates, mod3, final_g)


def _routing_tables(eid):
    ids = jnp.arange(N_EXPERTS, dtype=jnp.int32)
    onehot = (eid[None, :] == ids[:, None]).astype(jnp.int32)
    csum = jnp.cumsum(onehot, axis=1)
    counts = csum[:, -1]
    padded = (counts + MOE_BLOCK - 1) // MOE_BLOCK * MOE_BLOCK
    pend = jnp.cumsum(padded)
    pstart = pend - padded
    dest = (jnp.sum(onehot * (csum + pstart[:, None]), axis=0) - 1).astype(jnp.int32)
    blk_row = jnp.arange(N_BLOCKS, dtype=jnp.int32) * MOE_BLOCK
    block_e = jnp.minimum(jnp.sum((pend[None, :] <= blk_row[:, None]).astype(jnp.int32), axis=1),
                          N_EXPERTS - 1).astype(jnp.int32)
    tok_sorted = (jnp.argsort(eid, stable=True) // TOP_K).astype(jnp.int32)
    starts = jnp.cumsum(counts) - counts
    is_e = (block_e[:, None] == ids[None, :]).astype(jnp.int32)
    blk_src = (blk_row + jnp.sum(is_e * (starts - pstart)[None, :], axis=1)).astype(jnp.int32)
    nused = (pend[-1] // MOE_BLOCK).astype(jnp.int32).reshape(1)
    later_used = (ids[None, :] > ids[:, None]) & (counts[None, :] > 0)
    next_used = jnp.min(jnp.where(later_used, ids[None, :], N_EXPERTS), axis=1)
    next_used = jnp.where(next_used == N_EXPERTS, -1, next_used).astype(jnp.int32)
    return dest, tok_sorted, blk_src, block_e, nused, next_used


def kernel(x, c, w_ada, b_ada, norm_mix_g, w_in, b_forget, sinks, out_norm_swa_g, out_norm_fox_g, w_out,
           norm_moe_g, w_group, b_group, w_expert, b_expert, w_gate, w_up, w_down, final_g):
    assert IN_TM == FOX_T
    x2 = x.reshape(N_TOK, D_MODEL)

    w_in0 = w_in[0]
    w_row = w_in0[:, :VB_COL].astype(BF16)
    w_vt = w_in0[:, VB_COL:MAIN_COLS].astype(BF16)
    w_f = jnp.pad(w_in0[:, MAIN_COLS:], ((0, 0), (0, LANES - N_HEADS_FOX))).astype(BF16)
    colscale = jnp.concatenate([
        jnp.full((SWA_WIDTH,), HEAD_DIM ** -0.5 * LOG2E, F32),
        jnp.ones((2 * KV_WIDTH,), F32),
        jnp.full((FOX_WIDTH,), HEAD_DIM ** -0.5 * LOG2E, F32),
        jnp.ones((FOX_WIDTH,), F32)]).reshape(1, ROW_COLS)
    bf_row = jnp.pad(b_forget[0], (0, LANES - N_HEADS_FOX)).reshape(1, LANES)
    w_out_p = w_out[0].astype(BF16)
    ga = out_norm_swa_g[0].reshape(1, SWA_WIDTH)
    gb = out_norm_fox_g[0].reshape(1, FOX_WIDTH)
    wr = jnp.pad(jnp.concatenate([w_group[0], w_expert[0]], axis=1),
                 ((0, 0), (0, LANES - N_GROUPS - N_EXPERTS)))
    br = jnp.pad(jnp.concatenate([b_group[0], b_expert[0]]), (0, LANES - N_GROUPS - N_EXPERTS)).reshape(1, LANES)
    wg, wu, wd = w_gate[0], w_up[0], w_down[0]

    mod = _adaln(c, w_ada[0], b_ada[0])
    mod3 = mod.reshape(BATCH, 6, D_MODEL)

    proj, f, vt5 = _inproj(x2, mod3, norm_mix_g[0].reshape(1, D_MODEL), w_row, w_f, colscale, w_vt)
    aug = _forget_bias(f, bf_row)

    o_a = _swa(sinks[0], proj)
    o_b = _fox(proj, aug, vt5)

    x1, h2, ids, gates = _outproj(o_a, o_b, x2, mod3, ga, gb, w_out_p,
                                  norm_moe_g[0].reshape(1, D_MODEL), wr, br)

    eid = ids[:, :TOP_K].reshape(-1)
    dest, tok_sorted, blk_src, block_e, nused, next_used = _routing_tables(eid)
    ys = _experts(block_e, tok_sorted, blk_src, nused, next_used, h2, wg, wu, wd)
    out = _combine(dest, ys, x1, gates, mod3, final_g.reshape(1, D_MODEL))
    return out.reshape(BATCH, SEQ, D_MODEL)
```
